```python
import jax, jax.numpy as jnp
from jax import lax
import numpy as np

D_MODEL = 1024
BATCH = 8
SEQ = 2048
DEPTH = 2

HEAD_DIM = 64
N_MIXERS = 4
GROUP_WIDTH = D_MODEL // N_MIXERS
D_MIX = N_MIXERS * GROUP_WIDTH
FOX_HEADS = GROUP_WIDTH // HEAD_DIM
FOX_BLOCK = 128
CONV_CHANNELS = GROUP_WIDTH
CONV_WIDTH = 31
SWA_Q_HEADS = GROUP_WIDTH // HEAD_DIM
SWA_KV_HEADS = SWA_Q_HEADS // 2
SWA_WINDOW = 128
SGU_GROUPS = GROUP_WIDTH // HEAD_DIM
SGU_CHUNK = 128
D_FF = 11 * D_MODEL // 4
FFN_CONV_WIDTH = 3
EPS = 1e-6

FOX_COLS = 3 * GROUP_WIDTH + FOX_HEADS
CONV_COLS = 2 * CONV_CHANNELS
SWA_COLS = SWA_Q_HEADS * HEAD_DIM + 2 * SWA_KV_HEADS * HEAD_DIM
SGU_COLS = 2 * GROUP_WIDTH
IN_COLS = FOX_COLS + CONV_COLS + SWA_COLS + SGU_COLS

kernel_name = "hybrid_parallel_heads_fox_conformer_swa_sgu"


def rms_norm(x, g):
    xf = x.astype(jnp.float32)
    y = xf * lax.rsqrt(jnp.mean(xf * xf, axis=-1, keepdims=True) + EPS)
    return (y * g.astype(jnp.float32)).astype(x.dtype)


def layer_norm(x, g, b):
    xf = x.astype(jnp.float32)
    mu = jnp.mean(xf, axis=-1, keepdims=True)
    xc = xf - mu
    var = jnp.mean(xc * xc, axis=-1, keepdims=True)
    y = xc * lax.rsqrt(var + EPS) * g.astype(jnp.float32) + b.astype(jnp.float32)
    return y.astype(x.dtype)


def causal_depthwise_conv(x, w, b):
    k_width, ch = w.shape
    y = lax.conv_general_dilated(
        x, w[:, None, :].astype(x.dtype), window_strides=(1,), padding=[(k_width - 1, 0)],
        dimension_numbers=('NWC', 'WIO', 'NWC'), feature_group_count=ch)
    return y + b.astype(x.dtype)


def fox_attention(q, k, v, fg_logit, b_f):
    bsz, seq, heads, dh = q.shape
    log_f = jax.nn.log_sigmoid(fg_logit.astype(jnp.float32) + b_f.astype(jnp.float32))
    cum = jnp.cumsum(log_f, axis=1)
    nb = seq // FOX_BLOCK
    qb = q.reshape(bsz, nb, FOX_BLOCK, heads, dh).transpose(1, 0, 2, 3, 4)
    cb = cum.reshape(bsz, nb, FOX_BLOCK, heads).transpose(1, 0, 2, 3)
    cum_k = cum.transpose(0, 2, 1)
    key_pos = jnp.arange(seq)
    scale = HEAD_DIM ** -0.5

    def block(args):
        q_blk, c_blk, i = args
        s = jnp.einsum('bqhd,bkhd->bhqk', q_blk, k, preferred_element_type=jnp.float32) * scale
        s = s + c_blk.transpose(0, 2, 1)[..., None] - cum_k[:, :, None, :]
        q_pos = i * FOX_BLOCK + jnp.arange(FOX_BLOCK)
        s = jnp.where(key_pos[None, :] <= q_pos[:, None], s, -jnp.inf)
        p = jax.nn.softmax(s, axis=-1).astype(v.dtype)
        return jnp.einsum('bhqk,bkhd->bqhd', p, v)

    out = lax.map(block, (qb, cb, jnp.arange(nb)))
    return out.transpose(1, 0, 2, 3, 4).reshape(bsz, seq, heads * dh)


def swa_attention(q, k, v, sinks):
    bsz, seq, hq, dh = q.shape
    hkv = k.shape[2]
    grp = hq // hkv
    win = SWA_WINDOW
    nb = seq // win
    qb = q.reshape(bsz, nb, win, hkv, grp, dh)
    pad = jnp.zeros((bsz, win, hkv, dh), k.dtype)
    kp = jnp.concatenate([pad, k], axis=1).reshape(bsz, nb + 1, win, hkv, dh)
    vp = jnp.concatenate([pad.astype(v.dtype), v], axis=1).reshape(bsz, nb + 1, win, hkv, dh)
    kw = jnp.concatenate([kp[:, :-1], kp[:, 1:]], axis=2)
    vw = jnp.concatenate([vp[:, :-1], vp[:, 1:]], axis=2)
    s = jnp.einsum('bnqhgd,bnkhd->bnhgqk', qb, kw, preferred_element_type=jnp.float32) * (HEAD_DIM ** -0.5)
    qi = jnp.arange(win)[:, None]
    kj = jnp.arange(2 * win)[None, :]
    blk = jnp.arange(nb)[:, None, None]
    mask = (kj > qi) & (kj <= qi + win) & (blk * win + kj >= win)
    s = jnp.where(mask[None, :, None, None], s, -jnp.inf)
    sink = sinks.astype(jnp.float32).reshape(hkv, grp)[None, None, :, :, None, None]
    m = jnp.maximum(jnp.max(s, axis=-1, keepdims=True), sink)
    p = jnp.exp(s - m)
    p = p / (jnp.sum(p, axis=-1, keepdims=True) + jnp.exp(sink - m))
    out = jnp.einsum('bnhgqk,bnkhd->bnqhgd', p.astype(v.dtype), vw)
    return out.reshape(bsz, seq, hq * dh)


def conformer_conv(z, conv_w, conv_b, ln_g, ln_b, pw_w, pw_b):
    a, g = jnp.split(z, 2, axis=-1)
    h = a * jax.nn.sigmoid(g)
    h = causal_depthwise_conv(h, conv_w, conv_b)
    h = jax.nn.silu(layer_norm(h, ln_g, ln_b))
    return h @ pw_w + pw_b


def spatial_gating(z, ln_g, ln_b, w_s, b_s):
    z = jax.nn.gelu(z)
    u, v = jnp.split(z, 2, axis=-1)
    v = layer_norm(v, ln_g, ln_b)
    bsz, seq, _ = v.shape
    nc = seq // SGU_CHUNK
    gd = GROUP_WIDTH // SGU_GROUPS
    v = v.reshape(bsz, nc, SGU_CHUNK, SGU_GROUPS, gd)
    tri = jnp.tril(jnp.ones((SGU_CHUNK, SGU_CHUNK), dtype=bool))
    w = jnp.where(tri[None], w_s, 0).astype(v.dtype)
    mix = jnp.einsum('gts,bnsgc->bntgc', w, v) + b_s.T.astype(v.dtype)[:, :, None]
    return u * mix.reshape(bsz, seq, GROUP_WIDTH)


def hybrid_mixer(h, w_in, b_fgate, conv_w, conv_b, conv_ln_g, conv_ln_b, conv_pw_w, conv_pw_b,
                 swa_sinks, sgu_ln_g, sgu_ln_b, sgu_w, sgu_b, g_group, w_out):
    bsz, seq, _ = h.shape
    z = h @ w_in
    z_fox, z_conv, z_swa, z_sgu = jnp.split(
        z, [FOX_COLS, FOX_COLS + CONV_COLS, FOX_COLS + CONV_COLS + SWA_COLS], axis=-1)
    fq, fk, fv, ff = jnp.split(z_fox, [GROUP_WIDTH, 2 * GROUP_WIDTH, 3 * GROUP_WIDTH], axis=-1)
    hs = (bsz, seq, FOX_HEADS, HEAD_DIM)
    y_fox = fox_attention(fq.reshape(hs), fk.reshape(hs), fv.reshape(hs), ff, b_fgate)
    y_conv = conformer_conv(z_conv, conv_w, conv_b, conv_ln_g, conv_ln_b, conv_pw_w, conv_pw_b)
    nq = SWA_Q_HEADS * HEAD_DIM
    nkv = SWA_KV_HEADS * HEAD_DIM
    sq, sk, sv = jnp.split(z_swa, [nq, nq + nkv], axis=-1)
    y_swa = swa_attention(sq.reshape(bsz, seq, SWA_Q_HEADS, HEAD_DIM),
                          sk.reshape(bsz, seq, SWA_KV_HEADS, HEAD_DIM),
                          sv.reshape(bsz, seq, SWA_KV_HEADS, HEAD_DIM), swa_sinks)
    y_sgu = spatial_gating(z_sgu, sgu_ln_g, sgu_ln_b, sgu_w, sgu_b)
    y = jnp.stack([y_fox, y_conv, y_swa, y_sgu], axis=2)
    y = rms_norm(y, g_group.reshape(N_MIXERS, GROUP_WIDTH))
    return y.reshape(bsz, seq, D_MIX) @ w_out


def conv_ffn(h, w_up, conv_w, conv_b, w_down):
    u = causal_depthwise_conv(h @ w_up, conv_w, conv_b)
    g, val = jnp.split(u, 2, axis=-1)
    return (jax.nn.silu(g) * val) @ w_down


def _fwd_setup_inputs(seed: int = 0) -> dict:
    key = jax.random.key(seed)
    ks = jax.random.split(key, 32)
    L = DEPTH

    def nrm(k, shape, scale):
        return jax.random.normal(k, shape, jnp.float32) * scale

    return {
        "x": nrm(ks[0], (BATCH, SEQ, D_MODEL), 1.0),
        "c": nrm(ks[1], (BATCH, D_MODEL), 1.0),
        "w_ada": nrm(ks[2], (L, D_MODEL, 6 * D_MODEL), 0.5 * D_MODEL ** -0.5),
        "b_ada": nrm(ks[3], (L, 6 * D_MODEL), 0.02),
        "g_pre_mix": 1.0 + nrm(ks[4], (L, D_MODEL), 0.05),
        "g_post_mix": 1.0 + nrm(ks[5], (L, D_MODEL), 0.05),
        "g_pre_ffn": 1.0 + nrm(ks[6], (L, D_MODEL), 0.05),
        "g_post_ffn": 1.0 + nrm(ks[7], (L, D_MODEL), 0.05),
        "w_in": nrm(ks[8], (L, D_MODEL, IN_COLS), D_MODEL ** -0.5),
        "b_fgate": 2.0 + nrm(ks[9], (L, FOX_HEADS), 0.1),
        "conv_w": nrm(ks[10], (L, CONV_WIDTH, CONV_CHANNELS), CONV_WIDTH ** -0.5),
        "conv_b": nrm(ks[11], (L, CONV_CHANNELS), 0.02),
        "conv_ln_g": 1.0 + nrm(ks[12], (L, CONV_CHANNELS), 0.05),
        "conv_ln_b": nrm(ks[13], (L, CONV_CHANNELS), 0.02),
        "conv_pw_w": nrm(ks[14], (L, CONV_CHANNELS, CONV_CHANNELS), CONV_CHANNELS ** -0.5),
        "conv_pw_b": nrm(ks[15], (L, CONV_CHANNELS), 0.02),
        "swa_sinks": nrm(ks[16], (L, SWA_Q_HEADS), 0.5),
        "sgu_ln_g": 1.0 + nrm(ks[17], (L, GROUP_WIDTH), 0.05),
        "sgu_ln_b": nrm(ks[18], (L, GROUP_WIDTH), 0.02),
        "sgu_w": nrm(ks[19], (L, SGU_GROUPS, SGU_CHUNK, SGU_CHUNK), SGU_CHUNK ** -0.5),
        "sgu_b": 1.0 + nrm(ks[20], (L, SGU_GROUPS, SGU_CHUNK), 0.05),
        "g_group": 1.0 + nrm(ks[21], (L, D_MIX), 0.05),
        "w_out": nrm(ks[22], (L, D_MIX, D_MODEL), D_MIX ** -0.5),
        "ffn_w_up": nrm(ks[23], (L, D_MODEL, 2 * D_FF), D_MODEL ** -0.5),
        "ffn_conv_w": nrm(ks[24], (L, FFN_CONV_WIDTH, 2 * D_FF), FFN_CONV_WIDTH ** -0.5),
        "ffn_conv_b": nrm(ks[25], (L, 2 * D_FF), 0.02),
        "ffn_w_down": nrm(ks[26], (L, D_FF, D_MODEL), D_FF ** -0.5),
    }


def _fwd_reference(x, c, w_ada, b_ada, g_pre_mix, g_post_mix, g_pre_ffn, g_post_ffn, w_in, b_fgate,
              conv_w, conv_b, conv_ln_g, conv_ln_b, conv_pw_w, conv_pw_b, swa_sinks,
              sgu_ln_g, sgu_ln_b, sgu_w, sgu_b, g_group, w_out,
              ffn_w_up, ffn_conv_w, ffn_conv_b, ffn_w_down):
    c_act = jax.nn.silu(c)
    for l in range(DEPTH):
        mod = (c_act @ w_ada[l] + b_ada[l])[:, None, :]
        sh1, sc1, ga1, sh2, sc2, ga2 = jnp.split(mod, 6, axis=-1)
        h = rms_norm(x, g_pre_mix[l]) * (1.0 + sc1) + sh1
        y = hybrid_mixer(h, w_in[l], b_fgate[l], conv_w[l], conv_b[l], conv_ln_g[l], conv_ln_b[l],
                         conv_pw_w[l], conv_pw_b[l], swa_sinks[l], sgu_ln_g[l], sgu_ln_b[l],
                         sgu_w[l], sgu_b[l], g_group[l], w_out[l])
        x = x + ga1 * rms_norm(y, g_post_mix[l])
        h = rms_norm(x, g_pre_ffn[l]) * (1.0 + sc2) + sh2
        y = conv_ffn(h, ffn_w_up[l], ffn_conv_w[l], ffn_conv_b[l], ffn_w_down[l])
        x = x + ga2 * rms_norm(y, g_post_ffn[l])
    return x


import jax as _jax
import jax.numpy as _jnp

TWIN_FORMAT = 'train_step'
FWD_PARAMS = ['x', 'c', 'w_ada', 'b_ada', 'g_pre_mix', 'g_post_mix', 'g_pre_ffn', 'g_post_ffn', 'w_in', 'b_fgate', 'conv_w', 'conv_b', 'conv_ln_g', 'conv_ln_b', 'conv_pw_w', 'conv_pw_b', 'swa_sinks', 'sgu_ln_g', 'sgu_ln_b', 'sgu_w', 'sgu_b', 'g_group', 'w_out', 'ffn_w_up', 'ffn_conv_w', 'ffn_conv_b', 'ffn_w_down']
TWIN_WEIGHTS = ['w_ada', 'b_ada', 'g_pre_mix', 'g_post_mix', 'g_pre_ffn', 'g_post_ffn', 'w_in', 'b_fgate', 'conv_w', 'conv_b', 'conv_ln_g', 'conv_ln_b', 'conv_pw_w', 'conv_pw_b', 'swa_sinks', 'sgu_ln_g', 'sgu_ln_b', 'sgu_w', 'sgu_b', 'g_group', 'w_out', 'ffn_w_up', 'ffn_conv_w', 'ffn_conv_b', 'ffn_w_down']
TWIN_DIFF_INPUT = 'x'
TWIN_INPUTS = ['x', 'c', 'w_ada', 'b_ada', 'g_pre_mix', 'g_post_mix', 'g_pre_ffn', 'g_post_ffn', 'w_in', 'b_fgate', 'conv_w', 'conv_b', 'conv_ln_g', 'conv_ln_b', 'conv_pw_w', 'conv_pw_b', 'swa_sinks', 'sgu_ln_g', 'sgu_ln_b', 'sgu_w', 'sgu_b', 'g_group', 'w_out', 'ffn_w_up', 'ffn_conv_w', 'ffn_conv_b', 'ffn_w_down', 'loss_target', 'm_w_ada', 'm_b_ada', 'm_g_pre_mix', 'm_g_post_mix', 'm_g_pre_ffn', 'm_g_post_ffn', 'm_w_in', 'm_b_fgate', 'm_conv_w', 'm_conv_b', 'm_conv_ln_g', 'm_conv_ln_b', 'm_conv_pw_w', 'm_conv_pw_b', 'm_swa_sinks', 'm_sgu_ln_g', 'm_sgu_ln_b', 'm_sgu_w', 'm_sgu_b', 'm_g_group', 'm_w_out', 'm_ffn_w_up', 'm_ffn_conv_w', 'm_ffn_conv_b', 'm_ffn_w_down', 'v_w_ada', 'v_b_ada', 'v_g_pre_mix', 'v_g_post_mix', 'v_g_pre_ffn', 'v_g_post_ffn', 'v_w_in', 'v_b_fgate', 'v_conv_w', 'v_conv_b', 'v_conv_ln_g', 'v_conv_ln_b', 'v_conv_pw_w', 'v_conv_pw_b', 'v_swa_sinks', 'v_sgu_ln_g', 'v_sgu_ln_b', 'v_sgu_w', 'v_sgu_b', 'v_g_group', 'v_w_out', 'v_ffn_w_up', 'v_ffn_conv_w', 'v_ffn_conv_b', 'v_ffn_w_down']
TWIN_OUTPUTS = ['loss', 'grad_x', 'grad_w_ada', 'grad_b_ada', 'grad_g_pre_mix', 'grad_g_post_mix', 'grad_g_pre_ffn', 'grad_g_post_ffn', 'grad_w_in', 'grad_b_fgate', 'grad_conv_w', 'grad_conv_b', 'grad_conv_ln_g', 'grad_conv_ln_b', 'grad_conv_pw_w', 'grad_conv_pw_b', 'grad_swa_sinks', 'grad_sgu_ln_g', 'grad_sgu_ln_b', 'grad_sgu_w', 'grad_sgu_b', 'grad_g_group', 'grad_w_out', 'grad_ffn_w_up', 'grad_ffn_conv_w', 'grad_ffn_conv_b', 'grad_ffn_w_down', 'delta_w_ada', 'delta_b_ada', 'delta_g_pre_mix', 'delta_g_post_mix', 'delta_g_pre_ffn', 'delta_g_post_ffn', 'delta_w_in', 'delta_b_fgate', 'delta_conv_w', 'delta_conv_b', 'delta_conv_ln_g', 'delta_conv_ln_b', 'delta_conv_pw_w', 'delta_conv_pw_b', 'delta_swa_sinks', 'delta_sgu_ln_g', 'delta_sgu_ln_b', 'delta_sgu_w', 'delta_sgu_b', 'delta_g_group', 'delta_w_out', 'delta_ffn_w_up', 'delta_ffn_conv_w', 'delta_ffn_conv_b', 'delta_ffn_w_down', 'new_m_w_ada', 'new_m_b_ada', 'new_m_g_pre_mix', 'new_m_g_post_mix', 'new_m_g_pre_ffn', 'new_m_g_post_ffn', 'new_m_w_in', 'new_m_b_fgate', 'new_m_conv_w', 'new_m_conv_b', 'new_m_conv_ln_g', 'new_m_conv_ln_b', 'new_m_conv_pw_w', 'new_m_conv_pw_b', 'new_m_swa_sinks', 'new_m_sgu_ln_g', 'new_m_sgu_ln_b', 'new_m_sgu_w', 'new_m_sgu_b', 'new_m_g_group', 'new_m_w_out', 'new_m_ffn_w_up', 'new_m_ffn_conv_w', 'new_m_ffn_conv_b', 'new_m_ffn_w_down', 'new_v_w_ada', 'new_v_b_ada', 'new_v_g_pre_mix', 'new_v_g_post_mix', 'new_v_g_pre_ffn', 'new_v_g_post_ffn', 'new_v_w_in', 'new_v_b_fgate', 'new_v_conv_w', 'new_v_conv_b', 'new_v_conv_ln_g', 'new_v_conv_ln_b', 'new_v_conv_pw_w', 'new_v_conv_pw_b', 'new_v_swa_sinks', 'new_v_sgu_ln_g', 'new_v_sgu_ln_b', 'new_v_sgu_w', 'new_v_sgu_b', 'new_v_g_group', 'new_v_w_out', 'new_v_ffn_w_up', 'new_v_ffn_conv_w', 'new_v_ffn_conv_b', 'new_v_ffn_w_down']
TWIN_LEAF_KINDS = {'loss': 'loss', 'grad_x': 'grad_x', 'grad_w_ada': 'grad_w', 'grad_b_ada': 'grad_w', 'grad_g_pre_mix': 'grad_w', 'grad_g_post_mix': 'grad_w', 'grad_g_pre_ffn': 'grad_w', 'grad_g_post_ffn': 'grad_w', 'grad_w_in': 'grad_w', 'grad_b_fgate': 'grad_w', 'grad_conv_w': 'grad_w', 'grad_conv_b': 'grad_w', 'grad_conv_ln_g': 'grad_w', 'grad_conv_ln_b': 'grad_w', 'grad_conv_pw_w': 'grad_w', 'grad_conv_pw_b': 'grad_w', 'grad_swa_sinks': 'grad_w', 'grad_sgu_ln_g': 'grad_w', 'grad_sgu_ln_b': 'grad_w', 'grad_sgu_w': 'grad_w', 'grad_sgu_b': 'grad_w', 'grad_g_group': 'grad_w', 'grad_w_out': 'grad_w', 'grad_ffn_w_up': 'grad_w', 'grad_ffn_conv_w': 'grad_w', 'grad_ffn_conv_b': 'grad_w', 'grad_ffn_w_down': 'grad_w', 'delta_w_ada': 'delta_w', 'delta_b_ada': 'delta_w', 'delta_g_pre_mix': 'delta_w', 'delta_g_post_mix': 'delta_w', 'delta_g_pre_ffn': 'delta_w', 'delta_g_post_ffn': 'delta_w', 'delta_w_in': 'delta_w', 'delta_b_fgate': 'delta_w', 'delta_conv_w': 'delta_w', 'delta_conv_b': 'delta_w', 'delta_conv_ln_g': 'delta_w', 'delta_conv_ln_b': 'delta_w', 'delta_conv_pw_w': 'delta_w', 'delta_conv_pw_b': 'delta_w', 'delta_swa_sinks': 'delta_w', 'delta_sgu_ln_g': 'delta_w', 'delta_sgu_ln_b': 'delta_w', 'delta_sgu_w': 'delta_w', 'delta_sgu_b': 'delta_w', 'delta_g_group': 'delta_w', 'delta_w_out': 'delta_w', 'delta_ffn_w_up': 'delta_w', 'delta_ffn_conv_w': 'delta_w', 'delta_ffn_conv_b': 'delta_w', 'delta_ffn_w_down': 'delta_w', 'new_m_w_ada': 'new_m', 'new_m_b_ada': 'new_m', 'new_m_g_pre_mix': 'new_m', 'new_m_g_post_mix': 'new_m', 'new_m_g_pre_ffn': 'new_m', 'new_m_g_post_ffn': 'new_m', 'new_m_w_in': 'new_m', 'new_m_b_fgate': 'new_m', 'new_m_conv_w': 'new_m', 'new_m_conv_b': 'new_m', 'new_m_conv_ln_g': 'new_m', 'new_m_conv_ln_b': 'new_m', 'new_m_conv_pw_w': 'new_m', 'new_m_conv_pw_b': 'new_m', 'new_m_swa_sinks': 'new_m', 'new_m_sgu_ln_g': 'new_m', 'new_m_sgu_ln_b': 'new_m', 'new_m_sgu_w': 'new_m', 'new_m_sgu_b': 'new_m', 'new_m_g_group': 'new_m', 'new_m_w_out': 'new_m', 'new_m_ffn_w_up': 'new_m', 'new_m_ffn_conv_w': 'new_m', 'new_m_ffn_conv_b': 'new_m', 'new_m_ffn_w_down': 'new_m', 'new_v_w_ada': 'new_v', 'new_v_b_ada': 'new_v', 'new_v_g_pre_mix': 'new_v', 'new_v_g_post_mix': 'new_v', 'new_v_g_pre_ffn': 'new_v', 'new_v_g_post_ffn': 'new_v', 'new_v_w_in': 'new_v', 'new_v_b_fgate': 'new_v', 'new_v_conv_w': 'new_v', 'new_v_conv_b': 'new_v', 'new_v_conv_ln_g': 'new_v', 'new_v_conv_ln_b': 'new_v', 'new_v_conv_pw_w': 'new_v', 'new_v_conv_pw_b': 'new_v', 'new_v_swa_sinks': 'new_v', 'new_v_sgu_ln_g': 'new_v', 'new_v_sgu_ln_b': 'new_v', 'new_v_sgu_w': 'new_v', 'new_v_sgu_b': 'new_v', 'new_v_g_group': 'new_v', 'new_v_w_out': 'new_v', 'new_v_ffn_w_up': 'new_v', 'new_v_ffn_conv_w': 'new_v', 'new_v_ffn_conv_b': 'new_v', 'new_v_ffn_w_down': 'new_v'}


def _forward(args):
    return _fwd_reference(*[args[k] for k in FWD_PARAMS])


def _output_shape():
    out = _jax.eval_shape(lambda: _forward(_fwd_setup_inputs(0)))
    return out.shape, out.dtype

N_MICROBATCH = 1
ADAM_LR = 0.001
ADAM_B1 = 0.9
ADAM_B2 = 0.999
ADAM_EPS = 1e-08
ADAM_WD = 0.01
ADAM_STEP = 10
PER_EXAMPLE_BATCH_AXIS = {'x': 0, 'c': 0, 'loss_target': 0}
SHARED_INPUTS = []
_WEIGHT_DTYPES = {'w_ada': _jnp.float32, 'b_ada': _jnp.float32, 'g_pre_mix': _jnp.float32, 'g_post_mix': _jnp.float32, 'g_pre_ffn': _jnp.float32, 'g_post_ffn': _jnp.float32, 'w_in': _jnp.float32, 'b_fgate': _jnp.float32, 'conv_w': _jnp.float32, 'conv_b': _jnp.float32, 'conv_ln_g': _jnp.float32, 'conv_ln_b': _jnp.float32, 'conv_pw_w': _jnp.float32, 'conv_pw_b': _jnp.float32, 'swa_sinks': _jnp.float32, 'sgu_ln_g': _jnp.float32, 'sgu_ln_b': _jnp.float32, 'sgu_w': _jnp.float32, 'sgu_b': _jnp.float32, 'g_group': _jnp.float32, 'w_out': _jnp.float32, 'ffn_w_up': _jnp.float32, 'ffn_conv_w': _jnp.float32, 'ffn_conv_b': _jnp.float32, 'ffn_w_down': _jnp.float32}
MOMENT_SCALE = {'w_ada': 1.086150e+00, 'b_ada': 1.946875e+00, 'g_pre_mix': 1.708051e-01, 'g_post_mix': 2.042399e+00, 'g_pre_ffn': 1.117498e-01, 'g_post_ffn': 1.882574e+00, 'w_in': 3.214061e-01, 'b_fgate': 3.837349e-01, 'conv_w': 1.934671e-01, 'conv_b': 1.192519e+00, 'conv_ln_g': 5.322451e-01, 'conv_ln_b': 7.282178e-01, 'conv_pw_w': 3.577348e-01, 'conv_pw_b': 1.551712e+00, 'swa_sinks': 2.780119e-02, 'sgu_ln_g': 4.182666e-02, 'sgu_ln_b': 4.571662e-02, 'sgu_w': 2.775264e-02, 'sgu_b': 4.307835e-02, 'g_group': 5.217498e-01, 'w_out': 5.458475e-01, 'ffn_w_up': 6.520372e-02, 'ffn_conv_w': 7.623725e-02, 'ffn_conv_b': 1.677285e-01, 'ffn_w_down': 1.278538e-01}


def _to_microbatches(a, axis):
    t = _jnp.moveaxis(a, axis, 0)
    t = t.reshape((N_MICROBATCH, t.shape[0] // N_MICROBATCH) + t.shape[1:])
    return _jnp.moveaxis(t, 1, axis + 1)


def setup_inputs(seed: int = 0) -> dict:
    inp = _fwd_setup_inputs(seed)
    key = _jax.random.fold_in(_jax.random.key(seed), 7919)
    shape, _ = _output_shape()
    out = dict(inp)
    out["loss_target"] = _jax.random.normal(_jax.random.fold_in(key, 0), shape, _jnp.float32)
    for i, name in enumerate(TWIN_WEIGHTS):
        w = inp[name].astype(_jnp.float32)
        if MOMENT_SCALE is None:
            s = _jnp.sqrt(_jnp.mean(_jnp.square(w)) + 1e-30)
        else:
            s = MOMENT_SCALE[name]
        km, kv = _jax.random.split(_jax.random.fold_in(key, i + 1))
        out[name] = w
        out["m_" + name] = s * _jax.random.normal(km, w.shape, _jnp.float32)
        out["v_" + name] = (s * s) * _jax.random.uniform(kv, w.shape, _jnp.float32, 0.5, 1.5)
    if N_MICROBATCH > 1:
        for name, axis in PER_EXAMPLE_BATCH_AXIS.items():
            out[name] = _to_microbatches(out[name], axis)
    return {'x': out['x'], 'c': out['c'], 'w_ada': out['w_ada'], 'b_ada': out['b_ada'], 'g_pre_mix': out['g_pre_mix'], 'g_post_mix': out['g_post_mix'], 'g_pre_ffn': out['g_pre_ffn'], 'g_post_ffn': out['g_post_ffn'], 'w_in': out['w_in'], 'b_fgate': out['b_fgate'], 'conv_w': out['conv_w'], 'conv_b': out['conv_b'], 'conv_ln_g': out['conv_ln_g'], 'conv_ln_b': out['conv_ln_b'], 'conv_pw_w': out['conv_pw_w'], 'conv_pw_b': out['conv_pw_b'], 'swa_sinks': out['swa_sinks'], 'sgu_ln_g': out['sgu_ln_g'], 'sgu_ln_b': out['sgu_ln_b'], 'sgu_w': out['sgu_w'], 'sgu_b': out['sgu_b'], 'g_group': out['g_group'], 'w_out': out['w_out'], 'ffn_w_up': out['ffn_w_up'], 'ffn_conv_w': out['ffn_conv_w'], 'ffn_conv_b': out['ffn_conv_b'], 'ffn_w_down': out['ffn_w_down'], 'loss_target': out['loss_target'], 'm_w_ada': out['m_w_ada'], 'm_b_ada': out['m_b_ada'], 'm_g_pre_mix': out['m_g_pre_mix'], 'm_g_post_mix': out['m_g_post_mix'], 'm_g_pre_ffn': out['m_g_pre_ffn'], 'm_g_post_ffn': out['m_g_post_ffn'], 'm_w_in': out['m_w_in'], 'm_b_fgate': out['m_b_fgate'], 'm_conv_w': out['m_conv_w'], 'm_conv_b': out['m_conv_b'], 'm_conv_ln_g': out['m_conv_ln_g'], 'm_conv_ln_b': out['m_conv_ln_b'], 'm_conv_pw_w': out['m_conv_pw_w'], 'm_conv_pw_b': out['m_conv_pw_b'], 'm_swa_sinks': out['m_swa_sinks'], 'm_sgu_ln_g': out['m_sgu_ln_g'], 'm_sgu_ln_b': out['m_sgu_ln_b'], 'm_sgu_w': out['m_sgu_w'], 'm_sgu_b': out['m_sgu_b'], 'm_g_group': out['m_g_group'], 'm_w_out': out['m_w_out'], 'm_ffn_w_up': out['m_ffn_w_up'], 'm_ffn_conv_w': out['m_ffn_conv_w'], 'm_ffn_conv_b': out['m_ffn_conv_b'], 'm_ffn_w_down': out['m_ffn_w_down'], 'v_w_ada': out['v_w_ada'], 'v_b_ada': out['v_b_ada'], 'v_g_pre_mix': out['v_g_pre_mix'], 'v_g_post_mix': out['v_g_post_mix'], 'v_g_pre_ffn': out['v_g_pre_ffn'], 'v_g_post_ffn': out['v_g_post_ffn'], 'v_w_in': out['v_w_in'], 'v_b_fgate': out['v_b_fgate'], 'v_conv_w': out['v_conv_w'], 'v_conv_b': out['v_conv_b'], 'v_conv_ln_g': out['v_conv_ln_g'], 'v_conv_ln_b': out['v_conv_ln_b'], 'v_conv_pw_w': out['v_conv_pw_w'], 'v_conv_pw_b': out['v_conv_pw_b'], 'v_swa_sinks': out['v_swa_sinks'], 'v_sgu_ln_g': out['v_sgu_ln_g'], 'v_sgu_ln_b': out['v_sgu_ln_b'], 'v_sgu_w': out['v_sgu_w'], 'v_sgu_b': out['v_sgu_b'], 'v_g_group': out['v_g_group'], 'v_w_out': out['v_w_out'], 'v_ffn_w_up': out['v_ffn_w_up'], 'v_ffn_conv_w': out['v_ffn_conv_w'], 'v_ffn_conv_b': out['v_ffn_conv_b'], 'v_ffn_w_down': out['v_ffn_w_down']}


def _loss(weights, diff, rest, loss_target):
    with _jax.named_scope("forward"):
        args = {**rest, TWIN_DIFF_INPUT: diff, **{k: w.astype(_WEIGHT_DTYPES[k]) for k, w in weights.items()}}
        y = _forward(args)
    with _jax.named_scope("loss_head"):
        err = _jnp.square(y.astype(_jnp.float32) - loss_target)
        return 0.5 * _jnp.sum(_jnp.mean(err, axis=-1)) if err.ndim else 0.5 * err


def _adamw(w, g, m, v):
    m = ADAM_B1 * m + (1.0 - ADAM_B1) * g
    v = ADAM_B2 * v + (1.0 - ADAM_B2) * _jnp.square(g)
    m_hat = m / (1.0 - ADAM_B1 ** ADAM_STEP)
    v_hat = v / (1.0 - ADAM_B2 ** ADAM_STEP)
    delta = -ADAM_LR * (m_hat / (_jnp.sqrt(v_hat) + ADAM_EPS) + ADAM_WD * w)
    return delta, m, v


def reference(x, c, w_ada, b_ada, g_pre_mix, g_post_mix, g_pre_ffn, g_post_ffn, w_in, b_fgate, conv_w, conv_b, conv_ln_g, conv_ln_b, conv_pw_w, conv_pw_b, swa_sinks, sgu_ln_g, sgu_ln_b, sgu_w, sgu_b, g_group, w_out, ffn_w_up, ffn_conv_w, ffn_conv_b, ffn_w_down, loss_target, m_w_ada, m_b_ada, m_g_pre_mix, m_g_post_mix, m_g_pre_ffn, m_g_post_ffn, m_w_in, m_b_fgate, m_conv_w, m_conv_b, m_conv_ln_g, m_conv_ln_b, m_conv_pw_w, m_conv_pw_b, m_swa_sinks, m_sgu_ln_g, m_sgu_ln_b, m_sgu_w, m_sgu_b, m_g_group, m_w_out, m_ffn_w_up, m_ffn_conv_w, m_ffn_conv_b, m_ffn_w_down, v_w_ada, v_b_ada, v_g_pre_mix, v_g_post_mix, v_g_pre_ffn, v_g_post_ffn, v_w_in, v_b_fgate, v_conv_w, v_conv_b, v_conv_ln_g, v_conv_ln_b, v_conv_pw_w, v_conv_pw_b, v_swa_sinks, v_sgu_ln_g, v_sgu_ln_b, v_sgu_w, v_sgu_b, v_g_group, v_w_out, v_ffn_w_up, v_ffn_conv_w, v_ffn_conv_b, v_ffn_w_down):
    given = dict(x=x, c=c, w_ada=w_ada, b_ada=b_ada, g_pre_mix=g_pre_mix, g_post_mix=g_post_mix, g_pre_ffn=g_pre_ffn, g_post_ffn=g_post_ffn, w_in=w_in, b_fgate=b_fgate, conv_w=conv_w, conv_b=conv_b, conv_ln_g=conv_ln_g, conv_ln_b=conv_ln_b, conv_pw_w=conv_pw_w, conv_pw_b=conv_pw_b, swa_sinks=swa_sinks, sgu_ln_g=sgu_ln_g, sgu_ln_b=sgu_ln_b, sgu_w=sgu_w, sgu_b=sgu_b, g_group=g_group, w_out=w_out, ffn_w_up=ffn_w_up, ffn_conv_w=ffn_conv_w, ffn_conv_b=ffn_conv_b, ffn_w_down=ffn_w_down, loss_target=loss_target, m_w_ada=m_w_ada, m_b_ada=m_b_ada, m_g_pre_mix=m_g_pre_mix, m_g_post_mix=m_g_post_mix, m_g_pre_ffn=m_g_pre_ffn, m_g_post_ffn=m_g_post_ffn, m_w_in=m_w_in, m_b_fgate=m_b_fgate, m_conv_w=m_conv_w, m_conv_b=m_conv_b, m_conv_ln_g=m_conv_ln_g, m_conv_ln_b=m_conv_ln_b, m_conv_pw_w=m_conv_pw_w, m_conv_pw_b=m_conv_pw_b, m_swa_sinks=m_swa_sinks, m_sgu_ln_g=m_sgu_ln_g, m_sgu_ln_b=m_sgu_ln_b, m_sgu_w=m_sgu_w, m_sgu_b=m_sgu_b, m_g_group=m_g_group, m_w_out=m_w_out, m_ffn_w_up=m_ffn_w_up, m_ffn_conv_w=m_ffn_conv_w, m_ffn_conv_b=m_ffn_conv_b, m_ffn_w_down=m_ffn_w_down, v_w_ada=v_w_ada, v_b_ada=v_b_ada, v_g_pre_mix=v_g_pre_mix, v_g_post_mix=v_g_post_mix, v_g_pre_ffn=v_g_pre_ffn, v_g_post_ffn=v_g_post_ffn, v_w_in=v_w_in, v_b_fgate=v_b_fgate, v_conv_w=v_conv_w, v_conv_b=v_conv_b, v_conv_ln_g=v_conv_ln_g, v_conv_ln_b=v_conv_ln_b, v_conv_pw_w=v_conv_pw_w, v_conv_pw_b=v_conv_pw_b, v_swa_sinks=v_swa_sinks, v_sgu_ln_g=v_sgu_ln_g, v_sgu_ln_b=v_sgu_ln_b, v_sgu_w=v_sgu_w, v_sgu_b=v_sgu_b, v_g_group=v_g_group, v_w_out=v_w_out, v_ffn_w_up=v_ffn_w_up, v_ffn_conv_w=v_ffn_conv_w, v_ffn_conv_b=v_ffn_conv_b, v_ffn_w_down=v_ffn_w_down)
    weights = {n: given[n] for n in TWIN_WEIGHTS}
    shared = {n: given[n] for n in SHARED_INPUTS}
    per_example = {n: given[n] for n in ['x', 'c']}
    grad_fn = _jax.value_and_grad(_loss, argnums=(0, 1))

    def one_microbatch(ex, loss_target):
        ex = dict(ex)
        diff = ex.pop(TWIN_DIFF_INPUT)
        return grad_fn(weights, diff, {**shared, **ex}, loss_target)

    if N_MICROBATCH == 1:
        loss, (grad_w, grad_x) = one_microbatch(per_example, given["loss_target"])
    else:
        def body(carry, xs):
            loss_sum, grad_sum = carry
            l_k, (gw_k, gx_k) = one_microbatch(xs[0], xs[1])
            with _jax.named_scope("update"):
                return (loss_sum + l_k, _jax.tree.map(_jnp.add, grad_sum, gw_k)), gx_k

        init = (_jnp.zeros((), _jnp.float32), _jax.tree.map(_jnp.zeros_like, weights))
        (loss, grad_w), grad_x = _jax.lax.scan(body, init, (per_example, given["loss_target"]))
    with _jax.named_scope("update"):
        delta_w, new_m, new_v = {}, {}, {}
        for n in TWIN_WEIGHTS:
            delta_w[n], new_m[n], new_v[n] = _adamw(weights[n], grad_w[n], given["m_" + n], given["v_" + n])
    return (loss, grad_x, *[grad_w[n] for n in TWIN_WEIGHTS], *[delta_w[n] for n in TWIN_WEIGHTS],
            *[new_m[n] for n in TWIN_WEIGHTS], *[new_v[n] for n in TWIN_WEIGHTS])
```

```python
import functools

import jax
import jax.numpy as jnp
from jax import lax
from jax.experimental import pallas as pl
from jax.experimental.pallas import tpu as pltpu

F32 = jnp.float32
BF16 = jnp.bfloat16
SDS = jax.ShapeDtypeStruct
MESH = pl.DeviceIdType.MESH
ANY = pl.BlockSpec(memory_space=pl.ANY)

DEPTH = 2
S = 2048
D = 1024
GW = 256
DFF = 2816
NUP = 2 * DFF
IN_COLS = 2308
ZC = 2432
CONV_K = 31
FFN_K = 3
EPS = 1e-6
SCALE = 0.125
NEG = -1e30
N_CHIP = 4
N_DEV = 8

Z_FOX_Q, Z_FOX_K, Z_FOX_V = 0, 256, 512
Z_CONV_A, Z_CONV_G = 768, 1024
Z_SWA_Q, Z_SWA_K, Z_SWA_V = 1280, 1536, 1664
Z_SGU_U, Z_SGU_V = 1792, 2048
Z_FG = 2304

ADAM_LR, ADAM_B1, ADAM_B2, ADAM_EPS, ADAM_WD, ADAM_STEP = 0.001, 0.9, 0.999, 1e-08, 0.01, 10

TS = 256
TM = 512
N_SPLIT = 2816
N_BLOCK = 1408

WEIGHTS = ['w_ada', 'b_ada', 'g_pre_mix', 'g_post_mix', 'g_pre_ffn', 'g_post_ffn', 'w_in', 'b_fgate', 'conv_w',
           'conv_b', 'conv_ln_g', 'conv_ln_b', 'conv_pw_w', 'conv_pw_b', 'swa_sinks', 'sgu_ln_g', 'sgu_ln_b',
           'sgu_w', 'sgu_b', 'g_group', 'w_out', 'ffn_w_up', 'ffn_conv_w', 'ffn_conv_b', 'ffn_w_down']
SHARDED = ['w_in', 'conv_w', 'conv_pw_w', 'w_out', 'ffn_w_up', 'ffn_conv_w', 'ffn_w_down']
SMALL = [n for n in WEIGHTS if n not in SHARDED and n != 'w_ada']


def _rms(x, g):
    return x * lax.rsqrt(jnp.mean(x * x, axis=-1, keepdims=True) + EPS) * g


def _modnorm(x, g, sc, sh):
    return _rms(x, g) * (1.0 + sc) + sh


def _resid(x, y, ga, g):
    return x + ga * _rms(y, g)


@functools.partial(jax.custom_vjp, nondiff_argnums=(1,))
def _shift_down(x, n):
    if n == 0:
        return x
    row = lax.broadcasted_iota(jnp.int32, x.shape, 0)
    return jnp.where(row >= n, pltpu.roll(x, n, axis=0), 0.0)


def _shift_up(x, n):
    if n == 0:
        return x
    rows = x.shape[0]
    row = lax.broadcasted_iota(jnp.int32, x.shape, 0)
    return jnp.where(row < rows - n, pltpu.roll(x, rows - n, axis=0), 0.0)


def _shift_down_fwd(x, n):
    return _shift_down(x, n), None


def _shift_down_bwd(n, _, ct):
    return (_shift_up(ct, n),)


_shift_down.defvjp(_shift_down_fwd, _shift_down_bwd)


@jax.custom_vjp
def _swap_halves(x):
    return pltpu.roll(x, 64, axis=1)


def _swap_halves_fwd(x):
    return _swap_halves(x), None


def _swap_halves_bwd(_, ct):
    return (pltpu.roll(ct, 64, axis=1),)


_swap_halves.defvjp(_swap_halves_fwd, _swap_halves_bwd)


def _nt(a, b):
    return lax.dot_general(a, b, (((1,), (1,)), ((), ())), preferred_element_type=F32)


def _tn(a, b):
    return lax.dot_general(a, b, (((0,), (0,)), ((), ())), preferred_element_type=F32)


def _nn(a, b):
    return jnp.dot(a, b, preferred_element_type=F32)


def _acc(ref, val, first):
    @pl.when(first)
    def _():
        ref[...] = val

    @pl.when(jnp.logical_not(first))
    def _():
        ref[...] += val


def _row_tile(rows, cols):
    limit = max(8, (1 << 20) // (4 * cols))
    best = None
    for t in range(8, rows + 1, 8):
        if rows % t == 0 and t <= limit:
            best = t
    return best if best is not None else rows


def _ncol(n):
    return n if n <= N_SPLIT else N_BLOCK


def mm_nn(a, b, out_dtype, name):
    m, k = a.shape
    n = b.shape[1]
    tn = _ncol(n)

    def body(a_ref, b_ref, o_ref):
        o_ref[...] = _nn(a_ref[...], b_ref[...]).astype(out_dtype)

    return pl.pallas_call(
        body, grid=(n // tn, m // TM), name=name,
        in_specs=[pl.BlockSpec((TM, k), lambda j, i: (i, 0)), pl.BlockSpec((k, tn), lambda j, i: (0, j))],
        out_specs=pl.BlockSpec((TM, tn), lambda j, i: (i, j)),
        out_shape=SDS((m, n), out_dtype),
    )(a, b)


def mm_nt(a, b, name):
    m, n = a.shape
    k = b.shape[0]
    tc = _ncol(n)

    def body(a_ref, b_ref, o_ref):
        _acc(o_ref, _nt(a_ref[...], b_ref[...]), pl.program_id(1) == 0)

    return pl.pallas_call(
        body, grid=(m // TM, n // tc), name=name,
        in_specs=[pl.BlockSpec((TM, tc), lambda i, c: (i, c)), pl.BlockSpec((k, tc), lambda i, c: (0, c))],
        out_specs=pl.BlockSpec((TM, k), lambda i, c: (i, 0)),
        out_shape=SDS((m, k), F32),
    )(a, b)


def mm_tn(a, b, name):
    m, k = a.shape
    n = b.shape[1]
    tn = _ncol(n)
    steps = m // TM

    def body(a_ref, b_ref, o_ref, acc_ref):
        i = pl.program_id(1)
        _acc(acc_ref, _tn(a_ref[...], b_ref[...]), i == 0)

        @pl.when(i == steps - 1)
        def _():
            o_ref[...] = acc_ref[...].astype(BF16)

    return pl.pallas_call(
        body, grid=(n // tn, steps), name=name,
        in_specs=[pl.BlockSpec((TM, k), lambda j, i: (i, 0)), pl.BlockSpec((TM, tn), lambda j, i: (i, j))],
        out_specs=pl.BlockSpec((k, tn), lambda j, i: (0, j)),
        out_shape=SDS((k, n), BF16),
        scratch_shapes=[pltpu.VMEM((k, tn), F32)],
    )(a, b)


_ROW = pl.BlockSpec((TS, D), lambda i: (i, 0))
_VEC = pl.BlockSpec((1, D), lambda i: (0, 0))


def modnorm_fwd(x, g, sc, sh, name):
    def body(x_ref, g_ref, sc_ref, sh_ref, o_ref):
        o_ref[...] = _modnorm(x_ref[...], g_ref[...], sc_ref[...], sh_ref[...]).astype(BF16)

    return pl.pallas_call(body, grid=(S // TS,), name=name, in_specs=[_ROW, _VEC, _VEC, _VEC], out_specs=_ROW,
                          out_shape=SDS((S, D), BF16))(x, g, sc, sh)


def modnorm_bwd(x, g, sc, sh, dh, dx_in, name):
    def body(x_ref, g_ref, sc_ref, sh_ref, dh_ref, dxin_ref, dx_ref, dg_ref, dsc_ref, dsh_ref):
        _, vjp = jax.vjp(_modnorm, x_ref[...], g_ref[...], sc_ref[...], sh_ref[...])
        dx, dg, dsc, dsh = vjp(dh_ref[...])
        dx_ref[...] = dxin_ref[...] + dx
        first = pl.program_id(0) == 0
        _acc(dg_ref, dg, first)
        _acc(dsc_ref, dsc, first)
        _acc(dsh_ref, dsh, first)

    vec = SDS((1, D), F32)
    return pl.pallas_call(body, grid=(S // TS,), name=name, in_specs=[_ROW, _VEC, _VEC, _VEC, _ROW, _ROW],
                          out_specs=[_ROW, _VEC, _VEC, _VEC], out_shape=[SDS((S, D), F32), vec, vec, vec])(
                              x, g, sc, sh, dh, dx_in)


def resid_fwd(x, y, ga, g, name):
    def body(x_ref, y_ref, ga_ref, g_ref, o_ref):
        o_ref[...] = _resid(x_ref[...], y_ref[...], ga_ref[...], g_ref[...])

    return pl.pallas_call(body, grid=(S // TS,), name=name, in_specs=[_ROW, _ROW, _VEC, _VEC], out_specs=_ROW,
                          out_shape=SDS((S, D), F32))(x, y, ga, g)


def resid_bwd(y, ga, g, dxo, name):
    def body(y_ref, ga_ref, g_ref, dxo_ref, dy_ref, dga_ref, dg_ref):
        _, vjp = jax.vjp(lambda y, ga, g: ga * _rms(y, g), y_ref[...], ga_ref[...], g_ref[...])
        dy, dga, dg = vjp(dxo_ref[...])
        dy_ref[...] = dy.astype(BF16)
        first = pl.program_id(0) == 0
        _acc(dga_ref, dga, first)
        _acc(dg_ref, dg, first)

    vec = SDS((1, D), F32)
    return pl.pallas_call(body, grid=(S // TS,), name=name, in_specs=[_ROW, _VEC, _VEC, _ROW],
                          out_specs=[_ROW, _VEC, _VEC], out_shape=[SDS((S, D), BF16), vec, vec])(y, ga, g, dxo)


def loss_grad(xf, target, name):
    def body(x_ref, t_ref, dx_ref, l_ref):
        err = x_ref[...] - t_ref[...]
        dx_ref[...] = err * (1.0 / D)
        part = 0.5 * jnp.sum(jnp.mean(err * err, axis=-1, keepdims=True), axis=0, keepdims=True)
        _acc(l_ref, jnp.broadcast_to(part, (1, 128)), pl.program_id(0) == 0)

    return pl.pallas_call(body, grid=(S // TS,), name=name, in_specs=[_ROW, _ROW],
                          out_specs=[_ROW, pl.BlockSpec((1, 128), lambda i: (0, 0))],
                          out_shape=[SDS((S, D), F32), SDS((1, 128), F32)])(xf, target)


_FG_SPEC = pl.BlockSpec((S, 128), lambda i: (0, Z_FG // 128))


def _tri128(lower):
    r = lax.broadcasted_iota(jnp.int32, (128, 128), 0)
    c = lax.broadcasted_iota(jnp.int32, (128, 128), 1)
    return ((r >= c) if lower else (r <= c)).astype(F32)


def fgate_fwd(z, bf, name):
    def body(z_ref, b_ref, cc_ref, cr_ref):
        tri = _tri128(True)
        carry = jnp.zeros((1, 128), F32)
        for i in range(S // 128):
            rows = pl.ds(i * 128, 128)
            lf = jax.nn.log_sigmoid(z_ref[rows, :] + b_ref[...])
            c = jnp.dot(tri, lf, precision=lax.Precision.HIGHEST, preferred_element_type=F32) + carry
            cc_ref[rows, :] = c
            carry = c[127:128, :]
        cr_ref[...] = cc_ref[...].T

    return pl.pallas_call(body, name=name, grid=(1,),
                          in_specs=[_FG_SPEC, pl.BlockSpec((1, 128), lambda i: (0, 0))],
                          out_specs=[pl.BlockSpec((S, 128), lambda i: (0, 0)), pl.BlockSpec((128, S), lambda i: (0, 0))],
                          out_shape=[SDS((S, 128), F32), SDS((128, S), F32)])(z, bf)


def fgate_bwd(z, bf, dcq, dck, name):
    def body(z_ref, b_ref, dcq_ref, dck_ref, dz_ref, db_ref, col_ref):
        col_ref[...] = dcq_ref[...] + jnp.concatenate([dck_ref[...], jnp.zeros((120, S), F32)], axis=0).T
        tri = _tri128(False)
        carry = jnp.zeros((1, 128), F32)
        db = jnp.zeros((1, 128), F32)
        for i in reversed(range(S // 128)):
            rows = pl.ds(i * 128, 128)
            dlf = jnp.dot(tri, col_ref[rows, :], precision=lax.Precision.HIGHEST, preferred_element_type=F32) + carry
            carry = dlf[0:1, :]
            dz = dlf * jax.nn.sigmoid(-(z_ref[rows, :] + b_ref[...]))
            dz_ref[rows, :] = dz.astype(BF16)
            db = db + jnp.sum(dz, axis=0, keepdims=True)
        db_ref[...] = db

    return pl.pallas_call(body, name=name, grid=(1,),
                          in_specs=[_FG_SPEC, pl.BlockSpec((1, 128), lambda i: (0, 0)),
                                    pl.BlockSpec((S, 128), lambda i: (0, 0)), pl.BlockSpec((8, S), lambda i: (0, 0))],
                          out_specs=[pl.BlockSpec((S, 128), lambda i: (0, 0)), pl.BlockSpec((1, 128), lambda i: (0, 0))],
                          out_shape=[SDS((S, 128), BF16), SDS((1, 128), F32)],
                          scratch_shapes=[pltpu.VMEM((S, 128), F32)])(z, bf, dcq, dck)


TQ = 256


def _head_mask(hh):
    lane = lax.broadcasted_iota(jnp.int32, (TQ, 128), 1)
    return (lane >= 64 * hh) & (lane < 64 * hh + 64)


def _fox_specs():
    q = pl.BlockSpec((TQ, 256), lambda i: (i, Z_FOX_Q // 256))
    k = pl.BlockSpec((S, 256), lambda i: (0, Z_FOX_K // 256))
    v = pl.BlockSpec((S, 256), lambda i: (0, Z_FOX_V // 256))
    cc = pl.BlockSpec((TQ, 128), lambda i: (i, 0))
    cr = pl.BlockSpec((8, S), lambda i: (0, 0))
    return q, k, v, cc, cr


def _fox_scores(qm, k, cc_h, cr_h, causal):
    s = _nt(qm, k) * SCALE + cc_h - cr_h
    return jnp.where(causal, s, NEG)


def _causal(i):
    qpos = i * TQ + lax.broadcasted_iota(jnp.int32, (TQ, S), 0)
    kpos = lax.broadcasted_iota(jnp.int32, (TQ, S), 1)
    return kpos <= qpos


def fox_fwd(z, cumc, cumr, name):
    def body(q_ref, k_ref, v_ref, cc_ref, cr_ref, o_ref, l_ref):
        causal = _causal(pl.program_id(0))
        lane = lax.broadcasted_iota(jnp.int32, (TQ, 128), 1)
        cc = cc_ref[...]
        lse = jnp.zeros((TQ, 128), F32)
        for p in range(2):
            cols = pl.ds(128 * p, 128)
            q = q_ref[:, cols]
            k = k_ref[:, cols].astype(BF16)
            v = v_ref[:, cols].astype(BF16)
            o_pair = jnp.zeros((TQ, 128), F32)
            for hh in range(2):
                h = 2 * p + hh
                hm = _head_mask(hh)
                qm = jnp.where(hm, q, 0.0).astype(BF16)
                s = _fox_scores(qm, k, cc[:, h:h + 1], cr_ref[h:h + 1, :], causal)
                m = jnp.max(s, axis=1, keepdims=True)
                e = jnp.exp(s - m)
                l = jnp.sum(e, axis=1, keepdims=True)
                o_pair = jnp.where(hm, _nn(e.astype(BF16), v) / l, o_pair)
                lse = jnp.where(lane == h, m + jnp.log(l), lse)
            o_ref[:, cols] = o_pair
        l_ref[...] = lse

    q, k, v, cc, cr = _fox_specs()
    return pl.pallas_call(body, grid=(S // TQ,), name=name, in_specs=[q, k, v, cc, cr],
                          out_specs=[pl.BlockSpec((TQ, 256), lambda i: (i, 0)), cc],
                          out_shape=[SDS((S, 256), F32), SDS((S, 128), F32)])(z, z, z, cumc, cumr)


def fox_bwd(z, cumc, cumr, lse, o, do, name):
    steps = S // TQ

    def body(q_ref, k_ref, v_ref, cc_ref, cr_ref, l_ref, o_ref, do_ref, dq_ref, dk_ref, dv_ref, dcq_ref, dck_ref,
             dk_acc, dv_acc):
        i = pl.program_id(0)
        causal = _causal(i)
        row8 = lax.broadcasted_iota(jnp.int32, (8, S), 0)
        lane = lax.broadcasted_iota(jnp.int32, (TQ, 128), 1)
        cc = cc_ref[...]
        lse_all = l_ref[...]
        dck = jnp.zeros((8, S), F32)
        dcq = jnp.zeros((TQ, 128), F32)

        @pl.when(i == 0)
        def _():
            dk_acc[...] = jnp.zeros_like(dk_acc)
            dv_acc[...] = jnp.zeros_like(dv_acc)

        for p in range(2):
            cols = pl.ds(128 * p, 128)
            q = q_ref[:, cols]
            k = k_ref[:, cols].astype(BF16)
            v = v_ref[:, cols].astype(BF16)
            o_p = o_ref[:, cols]
            do_p = do_ref[:, cols]
            dq_pair = jnp.zeros((TQ, 128), F32)
            dk_pair = jnp.zeros((S, 128), F32)
            dv_pair = jnp.zeros((S, 128), F32)
            for hh in range(2):
                h = 2 * p + hh
                hm = _head_mask(hh)
                qm = jnp.where(hm, q, 0.0).astype(BF16)
                s = _fox_scores(qm, k, cc[:, h:h + 1], cr_ref[h:h + 1, :], causal)
                pn = jnp.exp(s - lse_all[:, h:h + 1])
                dom = jnp.where(hm, do_p, 0.0)
                dl = jnp.sum(dom * o_p, axis=1, keepdims=True)
                dom = dom.astype(BF16)
                ds = pn * (_nt(dom, v) - dl)
                dsb = ds.astype(BF16)
                dq_pair = jnp.where(hm, _nn(dsb, k) * SCALE, dq_pair)
                dk_pair = dk_pair + _tn(dsb, qm) * SCALE
                dv_pair = dv_pair + _tn(pn.astype(BF16), dom)
                dck = jnp.where(row8 == h, -jnp.sum(ds, axis=0, keepdims=True), dck)
                dcq = jnp.where(lane == h, jnp.sum(ds, axis=1, keepdims=True), dcq)
            dq_ref[:, cols] = dq_pair.astype(BF16)
            dk_acc[:, cols] += dk_pair
            dv_acc[:, cols] += dv_pair
        dcq_ref[...] = dcq
        _acc(dck_ref, dck, i == 0)

        @pl.when(i == steps - 1)
        def _():
            dk_ref[...] = dk_acc[...].astype(BF16)
            dv_ref[...] = dv_acc[...].astype(BF16)

    q, k, v, cc, cr = _fox_specs()
    blk = pl.BlockSpec((TQ, 256), lambda i: (i, 0))
    full = pl.BlockSpec((S, 256), lambda i: (0, 0))
    return pl.pallas_call(body, grid=(steps,), name=name, in_specs=[q, k, v, cc, cr, cc, blk, blk],
                          out_specs=[blk, full, full, cc, cr],
                          out_shape=[SDS((S, 256), BF16), SDS((S, 256), BF16), SDS((S, 256), BF16), SDS((S, 128), F32),
                                     SDS((8, S), F32)],
                          scratch_shapes=[pltpu.VMEM((S, 256), F32), pltpu.VMEM((S, 256), F32)])(
                              z, z, z, cumc, cumr, lse, o, do)


W = 128


def _swa_block(first, q0, q1, kp, kc, vp, vc, s0, s1, s2, s3):
    qi = lax.broadcasted_iota(jnp.int32, (W, W), 0)
    kj = lax.broadcasted_iota(jnp.int32, (W, W), 1)
    lane = lax.broadcasted_iota(jnp.int32, (W, 128), 1)
    mask_prev = (kj > qi) & jnp.logical_not(first)
    mask_cur = kj <= qi
    kpb, kcb, vpb, vcb = (a.astype(BF16) for a in (kp, kc, vp, vc))
    sinks = (s0, s1, s2, s3)
    outs = []
    for kv, qpair in enumerate((q0, q1)):
        kvm = (lane >= 64 * kv) & (lane < 64 * kv + 64)
        o_pair = jnp.zeros((W, 128), F32)
        for hh in range(2):
            h = 2 * kv + hh
            qa = qpair if hh == kv else _swap_halves(qpair)
            qm = jnp.where(kvm, qa, 0.0).astype(BF16)
            sp = jnp.where(mask_prev, _nt(qm, kpb) * SCALE, NEG)
            sc = jnp.where(mask_cur, _nt(qm, kcb) * SCALE, NEG)
            m = jnp.maximum(jnp.maximum(jnp.max(sp, axis=1, keepdims=True), jnp.max(sc, axis=1, keepdims=True)),
                            sinks[h])
            m = lax.stop_gradient(m)
            ep = jnp.exp(sp - m)
            ec = jnp.exp(sc - m)
            den = jnp.sum(ep, axis=1, keepdims=True) + jnp.sum(ec, axis=1, keepdims=True) + jnp.exp(sinks[h] - m)
            o = _nn((ep / den).astype(BF16), vpb) + _nn((ec / den).astype(BF16), vcb)
            o = o if hh == kv else _swap_halves(o)
            o_pair = jnp.where((lane >= 64 * hh) & (lane < 64 * hh + 64), o, o_pair)
        outs.append(o_pair)
    return outs[0], outs[1]


def _swa_specs():
    q = pl.BlockSpec((W, 256), lambda n: (n, Z_SWA_Q // 256))
    kc = pl.BlockSpec((W, 128), lambda n: (n, Z_SWA_K // 128))
    kp = pl.BlockSpec((W, 128), lambda n: (jnp.maximum(n - 1, 0), Z_SWA_K // 128))
    vc = pl.BlockSpec((W, 128), lambda n: (n, Z_SWA_V // 128))
    vp = pl.BlockSpec((W, 128), lambda n: (jnp.maximum(n - 1, 0), Z_SWA_V // 128))
    sk = pl.BlockSpec((1, 128), lambda n: (0, 0))
    return q, kp, kc, vp, vc, sk


def _swa_args(q_ref, kp_ref, kc_ref, vp_ref, vc_ref, sk_ref):
    return (q_ref[:, 0:128], q_ref[:, 128:256], kp_ref[...], kc_ref[...], vp_ref[...], vc_ref[...],
            sk_ref[:, 0:1], sk_ref[:, 1:2], sk_ref[:, 2:3], sk_ref[:, 3:4])


def swa_fwd(z, sinks, name):
    def body(q_ref, kp_ref, kc_ref, vp_ref, vc_ref, sk_ref, o_ref):
        o0, o1 = _swa_block(pl.program_id(0) == 0, *_swa_args(q_ref, kp_ref, kc_ref, vp_ref, vc_ref, sk_ref))
        o_ref[:, 0:128] = o0
        o_ref[:, 128:256] = o1

    return pl.pallas_call(body, grid=(S // W,), name=name, in_specs=list(_swa_specs()),
                          out_specs=pl.BlockSpec((W, 256), lambda n: (n, 0)),
                          out_shape=SDS((S, 256), F32))(z, z, z, z, z, sinks)


def swa_bwd(z, sinks, do, name):
    steps = S // W

    def body(q_ref, kp_ref, kc_ref, vp_ref, vc_ref, sk_ref, do_ref, dq_ref, dk_ref, dv_ref, dsk_ref, dk_acc, dv_acc):
        n = pl.program_id(0)
        first = n == 0
        _, vjp = jax.vjp(functools.partial(_swa_block, first), *_swa_args(q_ref, kp_ref, kc_ref, vp_ref, vc_ref, sk_ref))
        dq0, dq1, dkp, dkc, dvp, dvc, d0, d1, d2, d3 = vjp((do_ref[:, 0:128], do_ref[:, 128:256]))
        dq_ref[:, 0:128] = dq0.astype(BF16)
        dq_ref[:, 128:256] = dq1.astype(BF16)

        @pl.when(first)
        def _():
            dk_acc[...] = jnp.zeros_like(dk_acc)
            dv_acc[...] = jnp.zeros_like(dv_acc)

        cur = pl.ds(pl.multiple_of(n * W, W), W)
        dk_acc[cur, :] += dkc
        dv_acc[cur, :] += dvc

        @pl.when(n > 0)
        def _():
            prev = pl.ds(pl.multiple_of((n - 1) * W, W), W)
            dk_acc[prev, :] += dkp
            dv_acc[prev, :] += dvp

        lane = lax.broadcasted_iota(jnp.int32, (1, 128), 1)
        dsk = jnp.zeros((1, 128), F32)
        for h, d in enumerate((d0, d1, d2, d3)):
            dsk = jnp.where(lane == h, d, dsk)
        _acc(dsk_ref, dsk, first)

        @pl.when(n == steps - 1)
        def _():
            dk_ref[...] = dk_acc[...].astype(BF16)
            dv_ref[...] = dv_acc[...].astype(BF16)

    blk = pl.BlockSpec((W, 256), lambda n: (n, 0))
    full = pl.BlockSpec((S, 128), lambda n: (0, 0))
    return pl.pallas_call(body, grid=(steps,), name=name, in_specs=list(_swa_specs()) + [blk],
                          out_specs=[blk, full, full, pl.BlockSpec((1, 128), lambda n: (0, 0))],
                          out_shape=[SDS((S, 256), BF16), SDS((S, 128), BF16), SDS((S, 128), BF16), SDS((1, 128), F32)],
                          scratch_shapes=[pltpu.VMEM((S, 128), F32), pltpu.VMEM((S, 128), F32)])(
                              z, z, z, z, z, sinks, do)


def _glu(a, g):
    return a * jax.nn.sigmoid(g)


def _cv1_specs():
    a = pl.BlockSpec((S, 128), lambda j: (0, Z_CONV_A // 128 + j))
    g = pl.BlockSpec((S, 128), lambda j: (0, Z_CONV_G // 128 + j))
    w = pl.BlockSpec((32, 128), lambda j: (0, j))
    b = pl.BlockSpec((1, 128), lambda j: (0, j))
    h = pl.BlockSpec((S, 128), lambda j: (0, j))
    return a, g, w, b, h


def conv_dw_fwd(z, cw, cb, name):
    def body(a_ref, g_ref, w_ref, b_ref, o_ref):
        hh = _glu(a_ref[...], g_ref[...])
        acc = jnp.zeros((S, 128), F32) + b_ref[...]
        for k in range(CONV_K):
            acc = acc + _shift_down(hh, CONV_K - 1 - k) * w_ref[k:k + 1, :]
        o_ref[...] = acc

    a, g, w, b, h = _cv1_specs()
    return pl.pallas_call(body, grid=(2,), name=name, in_specs=[a, g, w, b], out_specs=h,
                          out_shape=SDS((S, 256), F32))(z, z, cw, cb)


def conv_dw_bwd(z, cw, dhc, name):
    def body(a_ref, g_ref, w_ref, dh_ref, da_ref, dg_ref, dw_ref, db_ref):
        hh, vjp = jax.vjp(_glu, a_ref[...], g_ref[...])
        dh = dh_ref[...]
        dhh = jnp.zeros((S, 128), F32)
        for k in range(CONV_K):
            n = CONV_K - 1 - k
            dhh = dhh + _shift_up(dh, n) * w_ref[k:k + 1, :]
            dw_ref[k:k + 1, :] = jnp.sum(dh * _shift_down(hh, n), axis=0, keepdims=True)
        dw_ref[CONV_K:32, :] = jnp.zeros((32 - CONV_K, 128), F32)
        db_ref[...] = jnp.sum(dh, axis=0, keepdims=True)
        da, dg = vjp(dhh)
        da_ref[...] = da.astype(BF16)
        dg_ref[...] = dg.astype(BF16)

    a, g, w, b, h = _cv1_specs()
    return pl.pallas_call(body, grid=(2,), name=name, in_specs=[a, g, w, h], out_specs=[h, h, w, b],
                          out_shape=[SDS((S, 256), BF16), SDS((S, 256), BF16), SDS((32, 256), F32), SDS((1, 256), F32)])(
                              z, z, cw, dhc)


def _ln(x, g, b):
    mu = jnp.mean(x, axis=-1, keepdims=True)
    xc = x - mu
    var = jnp.mean(xc * xc, axis=-1, keepdims=True)
    return xc * lax.rsqrt(var + EPS) * g + b


def _conv_pw(hc, lg, lb, pw, pb):
    y = jax.nn.silu(_ln(hc, lg, lb))
    return _nn(y.astype(BF16), pw.astype(BF16)) + pb


TS2 = 512
_ROW2 = pl.BlockSpec((TS2, 256), lambda i: (i, 0))
_VEC2 = pl.BlockSpec((1, 256), lambda i: (0, 0))
_MAT2 = pl.BlockSpec((256, 256), lambda i: (0, 0))


def conv_pw_fwd(hc, lg, lb, pw, pb, name):
    def body(h_ref, lg_ref, lb_ref, pw_ref, pb_ref, o_ref):
        o_ref[...] = _conv_pw(h_ref[...], lg_ref[...], lb_ref[...], pw_ref[...], pb_ref[...])

    return pl.pallas_call(body, grid=(S // TS2,), name=name, in_specs=[_ROW2, _VEC2, _VEC2, _MAT2, _VEC2],
                          out_specs=_ROW2, out_shape=SDS((S, 256), F32))(hc, lg, lb, pw, pb)


def conv_pw_bwd(hc, lg, lb, pw, pb, dy, name):
    def body(h_ref, lg_ref, lb_ref, pw_ref, pb_ref, dy_ref, dh_ref, dlg_ref, dlb_ref, dpw_ref, dpb_ref):
        _, vjp = jax.vjp(_conv_pw, h_ref[...], lg_ref[...], lb_ref[...], pw_ref[...], pb_ref[...])
        dh, dlg, dlb, dpw, dpb = vjp(dy_ref[...])
        dh_ref[...] = dh
        first = pl.program_id(0) == 0
        _acc(dlg_ref, dlg, first)
        _acc(dlb_ref, dlb, first)
        _acc(dpw_ref, dpw, first)
        _acc(dpb_ref, dpb, first)

    vec = SDS((1, 256), F32)
    return pl.pallas_call(body, grid=(S // TS2,), name=name, in_specs=[_ROW2, _VEC2, _VEC2, _MAT2, _VEC2, _ROW2],
                          out_specs=[_ROW2, _VEC2, _VEC2, _MAT2, _VEC2],
                          out_shape=[SDS((S, 256), F32), vec, vec, SDS((256, 256), F32), vec])(hc, lg, lb, pw, pb, dy)


def _sgu_block(u0, u1, v0, v1, lg0, lg1, lb0, lb1, w0, w1, w2, w3, bt):
    u0, u1, v0, v1 = (jax.nn.gelu(a) for a in (u0, u1, v0, v1))
    mu = (jnp.sum(v0, axis=1, keepdims=True) + jnp.sum(v1, axis=1, keepdims=True)) * (1.0 / GW)
    c0, c1 = v0 - mu, v1 - mu
    var = (jnp.sum(c0 * c0, axis=1, keepdims=True) + jnp.sum(c1 * c1, axis=1, keepdims=True)) * (1.0 / GW)
    r = lax.rsqrt(var + EPS)
    n0 = c0 * r * lg0 + lb0
    n1 = c1 * r * lg1 + lb1
    row = lax.broadcasted_iota(jnp.int32, (128, 128), 0)
    col = lax.broadcasted_iota(jnp.int32, (128, 128), 1)
    tri = row >= col
    outs = []
    for p, (n, u, wa, wb) in enumerate(((n0, u0, w0, w1), (n1, u1, w2, w3))):
        nb = n.astype(BF16)
        ma = _nn(jnp.where(tri, wa, 0.0).astype(BF16), nb)
        mb = _nn(jnp.where(tri, wb, 0.0).astype(BF16), nb)
        expand = (row == 2 * p + col // 64).astype(F32)
        bias = jnp.dot(bt, expand, precision=lax.Precision.HIGHEST, preferred_element_type=F32)
        outs.append(u * (jnp.where(col < 64, ma, mb) + bias))
    return outs[0], outs[1]


def _sgu_specs():
    def col(c):
        return pl.BlockSpec((128, 128), lambda n, c=c: (n, c))
    zs = [col(Z_SGU_U // 128), col(Z_SGU_U // 128 + 1), col(Z_SGU_V // 128), col(Z_SGU_V // 128 + 1)]
    vec = [pl.BlockSpec((1, 128), lambda n: (0, 0)), pl.BlockSpec((1, 128), lambda n: (0, 1))]
    ws = [pl.BlockSpec((None, 128, 128), lambda n, g=g: (g, 0, 0)) for g in range(4)]
    bt = pl.BlockSpec((128, 128), lambda n: (0, 0))
    return zs + vec + vec + ws + [bt]


def sgu_fwd(z, lg, lb, w, bt, name):
    def body(*refs):
        o_ref = refs[-1]
        y0, y1 = _sgu_block(*[r[...] for r in refs[:-1]])
        o_ref[:, 0:128] = y0
        o_ref[:, 128:256] = y1

    return pl.pallas_call(body, grid=(S // 128,), name=name, in_specs=_sgu_specs(),
                          out_specs=pl.BlockSpec((128, 256), lambda n: (n, 0)),
                          out_shape=SDS((S, 256), F32))(z, z, z, z, lg, lg, lb, lb, w, w, w, w, bt)


def sgu_bwd(z, lg, lb, w, bt, dy, name):
    def body(*refs):
        ins, dy_ref = refs[:13], refs[13]
        du_ref, dv_ref, dlg_ref, dlb_ref, dw_ref, dbt_ref = refs[14:]
        _, vjp = jax.vjp(_sgu_block, *[r[...] for r in ins])
        du0, du1, dv0, dv1, dlg0, dlg1, dlb0, dlb1, dw0, dw1, dw2, dw3, dbt = vjp((dy_ref[:, 0:128], dy_ref[:, 128:256]))
        du_ref[:, 0:128] = du0.astype(BF16)
        du_ref[:, 128:256] = du1.astype(BF16)
        dv_ref[:, 0:128] = dv0.astype(BF16)
        dv_ref[:, 128:256] = dv1.astype(BF16)
        first = pl.program_id(0) == 0

        @pl.when(first)
        def _():
            dlg_ref[...] = jnp.zeros_like(dlg_ref)
            dlb_ref[...] = jnp.zeros_like(dlb_ref)
            dw_ref[...] = jnp.zeros_like(dw_ref)
            dbt_ref[...] = jnp.zeros_like(dbt_ref)

        dlg_ref[:, 0:128] += dlg0
        dlg_ref[:, 128:256] += dlg1
        dlb_ref[:, 0:128] += dlb0
        dlb_ref[:, 128:256] += dlb1
        for g, d in enumerate((dw0, dw1, dw2, dw3)):
            dw_ref[g] += d
        dbt_ref[...] += dbt

    blk = pl.BlockSpec((128, 256), lambda n: (n, 0))
    vec = pl.BlockSpec((1, 256), lambda n: (0, 0))
    return pl.pallas_call(body, grid=(S // 128,), name=name, in_specs=_sgu_specs() + [blk],
                          out_specs=[blk, blk, vec, vec, pl.BlockSpec((4, 128, 128), lambda n: (0, 0, 0)),
                                     pl.BlockSpec((128, 128), lambda n: (0, 0))],
                          out_shape=[SDS((S, 256), BF16), SDS((S, 256), BF16), SDS((1, 256), F32), SDS((1, 256), F32),
                                     SDS((4, 128, 128), F32), SDS((128, 128), F32)])(
                                         z, z, z, z, lg, lg, lb, lb, w, w, w, w, bt, dy)


def _group_norm(y0, y1, y2, y3, g0, g1, g2, g3):
    return tuple(_rms(y, g) for y, g in zip((y0, y1, y2, y3), (g0, g1, g2, g3)))


_GROW = pl.BlockSpec((TS2, 256), lambda i: (i, 0))
_GCAT = pl.BlockSpec((TS2, D), lambda i: (i, 0))
_GVEC = [pl.BlockSpec((1, 256), lambda i, j=j: (0, j)) for j in range(4)]


def group_norm_fwd(ys, gg, name):
    def body(*refs):
        o_ref = refs[-1]
        outs = _group_norm(*[r[...] for r in refs[:-1]])
        for j, c in enumerate(outs):
            o_ref[:, 256 * j:256 * (j + 1)] = c.astype(BF16)

    return pl.pallas_call(body, grid=(S // TS2,), name=name, in_specs=[_GROW] * 4 + _GVEC, out_specs=_GCAT,
                          out_shape=SDS((S, D), BF16))(*ys, gg, gg, gg, gg)


def group_norm_bwd(ys, gg, dcat, name):
    def body(*refs):
        ins, dc_ref = refs[:8], refs[8]
        dy_refs, dg_ref = refs[9:13], refs[13]
        _, vjp = jax.vjp(_group_norm, *[r[...] for r in ins])
        grads = vjp(tuple(dc_ref[:, 256 * j:256 * (j + 1)] for j in range(4)))
        first = pl.program_id(0) == 0

        @pl.when(first)
        def _():
            dg_ref[...] = jnp.zeros_like(dg_ref)

        for j in range(4):
            dy_refs[j][...] = grads[j]
            dg_ref[:, 256 * j:256 * (j + 1)] += grads[4 + j]

    return pl.pallas_call(body, grid=(S // TS2,), name=name, in_specs=[_GROW] * 4 + _GVEC + [_GCAT],
                          out_specs=[_GROW] * 4 + [pl.BlockSpec((1, D), lambda i: (0, 0))],
                          out_shape=[SDS((S, 256), F32)] * 4 + [SDS((1, D), F32)])(*ys, gg, gg, gg, gg, dcat)


FB = 256
N_FB = DFF // FB


def _ffn_gate(ug, uv, wg0, wg1, wg2, wv0, wv1, wv2, bg, bv):
    cg = bg + _shift_down(ug, 2) * wg0 + _shift_down(ug, 1) * wg1 + ug * wg2
    cv = bv + _shift_down(uv, 2) * wv0 + _shift_down(uv, 1) * wv1 + uv * wv2
    return jax.nn.silu(cg) * cv


def _gate_specs():
    ug = pl.BlockSpec((S, FB), lambda j: (0, j))
    uv = pl.BlockSpec((S, FB), lambda j: (0, j + N_FB))
    wg = pl.BlockSpec((8, FB), lambda j: (0, j))
    wv = pl.BlockSpec((8, FB), lambda j: (0, j + N_FB))
    bg = pl.BlockSpec((1, FB), lambda j: (0, j))
    bv = pl.BlockSpec((1, FB), lambda j: (0, j + N_FB))
    return ug, uv, wg, wv, bg, bv


def _gate_args(ug_ref, uv_ref, wg_ref, wv_ref, bg_ref, bv_ref):
    return (ug_ref[...], uv_ref[...], wg_ref[0:1, :], wg_ref[1:2, :], wg_ref[2:3, :],
            wv_ref[0:1, :], wv_ref[1:2, :], wv_ref[2:3, :], bg_ref[...], bv_ref[...])


def ffn_gate_fwd(u, cw, cb, name):
    def body(ug_ref, uv_ref, wg_ref, wv_ref, bg_ref, bv_ref, o_ref):
        o_ref[...] = _ffn_gate(*_gate_args(ug_ref, uv_ref, wg_ref, wv_ref, bg_ref, bv_ref)).astype(BF16)

    return pl.pallas_call(body, grid=(N_FB,), name=name, in_specs=list(_gate_specs()),
                          out_specs=pl.BlockSpec((S, FB), lambda j: (0, j)),
                          out_shape=SDS((S, DFF), BF16))(u, u, cw, cw, cb, cb)


def ffn_gate_bwd(u, cw, cb, da, name):
    def body(ug_ref, uv_ref, wg_ref, wv_ref, bg_ref, bv_ref, da_ref, dug_ref, duv_ref, dwg_ref, dwv_ref, dbg_ref, dbv_ref):
        _, vjp = jax.vjp(_ffn_gate, *_gate_args(ug_ref, uv_ref, wg_ref, wv_ref, bg_ref, bv_ref))
        dug, duv, g0, g1, g2, v0, v1, v2, dbg, dbv = vjp(da_ref[...])
        dug_ref[...] = dug.astype(BF16)
        duv_ref[...] = duv.astype(BF16)
        for k, (a, b) in enumerate(((g0, v0), (g1, v1), (g2, v2))):
            dwg_ref[k:k + 1, :] = a
            dwv_ref[k:k + 1, :] = b
        dwg_ref[FFN_K:8, :] = jnp.zeros((8 - FFN_K, FB), F32)
        dwv_ref[FFN_K:8, :] = jnp.zeros((8 - FFN_K, FB), F32)
        dbg_ref[...] = dbg
        dbv_ref[...] = dbv

    ug, uv, wg, wv, bg, bv = _gate_specs()
    half = pl.BlockSpec((S, FB), lambda j: (0, j))
    whalf = pl.BlockSpec((8, FB), lambda j: (0, j))
    bhalf = pl.BlockSpec((1, FB), lambda j: (0, j))
    return pl.pallas_call(body, grid=(N_FB,), name=name, in_specs=[ug, uv, wg, wv, bg, bv, half],
                          out_specs=[half, half, whalf, whalf, bhalf, bhalf],
                          out_shape=[SDS((S, DFF), BF16), SDS((S, DFF), BF16), SDS((8, DFF), F32), SDS((8, DFF), F32),
                                     SDS((1, DFF), F32), SDS((1, DFF), F32)])(u, u, cw, cw, cb, cb, da)


def _adamw(w, g, m, v):
    m = ADAM_B1 * m + (1.0 - ADAM_B1) * g
    v = ADAM_B2 * v + (1.0 - ADAM_B2) * (g * g)
    m_hat = m / (1.0 - ADAM_B1 ** ADAM_STEP)
    v_hat = v / (1.0 - ADAM_B2 ** ADAM_STEP)
    delta = -ADAM_LR * (m_hat / (jnp.sqrt(v_hat) + ADAM_EPS) + ADAM_WD * w)
    return delta, m, v


def sum_pieces(r, name):
    n, rows, cols = r.shape
    tr = _row_tile(rows, cols)

    def body(r_ref, o_ref):
        acc = r_ref[0].astype(F32)
        for j in range(1, n):
            acc = acc + r_ref[j].astype(F32)
        o_ref[...] = acc

    return pl.pallas_call(body, grid=(rows // tr,), name=name,
                          in_specs=[pl.BlockSpec((n, tr, cols), lambda i: (0, i, 0))],
                          out_specs=pl.BlockSpec((tr, cols), lambda i: (i, 0)),
                          out_shape=SDS((rows, cols), F32))(r)


def adamw_pair(w, p, q, m, v, name):
    rows, cols = w.shape
    tr = _row_tile(rows, cols)

    def body(w_ref, p_ref, q_ref, m_ref, v_ref, g_ref, d_ref, nm_ref, nv_ref):
        g = p_ref[...] + q_ref[...]
        g_ref[...] = g
        d_ref[...], nm_ref[...], nv_ref[...] = _adamw(w_ref[...], g, m_ref[...], v_ref[...])

    spec = pl.BlockSpec((tr, cols), lambda i: (i, 0))
    return pl.pallas_call(body, grid=(rows // tr,), name=name, in_specs=[spec] * 5, out_specs=[spec] * 4,
                          out_shape=[SDS((rows, cols), F32)] * 4)(w, p, q, m, v)


def adamw_gathered(w, gall, m, v, name):
    rows = w.shape[0]
    tr = _row_tile(rows, 128 * N_DEV)

    def body(w_ref, ga_ref, m_ref, v_ref, g_ref, d_ref, nm_ref, nv_ref):
        g = ga_ref[0]
        for j in range(1, N_DEV):
            g = g + ga_ref[j]
        g_ref[...] = g
        d_ref[...], nm_ref[...], nv_ref[...] = _adamw(w_ref[...], g, m_ref[...], v_ref[...])

    spec = pl.BlockSpec((tr, 128), lambda i: (i, 0))
    return pl.pallas_call(body, grid=(rows // tr,), name=name,
                          in_specs=[spec, pl.BlockSpec((N_DEV, tr, 128), lambda i: (0, i, 0)), spec, spec],
                          out_specs=[spec] * 4, out_shape=[SDS((rows, 128), F32)] * 4)(w, gall, m, v)


ADA_COLS = 6 * D // N_CHIP
ADA_TN = 512


def ada_mod(c_all, w, b, name):
    def body(c_ref, w_ref, b_ref, o_ref):
        ca = jax.nn.silu(c_ref[...])
        o_ref[...] = jnp.dot(ca, w_ref[...], precision=lax.Precision.HIGHEST, preferred_element_type=F32) + b_ref[...]

    return pl.pallas_call(
        body, grid=(DEPTH, ADA_COLS // ADA_TN), name=name,
        in_specs=[pl.BlockSpec((N_DEV, D), lambda l, j: (0, 0)),
                  pl.BlockSpec((None, D, ADA_TN), lambda l, j: (l, 0, j)),
                  pl.BlockSpec((None, 1, ADA_TN), lambda l, j: (l, 0, j))],
        out_specs=pl.BlockSpec((None, N_DEV, ADA_TN), lambda l, j: (l, 0, j)),
        out_shape=SDS((DEPTH, N_DEV, ADA_COLS), F32))(c_all, w, b)


def ada_update(c_all_t, dmod, w, m, v, name):
    def body(c_ref, dm_ref, w_ref, m_ref, v_ref, g_ref, d_ref, nm_ref, nv_ref):
        ca = jax.nn.silu(c_ref[...])
        g = jnp.dot(ca, dm_ref[...], precision=lax.Precision.HIGHEST, preferred_element_type=F32)
        g_ref[...] = g
        d_ref[...], nm_ref[...], nv_ref[...] = _adamw(w_ref[...], g, m_ref[...], v_ref[...])

    wspec = pl.BlockSpec((None, D, ADA_TN), lambda l, j: (l, 0, j))
    return pl.pallas_call(
        body, grid=(DEPTH, ADA_COLS // ADA_TN), name=name,
        in_specs=[pl.BlockSpec((D, N_DEV), lambda l, j: (0, 0)),
                  pl.BlockSpec((None, N_DEV, ADA_TN), lambda l, j: (l, 0, j)), wspec, wspec, wspec],
        out_specs=[wspec] * 4, out_shape=[SDS((DEPTH, D, ADA_COLS), F32)] * 4)(c_all_t, dmod, w, m, v)


_CHIP_FLIPS = ((1, 0), (0, 1), (1, 1))
_DEV_FLIPS = tuple((a, b, c) for a in (0, 1) for b in (0, 1) for c in (0, 1) if (a, b, c) != (0, 0, 0))


def _position():
    return lax.axis_index("x"), lax.axis_index("y"), lax.axis_index("c")


def _hbm_call(body, arrs, out_shapes, n_remote, name):
    n = len(arrs)
    return pl.pallas_call(
        body, name=name, in_specs=[ANY] * n, out_specs=[ANY] * n, out_shape=out_shapes,
        scratch_shapes=[pltpu.SemaphoreType.DMA((n, n_remote)), pltpu.SemaphoreType.DMA((n, n_remote)),
                        pltpu.SemaphoreType.DMA((n,))])(*arrs)


def chip_allgather(arrs, name):
    n = len(arrs)

    def body(*refs):
        ins, outs = refs[:n], refs[n:2 * n]
        send, recv, loc = refs[2 * n:]
        x, y, c = _position()
        me = 2 * x + y
        copies = []
        for a in range(n):
            local = pltpu.make_async_copy(ins[a], outs[a].at[me], loc.at[a])
            local.start()
            copies.append(local)
            for k, (fx, fy) in enumerate(_CHIP_FLIPS):
                cp = pltpu.make_async_remote_copy(
                    src_ref=ins[a], dst_ref=outs[a].at[me], send_sem=send.at[a, k], recv_sem=recv.at[a, k],
                    device_id=(x ^ fx, y ^ fy, c), device_id_type=MESH)
                cp.start()
                copies.append(cp)
        for cp in copies:
            cp.wait()

    shapes = [SDS((N_CHIP,) + a.shape, a.dtype) for a in arrs]
    return _hbm_call(body, arrs, shapes, 3, name)


def chip_alltoall(arrs, name):
    n = len(arrs)

    def body(*refs):
        ins, outs = refs[:n], refs[n:2 * n]
        send, recv, loc = refs[2 * n:]
        x, y, c = _position()
        me = 2 * x + y
        copies = []
        for a in range(n):
            local = pltpu.make_async_copy(ins[a].at[me], outs[a].at[me], loc.at[a])
            local.start()
            copies.append(local)
            for k, (fx, fy) in enumerate(_CHIP_FLIPS):
                tx, ty = x ^ fx, y ^ fy
                cp = pltpu.make_async_remote_copy(
                    src_ref=ins[a].at[2 * tx + ty], dst_ref=outs[a].at[me], send_sem=send.at[a, k],
                    recv_sem=recv.at[a, k], device_id=(tx, ty, c), device_id_type=MESH)
                cp.start()
                copies.append(cp)
        for cp in copies:
            cp.wait()

    shapes = [SDS(a.shape, a.dtype) for a in arrs]
    return _hbm_call(body, arrs, shapes, 3, name)


def sibling_swap(arrs, name):
    n = len(arrs)

    def body(*refs):
        ins, outs = refs[:n], refs[n:2 * n]
        send, recv, _ = refs[2 * n:]
        x, y, c = _position()
        copies = []
        for a in range(n):
            cp = pltpu.make_async_remote_copy(
                src_ref=ins[a], dst_ref=outs[a], send_sem=send.at[a, 0], recv_sem=recv.at[a, 0],
                device_id=(x, y, 1 - c), device_id_type=MESH)
            cp.start()
            copies.append(cp)
        for cp in copies:
            cp.wait()

    shapes = [SDS(a.shape, a.dtype) for a in arrs]
    return _hbm_call(body, arrs, shapes, 1, name)


def device_allgather(arrs, name):
    n = len(arrs)

    def body(*refs):
        ins, outs = refs[:n], refs[n:2 * n]
        send, recv, loc = refs[2 * n:]
        x, y, c = _position()
        me = 4 * x + 2 * y + c
        copies = []
        for a in range(n):
            local = pltpu.make_async_copy(ins[a], outs[a].at[me], loc.at[a])
            local.start()
            copies.append(local)
            for k, (fx, fy, fc) in enumerate(_DEV_FLIPS):
                cp = pltpu.make_async_remote_copy(
                    src_ref=ins[a], dst_ref=outs[a].at[me], send_sem=send.at[a, k], recv_sem=recv.at[a, k],
                    device_id=(x ^ fx, y ^ fy, c ^ fc), device_id_type=MESH)
                cp.start()
                copies.append(cp)
        for cp in copies:
            cp.wait()

    shapes = [SDS((N_DEV,) + a.shape, a.dtype) for a in arrs]
    return _hbm_call(body, arrs, shapes, 7, name)


def _pad_to(a, axis, size):
    pad = [(0, 0)] * a.ndim
    pad[axis] = (0, size - a.shape[axis])
    return jnp.pad(a, pad)


def _cols_to_z(w):
    return _pad_to(jnp.concatenate([w[..., :768], w[..., 772:], w[..., 768:772]], axis=-1), w.ndim - 1, ZC)


def _cols_from_z(w):
    return jnp.concatenate([w[..., :768], w[..., 2304:2308], w[..., 768:2304]], axis=-1)


def _unshard_cols(g4):
    _, l, r, n = g4.shape
    return jnp.transpose(g4, (1, 2, 0, 3)).reshape(l, r, N_CHIP * n)


def _unshard_rows(g4):
    _, l, r, n = g4.shape
    return jnp.transpose(g4, (1, 0, 2, 3)).reshape(l, N_CHIP * r, n)


def _shard_cols(full):
    l, r, n4 = full.shape
    return jnp.transpose(full.reshape(l, r, N_CHIP, n4 // N_CHIP), (2, 0, 1, 3))


def _shard_rows(full):
    l, r4, n = full.shape
    return jnp.transpose(full.reshape(l, N_CHIP, r4 // N_CHIP, n), (1, 0, 2, 3))


def _pack(parts):
    flat = jnp.concatenate([p.reshape(-1) for p in parts])
    rows = -(-flat.shape[0] // 1024) * 8
    return _pad_to(flat, 0, rows * 128).reshape(rows, 128)


def _unpack(packed, shapes):
    flat = packed.reshape(-1)
    out, off = [], 0
    for s in shapes:
        n = 1
        for d in s:
            n *= d
        out.append(flat[off:off + n].reshape(s))
        off += n
    return out


def _layer_fwd(l, x0, mod, p):
    sh1, sc1, ga1, sh2, sc2, ga2 = mod
    t = f"l{l}_"
    h1 = modnorm_fwd(x0, p['g_pre_mix'], sc1, sh1, t + "modnorm1")
    z = mm_nn(h1, p['w_in'], F32, t + "proj_in")
    cumc, cumr = fgate_fwd(z, p['b_fgate'], t + "fgate")
    y_fox, lse = fox_fwd(z, cumc, cumr, t + "fox")
    hc = conv_dw_fwd(z, p['conv_w'], p['conv_b'], t + "conv_dw")
    y_conv = conv_pw_fwd(hc, p['conv_ln_g'], p['conv_ln_b'], p['conv_pw_w'], p['conv_pw_b'], t + "conv_pw")
    y_swa = swa_fwd(z, p['swa_sinks'], t + "swa")
    y_sgu = sgu_fwd(z, p['sgu_ln_g'], p['sgu_ln_b'], p['sgu_w'], p['sgu_bt'], t + "sgu")
    ys = (y_fox, y_conv, y_swa, y_sgu)
    ycat = group_norm_fwd(ys, p['g_group'], t + "group_norm")
    ymix = mm_nn(ycat, p['w_out'], F32, t + "proj_out")
    x1 = resid_fwd(x0, ymix, ga1, p['g_post_mix'], t + "resid1")
    h2 = modnorm_fwd(x1, p['g_pre_ffn'], sc2, sh2, t + "modnorm2")
    u = mm_nn(h2, p['ffn_w_up'], F32, t + "ffn_up")
    act = ffn_gate_fwd(u, p['ffn_conv_w'], p['ffn_conv_b'], t + "ffn_gate")
    yffn = mm_nn(act, p['ffn_w_down'], F32, t + "ffn_down")
    x2 = resid_fwd(x1, yffn, ga2, p['g_post_ffn'], t + "resid2")
    res = dict(x0=x0, h1=h1, z=z, cumc=cumc, cumr=cumr, y_fox=y_fox, lse=lse, hc=hc, ys=ys, ycat=ycat, ymix=ymix,
               x1=x1, h2=h2, u=u, act=act, yffn=yffn)
    return x2, res


def _layer_bwd(l, dx2, mod, p, r):
    sh1, sc1, ga1, sh2, sc2, ga2 = mod
    t = f"l{l}_bwd_"
    g = {}
    dyffn, dga2, g['g_post_ffn'] = resid_bwd(r['yffn'], ga2, p['g_post_ffn'], dx2, t + "resid2")
    g['ffn_w_down'] = mm_tn(r['act'], dyffn, t + "ffn_down_dw")
    dact = mm_nt(dyffn, p['ffn_w_down'], t + "ffn_down_dx")
    dug, duv, dwg, dwv, dbg, dbv = ffn_gate_bwd(r['u'], p['ffn_conv_w'], p['ffn_conv_b'], dact, t + "ffn_gate")
    du = jnp.concatenate([dug, duv], axis=1)
    g['ffn_conv_w'] = jnp.concatenate([dwg, dwv], axis=1)
    g['ffn_conv_b'] = jnp.concatenate([dbg, dbv], axis=1)
    g['ffn_w_up'] = mm_tn(r['h2'], du, t + "ffn_up_dw")
    dh2 = mm_nt(du, p['ffn_w_up'], t + "ffn_up_dx")
    dx1, g['g_pre_ffn'], dsc2, dsh2 = modnorm_bwd(r['x1'], p['g_pre_ffn'], sc2, sh2, dh2, dx2, t + "modnorm2")
    dymix, dga1, g['g_post_mix'] = resid_bwd(r['ymix'], ga1, p['g_post_mix'], dx1, t + "resid1")
    g['w_out'] = mm_tn(r['ycat'], dymix, t + "proj_out_dw")
    dcat = mm_nt(dymix, p['w_out'], t + "proj_out_dx")
    dy_fox, dy_conv, dy_swa, dy_sgu, g['g_group'] = group_norm_bwd(r['ys'], p['g_group'], dcat, t + "group_norm")
    z = r['z']
    fq, fk, fv, dcq, dck = fox_bwd(z, r['cumc'], r['cumr'], r['lse'], r['y_fox'], dy_fox, t + "fox")
    dzf, g['b_fgate'] = fgate_bwd(z, p['b_fgate'], dcq, dck, t + "fgate")
    dhc, g['conv_ln_g'], g['conv_ln_b'], g['conv_pw_w'], g['conv_pw_b'] = conv_pw_bwd(
        r['hc'], p['conv_ln_g'], p['conv_ln_b'], p['conv_pw_w'], p['conv_pw_b'], dy_conv, t + "conv_pw")
    ca, cg, g['conv_w'], g['conv_b'] = conv_dw_bwd(z, p['conv_w'], dhc, t + "conv_dw")
    sq, sk, sv, g['swa_sinks'] = swa_bwd(z, p['swa_sinks'], dy_swa, t + "swa")
    gu, gv, g['sgu_ln_g'], g['sgu_ln_b'], g['sgu_w'], g['sgu_bt'] = sgu_bwd(
        z, p['sgu_ln_g'], p['sgu_ln_b'], p['sgu_w'], p['sgu_bt'], dy_sgu, t + "sgu")
    dz = jnp.concatenate([fq, fk, fv, ca, cg, sq, sk, sv, gu, gv, dzf], axis=1)
    g['w_in'] = mm_tn(r['h1'], dz, t + "proj_in_dw")
    dh1 = mm_nt(dz, p['w_in'], t + "proj_in_dx")
    dx0, g['g_pre_mix'], dsc1, dsh1 = modnorm_bwd(r['x0'], p['g_pre_mix'], sc1, sh1, dh1, dx1, t + "modnorm1")
    return dx0, g, (dsh1, dsc1, dga1, dsh2, dsc2, dga2)


def _layer_params(l, w, full):
    def row(name, width=None):
        v = w[name][l].reshape(1, -1)
        return v if width is None else _pad_to(v, 1, width)
    return {
        'g_pre_mix': row('g_pre_mix'), 'g_post_mix': row('g_post_mix'), 'g_pre_ffn': row('g_pre_ffn'),
        'g_post_ffn': row('g_post_ffn'), 'b_fgate': row('b_fgate', 128), 'conv_b': row('conv_b'),
        'conv_ln_g': row('conv_ln_g'), 'conv_ln_b': row('conv_ln_b'), 'conv_pw_b': row('conv_pw_b'),
        'swa_sinks': row('swa_sinks', 128), 'sgu_ln_g': row('sgu_ln_g'), 'sgu_ln_b': row('sgu_ln_b'),
        'sgu_w': w['sgu_w'][l], 'sgu_bt': _pad_to(w['sgu_b'][l].T, 1, 128), 'g_group': row('g_group'),
        'ffn_conv_b': row('ffn_conv_b'),
        'w_in': full['w_in'][l], 'w_out': full['w_out'][l], 'ffn_w_up': full['ffn_w_up'][l],
        'ffn_w_down': full['ffn_w_down'][l], 'conv_w': _pad_to(full['conv_w'][l], 0, 32),
        'conv_pw_w': full['conv_pw_w'][l], 'ffn_conv_w': _pad_to(full['ffn_conv_w'][l], 0, 8),
    }


def _small_grads(g):
    return {
        'g_pre_mix': g['g_pre_mix'][0], 'g_post_mix': g['g_post_mix'][0], 'g_pre_ffn': g['g_pre_ffn'][0],
        'g_post_ffn': g['g_post_ffn'][0], 'b_fgate': g['b_fgate'][0, :4], 'conv_b': g['conv_b'][0],
        'conv_ln_g': g['conv_ln_g'][0], 'conv_ln_b': g['conv_ln_b'][0], 'conv_pw_b': g['conv_pw_b'][0],
        'swa_sinks': g['swa_sinks'][0, :4], 'sgu_ln_g': g['sgu_ln_g'][0], 'sgu_ln_b': g['sgu_ln_b'][0],
        'sgu_w': g['sgu_w'], 'sgu_b': g['sgu_bt'][:, :4].T, 'g_group': g['g_group'][0],
        'ffn_conv_b': g['ffn_conv_b'][0],
    }


def _local_step(xs, target, mod, w, full):
    params, resids, mods = [], [], []
    for l in range(DEPTH):
        params.append(_layer_params(l, w, full))
        mods.append(tuple(mod[l, j] for j in range(6)))
        xs, r = _layer_fwd(l, xs, mods[l], params[l])
        resids.append(r)
    dx, loss_row = loss_grad(xs, target, "loss")
    grads, dmods = [None] * DEPTH, [None] * DEPTH
    for l in reversed(range(DEPTH)):
        dx, grads[l], dmods[l] = _layer_bwd(l, dx, mods[l], params[l], resids[l])
    return loss_row, dx, grads, dmods


def kernel(x, c, w_ada, b_ada, g_pre_mix, g_post_mix, g_pre_ffn, g_post_ffn, w_in, b_fgate, conv_w, conv_b, conv_ln_g, conv_ln_b, conv_pw_w, conv_pw_b, swa_sinks, sgu_ln_g, sgu_ln_b, sgu_w, sgu_b, g_group, w_out, ffn_w_up, ffn_conv_w, ffn_conv_b, ffn_w_down, loss_target, m_w_ada, m_b_ada, m_g_pre_mix, m_g_post_mix, m_g_pre_ffn, m_g_post_ffn, m_w_in, m_b_fgate, m_conv_w, m_conv_b, m_conv_ln_g, m_conv_ln_b, m_conv_pw_w, m_conv_pw_b, m_swa_sinks, m_sgu_ln_g, m_sgu_ln_b, m_sgu_w, m_sgu_b, m_g_group, m_w_out, m_ffn_w_up, m_ffn_conv_w, m_ffn_conv_b, m_ffn_w_down, v_w_ada, v_b_ada, v_g_pre_mix, v_g_post_mix, v_g_pre_ffn, v_g_post_ffn, v_w_in, v_b_fgate, v_conv_w, v_conv_b, v_conv_ln_g, v_conv_ln_b, v_conv_pw_w, v_conv_pw_b, v_swa_sinks, v_sgu_ln_g, v_sgu_ln_b, v_sgu_w, v_sgu_b, v_g_group, v_w_out, v_ffn_w_up, v_ffn_conv_w, v_ffn_conv_b, v_ffn_w_down):
    args = locals()
    w = {n: args[n] for n in WEIGHTS}
    m = {n: args['m_' + n] for n in WEIGHTS}
    v = {n: args['v_' + n] for n in WEIGHTS}
    xi, yi, ci = _position()
    chip = 2 * xi + yi

    names = ['w_in', 'w_out', 'ffn_w_up', 'ffn_w_down', 'conv_w', 'conv_pw_w', 'ffn_conv_w']
    send = [w[n].astype(BF16) if n in ('w_in', 'w_out', 'ffn_w_up', 'ffn_w_down') else w[n] for n in names]
    got = dict(zip(names, chip_allgather(send, "gather_weights")))
    full = {
        'w_in': _cols_to_z(_unshard_cols(got['w_in'])), 'w_out': _unshard_rows(got['w_out']),
        'ffn_w_up': _unshard_cols(got['ffn_w_up']), 'ffn_w_down': _unshard_rows(got['ffn_w_down']),
        'conv_w': _unshard_cols(got['conv_w']), 'conv_pw_w': _unshard_rows(got['conv_pw_w']),
        'ffn_conv_w': _unshard_cols(got['ffn_conv_w']),
    }

    (c_all,) = device_allgather([c], "gather_c")
    c_all = c_all.reshape(N_DEV, D)
    b_loc = lax.dynamic_slice_in_dim(b_ada, chip * ADA_COLS, ADA_COLS, axis=1).reshape(DEPTH, 1, ADA_COLS)
    mod_all = ada_mod(c_all, w_ada, b_loc, "ada_mod")
    mine = lax.dynamic_index_in_dim(mod_all.reshape(DEPTH, N_CHIP, 2, ADA_COLS), ci, axis=2, keepdims=False)
    (mod4,) = chip_alltoall([jnp.transpose(mine, (1, 0, 2))], "scatter_mod")
    mod = jnp.transpose(mod4, (1, 0, 2)).reshape(DEPTH, 6, 1, D)

    loss_row, dx, grads, dmods = _local_step(x.reshape(S, D), loss_target.reshape(S, D), mod, w, full)
    loss = lax.psum(loss_row[0, 0], ("x", "y", "c"))
    grad_x = dx.reshape(1, S, D)

    small = [_small_grads(g) for g in grads]
    local = {n: jnp.stack([small[l][n] for l in range(DEPTH)]) for n in SMALL if n != 'b_ada'}
    local['b_ada'] = jnp.stack([jnp.concatenate(dmods[l], axis=1)[0] for l in range(DEPTH)])
    (g_all,) = device_allgather([_pack([local[n] for n in SMALL])], "gather_small_grads")
    shapes = [w[n].shape for n in SMALL]
    packed = adamw_gathered(_pack([w[n] for n in SMALL]), g_all, _pack([m[n] for n in SMALL]),
                            _pack([v[n] for n in SMALL]), "adamw_small")
    out = {n: tuple(a) for n, a in zip(SMALL, zip(*[_unpack(pk, shapes) for pk in packed]))}

    n_ada = DEPTH * 6 * D // 128
    dmod_all = g_all[:, :n_ada, :].reshape(N_DEV, DEPTH, 6 * D)
    dmod_loc = jnp.transpose(lax.dynamic_slice_in_dim(dmod_all, chip * ADA_COLS, ADA_COLS, axis=2), (1, 0, 2))
    out['w_ada'] = tuple(ada_update(c_all.T, dmod_loc, w_ada, m['w_ada'], v['w_ada'], "adamw_ada"))

    def stacked(name):
        return jnp.stack([grads[l][name] for l in range(DEPTH)])
    pieces = {
        'w_in': _shard_cols(_cols_from_z(stacked('w_in'))),
        'w_out': _shard_rows(stacked('w_out')),
        'ffn_w_up': _shard_cols(stacked('ffn_w_up')),
        'ffn_w_down': _shard_rows(stacked('ffn_w_down')),
        'conv_w': _shard_cols(stacked('conv_w')[:, :CONV_K]),
        'conv_pw_w': _shard_rows(stacked('conv_pw_w')),
        'ffn_conv_w': _shard_cols(stacked('ffn_conv_w')[:, :FFN_K]),
    }
    recv = dict(zip(SHARDED, chip_alltoall([pieces[n] for n in SHARDED], "scatter_grads")))
    part = {}
    for n in SHARDED:
        cols = w[n].shape[-1]
        part[n] = sum_pieces(recv[n].reshape(N_CHIP, -1, cols), "sum_" + n)
    other = dict(zip(SHARDED, sibling_swap([part[n] for n in SHARDED], "swap_partials")))
    for n in SHARDED:
        cols = w[n].shape[-1]
        res = adamw_pair(w[n].reshape(-1, cols), part[n], other[n], m[n].reshape(-1, cols), v[n].reshape(-1, cols),
                         "adamw_" + n)
        out[n] = tuple(a.reshape(w[n].shape) for a in res)

    return (loss, grad_x, *[out[n][0] for n in WEIGHTS], *[out[n][1] for n in WEIGHTS],
            *[out[n][2] for n in WEIGHTS], *[out[n][3] for n in WEIGHTS])
```

```python
import functools

import jax
import jax.numpy as jnp
from jax import lax
from jax.experimental import pallas as pl
from jax.experimental.pallas import tpu as pltpu

F32 = jnp.float32
BF16 = jnp.bfloat16
SDS = jax.ShapeDtypeStruct
MESH = pl.DeviceIdType.MESH
ANY = pl.BlockSpec(memory_space=pl.ANY)

DEPTH = 2
S = 2048
D = 1024
GW = 256
DFF = 2816
NUP = 2 * DFF
IN_COLS = 2308
ZC = 2432
CONV_K = 31
FFN_K = 3
EPS = 1e-6
SCALE = 0.125
NEG = -1e30
N_CHIP = 4
N_DEV = 8

Z_FOX_Q, Z_FOX_K, Z_FOX_V = 0, 256, 512
Z_CONV_A, Z_CONV_G = 768, 1024
Z_SWA_Q, Z_SWA_K, Z_SWA_V = 1280, 1536, 1664
Z_SGU_U, Z_SGU_V = 1792, 2048
Z_FG = 2304

ADAM_LR, ADAM_B1, ADAM_B2, ADAM_EPS, ADAM_WD, ADAM_STEP = 0.001, 0.9, 0.999, 1e-08, 0.01, 10

TS = 256
TM = 512
N_SPLIT = 2816
N_BLOCK = 1408

WEIGHTS = ['w_ada', 'b_ada', 'g_pre_mix', 'g_post_mix', 'g_pre_ffn', 'g_post_ffn', 'w_in', 'b_fgate', 'conv_w',
           'conv_b', 'conv_ln_g', 'conv_ln_b', 'conv_pw_w', 'conv_pw_b', 'swa_sinks', 'sgu_ln_g', 'sgu_ln_b',
           'sgu_w', 'sgu_b', 'g_group', 'w_out', 'ffn_w_up', 'ffn_conv_w', 'ffn_conv_b', 'ffn_w_down']
SHARDED = ['w_in', 'conv_w', 'conv_pw_w', 'w_out', 'ffn_w_up', 'ffn_conv_w', 'ffn_w_down']
SMALL = [n for n in WEIGHTS if n not in SHARDED and n != 'w_ada']


def _rms(x, g):
    return x * lax.rsqrt(jnp.mean(x * x, axis=-1, keepdims=True) + EPS) * g


def _modnorm(x, g, sc, sh):
    return _rms(x, g) * (1.0 + sc) + sh


def _resid(x, y, ga, g):
    return x + ga * _rms(y, g)


@functools.partial(jax.custom_vjp, nondiff_argnums=(1,))
def _shift_down(x, n):
    if n == 0:
        return x
    row = lax.broadcasted_iota(jnp.int32, x.shape, 0)
    return jnp.where(row >= n, pltpu.roll(x, n, axis=0), 0.0)


def _shift_up(x, n):
    if n == 0:
        return x
    rows = x.shape[0]
    row = lax.broadcasted_iota(jnp.int32, x.shape, 0)
    return jnp.where(row < rows - n, pltpu.roll(x, rows - n, axis=0), 0.0)


def _shift_down_fwd(x, n):
    return _shift_down(x, n), None


def _shift_down_bwd(n, _, ct):
    return (_shift_up(ct, n),)


_shift_down.defvjp(_shift_down_fwd, _shift_down_bwd)


@jax.custom_vjp
def _swap_halves(x):
    return pltpu.roll(x, 64, axis=1)


def _swap_halves_fwd(x):
    return _swap_halves(x), None


def _swap_halves_bwd(_, ct):
    return (pltpu.roll(ct, 64, axis=1),)


_swap_halves.defvjp(_swap_halves_fwd, _swap_halves_bwd)


def _nt(a, b):
    return lax.dot_general(a, b, (((1,), (1,)), ((), ())), preferred_element_type=F32)


def _tn(a, b):
    return lax.dot_general(a, b, (((0,), (0,)), ((), ())), preferred_element_type=F32)


def _nn(a, b):
    return jnp.dot(a, b, preferred_element_type=F32)


def _acc(ref, val, first):
    @pl.when(first)
    def _():
        ref[...] = val

    @pl.when(jnp.logical_not(first))
    def _():
        ref[...] += val


def _row_tile(rows, cols):
    limit = max(8, (1 << 20) // (4 * cols))
    best = None
    for t in range(8, rows + 1, 8):
        if rows % t == 0 and t <= limit:
            best = t
    return best if best is not None else rows


def _ncol(n):
    return n if n <= N_SPLIT else N_BLOCK


def _weight_spec(b, order):
    pick = (lambda j, i: j) if order == 0 else (lambda i, j: j)
    if b.ndim == 3:
        _, k, tn = b.shape
        return pl.BlockSpec((None, k, tn), lambda *g: (pick(*g), 0, 0)), k, N_CHIP * tn, tn
    k, n = b.shape
    tn = _ncol(n)
    return pl.BlockSpec((k, tn), lambda *g: (0, pick(*g))), k, n, tn


def mm_nn(a, b, out_dtype, name):
    m = a.shape[0]
    b_spec, k, n, tn = _weight_spec(b, 0)

    def body(a_ref, b_ref, o_ref):
        o_ref[...] = _nn(a_ref[...], b_ref[...]).astype(out_dtype)

    return pl.pallas_call(
        body, grid=(n // tn, m // TM), name=name,
        in_specs=[pl.BlockSpec((TM, k), lambda j, i: (i, 0)), b_spec],
        out_specs=pl.BlockSpec((TM, tn), lambda j, i: (i, j)),
        out_shape=SDS((m, n), out_dtype),
    )(a, b)


def mm_nt(a, b, name):
    m = a.shape[0]
    b_spec, k, n, tc = _weight_spec(b, 1)

    def body(a_ref, b_ref, o_ref):
        _acc(o_ref, _nt(a_ref[...], b_ref[...]), pl.program_id(1) == 0)

    return pl.pallas_call(
        body, grid=(m // TM, n // tc), name=name,
        in_specs=[pl.BlockSpec((TM, tc), lambda i, c: (i, c)), b_spec],
        out_specs=pl.BlockSpec((TM, k), lambda i, c: (i, 0)),
        out_shape=SDS((m, k), F32),
    )(a, b)


def mm_tn(a, b, name, sharded=False):
    m, k = a.shape
    n = b.shape[1]
    tn = n // N_CHIP if sharded else _ncol(n)
    steps = m // TM

    def body(a_ref, b_ref, o_ref, acc_ref):
        i = pl.program_id(1)
        _acc(acc_ref, _tn(a_ref[...], b_ref[...]), i == 0)

        @pl.when(i == steps - 1)
        def _():
            o_ref[...] = acc_ref[...].astype(BF16)

    if sharded:
        out_spec, out_shape = pl.BlockSpec((None, k, tn), lambda j, i: (j, 0, 0)), SDS((N_CHIP, k, tn), BF16)
    else:
        out_spec, out_shape = pl.BlockSpec((k, tn), lambda j, i: (0, j)), SDS((k, n), BF16)
    return pl.pallas_call(
        body, grid=(n // tn, steps), name=name,
        in_specs=[pl.BlockSpec((TM, k), lambda j, i: (i, 0)), pl.BlockSpec((TM, tn), lambda j, i: (i, j))],
        out_specs=out_spec, out_shape=out_shape,
        scratch_shapes=[pltpu.VMEM((k, tn), F32)],
    )(a, b)


_ROW = pl.BlockSpec((TS, D), lambda i: (i, 0))
_VEC = pl.BlockSpec((1, D), lambda i: (0, 0))


def modnorm_fwd(x, g, sc, sh, name):
    def body(x_ref, g_ref, sc_ref, sh_ref, o_ref):
        o_ref[...] = _modnorm(x_ref[...], g_ref[...], sc_ref[...], sh_ref[...]).astype(BF16)

    return pl.pallas_call(body, grid=(S // TS,), name=name, in_specs=[_ROW, _VEC, _VEC, _VEC], out_specs=_ROW,
                          out_shape=SDS((S, D), BF16))(x, g, sc, sh)


def modnorm_bwd(x, g, sc, sh, dh, dx_in, name):
    def body(x_ref, g_ref, sc_ref, sh_ref, dh_ref, dxin_ref, dx_ref, dg_ref, dsc_ref, dsh_ref):
        _, vjp = jax.vjp(_modnorm, x_ref[...], g_ref[...], sc_ref[...], sh_ref[...])
        dx, dg, dsc, dsh = vjp(dh_ref[...])
        dx_ref[...] = dxin_ref[...] + dx
        first = pl.program_id(0) == 0
        _acc(dg_ref, dg, first)
        _acc(dsc_ref, dsc, first)
        _acc(dsh_ref, dsh, first)

    vec = SDS((1, D), F32)
    return pl.pallas_call(body, grid=(S // TS,), name=name, in_specs=[_ROW, _VEC, _VEC, _VEC, _ROW, _ROW],
                          out_specs=[_ROW, _VEC, _VEC, _VEC], out_shape=[SDS((S, D), F32), vec, vec, vec])(
                              x, g, sc, sh, dh, dx_in)


def resid_fwd(x, y, ga, g, name):
    def body(x_ref, y_ref, ga_ref, g_ref, o_ref):
        o_ref[...] = _resid(x_ref[...], y_ref[...], ga_ref[...], g_ref[...])

    return pl.pallas_call(body, grid=(S // TS,), name=name, in_specs=[_ROW, _ROW, _VEC, _VEC], out_specs=_ROW,
                          out_shape=SDS((S, D), F32))(x, y, ga, g)


def resid_bwd(y, ga, g, dxo, name):
    def body(y_ref, ga_ref, g_ref, dxo_ref, dy_ref, dga_ref, dg_ref):
        _, vjp = jax.vjp(lambda y, ga, g: ga * _rms(y, g), y_ref[...], ga_ref[...], g_ref[...])
        dy, dga, dg = vjp(dxo_ref[...])
        dy_ref[...] = dy.astype(BF16)
        first = pl.program_id(0) == 0
        _acc(dga_ref, dga, first)
        _acc(dg_ref, dg, first)

    vec = SDS((1, D), F32)
    return pl.pallas_call(body, grid=(S // TS,), name=name, in_specs=[_ROW, _VEC, _VEC, _ROW],
                          out_specs=[_ROW, _VEC, _VEC], out_shape=[SDS((S, D), BF16), vec, vec])(y, ga, g, dxo)


def loss_grad(xf, target, name):
    def body(x_ref, t_ref, dx_ref, l_ref):
        err = x_ref[...] - t_ref[...]
        dx_ref[...] = err * (1.0 / D)
        part = 0.5 * jnp.sum(jnp.mean(err * err, axis=-1, keepdims=True), axis=0, keepdims=True)
        _acc(l_ref, jnp.broadcast_to(part, (1, 128)), pl.program_id(0) == 0)

    return pl.pallas_call(body, grid=(S // TS,), name=name, in_specs=[_ROW, _ROW],
                          out_specs=[_ROW, pl.BlockSpec((1, 128), lambda i: (0, 0))],
                          out_shape=[SDS((S, D), F32), SDS((1, 128), F32)])(xf, target)


_FG_SPEC = pl.BlockSpec((S, 128), lambda i: (0, Z_FG // 128))


def _tri128(lower):
    r = lax.broadcasted_iota(jnp.int32, (128, 128), 0)
    c = lax.broadcasted_iota(jnp.int32, (128, 128), 1)
    return ((r >= c) if lower else (r <= c)).astype(F32)


def fgate_fwd(z, bf, name):
    def body(z_ref, b_ref, cc_ref, cr_ref):
        tri = _tri128(True)
        carry = jnp.zeros((1, 128), F32)
        for i in range(S // 128):
            rows = pl.ds(i * 128, 128)
            lf = jax.nn.log_sigmoid(z_ref[rows, :] + b_ref[...])
            c = jnp.dot(tri, lf, precision=lax.Precision.HIGHEST, preferred_element_type=F32) + carry
            cc_ref[rows, :] = c
            carry = c[127:128, :]
        cr_ref[...] = cc_ref[...].T

    return pl.pallas_call(body, name=name, grid=(1,),
                          in_specs=[_FG_SPEC, pl.BlockSpec((1, 128), lambda i: (0, 0))],
                          out_specs=[pl.BlockSpec((S, 128), lambda i: (0, 0)), pl.BlockSpec((128, S), lambda i: (0, 0))],
                          out_shape=[SDS((S, 128), F32), SDS((128, S), F32)])(z, bf)


def fgate_bwd(z, bf, dcq, dck, name):
    def body(z_ref, b_ref, dcq_ref, dck_ref, dz_ref, db_ref, col_ref):
        col_ref[...] = dcq_ref[...] + jnp.concatenate([dck_ref[...], jnp.zeros((120, S), F32)], axis=0).T
        tri = _tri128(False)
        carry = jnp.zeros((1, 128), F32)
        db = jnp.zeros((1, 128), F32)
        for i in reversed(range(S // 128)):
            rows = pl.ds(i * 128, 128)
            dlf = jnp.dot(tri, col_ref[rows, :], precision=lax.Precision.HIGHEST, preferred_element_type=F32) + carry
            carry = dlf[0:1, :]
            dz = dlf * jax.nn.sigmoid(-(z_ref[rows, :] + b_ref[...]))
            dz_ref[rows, :] = dz.astype(BF16)
            db = db + jnp.sum(dz, axis=0, keepdims=True)
        db_ref[...] = db

    return pl.pallas_call(body, name=name, grid=(1,),
                          in_specs=[_FG_SPEC, pl.BlockSpec((1, 128), lambda i: (0, 0)),
                                    pl.BlockSpec((S, 128), lambda i: (0, 0)), pl.BlockSpec((8, S), lambda i: (0, 0))],
                          out_specs=[pl.BlockSpec((S, 128), lambda i: (0, 0)), pl.BlockSpec((1, 128), lambda i: (0, 0))],
                          out_shape=[SDS((S, 128), BF16), SDS((1, 128), F32)],
                          scratch_shapes=[pltpu.VMEM((S, 128), F32)])(z, bf, dcq, dck)


TQ = 256


def _head_mask(hh):
    lane = lax.broadcasted_iota(jnp.int32, (TQ, 128), 1)
    return (lane >= 64 * hh) & (lane < 64 * hh + 64)


def _fox_specs():
    q = pl.BlockSpec((TQ, 256), lambda i: (i, Z_FOX_Q // 256))
    k = pl.BlockSpec((S, 256), lambda i: (0, Z_FOX_K // 256))
    v = pl.BlockSpec((S, 256), lambda i: (0, Z_FOX_V // 256))
    cc = pl.BlockSpec((TQ, 128), lambda i: (i, 0))
    cr = pl.BlockSpec((8, S), lambda i: (0, 0))
    return q, k, v, cc, cr


def _fox_scores(qm, k, cc_h, cr_h, causal):
    s = _nt(qm, k) * SCALE + cc_h - cr_h
    return jnp.where(causal, s, NEG)


def _causal(i):
    qpos = i * TQ + lax.broadcasted_iota(jnp.int32, (TQ, S), 0)
    kpos = lax.broadcasted_iota(jnp.int32, (TQ, S), 1)
    return kpos <= qpos


def fox_fwd(z, cumc, cumr, name):
    def body(q_ref, k_ref, v_ref, cc_ref, cr_ref, o_ref, l_ref):
        causal = _causal(pl.program_id(0))
        lane = lax.broadcasted_iota(jnp.int32, (TQ, 128), 1)
        cc = cc_ref[...]
        lse = jnp.zeros((TQ, 128), F32)
        for p in range(2):
            cols = pl.ds(128 * p, 128)
            q = q_ref[:, cols]
            k = k_ref[:, cols].astype(BF16)
            v = v_ref[:, cols].astype(BF16)
            o_pair = jnp.zeros((TQ, 128), F32)
            for hh in range(2):
                h = 2 * p + hh
                hm = _head_mask(hh)
                qm = jnp.where(hm, q, 0.0).astype(BF16)
                s = _fox_scores(qm, k, cc[:, h:h + 1], cr_ref[h:h + 1, :], causal)
                m = jnp.max(s, axis=1, keepdims=True)
                e = jnp.exp(s - m)
                l = jnp.sum(e, axis=1, keepdims=True)
                o_pair = jnp.where(hm, _nn(e.astype(BF16), v) / l, o_pair)
                lse = jnp.where(lane == h, m + jnp.log(l), lse)
            o_ref[:, cols] = o_pair
        l_ref[...] = lse

    q, k, v, cc, cr = _fox_specs()
    return pl.pallas_call(body, grid=(S // TQ,), name=name, in_specs=[q, k, v, cc, cr],
                          out_specs=[pl.BlockSpec((TQ, 256), lambda i: (i, 0)), cc],
                          out_shape=[SDS((S, 256), F32), SDS((S, 128), F32)])(z, z, z, cumc, cumr)


def fox_bwd(z, cumc, cumr, lse, o, do, name):
    steps = S // TQ

    def body(q_ref, k_ref, v_ref, cc_ref, cr_ref, l_ref, o_ref, do_ref, dq_ref, dk_ref, dv_ref, dcq_ref, dck_ref,
             dk_acc, dv_acc):
        i = pl.program_id(0)
        causal = _causal(i)
        row8 = lax.broadcasted_iota(jnp.int32, (8, S), 0)
        lane = lax.broadcasted_iota(jnp.int32, (TQ, 128), 1)
        cc = cc_ref[...]
        lse_all = l_ref[...]
        dck = jnp.zeros((8, S), F32)
        dcq = jnp.zeros((TQ, 128), F32)

        @pl.when(i == 0)
        def _():
            dk_acc[...] = jnp.zeros_like(dk_acc)
            dv_acc[...] = jnp.zeros_like(dv_acc)

        for p in range(2):
            cols = pl.ds(128 * p, 128)
            q = q_ref[:, cols]
            k = k_ref[:, cols].astype(BF16)
            v = v_ref[:, cols].astype(BF16)
            o_p = o_ref[:, cols]
            do_p = do_ref[:, cols]
            dq_pair = jnp.zeros((TQ, 128), F32)
            dk_pair = jnp.zeros((S, 128), F32)
            dv_pair = jnp.zeros((S, 128), F32)
            for hh in range(2):
                h = 2 * p + hh
                hm = _head_mask(hh)
                qm = jnp.where(hm, q, 0.0).astype(BF16)
                s = _fox_scores(qm, k, cc[:, h:h + 1], cr_ref[h:h + 1, :], causal)
                pn = jnp.exp(s - lse_all[:, h:h + 1])
                dom = jnp.where(hm, do_p, 0.0)
                dl = jnp.sum(dom * o_p, axis=1, keepdims=True)
                dom = dom.astype(BF16)
                ds = pn * (_nt(dom, v) - dl)
                dsb = ds.astype(BF16)
                dq_pair = jnp.where(hm, _nn(dsb, k) * SCALE, dq_pair)
                dk_pair = dk_pair + _tn(dsb, qm) * SCALE
                dv_pair = dv_pair + _tn(pn.astype(BF16), dom)
                dck = jnp.where(row8 == h, -jnp.sum(ds, axis=0, keepdims=True), dck)
                dcq = jnp.where(lane == h, jnp.sum(ds, axis=1, keepdims=True), dcq)
            dq_ref[:, cols] = dq_pair.astype(BF16)
            dk_acc[:, cols] += dk_pair
            dv_acc[:, cols] += dv_pair
        dcq_ref[...] = dcq
        _acc(dck_ref, dck, i == 0)

        @pl.when(i == steps - 1)
        def _():
            dk_ref[...] = dk_acc[...].astype(BF16)
            dv_ref[...] = dv_acc[...].astype(BF16)

    q, k, v, cc, cr = _fox_specs()
    blk = pl.BlockSpec((TQ, 256), lambda i: (i, 0))
    full = pl.BlockSpec((S, 256), lambda i: (0, 0))
    return pl.pallas_call(body, grid=(steps,), name=name, in_specs=[q, k, v, cc, cr, cc, blk, blk],
                          out_specs=[blk, full, full, cc, cr],
                          out_shape=[SDS((S, 256), BF16), SDS((S, 256), BF16), SDS((S, 256), BF16), SDS((S, 128), F32),
                                     SDS((8, S), F32)],
                          scratch_shapes=[pltpu.VMEM((S, 256), F32), pltpu.VMEM((S, 256), F32)])(
                              z, z, z, cumc, cumr, lse, o, do)


W = 128


def _swa_block(first, q0, q1, kp, kc, vp, vc, s0, s1, s2, s3):
    qi = lax.broadcasted_iota(jnp.int32, (W, W), 0)
    kj = lax.broadcasted_iota(jnp.int32, (W, W), 1)
    lane = lax.broadcasted_iota(jnp.int32, (W, 128), 1)
    mask_prev = (kj > qi) & jnp.logical_not(first)
    mask_cur = kj <= qi
    kpb, kcb, vpb, vcb = (a.astype(BF16) for a in (kp, kc, vp, vc))
    sinks = (s0, s1, s2, s3)
    outs = []
    for kv, qpair in enumerate((q0, q1)):
        kvm = (lane >= 64 * kv) & (lane < 64 * kv + 64)
        o_pair = jnp.zeros((W, 128), F32)
        for hh in range(2):
            h = 2 * kv + hh
            qa = qpair if hh == kv else _swap_halves(qpair)
            qm = jnp.where(kvm, qa, 0.0).astype(BF16)
            sp = jnp.where(mask_prev, _nt(qm, kpb) * SCALE, NEG)
            sc = jnp.where(mask_cur, _nt(qm, kcb) * SCALE, NEG)
            m = jnp.maximum(jnp.maximum(jnp.max(sp, axis=1, keepdims=True), jnp.max(sc, axis=1, keepdims=True)),
                            sinks[h])
            m = lax.stop_gradient(m)
            ep = jnp.exp(sp - m)
            ec = jnp.exp(sc - m)
            den = jnp.sum(ep, axis=1, keepdims=True) + jnp.sum(ec, axis=1, keepdims=True) + jnp.exp(sinks[h] - m)
            o = _nn((ep / den).astype(BF16), vpb) + _nn((ec / den).astype(BF16), vcb)
            o = o if hh == kv else _swap_halves(o)
            o_pair = jnp.where((lane >= 64 * hh) & (lane < 64 * hh + 64), o, o_pair)
        outs.append(o_pair)
    return outs[0], outs[1]


def _swa_specs():
    q = pl.BlockSpec((W, 256), lambda n: (n, Z_SWA_Q // 256))
    kc = pl.BlockSpec((W, 128), lambda n: (n, Z_SWA_K // 128))
    kp = pl.BlockSpec((W, 128), lambda n: (jnp.maximum(n - 1, 0), Z_SWA_K // 128))
    vc = pl.BlockSpec((W, 128), lambda n: (n, Z_SWA_V // 128))
    vp = pl.BlockSpec((W, 128), lambda n: (jnp.maximum(n - 1, 0), Z_SWA_V // 128))
    sk = pl.BlockSpec((1, 128), lambda n: (0, 0))
    return q, kp, kc, vp, vc, sk


def _swa_args(q_ref, kp_ref, kc_ref, vp_ref, vc_ref, sk_ref):
    return (q_ref[:, 0:128], q_ref[:, 128:256], kp_ref[...], kc_ref[...], vp_ref[...], vc_ref[...],
            sk_ref[:, 0:1], sk_ref[:, 1:2], sk_ref[:, 2:3], sk_ref[:, 3:4])


def swa_fwd(z, sinks, name):
    def body(q_ref, kp_ref, kc_ref, vp_ref, vc_ref, sk_ref, o_ref):
        o0, o1 = _swa_block(pl.program_id(0) == 0, *_swa_args(q_ref, kp_ref, kc_ref, vp_ref, vc_ref, sk_ref))
        o_ref[:, 0:128] = o0
        o_ref[:, 128:256] = o1

    return pl.pallas_call(body, grid=(S // W,), name=name, in_specs=list(_swa_specs()),
                          out_specs=pl.BlockSpec((W, 256), lambda n: (n, 0)),
                          out_shape=SDS((S, 256), F32))(z, z, z, z, z, sinks)


def swa_bwd(z, sinks, do, name):
    steps = S // W

    def body(q_ref, kp_ref, kc_ref, vp_ref, vc_ref, sk_ref, do_ref, dq_ref, dk_ref, dv_ref, dsk_ref, dk_acc, dv_acc):
        n = pl.program_id(0)
        first = n == 0
        _, vjp = jax.vjp(functools.partial(_swa_block, first), *_swa_args(q_ref, kp_ref, kc_ref, vp_ref, vc_ref, sk_ref))
        dq0, dq1, dkp, dkc, dvp, dvc, d0, d1, d2, d3 = vjp((do_ref[:, 0:128], do_ref[:, 128:256]))
        dq_ref[:, 0:128] = dq0.astype(BF16)
        dq_ref[:, 128:256] = dq1.astype(BF16)

        @pl.when(first)
        def _():
            dk_acc[...] = jnp.zeros_like(dk_acc)
            dv_acc[...] = jnp.zeros_like(dv_acc)

        cur = pl.ds(pl.multiple_of(n * W, W), W)
        dk_acc[cur, :] += dkc
        dv_acc[cur, :] += dvc

        @pl.when(n > 0)
        def _():
            prev = pl.ds(pl.multiple_of((n - 1) * W, W), W)
            dk_acc[prev, :] += dkp
            dv_acc[prev, :] += dvp

        lane = lax.broadcasted_iota(jnp.int32, (1, 128), 1)
        dsk = jnp.zeros((1, 128), F32)
        for h, d in enumerate((d0, d1, d2, d3)):
            dsk = jnp.where(lane == h, d, dsk)
        _acc(dsk_ref, dsk, first)

        @pl.when(n == steps - 1)
        def _():
            dk_ref[...] = dk_acc[...].astype(BF16)
            dv_ref[...] = dv_acc[...].astype(BF16)

    blk = pl.BlockSpec((W, 256), lambda n: (n, 0))
    full = pl.BlockSpec((S, 128), lambda n: (0, 0))
    return pl.pallas_call(body, grid=(steps,), name=name, in_specs=list(_swa_specs()) + [blk],
                          out_specs=[blk, full, full, pl.BlockSpec((1, 128), lambda n: (0, 0))],
                          out_shape=[SDS((S, 256), BF16), SDS((S, 128), BF16), SDS((S, 128), BF16), SDS((1, 128), F32)],
                          scratch_shapes=[pltpu.VMEM((S, 128), F32), pltpu.VMEM((S, 128), F32)])(
                              z, z, z, z, z, sinks, do)


def _glu(a, g):
    return a * jax.nn.sigmoid(g)


def _cv1_specs():
    a = pl.BlockSpec((S, 128), lambda j: (0, Z_CONV_A // 128 + j))
    g = pl.BlockSpec((S, 128), lambda j: (0, Z_CONV_G // 128 + j))
    w = pl.BlockSpec((32, 128), lambda j: (0, j))
    b = pl.BlockSpec((1, 128), lambda j: (0, j))
    h = pl.BlockSpec((S, 128), lambda j: (0, j))
    return a, g, w, b, h


def conv_dw_fwd(z, cw, cb, name):
    def body(a_ref, g_ref, w_ref, b_ref, o_ref):
        hh = _glu(a_ref[...], g_ref[...])
        acc = jnp.zeros((S, 128), F32) + b_ref[...]
        for k in range(CONV_K):
            acc = acc + _shift_down(hh, CONV_K - 1 - k) * w_ref[k:k + 1, :]
        o_ref[...] = acc

    a, g, w, b, h = _cv1_specs()
    return pl.pallas_call(body, grid=(2,), name=name, in_specs=[a, g, w, b], out_specs=h,
                          out_shape=SDS((S, 256), F32))(z, z, cw, cb)


def conv_dw_bwd(z, cw, dhc, name):
    def body(a_ref, g_ref, w_ref, dh_ref, da_ref, dg_ref, dw_ref, db_ref):
        hh, vjp = jax.vjp(_glu, a_ref[...], g_ref[...])
        dh = dh_ref[...]
        dhh = jnp.zeros((S, 128), F32)
        for k in range(CONV_K):
            n = CONV_K - 1 - k
            dhh = dhh + _shift_up(dh, n) * w_ref[k:k + 1, :]
            dw_ref[k:k + 1, :] = jnp.sum(dh * _shift_down(hh, n), axis=0, keepdims=True)
        dw_ref[CONV_K:32, :] = jnp.zeros((32 - CONV_K, 128), F32)
        db_ref[...] = jnp.sum(dh, axis=0, keepdims=True)
        da, dg = vjp(dhh)
        da_ref[...] = da.astype(BF16)
        dg_ref[...] = dg.astype(BF16)

    a, g, w, b, h = _cv1_specs()
    return pl.pallas_call(body, grid=(2,), name=name, in_specs=[a, g, w, h], out_specs=[h, h, w, b],
                          out_shape=[SDS((S, 256), BF16), SDS((S, 256), BF16), SDS((32, 256), F32), SDS((1, 256), F32)])(
                              z, z, cw, dhc)


def _ln(x, g, b):
    mu = jnp.mean(x, axis=-1, keepdims=True)
    xc = x - mu
    var = jnp.mean(xc * xc, axis=-1, keepdims=True)
    return xc * lax.rsqrt(var + EPS) * g + b


def _conv_pw(hc, lg, lb, pw, pb):
    y = jax.nn.silu(_ln(hc, lg, lb))
    return _nn(y.astype(BF16), pw.astype(BF16)) + pb


TS2 = 512
_ROW2 = pl.BlockSpec((TS2, 256), lambda i: (i, 0))
_VEC2 = pl.BlockSpec((1, 256), lambda i: (0, 0))
_MAT2 = pl.BlockSpec((256, 256), lambda i: (0, 0))


def conv_pw_fwd(hc, lg, lb, pw, pb, name):
    def body(h_ref, lg_ref, lb_ref, pw_ref, pb_ref, o_ref):
        o_ref[...] = _conv_pw(h_ref[...], lg_ref[...], lb_ref[...], pw_ref[...], pb_ref[...])

    return pl.pallas_call(body, grid=(S // TS2,), name=name, in_specs=[_ROW2, _VEC2, _VEC2, _MAT2, _VEC2],
                          out_specs=_ROW2, out_shape=SDS((S, 256), F32))(hc, lg, lb, pw, pb)


def conv_pw_bwd(hc, lg, lb, pw, pb, dy, name):
    def body(h_ref, lg_ref, lb_ref, pw_ref, pb_ref, dy_ref, dh_ref, dlg_ref, dlb_ref, dpw_ref, dpb_ref):
        _, vjp = jax.vjp(_conv_pw, h_ref[...], lg_ref[...], lb_ref[...], pw_ref[...], pb_ref[...])
        dh, dlg, dlb, dpw, dpb = vjp(dy_ref[...])
        dh_ref[...] = dh
        first = pl.program_id(0) == 0
        _acc(dlg_ref, dlg, first)
        _acc(dlb_ref, dlb, first)
        _acc(dpw_ref, dpw, first)
        _acc(dpb_ref, dpb, first)

    vec = SDS((1, 256), F32)
    return pl.pallas_call(body, grid=(S // TS2,), name=name, in_specs=[_ROW2, _VEC2, _VEC2, _MAT2, _VEC2, _ROW2],
                          out_specs=[_ROW2, _VEC2, _VEC2, _MAT2, _VEC2],
                          out_shape=[SDS((S, 256), F32), vec, vec, SDS((256, 256), F32), vec])(hc, lg, lb, pw, pb, dy)


def _sgu_block(u0, u1, v0, v1, lg0, lg1, lb0, lb1, w0, w1, w2, w3, bt):
    u0, u1, v0, v1 = (jax.nn.gelu(a) for a in (u0, u1, v0, v1))
    mu = (jnp.sum(v0, axis=1, keepdims=True) + jnp.sum(v1, axis=1, keepdims=True)) * (1.0 / GW)
    c0, c1 = v0 - mu, v1 - mu
    var = (jnp.sum(c0 * c0, axis=1, keepdims=True) + jnp.sum(c1 * c1, axis=1, keepdims=True)) * (1.0 / GW)
    r = lax.rsqrt(var + EPS)
    n0 = c0 * r * lg0 + lb0
    n1 = c1 * r * lg1 + lb1
    row = lax.broadcasted_iota(jnp.int32, (128, 128), 0)
    col = lax.broadcasted_iota(jnp.int32, (128, 128), 1)
    tri = row >= col
    outs = []
    for p, (n, u, wa, wb) in enumerate(((n0, u0, w0, w1), (n1, u1, w2, w3))):
        nb = n.astype(BF16)
        ma = _nn(jnp.where(tri, wa, 0.0).astype(BF16), nb)
        mb = _nn(jnp.where(tri, wb, 0.0).astype(BF16), nb)
        expand = (row == 2 * p + col // 64).astype(F32)
        bias = jnp.dot(bt, expand, precision=lax.Precision.HIGHEST, preferred_element_type=F32)
        outs.append(u * (jnp.where(col < 64, ma, mb) + bias))
    return outs[0], outs[1]


def _sgu_specs():
    def col(c):
        return pl.BlockSpec((128, 128), lambda n, c=c: (n, c))
    zs = [col(Z_SGU_U // 128), col(Z_SGU_U // 128 + 1), col(Z_SGU_V // 128), col(Z_SGU_V // 128 + 1)]
    vec = [pl.BlockSpec((1, 128), lambda n: (0, 0)), pl.BlockSpec((1, 128), lambda n: (0, 1))]
    ws = [pl.BlockSpec((None, 128, 128), lambda n, g=g: (g, 0, 0)) for g in range(4)]
    bt = pl.BlockSpec((128, 128), lambda n: (0, 0))
    return zs + vec + vec + ws + [bt]


def sgu_fwd(z, lg, lb, w, bt, name):
    def body(*refs):
        o_ref = refs[-1]
        y0, y1 = _sgu_block(*[r[...] for r in refs[:-1]])
        o_ref[:, 0:128] = y0
        o_ref[:, 128:256] = y1

    return pl.pallas_call(body, grid=(S // 128,), name=name, in_specs=_sgu_specs(),
                          out_specs=pl.BlockSpec((128, 256), lambda n: (n, 0)),
                          out_shape=SDS((S, 256), F32))(z, z, z, z, lg, lg, lb, lb, w, w, w, w, bt)


def sgu_bwd(z, lg, lb, w, bt, dy, name):
    def body(*refs):
        ins, dy_ref = refs[:13], refs[13]
        du_ref, dv_ref, dlg_ref, dlb_ref, dw_ref, dbt_ref = refs[14:]
        _, vjp = jax.vjp(_sgu_block, *[r[...] for r in ins])
        du0, du1, dv0, dv1, dlg0, dlg1, dlb0, dlb1, dw0, dw1, dw2, dw3, dbt = vjp((dy_ref[:, 0:128], dy_ref[:, 128:256]))
        du_ref[:, 0:128] = du0.astype(BF16)
        du_ref[:, 128:256] = du1.astype(BF16)
        dv_ref[:, 0:128] = dv0.astype(BF16)
        dv_ref[:, 128:256] = dv1.astype(BF16)
        first = pl.program_id(0) == 0

        @pl.when(first)
        def _():
            dlg_ref[...] = jnp.zeros_like(dlg_ref)
            dlb_ref[...] = jnp.zeros_like(dlb_ref)
            dw_ref[...] = jnp.zeros_like(dw_ref)
            dbt_ref[...] = jnp.zeros_like(dbt_ref)

        dlg_ref[:, 0:128] += dlg0
        dlg_ref[:, 128:256] += dlg1
        dlb_ref[:, 0:128] += dlb0
        dlb_ref[:, 128:256] += dlb1
        for g, d in enumerate((dw0, dw1, dw2, dw3)):
            dw_ref[g] += d
        dbt_ref[...] += dbt

    blk = pl.BlockSpec((128, 256), lambda n: (n, 0))
    vec = pl.BlockSpec((1, 256), lambda n: (0, 0))
    return pl.pallas_call(body, grid=(S // 128,), name=name, in_specs=_sgu_specs() + [blk],
                          out_specs=[blk, blk, vec, vec, pl.BlockSpec((4, 128, 128), lambda n: (0, 0, 0)),
                                     pl.BlockSpec((128, 128), lambda n: (0, 0))],
                          out_shape=[SDS((S, 256), BF16), SDS((S, 256), BF16), SDS((1, 256), F32), SDS((1, 256), F32),
                                     SDS((4, 128, 128), F32), SDS((128, 128), F32)])(
                                         z, z, z, z, lg, lg, lb, lb, w, w, w, w, bt, dy)


def _group_norm(y0, y1, y2, y3, g0, g1, g2, g3):
    return tuple(_rms(y, g) for y, g in zip((y0, y1, y2, y3), (g0, g1, g2, g3)))


_GROW = pl.BlockSpec((TS2, 256), lambda i: (i, 0))
_GCAT = pl.BlockSpec((TS2, D), lambda i: (i, 0))
_GVEC = [pl.BlockSpec((1, 256), lambda i, j=j: (0, j)) for j in range(4)]


def group_norm_fwd(ys, gg, name):
    def body(*refs):
        o_ref = refs[-1]
        outs = _group_norm(*[r[...] for r in refs[:-1]])
        for j, c in enumerate(outs):
            o_ref[:, 256 * j:256 * (j + 1)] = c.astype(BF16)

    return pl.pallas_call(body, grid=(S // TS2,), name=name, in_specs=[_GROW] * 4 + _GVEC, out_specs=_GCAT,
                          out_shape=SDS((S, D), BF16))(*ys, gg, gg, gg, gg)


def group_norm_bwd(ys, gg, dcat, name):
    def body(*refs):
        ins, dc_ref = refs[:8], refs[8]
        dy_refs, dg_ref = refs[9:13], refs[13]
        _, vjp = jax.vjp(_group_norm, *[r[...] for r in ins])
        grads = vjp(tuple(dc_ref[:, 256 * j:256 * (j + 1)] for j in range(4)))
        first = pl.program_id(0) == 0

        @pl.when(first)
        def _():
            dg_ref[...] = jnp.zeros_like(dg_ref)

        for j in range(4):
            dy_refs[j][...] = grads[j]
            dg_ref[:, 256 * j:256 * (j + 1)] += grads[4 + j]

    return pl.pallas_call(body, grid=(S // TS2,), name=name, in_specs=[_GROW] * 4 + _GVEC + [_GCAT],
                          out_specs=[_GROW] * 4 + [pl.BlockSpec((1, D), lambda i: (0, 0))],
                          out_shape=[SDS((S, 256), F32)] * 4 + [SDS((1, D), F32)])(*ys, gg, gg, gg, gg, dcat)


FB = 256
N_FB = DFF // FB


def _ffn_gate(ug, uv, wg0, wg1, wg2, wv0, wv1, wv2, bg, bv):
    cg = bg + _shift_down(ug, 2) * wg0 + _shift_down(ug, 1) * wg1 + ug * wg2
    cv = bv + _shift_down(uv, 2) * wv0 + _shift_down(uv, 1) * wv1 + uv * wv2
    return jax.nn.silu(cg) * cv


def _gate_specs():
    ug = pl.BlockSpec((S, FB), lambda j: (0, j))
    uv = pl.BlockSpec((S, FB), lambda j: (0, j + N_FB))
    wg = pl.BlockSpec((8, FB), lambda j: (0, j))
    wv = pl.BlockSpec((8, FB), lambda j: (0, j + N_FB))
    bg = pl.BlockSpec((1, FB), lambda j: (0, j))
    bv = pl.BlockSpec((1, FB), lambda j: (0, j + N_FB))
    return ug, uv, wg, wv, bg, bv


def _gate_args(ug_ref, uv_ref, wg_ref, wv_ref, bg_ref, bv_ref):
    return (ug_ref[...], uv_ref[...], wg_ref[0:1, :], wg_ref[1:2, :], wg_ref[2:3, :],
            wv_ref[0:1, :], wv_ref[1:2, :], wv_ref[2:3, :], bg_ref[...], bv_ref[...])


def ffn_gate_fwd(u, cw, cb, name):
    def body(ug_ref, uv_ref, wg_ref, wv_ref, bg_ref, bv_ref, o_ref):
        o_ref[...] = _ffn_gate(*_gate_args(ug_ref, uv_ref, wg_ref, wv_ref, bg_ref, bv_ref)).astype(BF16)

    return pl.pallas_call(body, grid=(N_FB,), name=name, in_specs=list(_gate_specs()),
                          out_specs=pl.BlockSpec((S, FB), lambda j: (0, j)),
                          out_shape=SDS((S, DFF), BF16))(u, u, cw, cw, cb, cb)


def ffn_gate_bwd(u, cw, cb, da, name):
    def body(ug_ref, uv_ref, wg_ref, wv_ref, bg_ref, bv_ref, da_ref, dug_ref, duv_ref, dwg_ref, dwv_ref, dbg_ref, dbv_ref):
        _, vjp = jax.vjp(_ffn_gate, *_gate_args(ug_ref, uv_ref, wg_ref, wv_ref, bg_ref, bv_ref))
        dug, duv, g0, g1, g2, v0, v1, v2, dbg, dbv = vjp(da_ref[...])
        dug_ref[...] = dug.astype(BF16)
        duv_ref[...] = duv.astype(BF16)
        for k, (a, b) in enumerate(((g0, v0), (g1, v1), (g2, v2))):
            dwg_ref[k:k + 1, :] = a
            dwv_ref[k:k + 1, :] = b
        dwg_ref[FFN_K:8, :] = jnp.zeros((8 - FFN_K, FB), F32)
        dwv_ref[FFN_K:8, :] = jnp.zeros((8 - FFN_K, FB), F32)
        dbg_ref[...] = dbg
        dbv_ref[...] = dbv

    ug, uv, wg, wv, bg, bv = _gate_specs()
    half = pl.BlockSpec((S, FB), lambda j: (0, j))
    whalf = pl.BlockSpec((8, FB), lambda j: (0, j))
    bhalf = pl.BlockSpec((1, FB), lambda j: (0, j))
    return pl.pallas_call(body, grid=(N_FB,), name=name, in_specs=[ug, uv, wg, wv, bg, bv, half],
                          out_specs=[half, half, whalf, whalf, bhalf, bhalf],
                          out_shape=[SDS((S, DFF), BF16), SDS((S, DFF), BF16), SDS((8, DFF), F32), SDS((8, DFF), F32),
                                     SDS((1, DFF), F32), SDS((1, DFF), F32)])(u, u, cw, cw, cb, cb, da)


def _adamw(w, g, m, v):
    m = ADAM_B1 * m + (1.0 - ADAM_B1) * g
    v = ADAM_B2 * v + (1.0 - ADAM_B2) * (g * g)
    m_hat = m / (1.0 - ADAM_B1 ** ADAM_STEP)
    v_hat = v / (1.0 - ADAM_B2 ** ADAM_STEP)
    delta = -ADAM_LR * (m_hat / (jnp.sqrt(v_hat) + ADAM_EPS) + ADAM_WD * w)
    return delta, m, v


def sum_pieces(r, layer, base, name):
    n, rows, cols = r.shape
    tr = _row_tile(rows, cols)

    def body(r_ref, *rest):
        o_ref = rest[-1]
        acc = r_ref[0].astype(F32)
        for j in range(1, n):
            acc = acc + r_ref[j].astype(F32)
        o_ref[...] = acc

    extra = {} if base is None else dict(input_output_aliases={1: 0})
    return pl.pallas_call(body, grid=(rows // tr,), name=name,
                          in_specs=[pl.BlockSpec((n, tr, cols), lambda i: (0, i, 0))] + ([] if base is None else [ANY]),
                          out_specs=pl.BlockSpec((None, tr, cols), lambda i: (layer, i, 0)),
                          out_shape=SDS((DEPTH, rows, cols), F32), **extra)(*([r] if base is None else [r, base]))


def adamw_pair(w, p, q, m, v, name):
    rows, cols = w.shape
    tr = _row_tile(rows, cols)

    def body(w_ref, p_ref, q_ref, m_ref, v_ref, g_ref, d_ref, nm_ref, nv_ref):
        g = p_ref[...] + q_ref[...]
        g_ref[...] = g
        d_ref[...], nm_ref[...], nv_ref[...] = _adamw(w_ref[...], g, m_ref[...], v_ref[...])

    spec = pl.BlockSpec((tr, cols), lambda i: (i, 0))
    return pl.pallas_call(body, grid=(rows // tr,), name=name, in_specs=[spec] * 5, out_specs=[spec] * 4,
                          out_shape=[SDS((rows, cols), F32)] * 4)(w, p, q, m, v)


def adamw_gathered(w, gall, m, v, name):
    rows = w.shape[0]
    tr = rows // 2 if rows % 16 == 0 else rows

    def body(w_ref, ga_ref, m_ref, v_ref, g_ref, d_ref, nm_ref, nv_ref):
        g = ga_ref[0]
        for j in range(1, N_DEV):
            g = g + ga_ref[j]
        g_ref[...] = g
        d_ref[...], nm_ref[...], nv_ref[...] = _adamw(w_ref[...], g, m_ref[...], v_ref[...])

    spec = pl.BlockSpec((tr, 128), lambda i: (i, 0))
    return pl.pallas_call(body, grid=(rows // tr,), name=name,
                          in_specs=[spec, pl.BlockSpec((N_DEV, tr, 128), lambda i: (0, i, 0)), spec, spec],
                          out_specs=[spec] * 4, out_shape=[SDS((rows, 128), F32)] * 4)(w, gall, m, v)


ADA_COLS = 6 * D // N_CHIP
ADA_TN = 512


def ada_mod(c_all, w, b, name):
    def body(c_ref, w_ref, b_ref, o_ref):
        ca = jax.nn.silu(c_ref[...])
        o_ref[...] = jnp.dot(ca, w_ref[...], precision=lax.Precision.HIGHEST, preferred_element_type=F32) + b_ref[...]

    return pl.pallas_call(
        body, grid=(DEPTH, ADA_COLS // ADA_TN), name=name,
        in_specs=[pl.BlockSpec((N_DEV, D), lambda l, j: (0, 0)),
                  pl.BlockSpec((None, D, ADA_TN), lambda l, j: (l, 0, j)),
                  pl.BlockSpec((None, 1, ADA_TN), lambda l, j: (l, 0, j))],
        out_specs=pl.BlockSpec((None, N_DEV, ADA_TN), lambda l, j: (l, 0, j)),
        out_shape=SDS((DEPTH, N_DEV, ADA_COLS), F32))(c_all, w, b)


def ada_update(c_all_t, dmod, w, m, v, name):
    def body(c_ref, dm_ref, w_ref, m_ref, v_ref, g_ref, d_ref, nm_ref, nv_ref):
        ca = jax.nn.silu(c_ref[...])
        g = jnp.dot(ca, dm_ref[...], precision=lax.Precision.HIGHEST, preferred_element_type=F32)
        g_ref[...] = g
        d_ref[...], nm_ref[...], nv_ref[...] = _adamw(w_ref[...], g, m_ref[...], v_ref[...])

    wspec = pl.BlockSpec((None, D, ADA_TN), lambda l, j: (l, 0, j))
    return pl.pallas_call(
        body, grid=(DEPTH, ADA_COLS // ADA_TN), name=name,
        in_specs=[pl.BlockSpec((D, N_DEV), lambda l, j: (0, 0)),
                  pl.BlockSpec((None, N_DEV, ADA_TN), lambda l, j: (l, 0, j)), wspec, wspec, wspec],
        out_specs=[wspec] * 4, out_shape=[SDS((DEPTH, D, ADA_COLS), F32)] * 4)(c_all_t, dmod, w, m, v)


_CHIP_FLIPS = ((1, 0), (0, 1), (1, 1))
_DEV_FLIPS = tuple((a, b, c) for a in (0, 1) for b in (0, 1) for c in (0, 1) if (a, b, c) != (0, 0, 0))


def _position():
    return lax.axis_index("x"), lax.axis_index("y"), lax.axis_index("c")


def _hbm_call(body, arrs, out_shapes, n_remote, name):
    n = len(arrs)
    return pl.pallas_call(
        body, name=name, in_specs=[ANY] * n, out_specs=[ANY] * n, out_shape=out_shapes,
        scratch_shapes=[pltpu.SemaphoreType.DMA((n, n_remote)), pltpu.SemaphoreType.DMA((n, n_remote)),
                        pltpu.SemaphoreType.DMA((n,))])(*arrs)


_HBM = pl.BlockSpec(memory_space=pltpu.HBM)
_SEM = pl.BlockSpec(memory_space=pltpu.SEMAPHORE)
_EFFECT = pltpu.SideEffectType.DATAFLOW_SIDE_EFFECTING


def _chip_copies(src, land, send, recv, scatter):
    x, y, c = _position()
    me = 2 * x + y
    out = []
    for k, (fx, fy) in enumerate(_CHIP_FLIPS):
        tx, ty = x ^ fx, y ^ fy
        peer = 2 * tx + ty
        start = pltpu.make_async_remote_copy(
            src_ref=src.at[peer] if scatter else src, dst_ref=land.at[me], send_sem=send.at[k], recv_sem=recv.at[k],
            device_id=(tx, ty, c), device_id_type=MESH)
        arrival = pltpu.make_async_remote_copy(
            src_ref=src.at[peer] if scatter else src, dst_ref=land.at[peer], send_sem=send.at[k], recv_sem=recv.at[k],
            device_id=(tx, ty, c), device_id_type=MESH)
        out.append((start, arrival))
    return out


def exchange_start(srcs, lands, scatter, name):
    n = len(srcs)

    def body(*refs):
        src, land = refs[:n], refs[n:2 * n]
        send, recv = refs[2 * n:3 * n], refs[3 * n:4 * n]
        token = refs[-1]
        for a in range(n):
            for start, _ in _chip_copies(src[a], land[a], send[a], recv[a], scatter):
                start.start()
        token[...] = jnp.zeros_like(token)

    bufs = list(srcs) + list(lands)
    outs = pl.pallas_call(
        body, name=name, in_specs=[_HBM] * (2 * n),
        out_specs=[_SEM] * (2 * n) + [_HBM] * (2 * n) + [pl.BlockSpec(memory_space=pltpu.VMEM)],
        out_shape=[pltpu.SemaphoreType.DMA((3,))] * (2 * n) + [pltpu.HBM(a.shape, a.dtype) for a in bufs]
        + [SDS((8, 128), F32)],
        input_output_aliases={i: 2 * n + i for i in range(2 * n)},
        compiler_params=pltpu.CompilerParams(has_side_effects=_EFFECT),
    )(*[pltpu.with_memory_space_constraint(a, pltpu.HBM) for a in bufs])
    flights = [(outs[a], outs[n + a], outs[2 * n + a], outs[3 * n + a]) for a in range(n)]
    return flights, outs[-1]


def exchange_wait(flights, scatter, after, name):
    n = len(flights)

    def body(*refs):
        src, land = refs[:n], refs[n:2 * n]
        send, recv = refs[2 * n:3 * n], refs[3 * n:4 * n]
        for a in range(n):
            for _, arrival in _chip_copies(src[a], land[a], send[a], recv[a], scatter):
                arrival.wait_send()
                arrival.wait_recv()

    bufs = [f[2] for f in flights] + [f[3] for f in flights]
    sems = [f[0] for f in flights] + [f[1] for f in flights]
    outs = pl.pallas_call(
        body, name=name, in_specs=[_HBM] * (2 * n) + [_SEM] * (2 * n) + [ANY], out_specs=[_HBM] * (2 * n),
        out_shape=[pltpu.HBM(a.shape, a.dtype) for a in bufs],
        input_output_aliases={i: i for i in range(2 * n)},
        compiler_params=pltpu.CompilerParams(has_side_effects=_EFFECT),
    )(*bufs, *sems, after)
    return [(outs[a], outs[n + a]) for a in range(n)]


def chip_alltoall(arrs, name):
    n = len(arrs)

    def body(*refs):
        ins, outs = refs[:n], refs[n:2 * n]
        send, recv, loc = refs[2 * n:]
        x, y, c = _position()
        me = 2 * x + y
        copies = []
        for a in range(n):
            local = pltpu.make_async_copy(ins[a].at[me], outs[a].at[me], loc.at[a])
            local.start()
            copies.append(local)
            for k, (fx, fy) in enumerate(_CHIP_FLIPS):
                tx, ty = x ^ fx, y ^ fy
                cp = pltpu.make_async_remote_copy(
                    src_ref=ins[a].at[2 * tx + ty], dst_ref=outs[a].at[me], send_sem=send.at[a, k],
                    recv_sem=recv.at[a, k], device_id=(tx, ty, c), device_id_type=MESH)
                cp.start()
                copies.append(cp)
        for cp in copies:
            cp.wait()

    shapes = [SDS(a.shape, a.dtype) for a in arrs]
    return _hbm_call(body, arrs, shapes, 3, name)


def sibling_swap(arrs, name):
    n = len(arrs)

    def body(*refs):
        ins, outs = refs[:n], refs[n:2 * n]
        send, recv, _ = refs[2 * n:]
        x, y, c = _position()
        copies = []
        for a in range(n):
            cp = pltpu.make_async_remote_copy(
                src_ref=ins[a], dst_ref=outs[a], send_sem=send.at[a, 0], recv_sem=recv.at[a, 0],
                device_id=(x, y, 1 - c), device_id_type=MESH)
            cp.start()
            copies.append(cp)
        for cp in copies:
            cp.wait()

    shapes = [SDS(a.shape, a.dtype) for a in arrs]
    return _hbm_call(body, arrs, shapes, 1, name)


def device_allgather(arrs, name):
    n = len(arrs)

    def body(*refs):
        ins, outs = refs[:n], refs[n:2 * n]
        send, recv, loc = refs[2 * n:]
        x, y, c = _position()
        me = 4 * x + 2 * y + c
        copies = []
        for a in range(n):
            local = pltpu.make_async_copy(ins[a], outs[a].at[me], loc.at[a])
            local.start()
            copies.append(local)
            for k, (fx, fy, fc) in enumerate(_DEV_FLIPS):
                cp = pltpu.make_async_remote_copy(
                    src_ref=ins[a], dst_ref=outs[a].at[me], send_sem=send.at[a, k], recv_sem=recv.at[a, k],
                    device_id=(x ^ fx, y ^ fy, c ^ fc), device_id_type=MESH)
                cp.start()
                copies.append(cp)
        for cp in copies:
            cp.wait()

    shapes = [SDS((N_DEV,) + a.shape, a.dtype) for a in arrs]
    return _hbm_call(body, arrs, shapes, 7, name)


def _pad_to(a, axis, size):
    pad = [(0, 0)] * a.ndim
    pad[axis] = (0, size - a.shape[axis])
    return jnp.pad(a, pad)


def _cols_to_z(w):
    return _pad_to(jnp.concatenate([w[..., :768], w[..., 772:], w[..., 768:772]], axis=-1), w.ndim - 1, ZC)


def _cols_from_z(w):
    return jnp.concatenate([w[..., :768], w[..., 2304:2308], w[..., 768:2304]], axis=-1)


def _size(shape):
    n = 1
    for d in shape:
        n *= d
    return n


def _pack(parts):
    rows = []
    for p in parts:
        n = -(-p.size // 128)
        rows.append(_pad_to(p.reshape(-1), 0, n * 128).reshape(n, 128))
    packed = jnp.concatenate(rows, axis=0)
    return _pad_to(packed, 0, -(-packed.shape[0] // 16) * 16)


def _unpack(packed, shapes):
    out, off = [], 0
    for s in shapes:
        n = -(-_size(s) // 128)
        out.append(packed[off:off + n].reshape(-1)[:_size(s)].reshape(s))
        off += n
    return out


def _layer_fwd(l, x0, mod, p, fetch):
    sh1, sc1, ga1, sh2, sc2, ga2 = mod
    t = f"l{l}_"
    h1 = modnorm_fwd(x0, p['g_pre_mix'], sc1, sh1, t + "modnorm1")
    wt = dict(fetch('w_in', h1))
    z = mm_nn(h1, wt['w_in'], F32, t + "proj_in")
    cumc, cumr = fgate_fwd(z, p['b_fgate'], t + "fgate")
    y_fox, lse = fox_fwd(z, cumc, cumr, t + "fox")
    hc = conv_dw_fwd(z, wt['conv_w'], p['conv_b'], t + "conv_dw")
    y_conv = conv_pw_fwd(hc, p['conv_ln_g'], p['conv_ln_b'], wt['conv_pw_w'], p['conv_pw_b'], t + "conv_pw")
    y_swa = swa_fwd(z, p['swa_sinks'], t + "swa")
    y_sgu = sgu_fwd(z, p['sgu_ln_g'], p['sgu_ln_b'], p['sgu_w'], p['sgu_bt'], t + "sgu")
    ys = (y_fox, y_conv, y_swa, y_sgu)
    ycat = group_norm_fwd(ys, p['g_group'], t + "group_norm")
    wt.update(fetch('w_out', ycat))
    ymix = mm_nn(ycat, wt['w_out'], F32, t + "proj_out")
    x1 = resid_fwd(x0, ymix, ga1, p['g_post_mix'], t + "resid1")
    h2 = modnorm_fwd(x1, p['g_pre_ffn'], sc2, sh2, t + "modnorm2")
    wt.update(fetch('ffn_w_up', h2))
    u = mm_nn(h2, wt['ffn_w_up'], F32, t + "ffn_up")
    act = ffn_gate_fwd(u, wt['ffn_conv_w'], p['ffn_conv_b'], t + "ffn_gate")
    wt.update(fetch('ffn_w_down', act))
    yffn = mm_nn(act, wt['ffn_w_down'], F32, t + "ffn_down")
    x2 = resid_fwd(x1, yffn, ga2, p['g_post_ffn'], t + "resid2")
    res = dict(x0=x0, h1=h1, z=z, cumc=cumc, cumr=cumr, y_fox=y_fox, lse=lse, hc=hc, ys=ys, ycat=ycat, ymix=ymix,
               x1=x1, h2=h2, u=u, act=act, yffn=yffn, wt=wt)
    return x2, res


def _layer_bwd(l, dx2, mod, p, r, emit):
    sh1, sc1, ga1, sh2, sc2, ga2 = mod
    t = f"l{l}_bwd_"
    g = {}
    wt = r['wt']
    dyffn, dga2, g['g_post_ffn'] = resid_bwd(r['yffn'], ga2, p['g_post_ffn'], dx2, t + "resid2")
    tok = emit({'ffn_w_down': mm_tn(r['act'], dyffn, t + "ffn_down_dw")})
    dact = mm_nt(dyffn, wt['ffn_w_down'], t + "ffn_down_dx")
    dug, duv, dwg, dwv, dbg, dbv = ffn_gate_bwd(r['u'], wt['ffn_conv_w'], p['ffn_conv_b'] + tok, dact, t + "ffn_gate")
    du = jnp.concatenate([dug, duv], axis=1)
    g['ffn_conv_b'] = jnp.concatenate([dbg, dbv], axis=1)
    tok = emit({'ffn_w_up': mm_tn(r['h2'], du, t + "ffn_up_dw", sharded=True)})
    dh2 = mm_nt(du, wt['ffn_w_up'], t + "ffn_up_dx")
    dx1, g['g_pre_ffn'], dsc2, dsh2 = modnorm_bwd(r['x1'], p['g_pre_ffn'] + tok, sc2, sh2, dh2, dx2, t + "modnorm2")
    dymix, dga1, g['g_post_mix'] = resid_bwd(r['ymix'], ga1, p['g_post_mix'], dx1, t + "resid1")
    tok = emit({'w_out': mm_tn(r['ycat'], dymix, t + "proj_out_dw")})
    dcat = mm_nt(dymix, wt['w_out'], t + "proj_out_dx")
    dy_fox, dy_conv, dy_swa, dy_sgu, g['g_group'] = group_norm_bwd(r['ys'], p['g_group'] + tok, dcat, t + "group_norm")
    z = r['z']
    fq, fk, fv, dcq, dck = fox_bwd(z, r['cumc'], r['cumr'], r['lse'], r['y_fox'], dy_fox, t + "fox")
    dzf, g['b_fgate'] = fgate_bwd(z, p['b_fgate'], dcq, dck, t + "fgate")
    dhc, g['conv_ln_g'], g['conv_ln_b'], dpw, g['conv_pw_b'] = conv_pw_bwd(
        r['hc'], p['conv_ln_g'], p['conv_ln_b'], wt['conv_pw_w'], p['conv_pw_b'], dy_conv, t + "conv_pw")
    ca, cg, dcw, g['conv_b'] = conv_dw_bwd(z, wt['conv_w'], dhc, t + "conv_dw")
    sq, sk, sv, g['swa_sinks'] = swa_bwd(z, p['swa_sinks'], dy_swa, t + "swa")
    gu, gv, g['sgu_ln_g'], g['sgu_ln_b'], g['sgu_w'], g['sgu_bt'] = sgu_bwd(
        z, p['sgu_ln_g'], p['sgu_ln_b'], p['sgu_w'], p['sgu_bt'], dy_sgu, t + "sgu")
    dz = jnp.concatenate([fq, fk, fv, ca, cg, sq, sk, sv, gu, gv, dzf], axis=1)
    tok = emit({'w_in': mm_tn(r['h1'], dz, t + "proj_in_dw"), 'conv_w': dcw, 'conv_pw_w': dpw,
                'ffn_conv_w': jnp.concatenate([dwg, dwv], axis=1)})
    dh1 = mm_nt(dz, wt['w_in'], t + "proj_in_dx")
    dx0, g['g_pre_mix'], dsc1, dsh1 = modnorm_bwd(r['x0'], p['g_pre_mix'] + tok, sc1, sh1, dh1, dx1, t + "modnorm1")
    return dx0, g, (dsh1, dsc1, dga1, dsh2, dsc2, dga2)


def _layer_params(l, w):
    def row(name, width=None):
        v = w[name][l].reshape(1, -1)
        return v if width is None else _pad_to(v, 1, width)
    return {
        'g_pre_mix': row('g_pre_mix'), 'g_post_mix': row('g_post_mix'), 'g_pre_ffn': row('g_pre_ffn'),
        'g_post_ffn': row('g_post_ffn'), 'b_fgate': row('b_fgate', 128), 'conv_b': row('conv_b'),
        'conv_ln_g': row('conv_ln_g'), 'conv_ln_b': row('conv_ln_b'), 'conv_pw_b': row('conv_pw_b'),
        'swa_sinks': row('swa_sinks', 128), 'sgu_ln_g': row('sgu_ln_g'), 'sgu_ln_b': row('sgu_ln_b'),
        'sgu_w': w['sgu_w'][l], 'sgu_bt': _pad_to(w['sgu_b'][l].T, 1, 128), 'g_group': row('g_group'),
        'ffn_conv_b': row('ffn_conv_b'),
    }


def _w_in_from_shards(s):
    return _cols_to_z(jnp.transpose(s, (1, 0, 2)).reshape(D, IN_COLS))


def _w_in_to_shards(g):
    return jnp.transpose(_cols_from_z(g).reshape(D, N_CHIP, IN_COLS // N_CHIP), (1, 0, 2))


def _cols_from_shards(s, rows):
    _, r, n = s.shape
    return _pad_to(jnp.transpose(s, (1, 0, 2)).reshape(r, N_CHIP * n), 0, rows)


def _cols_to_shards(g, r):
    n = g.shape[1] // N_CHIP
    return jnp.transpose(g[:r].reshape(r, N_CHIP, n), (1, 0, 2))


_FROM_SHARDS = {
    'w_in': _w_in_from_shards,
    'w_out': lambda s: s.reshape(D, D),
    'ffn_w_up': lambda s: s,
    'ffn_w_down': lambda s: s.reshape(DFF, D),
    'conv_w': lambda s: _cols_from_shards(s, 32),
    'conv_pw_w': lambda s: s.reshape(GW, GW),
    'ffn_conv_w': lambda s: _cols_from_shards(s, 8),
}
_TO_SHARDS = {
    'w_in': _w_in_to_shards,
    'w_out': lambda g: g.reshape(N_CHIP, D // N_CHIP, D),
    'ffn_w_up': lambda g: g,
    'ffn_w_down': lambda g: g.reshape(N_CHIP, DFF // N_CHIP, D),
    'conv_w': lambda g: _cols_to_shards(g, CONV_K),
    'conv_pw_w': lambda g: g.reshape(N_CHIP, GW // N_CHIP, GW),
    'ffn_conv_w': lambda g: _cols_to_shards(g, FFN_K),
}


def _small_grads(g):
    return {
        'g_pre_mix': g['g_pre_mix'][0], 'g_post_mix': g['g_post_mix'][0], 'g_pre_ffn': g['g_pre_ffn'][0],
        'g_post_ffn': g['g_post_ffn'][0], 'b_fgate': g['b_fgate'][0, :4], 'conv_b': g['conv_b'][0],
        'conv_ln_g': g['conv_ln_g'][0], 'conv_ln_b': g['conv_ln_b'][0], 'conv_pw_b': g['conv_pw_b'][0],
        'swa_sinks': g['swa_sinks'][0, :4], 'sgu_ln_g': g['sgu_ln_g'][0], 'sgu_ln_b': g['sgu_ln_b'][0],
        'sgu_w': g['sgu_w'], 'sgu_b': g['sgu_bt'][:, :4].T, 'g_group': g['g_group'][0],
        'ffn_conv_b': g['ffn_conv_b'][0],
    }


def _local_step(xs, target, mod, w, fetch, emit):
    params, resids, mods = [], [], []
    for l in range(DEPTH):
        params.append(_layer_params(l, w))
        mods.append(tuple(mod[l, j] for j in range(6)))
        xs, r = _layer_fwd(l, xs, mods[l], params[l], functools.partial(fetch, l))
        resids.append(r)
    dx, loss_row = loss_grad(xs, target, "loss")
    grads, dmods = [None] * DEPTH, [None] * DEPTH
    for l in reversed(range(DEPTH)):
        dx, grads[l], dmods[l] = _layer_bwd(l, dx, mods[l], params[l], resids[l], functools.partial(emit, l))
    return loss_row, dx, grads, dmods


_MATMUL_WEIGHTS = ('w_in', 'w_out', 'ffn_w_up', 'ffn_w_down')
_CONV_WEIGHTS = ('conv_w', 'conv_pw_w', 'ffn_conv_w')
_FETCH_GROUPS = {'w_in': ('w_in',) + _CONV_WEIGHTS, 'w_out': ('w_out',), 'ffn_w_up': ('ffn_w_up',),
                 'ffn_w_down': ('ffn_w_down',)}


def kernel(x, c, w_ada, b_ada, g_pre_mix, g_post_mix, g_pre_ffn, g_post_ffn, w_in, b_fgate, conv_w, conv_b, conv_ln_g, conv_ln_b, conv_pw_w, conv_pw_b, swa_sinks, sgu_ln_g, sgu_ln_b, sgu_w, sgu_b, g_group, w_out, ffn_w_up, ffn_conv_w, ffn_conv_b, ffn_w_down, loss_target, m_w_ada, m_b_ada, m_g_pre_mix, m_g_post_mix, m_g_pre_ffn, m_g_post_ffn, m_w_in, m_b_fgate, m_conv_w, m_conv_b, m_conv_ln_g, m_conv_ln_b, m_conv_pw_w, m_conv_pw_b, m_swa_sinks, m_sgu_ln_g, m_sgu_ln_b, m_sgu_w, m_sgu_b, m_g_group, m_w_out, m_ffn_w_up, m_ffn_conv_w, m_ffn_conv_b, m_ffn_w_down, v_w_ada, v_b_ada, v_g_pre_mix, v_g_post_mix, v_g_pre_ffn, v_g_post_ffn, v_w_in, v_b_fgate, v_conv_w, v_conv_b, v_conv_ln_g, v_conv_ln_b, v_conv_pw_w, v_conv_pw_b, v_swa_sinks, v_sgu_ln_g, v_sgu_ln_b, v_sgu_w, v_sgu_b, v_g_group, v_w_out, v_ffn_w_up, v_ffn_conv_w, v_ffn_conv_b, v_ffn_w_down):
    args = locals()
    w = {n: args[n] for n in WEIGHTS}
    m = {n: args['m_' + n] for n in WEIGHTS}
    v = {n: args['v_' + n] for n in WEIGHTS}
    xi, yi, ci = _position()
    chip = 2 * xi + yi

    keys = [(n, l) for l in range(DEPTH) for n in _CONV_WEIGHTS]
    keys += [(n, l) for l in range(DEPTH) for n in _MATMUL_WEIGHTS]
    srcs = [w[n][l].astype(BF16) if n in _MATMUL_WEIGHTS else w[n][l] for n, l in keys]
    lands = [lax.dynamic_update_slice_in_dim(lax.empty((N_CHIP,) + s.shape, s.dtype), s[None], chip, axis=0)
             for s in srcs]
    flights, token = exchange_start(srcs, lands, False, "gather_start")
    gathering = dict(zip(keys, flights))

    def fetch(l, name, after):
        names = _FETCH_GROUPS[name]
        landed = exchange_wait([gathering[(n, l)] for n in names], False, after, f"gather_wait_l{l}_{name}")
        return {n: _FROM_SHARDS[n](land) for n, (_, land) in zip(names, landed)}

    scattering = {}

    def emit(l, grads):
        names = list(grads)
        pieces = [_TO_SHARDS[n](grads[n]) for n in names]
        lands = [lax.empty(p.shape, p.dtype) for p in pieces]
        started, token = exchange_start(pieces, lands, True, f"scatter_start_l{l}_{names[0]}")
        scattering.update({(n, l): f for n, f in zip(names, started)})
        return token[0:1, 0:1]

    (c_all,) = device_allgather([c + token[0:1, 0:1]], "gather_c")
    c_all = c_all.reshape(N_DEV, D)
    b_loc = lax.dynamic_slice_in_dim(b_ada, chip * ADA_COLS, ADA_COLS, axis=1).reshape(DEPTH, 1, ADA_COLS)
    mod_all = ada_mod(c_all, w_ada, b_loc, "ada_mod")
    mine = lax.dynamic_index_in_dim(mod_all.reshape(DEPTH, N_CHIP, 2, ADA_COLS), ci, axis=2, keepdims=False)
    (mod4,) = chip_alltoall([jnp.transpose(mine, (1, 0, 2))], "scatter_mod")
    mod = jnp.transpose(mod4, (1, 0, 2)).reshape(DEPTH, 6, 1, D)

    loss_row, dx, grads, dmods = _local_step(x.reshape(S, D), loss_target.reshape(S, D), mod, w, fetch, emit)
    loss = lax.psum(loss_row[0, 0], ("x", "y", "c"))
    grad_x = dx.reshape(1, S, D)

    small = [_small_grads(g) for g in grads]
    local = {n: jnp.stack([small[l][n] for l in range(DEPTH)]) for n in SMALL if n != 'b_ada'}
    local['b_ada'] = jnp.stack([jnp.concatenate(dmods[l], axis=1)[0] for l in range(DEPTH)])
    (g_all,) = device_allgather([_pack([local[n] for n in SMALL])], "gather_small_grads")
    shapes = [w[n].shape for n in SMALL]
    packed = adamw_gathered(_pack([w[n] for n in SMALL]), g_all, _pack([m[n] for n in SMALL]),
                            _pack([v[n] for n in SMALL]), "adamw_small")
    out = {n: tuple(a) for n, a in zip(SMALL, zip(*[_unpack(pk, shapes) for pk in packed]))}

    n_ada = DEPTH * 6 * D // 128
    dmod_all = g_all[:, :n_ada, :].reshape(N_DEV, DEPTH, 6 * D)
    dmod_loc = jnp.transpose(lax.dynamic_slice_in_dim(dmod_all, chip * ADA_COLS, ADA_COLS, axis=2), (1, 0, 2))
    out['w_ada'] = tuple(ada_update(c_all.T, dmod_loc, w_ada, m['w_ada'], v['w_ada'], "adamw_ada"))

    order = list(scattering)
    landed = dict(zip(order, exchange_wait([scattering[k] for k in order], True, dx, "scatter_wait")))
    part = {}
    for n in SHARDED:
        cols = w[n].shape[-1]
        for l in range(DEPTH):
            src, land = landed[(n, l)]
            own = lax.dynamic_index_in_dim(src, chip, axis=0, keepdims=True)
            recv = lax.dynamic_update_slice_in_dim(land, own, chip, axis=0)
            part[n] = sum_pieces(recv.reshape(N_CHIP, -1, cols), l, part.get(n), f"sum_{n}_l{l}")
        part[n] = part[n].reshape(-1, cols)
    other = dict(zip(SHARDED, sibling_swap([part[n] for n in SHARDED], "swap_partials")))
    for n in SHARDED:
        cols = w[n].shape[-1]
        res = adamw_pair(w[n].reshape(-1, cols), part[n], other[n], m[n].reshape(-1, cols), v[n].reshape(-1, cols),
                         "adamw_" + n)
        out[n] = tuple(a.reshape(w[n].shape) for a in res)

    return (loss, grad_x, *[out[n][0] for n in WEIGHTS], *[out[n][1] for n in WEIGHTS],
            *[out[n][2] for n in WEIGHTS], *[out[n][3] for n in WEIGHTS])
```

```python
import functools

import jax
import jax.numpy as jnp
from jax import lax
from jax.experimental import pallas as pl
from jax.experimental.pallas import tpu as pltpu

F32 = jnp.float32
BF16 = jnp.bfloat16
SDS = jax.ShapeDtypeStruct
MESH = pl.DeviceIdType.MESH
ANY = pl.BlockSpec(memory_space=pl.ANY)

DEPTH = 2
S = 2048
D = 1024
GW = 256
DFF = 2816
NUP = 2 * DFF
IN_COLS = 2308
ZC = 2432
CONV_K = 31
FFN_K = 3
EPS = 1e-6
SCALE = 0.125
NEG = -1e30
N_CHIP = 4
N_DEV = 8

Z_FOX_Q, Z_FOX_K, Z_FOX_V = 0, 256, 512
Z_CONV_A, Z_CONV_G = 768, 1024
Z_SWA_Q, Z_SWA_K, Z_SWA_V = 1280, 1536, 1664
Z_SGU_U, Z_SGU_V = 1792, 2048
Z_FG = 2304

ADAM_LR, ADAM_B1, ADAM_B2, ADAM_EPS, ADAM_WD, ADAM_STEP = 0.001, 0.9, 0.999, 1e-08, 0.01, 10

TS = 256
TM = 512
N_SPLIT = 2816
N_BLOCK = 1408

WEIGHTS = ['w_ada', 'b_ada', 'g_pre_mix', 'g_post_mix', 'g_pre_ffn', 'g_post_ffn', 'w_in', 'b_fgate', 'conv_w',
           'conv_b', 'conv_ln_g', 'conv_ln_b', 'conv_pw_w', 'conv_pw_b', 'swa_sinks', 'sgu_ln_g', 'sgu_ln_b',
           'sgu_w', 'sgu_b', 'g_group', 'w_out', 'ffn_w_up', 'ffn_conv_w', 'ffn_conv_b', 'ffn_w_down']
SHARDED = ['w_in', 'conv_w', 'conv_pw_w', 'w_out', 'ffn_w_up', 'ffn_conv_w', 'ffn_w_down']
SMALL = [n for n in WEIGHTS if n not in SHARDED and n != 'w_ada']


def _rms(x, g):
    return x * lax.rsqrt(jnp.mean(x * x, axis=-1, keepdims=True) + EPS) * g


def _modnorm(x, g, sc, sh):
    return _rms(x, g) * (1.0 + sc) + sh


def _resid(x, y, ga, g):
    return x + ga * _rms(y, g)


@functools.partial(jax.custom_vjp, nondiff_argnums=(1,))
def _shift_down(x, n):
    if n == 0:
        return x
    row = lax.broadcasted_iota(jnp.int32, x.shape, 0)
    return jnp.where(row >= n, pltpu.roll(x, n, axis=0), 0.0)


def _shift_up(x, n):
    if n == 0:
        return x
    rows = x.shape[0]
    row = lax.broadcasted_iota(jnp.int32, x.shape, 0)
    return jnp.where(row < rows - n, pltpu.roll(x, rows - n, axis=0), 0.0)


def _shift_down_fwd(x, n):
    return _shift_down(x, n), None


def _shift_down_bwd(n, _, ct):
    return (_shift_up(ct, n),)


_shift_down.defvjp(_shift_down_fwd, _shift_down_bwd)


def _nt(a, b):
    return lax.dot_general(a, b, (((1,), (1,)), ((), ())), preferred_element_type=F32)


def _tn(a, b):
    return lax.dot_general(a, b, (((0,), (0,)), ((), ())), preferred_element_type=F32)


def _nn(a, b):
    return jnp.dot(a, b, preferred_element_type=F32)


def _acc(ref, val, first):
    @pl.when(first)
    def _():
        ref[...] = val

    @pl.when(jnp.logical_not(first))
    def _():
        ref[...] += val


def _row_tile(rows, cols):
    limit = max(8, (1 << 20) // (4 * cols))
    best = None
    for t in range(8, rows + 1, 8):
        if rows % t == 0 and t <= limit:
            best = t
    return best if best is not None else rows


def _ncol(n):
    return n if n <= N_SPLIT else N_BLOCK


def _weight_spec(b, order):
    pick = (lambda j, i: j) if order == 0 else (lambda i, j: j)
    if b.ndim == 3:
        _, k, tn = b.shape
        return pl.BlockSpec((None, k, tn), lambda *g: (pick(*g), 0, 0)), k, N_CHIP * tn, tn
    k, n = b.shape
    tn = _ncol(n)
    return pl.BlockSpec((k, tn), lambda *g: (0, pick(*g))), k, n, tn


def mm_nn(a, b, out_dtype, name):
    m = a.shape[0]
    b_spec, k, n, tn = _weight_spec(b, 0)

    def body(a_ref, b_ref, o_ref):
        o_ref[...] = _nn(a_ref[...], b_ref[...]).astype(out_dtype)

    return pl.pallas_call(
        body, grid=(n // tn, m // TM), name=name,
        in_specs=[pl.BlockSpec((TM, k), lambda j, i: (i, 0)), b_spec],
        out_specs=pl.BlockSpec((TM, tn), lambda j, i: (i, j)),
        out_shape=SDS((m, n), out_dtype),
    )(a, b)


def mm_nt(a, b, name):
    m = a.shape[0]
    b_spec, k, n, tc = _weight_spec(b, 1)

    def body(a_ref, b_ref, o_ref):
        _acc(o_ref, _nt(a_ref[...], b_ref[...]), pl.program_id(1) == 0)

    return pl.pallas_call(
        body, grid=(m // TM, n // tc), name=name,
        in_specs=[pl.BlockSpec((TM, tc), lambda i, c: (i, c)), b_spec],
        out_specs=pl.BlockSpec((TM, k), lambda i, c: (i, 0)),
        out_shape=SDS((m, k), F32),
    )(a, b)


def mm_tn(a, b, name, sharded=False):
    m, k = a.shape
    n = b.shape[1]
    tn = n // N_CHIP if sharded else _ncol(n)
    steps = m // TM

    def body(a_ref, b_ref, o_ref, acc_ref):
        i = pl.program_id(1)
        _acc(acc_ref, _tn(a_ref[...], b_ref[...]), i == 0)

        @pl.when(i == steps - 1)
        def _():
            o_ref[...] = acc_ref[...].astype(BF16)

    if sharded:
        out_spec, out_shape = pl.BlockSpec((None, k, tn), lambda j, i: (j, 0, 0)), SDS((N_CHIP, k, tn), BF16)
    else:
        out_spec, out_shape = pl.BlockSpec((k, tn), lambda j, i: (0, j)), SDS((k, n), BF16)
    return pl.pallas_call(
        body, grid=(n // tn, steps), name=name,
        in_specs=[pl.BlockSpec((TM, k), lambda j, i: (i, 0)), pl.BlockSpec((TM, tn), lambda j, i: (i, j))],
        out_specs=out_spec, out_shape=out_shape,
        scratch_shapes=[pltpu.VMEM((k, tn), F32)],
    )(a, b)


_ROW = pl.BlockSpec((TS, D), lambda i: (i, 0))
_VEC = pl.BlockSpec((1, D), lambda i: (0, 0))


def modnorm_fwd(x, g, sc, sh, name):
    def body(x_ref, g_ref, sc_ref, sh_ref, o_ref):
        o_ref[...] = _modnorm(x_ref[...], g_ref[...], sc_ref[...], sh_ref[...]).astype(BF16)

    return pl.pallas_call(body, grid=(S // TS,), name=name, in_specs=[_ROW, _VEC, _VEC, _VEC], out_specs=_ROW,
                          out_shape=SDS((S, D), BF16))(x, g, sc, sh)


def modnorm_bwd(x, g, sc, sh, dh, dx_in, name):
    def body(x_ref, g_ref, sc_ref, sh_ref, dh_ref, dxin_ref, dx_ref, dg_ref, dsc_ref, dsh_ref):
        _, vjp = jax.vjp(_modnorm, x_ref[...], g_ref[...], sc_ref[...], sh_ref[...])
        dx, dg, dsc, dsh = vjp(dh_ref[...])
        dx_ref[...] = dxin_ref[...] + dx
        first = pl.program_id(0) == 0
        _acc(dg_ref, dg, first)
        _acc(dsc_ref, dsc, first)
        _acc(dsh_ref, dsh, first)

    vec = SDS((1, D), F32)
    return pl.pallas_call(body, grid=(S // TS,), name=name, in_specs=[_ROW, _VEC, _VEC, _VEC, _ROW, _ROW],
                          out_specs=[_ROW, _VEC, _VEC, _VEC], out_shape=[SDS((S, D), F32), vec, vec, vec])(
                              x, g, sc, sh, dh, dx_in)


def resid_fwd(x, y, ga, g, name):
    def body(x_ref, y_ref, ga_ref, g_ref, o_ref):
        o_ref[...] = _resid(x_ref[...], y_ref[...], ga_ref[...], g_ref[...])

    return pl.pallas_call(body, grid=(S // TS,), name=name, in_specs=[_ROW, _ROW, _VEC, _VEC], out_specs=_ROW,
                          out_shape=SDS((S, D), F32))(x, y, ga, g)


def resid_bwd(y, ga, g, dxo, name):
    def body(y_ref, ga_ref, g_ref, dxo_ref, dy_ref, dga_ref, dg_ref):
        _, vjp = jax.vjp(lambda y, ga, g: ga * _rms(y, g), y_ref[...], ga_ref[...], g_ref[...])
        dy, dga, dg = vjp(dxo_ref[...])
        dy_ref[...] = dy.astype(BF16)
        first = pl.program_id(0) == 0
        _acc(dga_ref, dga, first)
        _acc(dg_ref, dg, first)

    vec = SDS((1, D), F32)
    return pl.pallas_call(body, grid=(S // TS,), name=name, in_specs=[_ROW, _VEC, _VEC, _ROW],
                          out_specs=[_ROW, _VEC, _VEC], out_shape=[SDS((S, D), BF16), vec, vec])(y, ga, g, dxo)


def loss_grad(xf, target, name):
    def body(x_ref, t_ref, dx_ref, l_ref):
        err = x_ref[...] - t_ref[...]
        dx_ref[...] = err * (1.0 / D)
        part = 0.5 * jnp.sum(jnp.mean(err * err, axis=-1, keepdims=True), axis=0, keepdims=True)
        _acc(l_ref, jnp.broadcast_to(part, (1, 128)), pl.program_id(0) == 0)

    return pl.pallas_call(body, grid=(S // TS,), name=name, in_specs=[_ROW, _ROW],
                          out_specs=[_ROW, pl.BlockSpec((1, 128), lambda i: (0, 0))],
                          out_shape=[SDS((S, D), F32), SDS((1, 128), F32)])(xf, target)


_FG_SPEC = pl.BlockSpec((S, 128), lambda i: (0, Z_FG // 128))


def _tri128(lower):
    r = lax.broadcasted_iota(jnp.int32, (128, 128), 0)
    c = lax.broadcasted_iota(jnp.int32, (128, 128), 1)
    return ((r >= c) if lower else (r <= c)).astype(F32)


def fgate_fwd(z, bf, name):
    def body(z_ref, b_ref, cc_ref, cr_ref):
        tri = _tri128(True)
        carry = jnp.zeros((1, 128), F32)
        for i in range(S // 128):
            rows = pl.ds(i * 128, 128)
            lf = jax.nn.log_sigmoid(z_ref[rows, :] + b_ref[...])
            c = jnp.dot(tri, lf, precision=lax.Precision.HIGHEST, preferred_element_type=F32) + carry
            cc_ref[rows, :] = c
            carry = c[127:128, :]
        cr_ref[...] = cc_ref[...].T

    return pl.pallas_call(body, name=name, grid=(1,),
                          in_specs=[_FG_SPEC, pl.BlockSpec((1, 128), lambda i: (0, 0))],
                          out_specs=[pl.BlockSpec((S, 128), lambda i: (0, 0)), pl.BlockSpec((128, S), lambda i: (0, 0))],
                          out_shape=[SDS((S, 128), F32), SDS((128, S), F32)])(z, bf)


def fgate_bwd(z, bf, dcq, dck, name):
    def body(z_ref, b_ref, dcq_ref, dck_ref, dz_ref, db_ref, col_ref):
        col_ref[...] = dcq_ref[...] + jnp.concatenate([dck_ref[...], jnp.zeros((120, S), F32)], axis=0).T
        tri = _tri128(False)
        carry = jnp.zeros((1, 128), F32)
        db = jnp.zeros((1, 128), F32)
        for i in reversed(range(S // 128)):
            rows = pl.ds(i * 128, 128)
            dlf = jnp.dot(tri, col_ref[rows, :], precision=lax.Precision.HIGHEST, preferred_element_type=F32) + carry
            carry = dlf[0:1, :]
            dz = dlf * jax.nn.sigmoid(-(z_ref[rows, :] + b_ref[...]))
            dz_ref[rows, :] = dz.astype(BF16)
            db = db + jnp.sum(dz, axis=0, keepdims=True)
        db_ref[...] = db

    return pl.pallas_call(body, name=name, grid=(1,),
                          in_specs=[_FG_SPEC, pl.BlockSpec((1, 128), lambda i: (0, 0)),
                                    pl.BlockSpec((S, 128), lambda i: (0, 0)), pl.BlockSpec((8, S), lambda i: (0, 0))],
                          out_specs=[pl.BlockSpec((S, 128), lambda i: (0, 0)), pl.BlockSpec((1, 128), lambda i: (0, 0))],
                          out_shape=[SDS((S, 128), BF16), SDS((1, 128), F32)],
                          scratch_shapes=[pltpu.VMEM((S, 128), F32)])(z, bf, dcq, dck)


TQ = 256


def _head_mask(hh):
    lane = lax.broadcasted_iota(jnp.int32, (TQ, 128), 1)
    return (lane >= 64 * hh) & (lane < 64 * hh + 64)


def _fox_specs():
    q = pl.BlockSpec((TQ, 256), lambda i: (i, Z_FOX_Q // 256))
    k = pl.BlockSpec((S, 256), lambda i: (0, Z_FOX_K // 256))
    v = pl.BlockSpec((S, 256), lambda i: (0, Z_FOX_V // 256))
    cc = pl.BlockSpec((TQ, 128), lambda i: (i, 0))
    cr = pl.BlockSpec((8, S), lambda i: (0, 0))
    return q, k, v, cc, cr


def _fox_scores(qm, k, cc_h, cr_h, i, kb):
    s = _nt(qm, k) * SCALE + cc_h - cr_h
    qpos = i * TQ + lax.broadcasted_iota(jnp.int32, (TQ, TQ), 0)
    kpos = kb * TQ + lax.broadcasted_iota(jnp.int32, (TQ, TQ), 1)
    return jnp.where(kpos <= qpos, s, NEG)


def _key_rows(kb):
    return pl.ds(pl.multiple_of(kb * TQ, TQ), TQ)


def fox_fwd(z, cumc, cumr, name):
    def body(q_ref, k_ref, v_ref, cc_ref, cr_ref, o_ref, l_ref):
        i = pl.program_id(0)
        lane = lax.broadcasted_iota(jnp.int32, (TQ, 128), 1)
        cc = cc_ref[...]
        lse = jnp.zeros((TQ, 128), F32)
        for p in range(2):
            cols = pl.ds(128 * p, 128)
            q = q_ref[:, cols]
            o_pair = jnp.zeros((TQ, 128), F32)
            for hh in range(2):
                h = 2 * p + hh
                hm = _head_mask(hh)
                qm = jnp.where(hm, q, 0.0).astype(BF16)
                cq = cc[:, h:h + 1]

                def chunk(kb, carry, qm=qm, cq=cq, h=h, cols=cols):
                    m, l, acc = carry
                    rows = _key_rows(kb)
                    s = _fox_scores(qm, k_ref[rows, cols].astype(BF16), cq, cr_ref[h:h + 1, rows], i, kb)
                    m_new = jnp.maximum(m, jnp.max(s, axis=1, keepdims=True))
                    a = jnp.exp(m - m_new)
                    e = jnp.exp(s - m_new)
                    l = a * l + jnp.sum(e, axis=1, keepdims=True)
                    acc = a * acc + _nn(e.astype(BF16), v_ref[rows, cols].astype(BF16))
                    return m_new, l, acc

                init = (jnp.full((TQ, 1), NEG, F32), jnp.zeros((TQ, 1), F32), jnp.zeros((TQ, 128), F32))
                m, l, acc = lax.fori_loop(0, i + 1, chunk, init)
                o_pair = jnp.where(hm, acc / l, o_pair)
                lse = jnp.where(lane == h, m + jnp.log(l), lse)
            o_ref[:, cols] = o_pair
        l_ref[...] = lse

    q, k, v, cc, cr = _fox_specs()
    return pl.pallas_call(body, grid=(S // TQ,), name=name, in_specs=[q, k, v, cc, cr],
                          out_specs=[pl.BlockSpec((TQ, 256), lambda i: (i, 0)), cc],
                          out_shape=[SDS((S, 256), F32), SDS((S, 128), F32)])(z, z, z, cumc, cumr)


def fox_bwd(z, cumc, cumr, lse, o, do, name):
    steps = S // TQ

    def body(q_ref, k_ref, v_ref, cc_ref, cr_ref, l_ref, o_ref, do_ref, dq_ref, dk_ref, dv_ref, dcq_ref, dck_ref,
             dk_acc, dv_acc):
        i = pl.program_id(0)
        lane = lax.broadcasted_iota(jnp.int32, (TQ, 128), 1)
        cc = cc_ref[...]
        lse_all = l_ref[...]
        dcq = jnp.zeros((TQ, 128), F32)

        @pl.when(i == 0)
        def _():
            dk_acc[...] = jnp.zeros_like(dk_acc)
            dv_acc[...] = jnp.zeros_like(dv_acc)
            dck_ref[...] = jnp.zeros_like(dck_ref)

        for p in range(2):
            cols = pl.ds(128 * p, 128)
            q = q_ref[:, cols]
            o_p = o_ref[:, cols]
            do_p = do_ref[:, cols]
            dq_pair = jnp.zeros((TQ, 128), F32)
            for hh in range(2):
                h = 2 * p + hh
                hm = _head_mask(hh)
                qm = jnp.where(hm, q, 0.0).astype(BF16)
                cq = cc[:, h:h + 1]
                lse_h = lse_all[:, h:h + 1]
                dom = jnp.where(hm, do_p, 0.0)
                dl = jnp.sum(dom * o_p, axis=1, keepdims=True)
                dom = dom.astype(BF16)

                def chunk(kb, carry, qm=qm, cq=cq, lse_h=lse_h, dom=dom, dl=dl, h=h, cols=cols):
                    dq_h, dcq_h = carry
                    rows = _key_rows(kb)
                    k = k_ref[rows, cols].astype(BF16)
                    pn = jnp.exp(_fox_scores(qm, k, cq, cr_ref[h:h + 1, rows], i, kb) - lse_h)
                    ds = pn * (_nt(dom, v_ref[rows, cols].astype(BF16)) - dl)
                    dsb = ds.astype(BF16)
                    dk_acc[rows, cols] += _tn(dsb, qm) * SCALE
                    dv_acc[rows, cols] += _tn(pn.astype(BF16), dom)
                    dck_ref[h:h + 1, rows] -= jnp.sum(ds, axis=0, keepdims=True)
                    return dq_h + _nn(dsb, k) * SCALE, dcq_h + jnp.sum(ds, axis=1, keepdims=True)

                dq_h, dcq_h = lax.fori_loop(0, i + 1, chunk, (jnp.zeros((TQ, 128), F32), jnp.zeros((TQ, 1), F32)))
                dq_pair = jnp.where(hm, dq_h, dq_pair)
                dcq = jnp.where(lane == h, dcq_h, dcq)
            dq_ref[:, cols] = dq_pair.astype(BF16)
        dcq_ref[...] = dcq

        @pl.when(i == steps - 1)
        def _():
            dk_ref[...] = dk_acc[...].astype(BF16)
            dv_ref[...] = dv_acc[...].astype(BF16)

    q, k, v, cc, cr = _fox_specs()
    blk = pl.BlockSpec((TQ, 256), lambda i: (i, 0))
    full = pl.BlockSpec((S, 256), lambda i: (0, 0))
    return pl.pallas_call(body, grid=(steps,), name=name, in_specs=[q, k, v, cc, cr, cc, blk, blk],
                          out_specs=[blk, full, full, cc, cr],
                          out_shape=[SDS((S, 256), BF16), SDS((S, 256), BF16), SDS((S, 256), BF16), SDS((S, 128), F32),
                                     SDS((8, S), F32)],
                          scratch_shapes=[pltpu.VMEM((S, 256), F32), pltpu.VMEM((S, 256), F32)])(
                              z, z, z, cumc, cumr, lse, o, do)


W = 128
SWA_HEADS = 4


def _swa_core(first, qs, kcat, vcat, sink):
    r = lax.broadcasted_iota(jnp.int32, (SWA_HEADS * W, 2 * W), 0)
    j = lax.broadcasted_iota(jnp.int32, (SWA_HEADS * W, 2 * W), 1)
    qi = r & (W - 1)
    valid = ((j < W) & (j > qi) & jnp.logical_not(first)) | ((j >= W) & (j - W <= qi))
    s = jnp.where(valid, _nt(qs.astype(BF16), kcat.astype(BF16)) * SCALE, NEG)
    m = lax.stop_gradient(jnp.maximum(jnp.max(s, axis=1, keepdims=True), sink))
    e = jnp.exp(s - m)
    den = jnp.sum(e, axis=1, keepdims=True) + jnp.exp(sink - m)
    return _nn((e / den).astype(BF16), vcat.astype(BF16))


def _kv_lanes(kv):
    lane = lax.broadcasted_iota(jnp.int32, (W, 128), 1)
    return (lane >= 64 * kv) & (lane < 64 * kv + 64)


def _swa_stack(pair0, pair1):
    blocks = []
    for h in range(SWA_HEADS):
        kv, hh = h // 2, h % 2
        a = (pair0, pair1)[kv]
        a = a if hh == kv else pltpu.roll(a, 64, axis=1)
        blocks.append(jnp.where(_kv_lanes(kv), a, 0.0))
    return jnp.concatenate(blocks, axis=0)


def _swa_unstack(stacked):
    pairs = [jnp.zeros((W, 128), F32), jnp.zeros((W, 128), F32)]
    for h in range(SWA_HEADS):
        kv, hh = h // 2, h % 2
        a = jnp.where(_kv_lanes(kv), stacked[h * W:(h + 1) * W], 0.0)
        pairs[kv] = pairs[kv] + (a if hh == kv else pltpu.roll(a, 64, axis=1))
    return pairs


def _head_rows():
    return lax.broadcasted_iota(jnp.int32, (SWA_HEADS * W, 1), 0) // W


def _swa_operands(q_ref, kp_ref, kc_ref, vp_ref, vc_ref, sk_ref):
    qs = _swa_stack(q_ref[:, 0:128], q_ref[:, 128:256])
    kcat = jnp.concatenate([kp_ref[...], kc_ref[...]], axis=0)
    vcat = jnp.concatenate([vp_ref[...], vc_ref[...]], axis=0)
    sink = jnp.zeros((SWA_HEADS * W, 1), F32)
    for h in range(SWA_HEADS):
        sink = jnp.where(_head_rows() == h, sk_ref[:, h:h + 1], sink)
    return qs, kcat, vcat, sink


def _swa_specs():
    q = pl.BlockSpec((W, 256), lambda n: (n, Z_SWA_Q // 256))
    kc = pl.BlockSpec((W, 128), lambda n: (n, Z_SWA_K // 128))
    kp = pl.BlockSpec((W, 128), lambda n: (jnp.maximum(n - 1, 0), Z_SWA_K // 128))
    vc = pl.BlockSpec((W, 128), lambda n: (n, Z_SWA_V // 128))
    vp = pl.BlockSpec((W, 128), lambda n: (jnp.maximum(n - 1, 0), Z_SWA_V // 128))
    sk = pl.BlockSpec((1, 128), lambda n: (0, 0))
    return q, kp, kc, vp, vc, sk


def swa_fwd(z, sinks, name):
    def body(q_ref, kp_ref, kc_ref, vp_ref, vc_ref, sk_ref, o_ref):
        o = _swa_core(pl.program_id(0) == 0, *_swa_operands(q_ref, kp_ref, kc_ref, vp_ref, vc_ref, sk_ref))
        o0, o1 = _swa_unstack(o)
        o_ref[:, 0:128] = o0
        o_ref[:, 128:256] = o1

    return pl.pallas_call(body, grid=(S // W,), name=name, in_specs=list(_swa_specs()),
                          out_specs=pl.BlockSpec((W, 256), lambda n: (n, 0)),
                          out_shape=SDS((S, 256), F32))(z, z, z, z, z, sinks)


def swa_bwd(z, sinks, do, name):
    steps = S // W

    def body(q_ref, kp_ref, kc_ref, vp_ref, vc_ref, sk_ref, do_ref, dq_ref, dk_ref, dv_ref, dsk_ref, dk_acc, dv_acc):
        n = pl.program_id(0)
        first = n == 0
        _, vjp = jax.vjp(functools.partial(_swa_core, first), *_swa_operands(q_ref, kp_ref, kc_ref, vp_ref, vc_ref, sk_ref))
        dqs, dkcat, dvcat, dsink = vjp(_swa_stack(do_ref[:, 0:128], do_ref[:, 128:256]))
        dq0, dq1 = _swa_unstack(dqs)
        dq_ref[:, 0:128] = dq0.astype(BF16)
        dq_ref[:, 128:256] = dq1.astype(BF16)

        @pl.when(first)
        def _():
            dk_acc[...] = jnp.zeros_like(dk_acc)
            dv_acc[...] = jnp.zeros_like(dv_acc)

        cur = pl.ds(pl.multiple_of(n * W, W), W)
        dk_acc[cur, :] += dkcat[W:2 * W]
        dv_acc[cur, :] += dvcat[W:2 * W]

        @pl.when(n > 0)
        def _():
            prev = pl.ds(pl.multiple_of((n - 1) * W, W), W)
            dk_acc[prev, :] += dkcat[0:W]
            dv_acc[prev, :] += dvcat[0:W]

        lane = lax.broadcasted_iota(jnp.int32, (1, 128), 1)
        dsk = jnp.zeros((1, 128), F32)
        for h in range(SWA_HEADS):
            d = jnp.sum(jnp.where(_head_rows() == h, dsink, 0.0), axis=0, keepdims=True)
            dsk = jnp.where(lane == h, d, dsk)
        _acc(dsk_ref, dsk, first)

        @pl.when(n == steps - 1)
        def _():
            dk_ref[...] = dk_acc[...].astype(BF16)
            dv_ref[...] = dv_acc[...].astype(BF16)

    blk = pl.BlockSpec((W, 256), lambda n: (n, 0))
    full = pl.BlockSpec((S, 128), lambda n: (0, 0))
    return pl.pallas_call(body, grid=(steps,), name=name, in_specs=list(_swa_specs()) + [blk],
                          out_specs=[blk, full, full, pl.BlockSpec((1, 128), lambda n: (0, 0))],
                          out_shape=[SDS((S, 256), BF16), SDS((S, 128), BF16), SDS((S, 128), BF16), SDS((1, 128), F32)],
                          scratch_shapes=[pltpu.VMEM((S, 128), F32), pltpu.VMEM((S, 128), F32)])(
                              z, z, z, z, z, sinks, do)


def _glu(a, g):
    return a * jax.nn.sigmoid(g)


def _cv1_specs():
    a = pl.BlockSpec((S, 128), lambda j: (0, Z_CONV_A // 128 + j))
    g = pl.BlockSpec((S, 128), lambda j: (0, Z_CONV_G // 128 + j))
    w = pl.BlockSpec((32, 128), lambda j: (0, j))
    b = pl.BlockSpec((1, 128), lambda j: (0, j))
    h = pl.BlockSpec((S, 128), lambda j: (0, j))
    return a, g, w, b, h


def conv_dw_fwd(z, cw, cb, name):
    def body(a_ref, g_ref, w_ref, b_ref, o_ref):
        hh = _glu(a_ref[...], g_ref[...])
        acc = jnp.zeros((S, 128), F32) + b_ref[...]
        for k in range(CONV_K):
            acc = acc + _shift_down(hh, CONV_K - 1 - k) * w_ref[k:k + 1, :]
        o_ref[...] = acc

    a, g, w, b, h = _cv1_specs()
    return pl.pallas_call(body, grid=(2,), name=name, in_specs=[a, g, w, b], out_specs=h,
                          out_shape=SDS((S, 256), F32))(z, z, cw, cb)


def conv_dw_bwd(z, cw, dhc, name):
    def body(a_ref, g_ref, w_ref, dh_ref, da_ref, dg_ref, dw_ref, db_ref):
        hh, vjp = jax.vjp(_glu, a_ref[...], g_ref[...])
        dh = dh_ref[...]
        dhh = jnp.zeros((S, 128), F32)
        for k in range(CONV_K):
            n = CONV_K - 1 - k
            dhh = dhh + _shift_up(dh, n) * w_ref[k:k + 1, :]
            dw_ref[k:k + 1, :] = jnp.sum(dh * _shift_down(hh, n), axis=0, keepdims=True)
        dw_ref[CONV_K:32, :] = jnp.zeros((32 - CONV_K, 128), F32)
        db_ref[...] = jnp.sum(dh, axis=0, keepdims=True)
        da, dg = vjp(dhh)
        da_ref[...] = da.astype(BF16)
        dg_ref[...] = dg.astype(BF16)

    a, g, w, b, h = _cv1_specs()
    return pl.pallas_call(body, grid=(2,), name=name, in_specs=[a, g, w, h], out_specs=[h, h, w, b],
                          out_shape=[SDS((S, 256), BF16), SDS((S, 256), BF16), SDS((32, 256), F32), SDS((1, 256), F32)])(
                              z, z, cw, dhc)


def _ln(x, g, b):
    mu = jnp.mean(x, axis=-1, keepdims=True)
    xc = x - mu
    var = jnp.mean(xc * xc, axis=-1, keepdims=True)
    return xc * lax.rsqrt(var + EPS) * g + b


def _conv_pw(hc, lg, lb, pw, pb):
    y = jax.nn.silu(_ln(hc, lg, lb))
    return _nn(y.astype(BF16), pw.astype(BF16)) + pb


TS2 = 512
_ROW2 = pl.BlockSpec((TS2, 256), lambda i: (i, 0))
_VEC2 = pl.BlockSpec((1, 256), lambda i: (0, 0))
_MAT2 = pl.BlockSpec((256, 256), lambda i: (0, 0))


def conv_pw_fwd(hc, lg, lb, pw, pb, name):
    def body(h_ref, lg_ref, lb_ref, pw_ref, pb_ref, o_ref):
        o_ref[...] = _conv_pw(h_ref[...], lg_ref[...], lb_ref[...], pw_ref[...], pb_ref[...])

    return pl.pallas_call(body, grid=(S // TS2,), name=name, in_specs=[_ROW2, _VEC2, _VEC2, _MAT2, _VEC2],
                          out_specs=_ROW2, out_shape=SDS((S, 256), F32))(hc, lg, lb, pw, pb)


def conv_pw_bwd(hc, lg, lb, pw, pb, dy, name):
    def body(h_ref, lg_ref, lb_ref, pw_ref, pb_ref, dy_ref, dh_ref, dlg_ref, dlb_ref, dpw_ref, dpb_ref):
        _, vjp = jax.vjp(_conv_pw, h_ref[...], lg_ref[...], lb_ref[...], pw_ref[...], pb_ref[...])
        dh, dlg, dlb, dpw, dpb = vjp(dy_ref[...])
        dh_ref[...] = dh
        first = pl.program_id(0) == 0
        _acc(dlg_ref, dlg, first)
        _acc(dlb_ref, dlb, first)
        _acc(dpw_ref, dpw, first)
        _acc(dpb_ref, dpb, first)

    vec = SDS((1, 256), F32)
    return pl.pallas_call(body, grid=(S // TS2,), name=name, in_specs=[_ROW2, _VEC2, _VEC2, _MAT2, _VEC2, _ROW2],
                          out_specs=[_ROW2, _VEC2, _VEC2, _MAT2, _VEC2],
                          out_shape=[SDS((S, 256), F32), vec, vec, SDS((256, 256), F32), vec])(hc, lg, lb, pw, pb, dy)


def _sgu_block(u0, u1, v0, v1, lg0, lg1, lb0, lb1, w0, w1, w2, w3, bt):
    u0, u1, v0, v1 = (jax.nn.gelu(a) for a in (u0, u1, v0, v1))
    mu = (jnp.sum(v0, axis=1, keepdims=True) + jnp.sum(v1, axis=1, keepdims=True)) * (1.0 / GW)
    c0, c1 = v0 - mu, v1 - mu
    var = (jnp.sum(c0 * c0, axis=1, keepdims=True) + jnp.sum(c1 * c1, axis=1, keepdims=True)) * (1.0 / GW)
    r = lax.rsqrt(var + EPS)
    n0 = c0 * r * lg0 + lb0
    n1 = c1 * r * lg1 + lb1
    row = lax.broadcasted_iota(jnp.int32, (128, 128), 0)
    col = lax.broadcasted_iota(jnp.int32, (128, 128), 1)
    tri = row >= col
    outs = []
    for p, (n, u, wa, wb) in enumerate(((n0, u0, w0, w1), (n1, u1, w2, w3))):
        nb = n.astype(BF16)
        ma = _nn(jnp.where(tri, wa, 0.0).astype(BF16), nb)
        mb = _nn(jnp.where(tri, wb, 0.0).astype(BF16), nb)
        expand = (row == 2 * p + col // 64).astype(F32)
        bias = jnp.dot(bt, expand, precision=lax.Precision.HIGHEST, preferred_element_type=F32)
        outs.append(u * (jnp.where(col < 64, ma, mb) + bias))
    return outs[0], outs[1]


def _sgu_specs():
    def col(c):
        return pl.BlockSpec((128, 128), lambda n, c=c: (n, c))
    zs = [col(Z_SGU_U // 128), col(Z_SGU_U // 128 + 1), col(Z_SGU_V // 128), col(Z_SGU_V // 128 + 1)]
    vec = [pl.BlockSpec((1, 128), lambda n: (0, 0)), pl.BlockSpec((1, 128), lambda n: (0, 1))]
    ws = [pl.BlockSpec((None, 128, 128), lambda n, g=g: (g, 0, 0)) for g in range(4)]
    bt = pl.BlockSpec((128, 128), lambda n: (0, 0))
    return zs + vec + vec + ws + [bt]


def sgu_fwd(z, lg, lb, w, bt, name):
    def body(*refs):
        o_ref = refs[-1]
        y0, y1 = _sgu_block(*[r[...] for r in refs[:-1]])
        o_ref[:, 0:128] = y0
        o_ref[:, 128:256] = y1

    return pl.pallas_call(body, grid=(S // 128,), name=name, in_specs=_sgu_specs(),
                          out_specs=pl.BlockSpec((128, 256), lambda n: (n, 0)),
                          out_shape=SDS((S, 256), F32))(z, z, z, z, lg, lg, lb, lb, w, w, w, w, bt)


def sgu_bwd(z, lg, lb, w, bt, dy, name):
    def body(*refs):
        ins, dy_ref = refs[:13], refs[13]
        du_ref, dv_ref, dlg_ref, dlb_ref, dw_ref, dbt_ref = refs[14:]
        _, vjp = jax.vjp(_sgu_block, *[r[...] for r in ins])
        du0, du1, dv0, dv1, dlg0, dlg1, dlb0, dlb1, dw0, dw1, dw2, dw3, dbt = vjp((dy_ref[:, 0:128], dy_ref[:, 128:256]))
        du_ref[:, 0:128] = du0.astype(BF16)
        du_ref[:, 128:256] = du1.astype(BF16)
        dv_ref[:, 0:128] = dv0.astype(BF16)
        dv_ref[:, 128:256] = dv1.astype(BF16)
        first = pl.program_id(0) == 0

        @pl.when(first)
        def _():
            dlg_ref[...] = jnp.zeros_like(dlg_ref)
            dlb_ref[...] = jnp.zeros_like(dlb_ref)
            dw_ref[...] = jnp.zeros_like(dw_ref)
            dbt_ref[...] = jnp.zeros_like(dbt_ref)

        dlg_ref[:, 0:128] += dlg0
        dlg_ref[:, 128:256] += dlg1
        dlb_ref[:, 0:128] += dlb0
        dlb_ref[:, 128:256] += dlb1
        for g, d in enumerate((dw0, dw1, dw2, dw3)):
            dw_ref[g] += d
        dbt_ref[...] += dbt

    blk = pl.BlockSpec((128, 256), lambda n: (n, 0))
    vec = pl.BlockSpec((1, 256), lambda n: (0, 0))
    return pl.pallas_call(body, grid=(S // 128,), name=name, in_specs=_sgu_specs() + [blk],
                          out_specs=[blk, blk, vec, vec, pl.BlockSpec((4, 128, 128), lambda n: (0, 0, 0)),
                                     pl.BlockSpec((128, 128), lambda n: (0, 0))],
                          out_shape=[SDS((S, 256), BF16), SDS((S, 256), BF16), SDS((1, 256), F32), SDS((1, 256), F32),
                                     SDS((4, 128, 128), F32), SDS((128, 128), F32)])(
                                         z, z, z, z, lg, lg, lb, lb, w, w, w, w, bt, dy)


def _group_norm(y0, y1, y2, y3, g0, g1, g2, g3):
    return tuple(_rms(y, g) for y, g in zip((y0, y1, y2, y3), (g0, g1, g2, g3)))


_GROW = pl.BlockSpec((TS2, 256), lambda i: (i, 0))
_GCAT = pl.BlockSpec((TS2, D), lambda i: (i, 0))
_GVEC = [pl.BlockSpec((1, 256), lambda i, j=j: (0, j)) for j in range(4)]


def group_norm_fwd(ys, gg, name):
    def body(*refs):
        o_ref = refs[-1]
        outs = _group_norm(*[r[...] for r in refs[:-1]])
        for j, c in enumerate(outs):
            o_ref[:, 256 * j:256 * (j + 1)] = c.astype(BF16)

    return pl.pallas_call(body, grid=(S // TS2,), name=name, in_specs=[_GROW] * 4 + _GVEC, out_specs=_GCAT,
                          out_shape=SDS((S, D), BF16))(*ys, gg, gg, gg, gg)


def group_norm_bwd(ys, gg, dcat, name):
    def body(*refs):
        ins, dc_ref = refs[:8], refs[8]
        dy_refs, dg_ref = refs[9:13], refs[13]
        _, vjp = jax.vjp(_group_norm, *[r[...] for r in ins])
        grads = vjp(tuple(dc_ref[:, 256 * j:256 * (j + 1)] for j in range(4)))
        first = pl.program_id(0) == 0

        @pl.when(first)
        def _():
            dg_ref[...] = jnp.zeros_like(dg_ref)

        for j in range(4):
            dy_refs[j][...] = grads[j]
            dg_ref[:, 256 * j:256 * (j + 1)] += grads[4 + j]

    return pl.pallas_call(body, grid=(S // TS2,), name=name, in_specs=[_GROW] * 4 + _GVEC + [_GCAT],
                          out_specs=[_GROW] * 4 + [pl.BlockSpec((1, D), lambda i: (0, 0))],
                          out_shape=[SDS((S, 256), F32)] * 4 + [SDS((1, D), F32)])(*ys, gg, gg, gg, gg, dcat)


FB = 256
N_FB = DFF // FB


def _ffn_gate(ug, uv, wg0, wg1, wg2, wv0, wv1, wv2, bg, bv):
    cg = bg + _shift_down(ug, 2) * wg0 + _shift_down(ug, 1) * wg1 + ug * wg2
    cv = bv + _shift_down(uv, 2) * wv0 + _shift_down(uv, 1) * wv1 + uv * wv2
    return jax.nn.silu(cg) * cv


def _gate_specs():
    ug = pl.BlockSpec((S, FB), lambda j: (0, j))
    uv = pl.BlockSpec((S, FB), lambda j: (0, j + N_FB))
    wg = pl.BlockSpec((8, FB), lambda j: (0, j))
    wv = pl.BlockSpec((8, FB), lambda j: (0, j + N_FB))
    bg = pl.BlockSpec((1, FB), lambda j: (0, j))
    bv = pl.BlockSpec((1, FB), lambda j: (0, j + N_FB))
    return ug, uv, wg, wv, bg, bv


def _gate_args(ug_ref, uv_ref, wg_ref, wv_ref, bg_ref, bv_ref):
    return (ug_ref[...], uv_ref[...], wg_ref[0:1, :], wg_ref[1:2, :], wg_ref[2:3, :],
            wv_ref[0:1, :], wv_ref[1:2, :], wv_ref[2:3, :], bg_ref[...], bv_ref[...])


def ffn_gate_fwd(u, cw, cb, name):
    def body(ug_ref, uv_ref, wg_ref, wv_ref, bg_ref, bv_ref, o_ref):
        o_ref[...] = _ffn_gate(*_gate_args(ug_ref, uv_ref, wg_ref, wv_ref, bg_ref, bv_ref)).astype(BF16)

    return pl.pallas_call(body, grid=(N_FB,), name=name, in_specs=list(_gate_specs()),
                          out_specs=pl.BlockSpec((S, FB), lambda j: (0, j)),
                          out_shape=SDS((S, DFF), BF16))(u, u, cw, cw, cb, cb)


def ffn_gate_bwd(u, cw, cb, da, name):
    def body(ug_ref, uv_ref, wg_ref, wv_ref, bg_ref, bv_ref, da_ref, dug_ref, duv_ref, dwg_ref, dwv_ref, dbg_ref, dbv_ref):
        _, vjp = jax.vjp(_ffn_gate, *_gate_args(ug_ref, uv_ref, wg_ref, wv_ref, bg_ref, bv_ref))
        dug, duv, g0, g1, g2, v0, v1, v2, dbg, dbv = vjp(da_ref[...])
        dug_ref[...] = dug.astype(BF16)
        duv_ref[...] = duv.astype(BF16)
        for k, (a, b) in enumerate(((g0, v0), (g1, v1), (g2, v2))):
            dwg_ref[k:k + 1, :] = a
            dwv_ref[k:k + 1, :] = b
        dwg_ref[FFN_K:8, :] = jnp.zeros((8 - FFN_K, FB), F32)
        dwv_ref[FFN_K:8, :] = jnp.zeros((8 - FFN_K, FB), F32)
        dbg_ref[...] = dbg
        dbv_ref[...] = dbv

    ug, uv, wg, wv, bg, bv = _gate_specs()
    half = pl.BlockSpec((S, FB), lambda j: (0, j))
    whalf = pl.BlockSpec((8, FB), lambda j: (0, j))
    bhalf = pl.BlockSpec((1, FB), lambda j: (0, j))
    return pl.pallas_call(body, grid=(N_FB,), name=name, in_specs=[ug, uv, wg, wv, bg, bv, half],
                          out_specs=[half, half, whalf, whalf, bhalf, bhalf],
                          out_shape=[SDS((S, DFF), BF16), SDS((S, DFF), BF16), SDS((8, DFF), F32), SDS((8, DFF), F32),
                                     SDS((1, DFF), F32), SDS((1, DFF), F32)])(u, u, cw, cw, cb, cb, da)


def _adamw(w, g, m, v):
    m = ADAM_B1 * m + (1.0 - ADAM_B1) * g
    v = ADAM_B2 * v + (1.0 - ADAM_B2) * (g * g)
    m_hat = m / (1.0 - ADAM_B1 ** ADAM_STEP)
    v_hat = v / (1.0 - ADAM_B2 ** ADAM_STEP)
    delta = -ADAM_LR * (m_hat / (jnp.sqrt(v_hat) + ADAM_EPS) + ADAM_WD * w)
    return delta, m, v


def sum_pieces(r, layer, base, name):
    n, rows, cols = r.shape
    tr = _row_tile(rows, cols)

    def body(r_ref, *rest):
        o_ref = rest[-1]
        acc = r_ref[0].astype(F32)
        for j in range(1, n):
            acc = acc + r_ref[j].astype(F32)
        o_ref[...] = acc

    extra = {} if base is None else dict(input_output_aliases={1: 0})
    return pl.pallas_call(body, grid=(rows // tr,), name=name,
                          in_specs=[pl.BlockSpec((n, tr, cols), lambda i: (0, i, 0))] + ([] if base is None else [ANY]),
                          out_specs=pl.BlockSpec((None, tr, cols), lambda i: (layer, i, 0)),
                          out_shape=SDS((DEPTH, rows, cols), F32), **extra)(*([r] if base is None else [r, base]))


def adamw_pair(w, p, q, m, v, name):
    rows, cols = w.shape
    tr = _row_tile(rows, cols)

    def body(w_ref, p_ref, q_ref, m_ref, v_ref, g_ref, d_ref, nm_ref, nv_ref):
        g = p_ref[...] + q_ref[...]
        g_ref[...] = g
        d_ref[...], nm_ref[...], nv_ref[...] = _adamw(w_ref[...], g, m_ref[...], v_ref[...])

    spec = pl.BlockSpec((tr, cols), lambda i: (i, 0))
    return pl.pallas_call(body, grid=(rows // tr,), name=name, in_specs=[spec] * 5, out_specs=[spec] * 4,
                          out_shape=[SDS((rows, cols), F32)] * 4)(w, p, q, m, v)


def adamw_gathered(w, gall, m, v, name):
    rows = w.shape[0]
    tr = rows // 2 if rows % 16 == 0 else rows

    def body(w_ref, ga_ref, m_ref, v_ref, g_ref, d_ref, nm_ref, nv_ref):
        g = ga_ref[0]
        for j in range(1, N_DEV):
            g = g + ga_ref[j]
        g_ref[...] = g
        d_ref[...], nm_ref[...], nv_ref[...] = _adamw(w_ref[...], g, m_ref[...], v_ref[...])

    spec = pl.BlockSpec((tr, 128), lambda i: (i, 0))
    return pl.pallas_call(body, grid=(rows // tr,), name=name,
                          in_specs=[spec, pl.BlockSpec((N_DEV, tr, 128), lambda i: (0, i, 0)), spec, spec],
                          out_specs=[spec] * 4, out_shape=[SDS((rows, 128), F32)] * 4)(w, gall, m, v)


ADA_COLS = 6 * D // N_CHIP
ADA_TN = 512


def ada_mod(c_all, w, b, name):
    def body(c_ref, w_ref, b_ref, o_ref):
        ca = jax.nn.silu(c_ref[...])
        o_ref[...] = jnp.dot(ca, w_ref[...], precision=lax.Precision.HIGHEST, preferred_element_type=F32) + b_ref[...]

    return pl.pallas_call(
        body, grid=(DEPTH, ADA_COLS // ADA_TN), name=name,
        in_specs=[pl.BlockSpec((N_DEV, D), lambda l, j: (0, 0)),
                  pl.BlockSpec((None, D, ADA_TN), lambda l, j: (l, 0, j)),
                  pl.BlockSpec((None, 1, ADA_TN), lambda l, j: (l, 0, j))],
        out_specs=pl.BlockSpec((None, N_DEV, ADA_TN), lambda l, j: (l, 0, j)),
        out_shape=SDS((DEPTH, N_DEV, ADA_COLS), F32))(c_all, w, b)


def ada_update(c_all_t, dmod, w, m, v, name):
    def body(c_ref, dm_ref, w_ref, m_ref, v_ref, g_ref, d_ref, nm_ref, nv_ref):
        ca = jax.nn.silu(c_ref[...])
        g = jnp.dot(ca, dm_ref[...], precision=lax.Precision.HIGHEST, preferred_element_type=F32)
        g_ref[...] = g
        d_ref[...], nm_ref[...], nv_ref[...] = _adamw(w_ref[...], g, m_ref[...], v_ref[...])

    wspec = pl.BlockSpec((None, D, ADA_TN), lambda l, j: (l, 0, j))
    return pl.pallas_call(
        body, grid=(DEPTH, ADA_COLS // ADA_TN), name=name,
        in_specs=[pl.BlockSpec((D, N_DEV), lambda l, j: (0, 0)),
                  pl.BlockSpec((None, N_DEV, ADA_TN), lambda l, j: (l, 0, j)), wspec, wspec, wspec],
        out_specs=[wspec] * 4, out_shape=[SDS((DEPTH, D, ADA_COLS), F32)] * 4)(c_all_t, dmod, w, m, v)


_CHIP_FLIPS = ((1, 0), (0, 1), (1, 1))
_DEV_FLIPS = tuple((a, b, c) for a in (0, 1) for b in (0, 1) for c in (0, 1) if (a, b, c) != (0, 0, 0))


def _position():
    return lax.axis_index("x"), lax.axis_index("y"), lax.axis_index("c")


def _hbm_call(body, arrs, out_shapes, n_remote, name):
    n = len(arrs)
    return pl.pallas_call(
        body, name=name, in_specs=[ANY] * n, out_specs=[ANY] * n, out_shape=out_shapes,
        scratch_shapes=[pltpu.SemaphoreType.DMA((n, n_remote)), pltpu.SemaphoreType.DMA((n, n_remote)),
                        pltpu.SemaphoreType.DMA((n,))])(*arrs)


_HBM = pl.BlockSpec(memory_space=pltpu.HBM)
_SEM = pl.BlockSpec(memory_space=pltpu.SEMAPHORE)
_EFFECT = pltpu.SideEffectType.DATAFLOW_SIDE_EFFECTING


def _chip_copies(src, land, send, recv, scatter, arrivals):
    x, y, c = _position()
    me = 2 * x + y
    out = []
    for k, (fx, fy) in enumerate(_CHIP_FLIPS):
        tx, ty = x ^ fx, y ^ fy
        peer = 2 * tx + ty
        out.append(pltpu.make_async_remote_copy(
            src_ref=src.at[peer] if scatter else src, dst_ref=land.at[peer if arrivals else me], send_sem=send.at[k],
            recv_sem=recv.at[k], device_id=(tx, ty, c), device_id_type=MESH))
    return out


def exchange_start(srcs, lands, scatter, name):
    n = len(srcs)

    def body(*refs):
        src, land = refs[:n], refs[n:2 * n]
        send, recv = refs[2 * n:3 * n], refs[3 * n:4 * n]
        token = refs[-1]
        for a in range(n):
            for copy in _chip_copies(src[a], land[a], send[a], recv[a], scatter, False):
                copy.start()
        token[...] = jnp.zeros_like(token)

    bufs = list(srcs) + list(lands)
    outs = pl.pallas_call(
        body, name=name, in_specs=[_HBM] * (2 * n),
        out_specs=[_SEM] * (2 * n) + [_HBM] * (2 * n) + [pl.BlockSpec(memory_space=pltpu.VMEM)],
        out_shape=[pltpu.SemaphoreType.DMA((3,))] * (2 * n) + [pltpu.HBM(a.shape, a.dtype) for a in bufs]
        + [SDS((8, 128), F32)],
        input_output_aliases={i: 2 * n + i for i in range(2 * n)},
        compiler_params=pltpu.CompilerParams(has_side_effects=_EFFECT),
    )(*[pltpu.with_memory_space_constraint(a, pltpu.HBM) for a in bufs])
    flights = [(outs[a], outs[n + a], outs[2 * n + a], outs[3 * n + a]) for a in range(n)]
    return flights, outs[-1]


def exchange_wait(flights, scatter, after, name):
    n = len(flights)

    def body(*refs):
        src, land = refs[:n], refs[n:2 * n]
        send, recv = refs[2 * n:3 * n], refs[3 * n:4 * n]
        for a in range(n):
            for arrival in _chip_copies(src[a], land[a], send[a], recv[a], scatter, True):
                arrival.wait_send()
                arrival.wait_recv()

    bufs = [f[2] for f in flights] + [f[3] for f in flights]
    sems = [f[0] for f in flights] + [f[1] for f in flights]
    outs = pl.pallas_call(
        body, name=name, in_specs=[_HBM] * (2 * n) + [_SEM] * (2 * n) + [ANY], out_specs=[_HBM] * (2 * n),
        out_shape=[pltpu.HBM(a.shape, a.dtype) for a in bufs],
        input_output_aliases={i: i for i in range(2 * n)},
        compiler_params=pltpu.CompilerParams(has_side_effects=_EFFECT),
    )(*bufs, *sems, after)
    return [(outs[a], outs[n + a]) for a in range(n)]


def chip_alltoall(arrs, name):
    n = len(arrs)

    def body(*refs):
        ins, outs = refs[:n], refs[n:2 * n]
        send, recv, loc = refs[2 * n:]
        x, y, c = _position()
        me = 2 * x + y
        copies = []
        for a in range(n):
            local = pltpu.make_async_copy(ins[a].at[me], outs[a].at[me], loc.at[a])
            local.start()
            copies.append(local)
            for k, (fx, fy) in enumerate(_CHIP_FLIPS):
                tx, ty = x ^ fx, y ^ fy
                cp = pltpu.make_async_remote_copy(
                    src_ref=ins[a].at[2 * tx + ty], dst_ref=outs[a].at[me], send_sem=send.at[a, k],
                    recv_sem=recv.at[a, k], device_id=(tx, ty, c), device_id_type=MESH)
                cp.start()
                copies.append(cp)
        for cp in copies:
            cp.wait()

    shapes = [SDS(a.shape, a.dtype) for a in arrs]
    return _hbm_call(body, arrs, shapes, 3, name)


def sibling_swap(arrs, name):
    n = len(arrs)

    def body(*refs):
        ins, outs = refs[:n], refs[n:2 * n]
        send, recv, _ = refs[2 * n:]
        x, y, c = _position()
        copies = []
        for a in range(n):
            cp = pltpu.make_async_remote_copy(
                src_ref=ins[a], dst_ref=outs[a], send_sem=send.at[a, 0], recv_sem=recv.at[a, 0],
                device_id=(x, y, 1 - c), device_id_type=MESH)
            cp.start()
            copies.append(cp)
        for cp in copies:
            cp.wait()

    shapes = [SDS(a.shape, a.dtype) for a in arrs]
    return _hbm_call(body, arrs, shapes, 1, name)


def device_allgather(arrs, name):
    n = len(arrs)

    def body(*refs):
        ins, outs = refs[:n], refs[n:2 * n]
        send, recv, loc = refs[2 * n:]
        x, y, c = _position()
        me = 4 * x + 2 * y + c
        copies = []
        for a in range(n):
            local = pltpu.make_async_copy(ins[a], outs[a].at[me], loc.at[a])
            local.start()
            copies.append(local)
            for k, (fx, fy, fc) in enumerate(_DEV_FLIPS):
                cp = pltpu.make_async_remote_copy(
                    src_ref=ins[a], dst_ref=outs[a].at[me], send_sem=send.at[a, k], recv_sem=recv.at[a, k],
                    device_id=(x ^ fx, y ^ fy, c ^ fc), device_id_type=MESH)
                cp.start()
                copies.append(cp)
        for cp in copies:
            cp.wait()

    shapes = [SDS((N_DEV,) + a.shape, a.dtype) for a in arrs]
    return _hbm_call(body, arrs, shapes, 7, name)


def _pad_to(a, axis, size):
    pad = [(0, 0)] * a.ndim
    pad[axis] = (0, size - a.shape[axis])
    return jnp.pad(a, pad)


def _cols_to_z(w):
    return _pad_to(jnp.concatenate([w[..., :768], w[..., 772:], w[..., 768:772]], axis=-1), w.ndim - 1, ZC)


def _cols_from_z(w):
    return jnp.concatenate([w[..., :768], w[..., 2304:2308], w[..., 768:2304]], axis=-1)


def _size(shape):
    n = 1
    for d in shape:
        n *= d
    return n


def _pack(parts):
    rows = []
    for p in parts:
        n = -(-p.size // 128)
        rows.append(_pad_to(p.reshape(-1), 0, n * 128).reshape(n, 128))
    packed = jnp.concatenate(rows, axis=0)
    return _pad_to(packed, 0, -(-packed.shape[0] // 16) * 16)


def _unpack(packed, shapes):
    out, off = [], 0
    for s in shapes:
        n = -(-_size(s) // 128)
        out.append(packed[off:off + n].reshape(-1)[:_size(s)].reshape(s))
        off += n
    return out


def _layer_fwd(l, x0, mod, p, fetch):
    sh1, sc1, ga1, sh2, sc2, ga2 = mod
    t = f"l{l}_"
    h1 = modnorm_fwd(x0, p['g_pre_mix'], sc1, sh1, t + "modnorm1")
    wt = dict(fetch('w_in', h1))
    z = mm_nn(h1, wt['w_in'], F32, t + "proj_in")
    cumc, cumr = fgate_fwd(z, p['b_fgate'], t + "fgate")
    y_fox, lse = fox_fwd(z, cumc, cumr, t + "fox")
    hc = conv_dw_fwd(z, wt['conv_w'], p['conv_b'], t + "conv_dw")
    y_conv = conv_pw_fwd(hc, p['conv_ln_g'], p['conv_ln_b'], wt['conv_pw_w'], p['conv_pw_b'], t + "conv_pw")
    y_swa = swa_fwd(z, p['swa_sinks'], t + "swa")
    y_sgu = sgu_fwd(z, p['sgu_ln_g'], p['sgu_ln_b'], p['sgu_w'], p['sgu_bt'], t + "sgu")
    ys = (y_fox, y_conv, y_swa, y_sgu)
    ycat = group_norm_fwd(ys, p['g_group'], t + "group_norm")
    wt.update(fetch('w_out', ycat))
    ymix = mm_nn(ycat, wt['w_out'], F32, t + "proj_out")
    x1 = resid_fwd(x0, ymix, ga1, p['g_post_mix'], t + "resid1")
    h2 = modnorm_fwd(x1, p['g_pre_ffn'], sc2, sh2, t + "modnorm2")
    wt.update(fetch('ffn_w_up', h2))
    u = mm_nn(h2, wt['ffn_w_up'], F32, t + "ffn_up")
    act = ffn_gate_fwd(u, wt['ffn_conv_w'], p['ffn_conv_b'], t + "ffn_gate")
    wt.update(fetch('ffn_w_down', act))
    yffn = mm_nn(act, wt['ffn_w_down'], F32, t + "ffn_down")
    x2 = resid_fwd(x1, yffn, ga2, p['g_post_ffn'], t + "resid2")
    res = dict(x0=x0, h1=h1, z=z, cumc=cumc, cumr=cumr, y_fox=y_fox, lse=lse, hc=hc, ys=ys, ycat=ycat, ymix=ymix,
               x1=x1, h2=h2, u=u, act=act, yffn=yffn, wt=wt)
    return x2, res


def _layer_bwd(l, dx2, mod, p, r, emit):
    sh1, sc1, ga1, sh2, sc2, ga2 = mod
    t = f"l{l}_bwd_"
    g = {}
    wt = r['wt']
    dyffn, dga2, g['g_post_ffn'] = resid_bwd(r['yffn'], ga2, p['g_post_ffn'], dx2, t + "resid2")
    tok = emit({'ffn_w_down': mm_tn(r['act'], dyffn, t + "ffn_down_dw")})
    dact = mm_nt(dyffn, wt['ffn_w_down'], t + "ffn_down_dx")
    dug, duv, dwg, dwv, dbg, dbv = ffn_gate_bwd(r['u'], wt['ffn_conv_w'], p['ffn_conv_b'] + tok, dact, t + "ffn_gate")
    du = jnp.concatenate([dug, duv], axis=1)
    g['ffn_conv_b'] = jnp.concatenate([dbg, dbv], axis=1)
    tok = emit({'ffn_w_up': mm_tn(r['h2'], du, t + "ffn_up_dw", sharded=True)})
    dh2 = mm_nt(du, wt['ffn_w_up'], t + "ffn_up_dx")
    dx1, g['g_pre_ffn'], dsc2, dsh2 = modnorm_bwd(r['x1'], p['g_pre_ffn'] + tok, sc2, sh2, dh2, dx2, t + "modnorm2")
    dymix, dga1, g['g_post_mix'] = resid_bwd(r['ymix'], ga1, p['g_post_mix'], dx1, t + "resid1")
    tok = emit({'w_out': mm_tn(r['ycat'], dymix, t + "proj_out_dw")})
    dcat = mm_nt(dymix, wt['w_out'], t + "proj_out_dx")
    dy_fox, dy_conv, dy_swa, dy_sgu, g['g_group'] = group_norm_bwd(r['ys'], p['g_group'] + tok, dcat, t + "group_norm")
    z = r['z']
    fq, fk, fv, dcq, dck = fox_bwd(z, r['cumc'], r['cumr'], r['lse'], r['y_fox'], dy_fox, t + "fox")
    dzf, g['b_fgate'] = fgate_bwd(z, p['b_fgate'], dcq, dck, t + "fgate")
    dhc, g['conv_ln_g'], g['conv_ln_b'], dpw, g['conv_pw_b'] = conv_pw_bwd(
        r['hc'], p['conv_ln_g'], p['conv_ln_b'], wt['conv_pw_w'], p['conv_pw_b'], dy_conv, t + "conv_pw")
    ca, cg, dcw, g['conv_b'] = conv_dw_bwd(z, wt['conv_w'], dhc, t + "conv_dw")
    sq, sk, sv, g['swa_sinks'] = swa_bwd(z, p['swa_sinks'], dy_swa, t + "swa")
    gu, gv, g['sgu_ln_g'], g['sgu_ln_b'], g['sgu_w'], g['sgu_bt'] = sgu_bwd(
        z, p['sgu_ln_g'], p['sgu_ln_b'], p['sgu_w'], p['sgu_bt'], dy_sgu, t + "sgu")
    dz = jnp.concatenate([fq, fk, fv, ca, cg, sq, sk, sv, gu, gv, dzf], axis=1)
    tok = emit({'w_in': mm_tn(r['h1'], dz, t + "proj_in_dw"), 'conv_w': dcw, 'conv_pw_w': dpw,
                'ffn_conv_w': jnp.concatenate([dwg, dwv], axis=1)})
    dh1 = mm_nt(dz, wt['w_in'], t + "proj_in_dx")
    dx0, g['g_pre_mix'], dsc1, dsh1 = modnorm_bwd(r['x0'], p['g_pre_mix'] + tok, sc1, sh1, dh1, dx1, t + "modnorm1")
    return dx0, g, (dsh1, dsc1, dga1, dsh2, dsc2, dga2)


def _layer_params(l, w):
    def row(name, width=None):
        v = w[name][l].reshape(1, -1)
        return v if width is None else _pad_to(v, 1, width)
    return {
        'g_pre_mix': row('g_pre_mix'), 'g_post_mix': row('g_post_mix'), 'g_pre_ffn': row('g_pre_ffn'),
        'g_post_ffn': row('g_post_ffn'), 'b_fgate': row('b_fgate', 128), 'conv_b': row('conv_b'),
        'conv_ln_g': row('conv_ln_g'), 'conv_ln_b': row('conv_ln_b'), 'conv_pw_b': row('conv_pw_b'),
        'swa_sinks': row('swa_sinks', 128), 'sgu_ln_g': row('sgu_ln_g'), 'sgu_ln_b': row('sgu_ln_b'),
        'sgu_w': w['sgu_w'][l], 'sgu_bt': _pad_to(w['sgu_b'][l].T, 1, 128), 'g_group': row('g_group'),
        'ffn_conv_b': row('ffn_conv_b'),
    }


def _w_in_from_shards(s):
    return _cols_to_z(jnp.transpose(s, (1, 0, 2)).reshape(D, IN_COLS))


def _w_in_to_shards(g):
    return jnp.transpose(_cols_from_z(g).reshape(D, N_CHIP, IN_COLS // N_CHIP), (1, 0, 2))


def _cols_from_shards(s, rows):
    _, r, n = s.shape
    return _pad_to(jnp.transpose(s, (1, 0, 2)).reshape(r, N_CHIP * n), 0, rows)


def _cols_to_shards(g, r):
    n = g.shape[1] // N_CHIP
    return jnp.transpose(g[:r].reshape(r, N_CHIP, n), (1, 0, 2))


_FROM_SHARDS = {
    'w_in': _w_in_from_shards,
    'w_out': lambda s: s.reshape(D, D),
    'ffn_w_up': lambda s: s,
    'ffn_w_down': lambda s: s.reshape(DFF, D),
    'conv_w': lambda s: _cols_from_shards(s, 32),
    'conv_pw_w': lambda s: s.reshape(GW, GW),
    'ffn_conv_w': lambda s: _cols_from_shards(s, 8),
}
_TO_SHARDS = {
    'w_in': _w_in_to_shards,
    'w_out': lambda g: g.reshape(N_CHIP, D // N_CHIP, D),
    'ffn_w_up': lambda g: g,
    'ffn_w_down': lambda g: g.reshape(N_CHIP, DFF // N_CHIP, D),
    'conv_w': lambda g: _cols_to_shards(g, CONV_K),
    'conv_pw_w': lambda g: g.reshape(N_CHIP, GW // N_CHIP, GW),
    'ffn_conv_w': lambda g: _cols_to_shards(g, FFN_K),
}


def _small_grads(g):
    return {
        'g_pre_mix': g['g_pre_mix'][0], 'g_post_mix': g['g_post_mix'][0], 'g_pre_ffn': g['g_pre_ffn'][0],
        'g_post_ffn': g['g_post_ffn'][0], 'b_fgate': g['b_fgate'][0, :4], 'conv_b': g['conv_b'][0],
        'conv_ln_g': g['conv_ln_g'][0], 'conv_ln_b': g['conv_ln_b'][0], 'conv_pw_b': g['conv_pw_b'][0],
        'swa_sinks': g['swa_sinks'][0, :4], 'sgu_ln_g': g['sgu_ln_g'][0], 'sgu_ln_b': g['sgu_ln_b'][0],
        'sgu_w': g['sgu_w'], 'sgu_b': g['sgu_bt'][:, :4].T, 'g_group': g['g_group'][0],
        'ffn_conv_b': g['ffn_conv_b'][0],
    }


def _local_step(xs, target, mod, w, fetch, emit):
    params, resids, mods = [], [], []
    for l in range(DEPTH):
        params.append(_layer_params(l, w))
        mods.append(tuple(mod[l, j] for j in range(6)))
        xs, r = _layer_fwd(l, xs, mods[l], params[l], functools.partial(fetch, l))
        resids.append(r)
    dx, loss_row = loss_grad(xs, target, "loss")
    grads, dmods = [None] * DEPTH, [None] * DEPTH
    for l in reversed(range(DEPTH)):
        dx, grads[l], dmods[l] = _layer_bwd(l, dx, mods[l], params[l], resids[l], functools.partial(emit, l))
    return loss_row, dx, grads, dmods


_MATMUL_WEIGHTS = ('w_in', 'w_out', 'ffn_w_up', 'ffn_w_down')
_CONV_WEIGHTS = ('conv_w', 'conv_pw_w', 'ffn_conv_w')
_FETCH_GROUPS = {'w_in': ('w_in',) + _CONV_WEIGHTS, 'w_out': ('w_out',), 'ffn_w_up': ('ffn_w_up',),
                 'ffn_w_down': ('ffn_w_down',)}


def kernel(x, c, w_ada, b_ada, g_pre_mix, g_post_mix, g_pre_ffn, g_post_ffn, w_in, b_fgate, conv_w, conv_b, conv_ln_g, conv_ln_b, conv_pw_w, conv_pw_b, swa_sinks, sgu_ln_g, sgu_ln_b, sgu_w, sgu_b, g_group, w_out, ffn_w_up, ffn_conv_w, ffn_conv_b, ffn_w_down, loss_target, m_w_ada, m_b_ada, m_g_pre_mix, m_g_post_mix, m_g_pre_ffn, m_g_post_ffn, m_w_in, m_b_fgate, m_conv_w, m_conv_b, m_conv_ln_g, m_conv_ln_b, m_conv_pw_w, m_conv_pw_b, m_swa_sinks, m_sgu_ln_g, m_sgu_ln_b, m_sgu_w, m_sgu_b, m_g_group, m_w_out, m_ffn_w_up, m_ffn_conv_w, m_ffn_conv_b, m_ffn_w_down, v_w_ada, v_b_ada, v_g_pre_mix, v_g_post_mix, v_g_pre_ffn, v_g_post_ffn, v_w_in, v_b_fgate, v_conv_w, v_conv_b, v_conv_ln_g, v_conv_ln_b, v_conv_pw_w, v_conv_pw_b, v_swa_sinks, v_sgu_ln_g, v_sgu_ln_b, v_sgu_w, v_sgu_b, v_g_group, v_w_out, v_ffn_w_up, v_ffn_conv_w, v_ffn_conv_b, v_ffn_w_down):
    args = locals()
    w = {n: args[n] for n in WEIGHTS}
    m = {n: args['m_' + n] for n in WEIGHTS}
    v = {n: args['v_' + n] for n in WEIGHTS}
    xi, yi, ci = _position()
    chip = 2 * xi + yi

    (c_all,) = device_allgather([c], "gather_c")
    c_all = c_all.reshape(N_DEV, D)
    b_loc = lax.dynamic_slice_in_dim(b_ada, chip * ADA_COLS, ADA_COLS, axis=1).reshape(DEPTH, 1, ADA_COLS)
    mod_all = ada_mod(c_all, w_ada, b_loc, "ada_mod")
    mine = lax.dynamic_index_in_dim(mod_all.reshape(DEPTH, N_CHIP, 2, ADA_COLS), ci, axis=2, keepdims=False)
    (mod4,) = chip_alltoall([jnp.transpose(mine, (1, 0, 2))], "scatter_mod")

    keys = [(n, l) for l in range(DEPTH) for n in _CONV_WEIGHTS]
    keys += [(n, l) for l in range(DEPTH) for n in _MATMUL_WEIGHTS]
    srcs = [w[n][l].astype(BF16) if n in _MATMUL_WEIGHTS else w[n][l] for n, l in keys]
    mod4, srcs = lax.optimization_barrier((mod4, srcs))
    lands = [lax.dynamic_update_slice_in_dim(lax.empty((N_CHIP,) + s.shape, s.dtype), s[None], chip, axis=0)
             for s in srcs]
    flights, token = exchange_start(srcs, lands, False, "gather_start")
    gathering = dict(zip(keys, flights))
    mod = jnp.transpose(mod4, (1, 0, 2)).reshape(DEPTH, 6, 1, D) + token[0:1, 0:1]

    def fetch(l, name, after):
        names = _FETCH_GROUPS[name]
        landed = exchange_wait([gathering[(n, l)] for n in names], False, after, f"gather_wait_l{l}_{name}")
        return {n: _FROM_SHARDS[n](land) for n, (_, land) in zip(names, landed)}

    scattering = {}

    def emit(l, grads):
        names = list(grads)
        pieces = [_TO_SHARDS[n](grads[n]) for n in names]
        lands = [lax.empty(p.shape, p.dtype) for p in pieces]
        started, token = exchange_start(pieces, lands, True, f"scatter_start_l{l}_{names[0]}")
        scattering.update({(n, l): f for n, f in zip(names, started)})
        return token[0:1, 0:1]

    loss_row, dx, grads, dmods = _local_step(x.reshape(S, D), loss_target.reshape(S, D), mod, w, fetch, emit)
    loss = lax.psum(loss_row[0, 0], ("x", "y", "c"))
    grad_x = dx.reshape(1, S, D)

    small = [_small_grads(g) for g in grads]
    local = {n: jnp.stack([small[l][n] for l in range(DEPTH)]) for n in SMALL if n != 'b_ada'}
    local['b_ada'] = jnp.stack([jnp.concatenate(dmods[l], axis=1)[0] for l in range(DEPTH)])
    (g_all,) = device_allgather([_pack([local[n] for n in SMALL])], "gather_small_grads")
    shapes = [w[n].shape for n in SMALL]
    packed = adamw_gathered(_pack([w[n] for n in SMALL]), g_all, _pack([m[n] for n in SMALL]),
                            _pack([v[n] for n in SMALL]), "adamw_small")
    out = {n: tuple(a) for n, a in zip(SMALL, zip(*[_unpack(pk, shapes) for pk in packed]))}

    n_ada = DEPTH * 6 * D // 128
    dmod_all = g_all[:, :n_ada, :].reshape(N_DEV, DEPTH, 6 * D)
    dmod_loc = jnp.transpose(lax.dynamic_slice_in_dim(dmod_all, chip * ADA_COLS, ADA_COLS, axis=2), (1, 0, 2))
    out['w_ada'] = tuple(ada_update(c_all.T, dmod_loc, w_ada, m['w_ada'], v['w_ada'], "adamw_ada"))

    order = list(scattering)
    landed = dict(zip(order, exchange_wait([scattering[k] for k in order], True, dx, "scatter_wait")))
    part = {}
    for n in SHARDED:
        cols = w[n].shape[-1]
        for l in range(DEPTH):
            src, land = landed[(n, l)]
            own = lax.dynamic_index_in_dim(src, chip, axis=0, keepdims=True)
            recv = lax.dynamic_update_slice_in_dim(land, own, chip, axis=0)
            part[n] = sum_pieces(recv.reshape(N_CHIP, -1, cols), l, part.get(n), f"sum_{n}_l{l}")
        part[n] = part[n].reshape(-1, cols)
    other = dict(zip(SHARDED, sibling_swap([part[n] for n in SHARDED], "swap_partials")))
    for n in SHARDED:
        cols = w[n].shape[-1]
        res = adamw_pair(w[n].reshape(-1, cols), part[n], other[n], m[n].reshape(-1, cols), v[n].reshape(-1, cols),
                         "adamw_" + n)
        out[n] = tuple(a.reshape(w[n].shape) for a in res)

    return (loss, grad_x, *[out[n][0] for n in WEIGHTS], *[out[n][1] for n in WEIGHTS],
            *[out[n][2] for n in WEIGHTS], *[out[n][3] for n in WEIGHTS])
```

```python
import functools

import jax
import jax.numpy as jnp
from jax import lax
from jax.experimental import pallas as pl
from jax.experimental.pallas import tpu as pltpu

F32 = jnp.float32
BF16 = jnp.bfloat16
SDS = jax.ShapeDtypeStruct
MESH = pl.DeviceIdType.MESH
ANY = pl.BlockSpec(memory_space=pl.ANY)

DEPTH = 2
S = 2048
D = 1024
GW = 256
DFF = 2816
NUP = 2 * DFF
IN_COLS = 2308
ZC = 2432
CONV_K = 31
FFN_K = 3
EPS = 1e-6
SCALE = 0.125
NEG = -1e30
N_CHIP = 4
N_DEV = 8

Z_FOX_Q, Z_FOX_K, Z_FOX_V = 0, 256, 512
Z_CONV_A, Z_CONV_G = 768, 1024
Z_SWA_Q, Z_SWA_K, Z_SWA_V = 1280, 1536, 1664
Z_SGU_U, Z_SGU_V = 1792, 2048
Z_FG = 2304

ADAM_LR, ADAM_B1, ADAM_B2, ADAM_EPS, ADAM_WD, ADAM_STEP = 0.001, 0.9, 0.999, 1e-08, 0.01, 10

TS = 256
TM = 512
N_SPLIT = 2816
N_BLOCK = 1408

WEIGHTS = ['w_ada', 'b_ada', 'g_pre_mix', 'g_post_mix', 'g_pre_ffn', 'g_post_ffn', 'w_in', 'b_fgate', 'conv_w',
           'conv_b', 'conv_ln_g', 'conv_ln_b', 'conv_pw_w', 'conv_pw_b', 'swa_sinks', 'sgu_ln_g', 'sgu_ln_b',
           'sgu_w', 'sgu_b', 'g_group', 'w_out', 'ffn_w_up', 'ffn_conv_w', 'ffn_conv_b', 'ffn_w_down']
SHARDED = ['w_in', 'conv_w', 'conv_pw_w', 'w_out', 'ffn_w_up', 'ffn_conv_w', 'ffn_w_down']
SMALL = [n for n in WEIGHTS if n not in SHARDED and n != 'w_ada']


def _rms(x, g):
    return x * lax.rsqrt(jnp.mean(x * x, axis=-1, keepdims=True) + EPS) * g


def _modnorm(x, g, sc, sh):
    return _rms(x, g) * (1.0 + sc) + sh


def _resid(x, y, ga, g):
    return x + ga * _rms(y, g)


@functools.partial(jax.custom_vjp, nondiff_argnums=(1,))
def _shift_down(x, n):
    if n == 0:
        return x
    row = lax.broadcasted_iota(jnp.int32, x.shape, 0)
    return jnp.where(row >= n, pltpu.roll(x, n, axis=0), 0.0)


def _shift_up(x, n):
    if n == 0:
        return x
    rows = x.shape[0]
    row = lax.broadcasted_iota(jnp.int32, x.shape, 0)
    return jnp.where(row < rows - n, pltpu.roll(x, rows - n, axis=0), 0.0)


def _shift_down_fwd(x, n):
    return _shift_down(x, n), None


def _shift_down_bwd(n, _, ct):
    return (_shift_up(ct, n),)


_shift_down.defvjp(_shift_down_fwd, _shift_down_bwd)


def _nt(a, b):
    return lax.dot_general(a, b, (((1,), (1,)), ((), ())), preferred_element_type=F32)


def _tn(a, b):
    return lax.dot_general(a, b, (((0,), (0,)), ((), ())), preferred_element_type=F32)


def _nn(a, b):
    return jnp.dot(a, b, preferred_element_type=F32)


def _acc(ref, val, first):
    @pl.when(first)
    def _():
        ref[...] = val

    @pl.when(jnp.logical_not(first))
    def _():
        ref[...] += val


def _row_tile(rows, cols):
    limit = max(8, (1 << 20) // (4 * cols))
    best = None
    for t in range(8, rows + 1, 8):
        if rows % t == 0 and t <= limit:
            best = t
    return best if best is not None else rows


def _ncol(n):
    return n if n <= N_SPLIT else N_BLOCK


def _weight_spec(b, order):
    pick = (lambda j, i: j) if order == 0 else (lambda i, j: j)
    if b.ndim == 3:
        _, k, tn = b.shape
        return pl.BlockSpec((None, k, tn), lambda *g: (pick(*g), 0, 0)), k, N_CHIP * tn, tn
    k, n = b.shape
    tn = _ncol(n)
    return pl.BlockSpec((k, tn), lambda *g: (0, pick(*g))), k, n, tn


def mm_nn(a, b, out_dtype, name):
    m = a.shape[0]
    b_spec, k, n, tn = _weight_spec(b, 0)

    def body(a_ref, b_ref, o_ref):
        o_ref[...] = _nn(a_ref[...], b_ref[...]).astype(out_dtype)

    return pl.pallas_call(
        body, grid=(n // tn, m // TM), name=name,
        in_specs=[pl.BlockSpec((TM, k), lambda j, i: (i, 0)), b_spec],
        out_specs=pl.BlockSpec((TM, tn), lambda j, i: (i, j)),
        out_shape=SDS((m, n), out_dtype),
    )(a, b)


def mm_nt(a, b, name):
    m = a.shape[0]
    b_spec, k, n, tc = _weight_spec(b, 1)

    def body(a_ref, b_ref, o_ref):
        _acc(o_ref, _nt(a_ref[...], b_ref[...]), pl.program_id(1) == 0)

    return pl.pallas_call(
        body, grid=(m // TM, n // tc), name=name,
        in_specs=[pl.BlockSpec((TM, tc), lambda i, c: (i, c)), b_spec],
        out_specs=pl.BlockSpec((TM, k), lambda i, c: (i, 0)),
        out_shape=SDS((m, k), F32),
    )(a, b)


def mm_tn(a, b, name, sharded=False):
    m, k = a.shape
    n = b.shape[1]
    tn = n // N_CHIP if sharded else _ncol(n)
    steps = m // TM

    def body(a_ref, b_ref, o_ref, acc_ref):
        i = pl.program_id(1)
        _acc(acc_ref, _tn(a_ref[...], b_ref[...]), i == 0)

        @pl.when(i == steps - 1)
        def _():
            o_ref[...] = acc_ref[...].astype(BF16)

    if sharded:
        out_spec, out_shape = pl.BlockSpec((None, k, tn), lambda j, i: (j, 0, 0)), SDS((N_CHIP, k, tn), BF16)
    else:
        out_spec, out_shape = pl.BlockSpec((k, tn), lambda j, i: (0, j)), SDS((k, n), BF16)
    return pl.pallas_call(
        body, grid=(n // tn, steps), name=name,
        in_specs=[pl.BlockSpec((TM, k), lambda j, i: (i, 0)), pl.BlockSpec((TM, tn), lambda j, i: (i, j))],
        out_specs=out_spec, out_shape=out_shape,
        scratch_shapes=[pltpu.VMEM((k, tn), F32)],
    )(a, b)


_ROW = pl.BlockSpec((TS, D), lambda i: (i, 0))
_VEC = pl.BlockSpec((1, D), lambda i: (0, 0))


def modnorm_fwd(x, g, sc, sh, name):
    def body(x_ref, g_ref, sc_ref, sh_ref, o_ref):
        o_ref[...] = _modnorm(x_ref[...], g_ref[...], sc_ref[...], sh_ref[...]).astype(BF16)

    return pl.pallas_call(body, grid=(S // TS,), name=name, in_specs=[_ROW, _VEC, _VEC, _VEC], out_specs=_ROW,
                          out_shape=SDS((S, D), BF16))(x, g, sc, sh)


def modnorm_bwd(x, g, sc, sh, dh, dx_in, name):
    def body(x_ref, g_ref, sc_ref, sh_ref, dh_ref, dxin_ref, dx_ref, dg_ref, dsc_ref, dsh_ref):
        _, vjp = jax.vjp(_modnorm, x_ref[...], g_ref[...], sc_ref[...], sh_ref[...])
        dx, dg, dsc, dsh = vjp(dh_ref[...])
        dx_ref[...] = dxin_ref[...] + dx
        first = pl.program_id(0) == 0
        _acc(dg_ref, dg, first)
        _acc(dsc_ref, dsc, first)
        _acc(dsh_ref, dsh, first)

    vec = SDS((1, D), F32)
    return pl.pallas_call(body, grid=(S // TS,), name=name, in_specs=[_ROW, _VEC, _VEC, _VEC, _ROW, _ROW],
                          out_specs=[_ROW, _VEC, _VEC, _VEC], out_shape=[SDS((S, D), F32), vec, vec, vec])(
                              x, g, sc, sh, dh, dx_in)


def resid_fwd(x, y, ga, g, name):
    def body(x_ref, y_ref, ga_ref, g_ref, o_ref):
        o_ref[...] = _resid(x_ref[...], y_ref[...], ga_ref[...], g_ref[...])

    return pl.pallas_call(body, grid=(S // TS,), name=name, in_specs=[_ROW, _ROW, _VEC, _VEC], out_specs=_ROW,
                          out_shape=SDS((S, D), F32))(x, y, ga, g)


def resid_bwd(y, ga, g, dxo, name):
    def body(y_ref, ga_ref, g_ref, dxo_ref, dy_ref, dga_ref, dg_ref):
        _, vjp = jax.vjp(lambda y, ga, g: ga * _rms(y, g), y_ref[...], ga_ref[...], g_ref[...])
        dy, dga, dg = vjp(dxo_ref[...])
        dy_ref[...] = dy.astype(BF16)
        first = pl.program_id(0) == 0
        _acc(dga_ref, dga, first)
        _acc(dg_ref, dg, first)

    vec = SDS((1, D), F32)
    return pl.pallas_call(body, grid=(S // TS,), name=name, in_specs=[_ROW, _VEC, _VEC, _ROW],
                          out_specs=[_ROW, _VEC, _VEC], out_shape=[SDS((S, D), BF16), vec, vec])(y, ga, g, dxo)


def loss_grad(xf, target, name):
    def body(x_ref, t_ref, dx_ref, l_ref):
        err = x_ref[...] - t_ref[...]
        dx_ref[...] = err * (1.0 / D)
        part = 0.5 * jnp.sum(jnp.mean(err * err, axis=-1, keepdims=True), axis=0, keepdims=True)
        _acc(l_ref, jnp.broadcast_to(part, (1, 128)), pl.program_id(0) == 0)

    return pl.pallas_call(body, grid=(S // TS,), name=name, in_specs=[_ROW, _ROW],
                          out_specs=[_ROW, pl.BlockSpec((1, 128), lambda i: (0, 0))],
                          out_shape=[SDS((S, D), F32), SDS((1, 128), F32)])(xf, target)


_FG_SPEC = pl.BlockSpec((S, 128), lambda i: (0, Z_FG // 128))


def _tri128(lower):
    r = lax.broadcasted_iota(jnp.int32, (128, 128), 0)
    c = lax.broadcasted_iota(jnp.int32, (128, 128), 1)
    return ((r >= c) if lower else (r <= c)).astype(F32)


def fgate_fwd(z, bf, name):
    def body(z_ref, b_ref, cc_ref, cr_ref):
        tri = _tri128(True)
        carry = jnp.zeros((1, 128), F32)
        for i in range(S // 128):
            rows = pl.ds(i * 128, 128)
            lf = jax.nn.log_sigmoid(z_ref[rows, :] + b_ref[...])
            c = jnp.dot(tri, lf, precision=lax.Precision.HIGHEST, preferred_element_type=F32) + carry
            cc_ref[rows, :] = c
            carry = c[127:128, :]
        cr_ref[...] = cc_ref[...].T

    return pl.pallas_call(body, name=name, grid=(1,),
                          in_specs=[_FG_SPEC, pl.BlockSpec((1, 128), lambda i: (0, 0))],
                          out_specs=[pl.BlockSpec((S, 128), lambda i: (0, 0)), pl.BlockSpec((128, S), lambda i: (0, 0))],
                          out_shape=[SDS((S, 128), F32), SDS((128, S), F32)])(z, bf)


def fgate_bwd(z, bf, dcq, dck, name):
    def body(z_ref, b_ref, dcq_ref, dck_ref, dz_ref, db_ref, col_ref):
        col_ref[...] = dcq_ref[...] + jnp.concatenate([dck_ref[...], jnp.zeros((120, S), F32)], axis=0).T
        tri = _tri128(False)
        carry = jnp.zeros((1, 128), F32)
        db = jnp.zeros((1, 128), F32)
        for i in reversed(range(S // 128)):
            rows = pl.ds(i * 128, 128)
            dlf = jnp.dot(tri, col_ref[rows, :], precision=lax.Precision.HIGHEST, preferred_element_type=F32) + carry
            carry = dlf[0:1, :]
            dz = dlf * jax.nn.sigmoid(-(z_ref[rows, :] + b_ref[...]))
            dz_ref[rows, :] = dz.astype(BF16)
            db = db + jnp.sum(dz, axis=0, keepdims=True)
        db_ref[...] = db

    return pl.pallas_call(body, name=name, grid=(1,),
                          in_specs=[_FG_SPEC, pl.BlockSpec((1, 128), lambda i: (0, 0)),
                                    pl.BlockSpec((S, 128), lambda i: (0, 0)), pl.BlockSpec((8, S), lambda i: (0, 0))],
                          out_specs=[pl.BlockSpec((S, 128), lambda i: (0, 0)), pl.BlockSpec((1, 128), lambda i: (0, 0))],
                          out_shape=[SDS((S, 128), BF16), SDS((1, 128), F32)],
                          scratch_shapes=[pltpu.VMEM((S, 128), F32)])(z, bf, dcq, dck)


TQ = 256


def _head_mask(hh):
    lane = lax.broadcasted_iota(jnp.int32, (TQ, 128), 1)
    return (lane >= 64 * hh) & (lane < 64 * hh + 64)


def _fox_specs():
    q = pl.BlockSpec((TQ, 256), lambda i: (i, Z_FOX_Q // 256))
    k = pl.BlockSpec((S, 256), lambda i: (0, Z_FOX_K // 256))
    v = pl.BlockSpec((S, 256), lambda i: (0, Z_FOX_V // 256))
    cc = pl.BlockSpec((TQ, 128), lambda i: (i, 0))
    cr = pl.BlockSpec((8, S), lambda i: (0, 0))
    return q, k, v, cc, cr


FOX_SPAN = 2
FOX_GROUPS = S // (FOX_SPAN * TQ)


def _fox_scores(qm, k, cc_h, cr_h, i):
    klen = k.shape[0]
    s = _nt(qm, k) * SCALE + cc_h - cr_h
    qpos = i * TQ + lax.broadcasted_iota(jnp.int32, (TQ, klen), 0)
    kpos = lax.broadcasted_iota(jnp.int32, (TQ, klen), 1)
    return jnp.where(kpos <= qpos, s, NEG)


def _for_key_length(i, fn):
    for g in range(FOX_GROUPS):
        pl.when(i // FOX_SPAN == g)(functools.partial(fn, (g + 1) * FOX_SPAN * TQ))


def fox_fwd(z, cumc, cumr, name):
    def body(q_ref, k_ref, v_ref, cc_ref, cr_ref, o_ref, l_ref):
        i = pl.program_id(0)

        def block(klen):
            lane = lax.broadcasted_iota(jnp.int32, (TQ, 128), 1)
            cc = cc_ref[...]
            lse = jnp.zeros((TQ, 128), F32)
            for p in range(2):
                cols = pl.ds(128 * p, 128)
                q = q_ref[:, cols]
                k = k_ref[0:klen, cols].astype(BF16)
                v = v_ref[0:klen, cols].astype(BF16)
                o_pair = jnp.zeros((TQ, 128), F32)
                for hh in range(2):
                    h = 2 * p + hh
                    hm = _head_mask(hh)
                    qm = jnp.where(hm, q, 0.0).astype(BF16)
                    s = _fox_scores(qm, k, cc[:, h:h + 1], cr_ref[h:h + 1, 0:klen], i)
                    m = jnp.max(s, axis=1, keepdims=True)
                    e = jnp.exp(s - m)
                    l = jnp.sum(e, axis=1, keepdims=True)
                    o_pair = jnp.where(hm, _nn(e.astype(BF16), v) / l, o_pair)
                    lse = jnp.where(lane == h, m + jnp.log(l), lse)
                o_ref[:, cols] = o_pair
            l_ref[...] = lse

        _for_key_length(i, block)

    q, k, v, cc, cr = _fox_specs()
    return pl.pallas_call(body, grid=(S // TQ,), name=name, in_specs=[q, k, v, cc, cr],
                          out_specs=[pl.BlockSpec((TQ, 256), lambda i: (i, 0)), cc],
                          out_shape=[SDS((S, 256), F32), SDS((S, 128), F32)])(z, z, z, cumc, cumr)


def fox_bwd(z, cumc, cumr, lse, o, do, name):
    steps = S // TQ

    def body(q_ref, k_ref, v_ref, cc_ref, cr_ref, l_ref, o_ref, do_ref, dq_ref, dk_ref, dv_ref, dcq_ref, dck_ref,
             dk_acc, dv_acc):
        i = pl.program_id(0)

        @pl.when(i == 0)
        def _():
            dk_acc[...] = jnp.zeros_like(dk_acc)
            dv_acc[...] = jnp.zeros_like(dv_acc)
            dck_ref[...] = jnp.zeros_like(dck_ref)

        def block(klen):
            lane = lax.broadcasted_iota(jnp.int32, (TQ, 128), 1)
            cc = cc_ref[...]
            lse_all = l_ref[...]
            dcq = jnp.zeros((TQ, 128), F32)
            for p in range(2):
                cols = pl.ds(128 * p, 128)
                q = q_ref[:, cols]
                k = k_ref[0:klen, cols].astype(BF16)
                v = v_ref[0:klen, cols].astype(BF16)
                o_p = o_ref[:, cols]
                do_p = do_ref[:, cols]
                dq_pair = jnp.zeros((TQ, 128), F32)
                dk_pair = jnp.zeros((klen, 128), F32)
                dv_pair = jnp.zeros((klen, 128), F32)
                for hh in range(2):
                    h = 2 * p + hh
                    hm = _head_mask(hh)
                    qm = jnp.where(hm, q, 0.0).astype(BF16)
                    s = _fox_scores(qm, k, cc[:, h:h + 1], cr_ref[h:h + 1, 0:klen], i)
                    pn = jnp.exp(s - lse_all[:, h:h + 1])
                    dom = jnp.where(hm, do_p, 0.0)
                    dl = jnp.sum(dom * o_p, axis=1, keepdims=True)
                    dom = dom.astype(BF16)
                    ds = pn * (_nt(dom, v) - dl)
                    dsb = ds.astype(BF16)
                    dq_pair = jnp.where(hm, _nn(dsb, k) * SCALE, dq_pair)
                    dk_pair = dk_pair + _tn(dsb, qm) * SCALE
                    dv_pair = dv_pair + _tn(pn.astype(BF16), dom)
                    dck_ref[h:h + 1, 0:klen] -= jnp.sum(ds, axis=0, keepdims=True)
                    dcq = jnp.where(lane == h, jnp.sum(ds, axis=1, keepdims=True), dcq)
                dq_ref[:, cols] = dq_pair.astype(BF16)
                dk_acc[0:klen, cols] += dk_pair
                dv_acc[0:klen, cols] += dv_pair
            dcq_ref[...] = dcq

        _for_key_length(i, block)

        @pl.when(i == steps - 1)
        def _():
            dk_ref[...] = dk_acc[...].astype(BF16)
            dv_ref[...] = dv_acc[...].astype(BF16)

    q, k, v, cc, cr = _fox_specs()
    blk = pl.BlockSpec((TQ, 256), lambda i: (i, 0))
    full = pl.BlockSpec((S, 256), lambda i: (0, 0))
    return pl.pallas_call(body, grid=(steps,), name=name, in_specs=[q, k, v, cc, cr, cc, blk, blk],
                          out_specs=[blk, full, full, cc, cr],
                          out_shape=[SDS((S, 256), BF16), SDS((S, 256), BF16), SDS((S, 256), BF16), SDS((S, 128), F32),
                                     SDS((8, S), F32)],
                          scratch_shapes=[pltpu.VMEM((S, 256), F32), pltpu.VMEM((S, 256), F32)])(
                              z, z, z, cumc, cumr, lse, o, do)


W = 128
SWA_HEADS = 4


def _swa_core(first, qs, kcat, vcat, sink):
    r = lax.broadcasted_iota(jnp.int32, (SWA_HEADS * W, 2 * W), 0)
    j = lax.broadcasted_iota(jnp.int32, (SWA_HEADS * W, 2 * W), 1)
    qi = r & (W - 1)
    valid = ((j < W) & (j > qi) & jnp.logical_not(first)) | ((j >= W) & (j - W <= qi))
    s = jnp.where(valid, _nt(qs.astype(BF16), kcat.astype(BF16)) * SCALE, NEG)
    m = lax.stop_gradient(jnp.maximum(jnp.max(s, axis=1, keepdims=True), sink))
    e = jnp.exp(s - m)
    den = jnp.sum(e, axis=1, keepdims=True) + jnp.exp(sink - m)
    return _nn((e / den).astype(BF16), vcat.astype(BF16))


def _kv_lanes(kv):
    lane = lax.broadcasted_iota(jnp.int32, (W, 128), 1)
    return (lane >= 64 * kv) & (lane < 64 * kv + 64)


def _swa_stack(pair0, pair1):
    blocks = []
    for h in range(SWA_HEADS):
        kv, hh = h // 2, h % 2
        a = (pair0, pair1)[kv]
        a = a if hh == kv else pltpu.roll(a, 64, axis=1)
        blocks.append(jnp.where(_kv_lanes(kv), a, 0.0))
    return jnp.concatenate(blocks, axis=0)


def _swa_unstack(stacked):
    pairs = [jnp.zeros((W, 128), F32), jnp.zeros((W, 128), F32)]
    for h in range(SWA_HEADS):
        kv, hh = h // 2, h % 2
        a = jnp.where(_kv_lanes(kv), stacked[h * W:(h + 1) * W], 0.0)
        pairs[kv] = pairs[kv] + (a if hh == kv else pltpu.roll(a, 64, axis=1))
    return pairs


def _head_rows():
    return lax.broadcasted_iota(jnp.int32, (SWA_HEADS * W, 1), 0) // W


def _swa_operands(q_ref, kp_ref, kc_ref, vp_ref, vc_ref, sk_ref):
    qs = _swa_stack(q_ref[:, 0:128], q_ref[:, 128:256])
    kcat = jnp.concatenate([kp_ref[...], kc_ref[...]], axis=0)
    vcat = jnp.concatenate([vp_ref[...], vc_ref[...]], axis=0)
    sink = jnp.zeros((SWA_HEADS * W, 1), F32)
    for h in range(SWA_HEADS):
        sink = jnp.where(_head_rows() == h, sk_ref[:, h:h + 1], sink)
    return qs, kcat, vcat, sink


def _swa_specs():
    q = pl.BlockSpec((W, 256), lambda n: (n, Z_SWA_Q // 256))
    kc = pl.BlockSpec((W, 128), lambda n: (n, Z_SWA_K // 128))
    kp = pl.BlockSpec((W, 128), lambda n: (jnp.maximum(n - 1, 0), Z_SWA_K // 128))
    vc = pl.BlockSpec((W, 128), lambda n: (n, Z_SWA_V // 128))
    vp = pl.BlockSpec((W, 128), lambda n: (jnp.maximum(n - 1, 0), Z_SWA_V // 128))
    sk = pl.BlockSpec((1, 128), lambda n: (0, 0))
    return q, kp, kc, vp, vc, sk


def swa_fwd(z, sinks, name):
    def body(q_ref, kp_ref, kc_ref, vp_ref, vc_ref, sk_ref, o_ref):
        o = _swa_core(pl.program_id(0) == 0, *_swa_operands(q_ref, kp_ref, kc_ref, vp_ref, vc_ref, sk_ref))
        o0, o1 = _swa_unstack(o)
        o_ref[:, 0:128] = o0
        o_ref[:, 128:256] = o1

    return pl.pallas_call(body, grid=(S // W,), name=name, in_specs=list(_swa_specs()),
                          out_specs=pl.BlockSpec((W, 256), lambda n: (n, 0)),
                          out_shape=SDS((S, 256), F32))(z, z, z, z, z, sinks)


def swa_bwd(z, sinks, do, name):
    steps = S // W

    def body(q_ref, kp_ref, kc_ref, vp_ref, vc_ref, sk_ref, do_ref, dq_ref, dk_ref, dv_ref, dsk_ref, dk_acc, dv_acc):
        n = pl.program_id(0)
        first = n == 0
        _, vjp = jax.vjp(functools.partial(_swa_core, first), *_swa_operands(q_ref, kp_ref, kc_ref, vp_ref, vc_ref, sk_ref))
        dqs, dkcat, dvcat, dsink = vjp(_swa_stack(do_ref[:, 0:128], do_ref[:, 128:256]))
        dq0, dq1 = _swa_unstack(dqs)
        dq_ref[:, 0:128] = dq0.astype(BF16)
        dq_ref[:, 128:256] = dq1.astype(BF16)

        @pl.when(first)
        def _():
            dk_acc[...] = jnp.zeros_like(dk_acc)
            dv_acc[...] = jnp.zeros_like(dv_acc)

        cur = pl.ds(pl.multiple_of(n * W, W), W)
        dk_acc[cur, :] += dkcat[W:2 * W]
        dv_acc[cur, :] += dvcat[W:2 * W]

        @pl.when(n > 0)
        def _():
            prev = pl.ds(pl.multiple_of((n - 1) * W, W), W)
            dk_acc[prev, :] += dkcat[0:W]
            dv_acc[prev, :] += dvcat[0:W]

        lane = lax.broadcasted_iota(jnp.int32, (1, 128), 1)
        dsk = jnp.zeros((1, 128), F32)
        for h in range(SWA_HEADS):
            d = jnp.sum(jnp.where(_head_rows() == h, dsink, 0.0), axis=0, keepdims=True)
            dsk = jnp.where(lane == h, d, dsk)
        _acc(dsk_ref, dsk, first)

        @pl.when(n == steps - 1)
        def _():
            dk_ref[...] = dk_acc[...].astype(BF16)
            dv_ref[...] = dv_acc[...].astype(BF16)

    blk = pl.BlockSpec((W, 256), lambda n: (n, 0))
    full = pl.BlockSpec((S, 128), lambda n: (0, 0))
    return pl.pallas_call(body, grid=(steps,), name=name, in_specs=list(_swa_specs()) + [blk],
                          out_specs=[blk, full, full, pl.BlockSpec((1, 128), lambda n: (0, 0))],
                          out_shape=[SDS((S, 256), BF16), SDS((S, 128), BF16), SDS((S, 128), BF16), SDS((1, 128), F32)],
                          scratch_shapes=[pltpu.VMEM((S, 128), F32), pltpu.VMEM((S, 128), F32)])(
                              z, z, z, z, z, sinks, do)


def _glu(a, g):
    return a * jax.nn.sigmoid(g)


def _cv1_specs():
    a = pl.BlockSpec((S, 128), lambda j: (0, Z_CONV_A // 128 + j))
    g = pl.BlockSpec((S, 128), lambda j: (0, Z_CONV_G // 128 + j))
    w = pl.BlockSpec((32, 128), lambda j: (0, j))
    b = pl.BlockSpec((1, 128), lambda j: (0, j))
    h = pl.BlockSpec((S, 128), lambda j: (0, j))
    return a, g, w, b, h


def conv_dw_fwd(z, cw, cb, name):
    def body(a_ref, g_ref, w_ref, b_ref, o_ref):
        hh = _glu(a_ref[...], g_ref[...])
        acc = jnp.zeros((S, 128), F32) + b_ref[...]
        for k in range(CONV_K):
            acc = acc + _shift_down(hh, CONV_K - 1 - k) * w_ref[k:k + 1, :]
        o_ref[...] = acc

    a, g, w, b, h = _cv1_specs()
    return pl.pallas_call(body, grid=(2,), name=name, in_specs=[a, g, w, b], out_specs=h,
                          out_shape=SDS((S, 256), F32))(z, z, cw, cb)


def conv_dw_bwd(z, cw, dhc, name):
    def body(a_ref, g_ref, w_ref, dh_ref, da_ref, dg_ref, dw_ref, db_ref):
        hh, vjp = jax.vjp(_glu, a_ref[...], g_ref[...])
        dh = dh_ref[...]
        dhh = jnp.zeros((S, 128), F32)
        for k in range(CONV_K):
            n = CONV_K - 1 - k
            dhh = dhh + _shift_up(dh, n) * w_ref[k:k + 1, :]
            dw_ref[k:k + 1, :] = jnp.sum(dh * _shift_down(hh, n), axis=0, keepdims=True)
        dw_ref[CONV_K:32, :] = jnp.zeros((32 - CONV_K, 128), F32)
        db_ref[...] = jnp.sum(dh, axis=0, keepdims=True)
        da, dg = vjp(dhh)
        da_ref[...] = da.astype(BF16)
        dg_ref[...] = dg.astype(BF16)

    a, g, w, b, h = _cv1_specs()
    return pl.pallas_call(body, grid=(2,), name=name, in_specs=[a, g, w, h], out_specs=[h, h, w, b],
                          out_shape=[SDS((S, 256), BF16), SDS((S, 256), BF16), SDS((32, 256), F32), SDS((1, 256), F32)])(
                              z, z, cw, dhc)


def _ln(x, g, b):
    mu = jnp.mean(x, axis=-1, keepdims=True)
    xc = x - mu
    var = jnp.mean(xc * xc, axis=-1, keepdims=True)
    return xc * lax.rsqrt(var + EPS) * g + b


def _conv_pw(hc, lg, lb, pw, pb):
    y = jax.nn.silu(_ln(hc, lg, lb))
    return _nn(y.astype(BF16), pw.astype(BF16)) + pb


TS2 = 512
_ROW2 = pl.BlockSpec((TS2, 256), lambda i: (i, 0))
_VEC2 = pl.BlockSpec((1, 256), lambda i: (0, 0))
_MAT2 = pl.BlockSpec((256, 256), lambda i: (0, 0))


def conv_pw_fwd(hc, lg, lb, pw, pb, name):
    def body(h_ref, lg_ref, lb_ref, pw_ref, pb_ref, o_ref):
        o_ref[...] = _conv_pw(h_ref[...], lg_ref[...], lb_ref[...], pw_ref[...], pb_ref[...])

    return pl.pallas_call(body, grid=(S // TS2,), name=name, in_specs=[_ROW2, _VEC2, _VEC2, _MAT2, _VEC2],
                          out_specs=_ROW2, out_shape=SDS((S, 256), F32))(hc, lg, lb, pw, pb)


def conv_pw_bwd(hc, lg, lb, pw, pb, dy, name):
    def body(h_ref, lg_ref, lb_ref, pw_ref, pb_ref, dy_ref, dh_ref, dlg_ref, dlb_ref, dpw_ref, dpb_ref):
        _, vjp = jax.vjp(_conv_pw, h_ref[...], lg_ref[...], lb_ref[...], pw_ref[...], pb_ref[...])
        dh, dlg, dlb, dpw, dpb = vjp(dy_ref[...])
        dh_ref[...] = dh
        first = pl.program_id(0) == 0
        _acc(dlg_ref, dlg, first)
        _acc(dlb_ref, dlb, first)
        _acc(dpw_ref, dpw, first)
        _acc(dpb_ref, dpb, first)

    vec = SDS((1, 256), F32)
    return pl.pallas_call(body, grid=(S // TS2,), name=name, in_specs=[_ROW2, _VEC2, _VEC2, _MAT2, _VEC2, _ROW2],
                          out_specs=[_ROW2, _VEC2, _VEC2, _MAT2, _VEC2],
                          out_shape=[SDS((S, 256), F32), vec, vec, SDS((256, 256), F32), vec])(hc, lg, lb, pw, pb, dy)


def _sgu_block(u0, u1, v0, v1, lg0, lg1, lb0, lb1, w0, w1, w2, w3, bt):
    u0, u1, v0, v1 = (jax.nn.gelu(a) for a in (u0, u1, v0, v1))
    mu = (jnp.sum(v0, axis=1, keepdims=True) + jnp.sum(v1, axis=1, keepdims=True)) * (1.0 / GW)
    c0, c1 = v0 - mu, v1 - mu
    var = (jnp.sum(c0 * c0, axis=1, keepdims=True) + jnp.sum(c1 * c1, axis=1, keepdims=True)) * (1.0 / GW)
    r = lax.rsqrt(var + EPS)
    n0 = c0 * r * lg0 + lb0
    n1 = c1 * r * lg1 + lb1
    row = lax.broadcasted_iota(jnp.int32, (128, 128), 0)
    col = lax.broadcasted_iota(jnp.int32, (128, 128), 1)
    tri = row >= col
    outs = []
    for p, (n, u, wa, wb) in enumerate(((n0, u0, w0, w1), (n1, u1, w2, w3))):
        nb = n.astype(BF16)
        ma = _nn(jnp.where(tri, wa, 0.0).astype(BF16), nb)
        mb = _nn(jnp.where(tri, wb, 0.0).astype(BF16), nb)
        expand = (row == 2 * p + col // 64).astype(F32)
        bias = jnp.dot(bt, expand, precision=lax.Precision.HIGHEST, preferred_element_type=F32)
        outs.append(u * (jnp.where(col < 64, ma, mb) + bias))
    return outs[0], outs[1]


def _sgu_specs():
    def col(c):
        return pl.BlockSpec((128, 128), lambda n, c=c: (n, c))
    zs = [col(Z_SGU_U // 128), col(Z_SGU_U // 128 + 1), col(Z_SGU_V // 128), col(Z_SGU_V // 128 + 1)]
    vec = [pl.BlockSpec((1, 128), lambda n: (0, 0)), pl.BlockSpec((1, 128), lambda n: (0, 1))]
    ws = [pl.BlockSpec((None, 128, 128), lambda n, g=g: (g, 0, 0)) for g in range(4)]
    bt = pl.BlockSpec((128, 128), lambda n: (0, 0))
    return zs + vec + vec + ws + [bt]


def sgu_fwd(z, lg, lb, w, bt, name):
    def body(*refs):
        o_ref = refs[-1]
        y0, y1 = _sgu_block(*[r[...] for r in refs[:-1]])
        o_ref[:, 0:128] = y0
        o_ref[:, 128:256] = y1

    return pl.pallas_call(body, grid=(S // 128,), name=name, in_specs=_sgu_specs(),
                          out_specs=pl.BlockSpec((128, 256), lambda n: (n, 0)),
                          out_shape=SDS((S, 256), F32))(z, z, z, z, lg, lg, lb, lb, w, w, w, w, bt)


def sgu_bwd(z, lg, lb, w, bt, dy, name):
    def body(*refs):
        ins, dy_ref = refs[:13], refs[13]
        du_ref, dv_ref, dlg_ref, dlb_ref, dw_ref, dbt_ref = refs[14:]
        _, vjp = jax.vjp(_sgu_block, *[r[...] for r in ins])
        du0, du1, dv0, dv1, dlg0, dlg1, dlb0, dlb1, dw0, dw1, dw2, dw3, dbt = vjp((dy_ref[:, 0:128], dy_ref[:, 128:256]))
        du_ref[:, 0:128] = du0.astype(BF16)
        du_ref[:, 128:256] = du1.astype(BF16)
        dv_ref[:, 0:128] = dv0.astype(BF16)
        dv_ref[:, 128:256] = dv1.astype(BF16)
        first = pl.program_id(0) == 0

        @pl.when(first)
        def _():
            dlg_ref[...] = jnp.zeros_like(dlg_ref)
            dlb_ref[...] = jnp.zeros_like(dlb_ref)
            dw_ref[...] = jnp.zeros_like(dw_ref)
            dbt_ref[...] = jnp.zeros_like(dbt_ref)

        dlg_ref[:, 0:128] += dlg0
        dlg_ref[:, 128:256] += dlg1
        dlb_ref[:, 0:128] += dlb0
        dlb_ref[:, 128:256] += dlb1
        for g, d in enumerate((dw0, dw1, dw2, dw3)):
            dw_ref[g] += d
        dbt_ref[...] += dbt

    blk = pl.BlockSpec((128, 256), lambda n: (n, 0))
    vec = pl.BlockSpec((1, 256), lambda n: (0, 0))
    return pl.pallas_call(body, grid=(S // 128,), name=name, in_specs=_sgu_specs() + [blk],
                          out_specs=[blk, blk, vec, vec, pl.BlockSpec((4, 128, 128), lambda n: (0, 0, 0)),
                                     pl.BlockSpec((128, 128), lambda n: (0, 0))],
                          out_shape=[SDS((S, 256), BF16), SDS((S, 256), BF16), SDS((1, 256), F32), SDS((1, 256), F32),
                                     SDS((4, 128, 128), F32), SDS((128, 128), F32)])(
                                         z, z, z, z, lg, lg, lb, lb, w, w, w, w, bt, dy)


def _group_norm(y0, y1, y2, y3, g0, g1, g2, g3):
    return tuple(_rms(y, g) for y, g in zip((y0, y1, y2, y3), (g0, g1, g2, g3)))


_GROW = pl.BlockSpec((TS2, 256), lambda i: (i, 0))
_GCAT = pl.BlockSpec((TS2, D), lambda i: (i, 0))
_GVEC = [pl.BlockSpec((1, 256), lambda i, j=j: (0, j)) for j in range(4)]


def group_norm_fwd(ys, gg, name):
    def body(*refs):
        o_ref = refs[-1]
        outs = _group_norm(*[r[...] for r in refs[:-1]])
        for j, c in enumerate(outs):
            o_ref[:, 256 * j:256 * (j + 1)] = c.astype(BF16)

    return pl.pallas_call(body, grid=(S // TS2,), name=name, in_specs=[_GROW] * 4 + _GVEC, out_specs=_GCAT,
                          out_shape=SDS((S, D), BF16))(*ys, gg, gg, gg, gg)


def group_norm_bwd(ys, gg, dcat, name):
    def body(*refs):
        ins, dc_ref = refs[:8], refs[8]
        dy_refs, dg_ref = refs[9:13], refs[13]
        _, vjp = jax.vjp(_group_norm, *[r[...] for r in ins])
        grads = vjp(tuple(dc_ref[:, 256 * j:256 * (j + 1)] for j in range(4)))
        first = pl.program_id(0) == 0

        @pl.when(first)
        def _():
            dg_ref[...] = jnp.zeros_like(dg_ref)

        for j in range(4):
            dy_refs[j][...] = grads[j]
            dg_ref[:, 256 * j:256 * (j + 1)] += grads[4 + j]

    return pl.pallas_call(body, grid=(S // TS2,), name=name, in_specs=[_GROW] * 4 + _GVEC + [_GCAT],
                          out_specs=[_GROW] * 4 + [pl.BlockSpec((1, D), lambda i: (0, 0))],
                          out_shape=[SDS((S, 256), F32)] * 4 + [SDS((1, D), F32)])(*ys, gg, gg, gg, gg, dcat)


FB = 256
N_FB = DFF // FB


def _ffn_gate(ug, uv, wg0, wg1, wg2, wv0, wv1, wv2, bg, bv):
    cg = bg + _shift_down(ug, 2) * wg0 + _shift_down(ug, 1) * wg1 + ug * wg2
    cv = bv + _shift_down(uv, 2) * wv0 + _shift_down(uv, 1) * wv1 + uv * wv2
    return jax.nn.silu(cg) * cv


def _gate_specs():
    ug = pl.BlockSpec((S, FB), lambda j: (0, j))
    uv = pl.BlockSpec((S, FB), lambda j: (0, j + N_FB))
    wg = pl.BlockSpec((8, FB), lambda j: (0, j))
    wv = pl.BlockSpec((8, FB), lambda j: (0, j + N_FB))
    bg = pl.BlockSpec((1, FB), lambda j: (0, j))
    bv = pl.BlockSpec((1, FB), lambda j: (0, j + N_FB))
    return ug, uv, wg, wv, bg, bv


def _gate_args(ug_ref, uv_ref, wg_ref, wv_ref, bg_ref, bv_ref):
    return (ug_ref[...], uv_ref[...], wg_ref[0:1, :], wg_ref[1:2, :], wg_ref[2:3, :],
            wv_ref[0:1, :], wv_ref[1:2, :], wv_ref[2:3, :], bg_ref[...], bv_ref[...])


def ffn_gate_fwd(u, cw, cb, name):
    def body(ug_ref, uv_ref, wg_ref, wv_ref, bg_ref, bv_ref, o_ref):
        o_ref[...] = _ffn_gate(*_gate_args(ug_ref, uv_ref, wg_ref, wv_ref, bg_ref, bv_ref)).astype(BF16)

    return pl.pallas_call(body, grid=(N_FB,), name=name, in_specs=list(_gate_specs()),
                          out_specs=pl.BlockSpec((S, FB), lambda j: (0, j)),
                          out_shape=SDS((S, DFF), BF16))(u, u, cw, cw, cb, cb)


def ffn_gate_bwd(u, cw, cb, da, name):
    def body(ug_ref, uv_ref, wg_ref, wv_ref, bg_ref, bv_ref, da_ref, dug_ref, duv_ref, dwg_ref, dwv_ref, dbg_ref, dbv_ref):
        _, vjp = jax.vjp(_ffn_gate, *_gate_args(ug_ref, uv_ref, wg_ref, wv_ref, bg_ref, bv_ref))
        dug, duv, g0, g1, g2, v0, v1, v2, dbg, dbv = vjp(da_ref[...])
        dug_ref[...] = dug.astype(BF16)
        duv_ref[...] = duv.astype(BF16)
        for k, (a, b) in enumerate(((g0, v0), (g1, v1), (g2, v2))):
            dwg_ref[k:k + 1, :] = a
            dwv_ref[k:k + 1, :] = b
        dwg_ref[FFN_K:8, :] = jnp.zeros((8 - FFN_K, FB), F32)
        dwv_ref[FFN_K:8, :] = jnp.zeros((8 - FFN_K, FB), F32)
        dbg_ref[...] = dbg
        dbv_ref[...] = dbv

    ug, uv, wg, wv, bg, bv = _gate_specs()
    half = pl.BlockSpec((S, FB), lambda j: (0, j))
    whalf = pl.BlockSpec((8, FB), lambda j: (0, j))
    bhalf = pl.BlockSpec((1, FB), lambda j: (0, j))
    return pl.pallas_call(body, grid=(N_FB,), name=name, in_specs=[ug, uv, wg, wv, bg, bv, half],
                          out_specs=[half, half, whalf, whalf, bhalf, bhalf],
                          out_shape=[SDS((S, DFF), BF16), SDS((S, DFF), BF16), SDS((8, DFF), F32), SDS((8, DFF), F32),
                                     SDS((1, DFF), F32), SDS((1, DFF), F32)])(u, u, cw, cw, cb, cb, da)


def _adamw(w, g, m, v):
    m = ADAM_B1 * m + (1.0 - ADAM_B1) * g
    v = ADAM_B2 * v + (1.0 - ADAM_B2) * (g * g)
    m_hat = m / (1.0 - ADAM_B1 ** ADAM_STEP)
    v_hat = v / (1.0 - ADAM_B2 ** ADAM_STEP)
    delta = -ADAM_LR * (m_hat / (jnp.sqrt(v_hat) + ADAM_EPS) + ADAM_WD * w)
    return delta, m, v


def sum_pieces(r, layer, base, name):
    n, rows, cols = r.shape
    tr = _row_tile(rows, cols)

    def body(r_ref, *rest):
        o_ref = rest[-1]
        acc = r_ref[0].astype(F32)
        for j in range(1, n):
            acc = acc + r_ref[j].astype(F32)
        o_ref[...] = acc

    extra = {} if base is None else dict(input_output_aliases={1: 0})
    return pl.pallas_call(body, grid=(rows // tr,), name=name,
                          in_specs=[pl.BlockSpec((n, tr, cols), lambda i: (0, i, 0))] + ([] if base is None else [ANY]),
                          out_specs=pl.BlockSpec((None, tr, cols), lambda i: (layer, i, 0)),
                          out_shape=SDS((DEPTH, rows, cols), F32), **extra)(*([r] if base is None else [r, base]))


def adamw_pair(w, p, q, m, v, name):
    rows, cols = w.shape
    tr = _row_tile(rows, cols)

    def body(w_ref, p_ref, q_ref, m_ref, v_ref, g_ref, d_ref, nm_ref, nv_ref):
        g = p_ref[...] + q_ref[...]
        g_ref[...] = g
        d_ref[...], nm_ref[...], nv_ref[...] = _adamw(w_ref[...], g, m_ref[...], v_ref[...])

    spec = pl.BlockSpec((tr, cols), lambda i: (i, 0))
    return pl.pallas_call(body, grid=(rows // tr,), name=name, in_specs=[spec] * 5, out_specs=[spec] * 4,
                          out_shape=[SDS((rows, cols), F32)] * 4)(w, p, q, m, v)


def adamw_gathered(w, gall, m, v, name):
    rows = w.shape[0]
    tr = rows // 2 if rows % 16 == 0 else rows

    def body(w_ref, ga_ref, m_ref, v_ref, g_ref, d_ref, nm_ref, nv_ref):
        g = ga_ref[0]
        for j in range(1, N_DEV):
            g = g + ga_ref[j]
        g_ref[...] = g
        d_ref[...], nm_ref[...], nv_ref[...] = _adamw(w_ref[...], g, m_ref[...], v_ref[...])

    spec = pl.BlockSpec((tr, 128), lambda i: (i, 0))
    return pl.pallas_call(body, grid=(rows // tr,), name=name,
                          in_specs=[spec, pl.BlockSpec((N_DEV, tr, 128), lambda i: (0, i, 0)), spec, spec],
                          out_specs=[spec] * 4, out_shape=[SDS((rows, 128), F32)] * 4)(w, gall, m, v)


ADA_COLS = 6 * D // N_CHIP
ADA_TN = 512


def ada_mod(c_all, w, b, name):
    def body(c_ref, w_ref, b_ref, o_ref):
        ca = jax.nn.silu(c_ref[...])
        o_ref[...] = jnp.dot(ca, w_ref[...], precision=lax.Precision.HIGHEST, preferred_element_type=F32) + b_ref[...]

    return pl.pallas_call(
        body, grid=(DEPTH, ADA_COLS // ADA_TN), name=name,
        in_specs=[pl.BlockSpec((N_DEV, D), lambda l, j: (0, 0)),
                  pl.BlockSpec((None, D, ADA_TN), lambda l, j: (l, 0, j)),
                  pl.BlockSpec((None, 1, ADA_TN), lambda l, j: (l, 0, j))],
        out_specs=pl.BlockSpec((None, N_DEV, ADA_TN), lambda l, j: (l, 0, j)),
        out_shape=SDS((DEPTH, N_DEV, ADA_COLS), F32))(c_all, w, b)


def ada_update(c_all_t, dmod, w, m, v, name):
    def body(c_ref, dm_ref, w_ref, m_ref, v_ref, g_ref, d_ref, nm_ref, nv_ref):
        ca = jax.nn.silu(c_ref[...])
        g = jnp.dot(ca, dm_ref[...], precision=lax.Precision.HIGHEST, preferred_element_type=F32)
        g_ref[...] = g
        d_ref[...], nm_ref[...], nv_ref[...] = _adamw(w_ref[...], g, m_ref[...], v_ref[...])

    wspec = pl.BlockSpec((None, D, ADA_TN), lambda l, j: (l, 0, j))
    return pl.pallas_call(
        body, grid=(DEPTH, ADA_COLS // ADA_TN), name=name,
        in_specs=[pl.BlockSpec((D, N_DEV), lambda l, j: (0, 0)),
                  pl.BlockSpec((None, N_DEV, ADA_TN), lambda l, j: (l, 0, j)), wspec, wspec, wspec],
        out_specs=[wspec] * 4, out_shape=[SDS((DEPTH, D, ADA_COLS), F32)] * 4)(c_all_t, dmod, w, m, v)


_CHIP_FLIPS = ((1, 0), (0, 1), (1, 1))
_DEV_FLIPS = tuple((a, b, c) for a in (0, 1) for b in (0, 1) for c in (0, 1) if (a, b, c) != (0, 0, 0))


def _position():
    return lax.axis_index("x"), lax.axis_index("y"), lax.axis_index("c")


def _hbm_call(body, arrs, out_shapes, n_remote, name):
    n = len(arrs)
    return pl.pallas_call(
        body, name=name, in_specs=[ANY] * n, out_specs=[ANY] * n, out_shape=out_shapes,
        scratch_shapes=[pltpu.SemaphoreType.DMA((n, n_remote)), pltpu.SemaphoreType.DMA((n, n_remote)),
                        pltpu.SemaphoreType.DMA((n,))])(*arrs)


_HBM = pl.BlockSpec(memory_space=pltpu.HBM)
_SEM = pl.BlockSpec(memory_space=pltpu.SEMAPHORE)
_EFFECT = pltpu.SideEffectType.DATAFLOW_SIDE_EFFECTING


GATHER, SCATTER, ALL, SWAP = "gather", "scatter", "all", "swap"
_PEERS = {GATHER: tuple((fx, fy, 0) for fx, fy in _CHIP_FLIPS), SCATTER: tuple((fx, fy, 0) for fx, fy in _CHIP_FLIPS),
          ALL: _DEV_FLIPS, SWAP: ((0, 0, 1),)}


def _peer_copies(kind, src, land, send, recv, arrivals):
    x, y, c = _position()
    out = []
    for k, (fx, fy, fc) in enumerate(_PEERS[kind]):
        tx, ty, tc = x ^ fx, y ^ fy, c ^ fc
        if kind == ALL:
            me, peer = 4 * x + 2 * y + c, 4 * tx + 2 * ty + tc
        else:
            me, peer = 2 * x + y, 2 * tx + ty
        if kind == SWAP:
            dst = land
        else:
            dst = land.at[peer if arrivals else me]
        out.append(pltpu.make_async_remote_copy(
            src_ref=src.at[peer] if kind == SCATTER else src, dst_ref=dst, send_sem=send.at[k], recv_sem=recv.at[k],
            device_id=(tx, ty, tc), device_id_type=MESH))
    return out


def exchange_start(srcs, lands, kind, name):
    n = len(srcs)
    n_peers = len(_PEERS[kind])

    def body(*refs):
        src, land = refs[:n], refs[n:2 * n]
        send, recv = refs[2 * n:3 * n], refs[3 * n:4 * n]
        token = refs[-1]
        for a in range(n):
            for copy in _peer_copies(kind, src[a], land[a], send[a], recv[a], False):
                copy.start()
        token[...] = jnp.zeros_like(token)

    bufs = list(srcs) + list(lands)
    outs = pl.pallas_call(
        body, name=name, in_specs=[_HBM] * (2 * n),
        out_specs=[_SEM] * (2 * n) + [_HBM] * (2 * n) + [pl.BlockSpec(memory_space=pltpu.VMEM)],
        out_shape=[pltpu.SemaphoreType.DMA((n_peers,))] * (2 * n) + [pltpu.HBM(a.shape, a.dtype) for a in bufs]
        + [SDS((8, 128), F32)],
        input_output_aliases={i: 2 * n + i for i in range(2 * n)},
        compiler_params=pltpu.CompilerParams(has_side_effects=_EFFECT),
    )(*[pltpu.with_memory_space_constraint(a, pltpu.HBM) for a in bufs])
    flights = [(outs[a], outs[n + a], outs[2 * n + a], outs[3 * n + a]) for a in range(n)]
    return flights, outs[-1]


def exchange_wait(flights, kind, after, name):
    n = len(flights)

    def body(*refs):
        src, land = refs[:n], refs[n:2 * n]
        send, recv = refs[2 * n:3 * n], refs[3 * n:4 * n]
        for a in range(n):
            for arrival in _peer_copies(kind, src[a], land[a], send[a], recv[a], True):
                arrival.wait_send()
                arrival.wait_recv()

    bufs = [f[2] for f in flights] + [f[3] for f in flights]
    sems = [f[0] for f in flights] + [f[1] for f in flights]
    outs = pl.pallas_call(
        body, name=name, in_specs=[_HBM] * (2 * n) + [_SEM] * (2 * n) + [ANY], out_specs=[_HBM] * (2 * n),
        out_shape=[pltpu.HBM(a.shape, a.dtype) for a in bufs],
        input_output_aliases={i: i for i in range(2 * n)},
        compiler_params=pltpu.CompilerParams(has_side_effects=_EFFECT),
    )(*bufs, *sems, after)
    return [(outs[a], outs[n + a]) for a in range(n)]


def chip_alltoall(arrs, name):
    n = len(arrs)

    def body(*refs):
        ins, outs = refs[:n], refs[n:2 * n]
        send, recv, loc = refs[2 * n:]
        x, y, c = _position()
        me = 2 * x + y
        copies = []
        for a in range(n):
            local = pltpu.make_async_copy(ins[a].at[me], outs[a].at[me], loc.at[a])
            local.start()
            copies.append(local)
            for k, (fx, fy) in enumerate(_CHIP_FLIPS):
                tx, ty = x ^ fx, y ^ fy
                cp = pltpu.make_async_remote_copy(
                    src_ref=ins[a].at[2 * tx + ty], dst_ref=outs[a].at[me], send_sem=send.at[a, k],
                    recv_sem=recv.at[a, k], device_id=(tx, ty, c), device_id_type=MESH)
                cp.start()
                copies.append(cp)
        for cp in copies:
            cp.wait()

    shapes = [SDS(a.shape, a.dtype) for a in arrs]
    return _hbm_call(body, arrs, shapes, 3, name)


def device_allgather(arrs, name):
    n = len(arrs)

    def body(*refs):
        ins, outs = refs[:n], refs[n:2 * n]
        send, recv, loc = refs[2 * n:]
        x, y, c = _position()
        me = 4 * x + 2 * y + c
        copies = []
        for a in range(n):
            local = pltpu.make_async_copy(ins[a], outs[a].at[me], loc.at[a])
            local.start()
            copies.append(local)
            for k, (fx, fy, fc) in enumerate(_DEV_FLIPS):
                cp = pltpu.make_async_remote_copy(
                    src_ref=ins[a], dst_ref=outs[a].at[me], send_sem=send.at[a, k], recv_sem=recv.at[a, k],
                    device_id=(x ^ fx, y ^ fy, c ^ fc), device_id_type=MESH)
                cp.start()
                copies.append(cp)
        for cp in copies:
            cp.wait()

    shapes = [SDS((N_DEV,) + a.shape, a.dtype) for a in arrs]
    return _hbm_call(body, arrs, shapes, 7, name)


def _pad_to(a, axis, size):
    pad = [(0, 0)] * a.ndim
    pad[axis] = (0, size - a.shape[axis])
    return jnp.pad(a, pad)


def _cols_to_z(w):
    return _pad_to(jnp.concatenate([w[..., :768], w[..., 772:], w[..., 768:772]], axis=-1), w.ndim - 1, ZC)


def _cols_from_z(w):
    return jnp.concatenate([w[..., :768], w[..., 2304:2308], w[..., 768:2304]], axis=-1)


def _size(shape):
    n = 1
    for d in shape:
        n *= d
    return n


def _pack(parts):
    rows = []
    for p in parts:
        n = -(-p.size // 128)
        rows.append(_pad_to(p.reshape(-1), 0, n * 128).reshape(n, 128))
    packed = jnp.concatenate(rows, axis=0)
    return _pad_to(packed, 0, -(-packed.shape[0] // 16) * 16)


def _unpack(packed, shapes):
    out, off = [], 0
    for s in shapes:
        n = -(-_size(s) // 128)
        out.append(packed[off:off + n].reshape(-1)[:_size(s)].reshape(s))
        off += n
    return out


def _layer_fwd(l, x0, mod, p, fetch):
    sh1, sc1, ga1, sh2, sc2, ga2 = mod
    t = f"l{l}_"
    h1 = modnorm_fwd(x0, p['g_pre_mix'], sc1, sh1, t + "modnorm1")
    wt = dict(fetch('w_in', h1))
    z = mm_nn(h1, wt['w_in'], F32, t + "proj_in")
    cumc, cumr = fgate_fwd(z, p['b_fgate'], t + "fgate")
    y_fox, lse = fox_fwd(z, cumc, cumr, t + "fox")
    hc = conv_dw_fwd(z, wt['conv_w'], p['conv_b'], t + "conv_dw")
    y_conv = conv_pw_fwd(hc, p['conv_ln_g'], p['conv_ln_b'], wt['conv_pw_w'], p['conv_pw_b'], t + "conv_pw")
    y_swa = swa_fwd(z, p['swa_sinks'], t + "swa")
    y_sgu = sgu_fwd(z, p['sgu_ln_g'], p['sgu_ln_b'], p['sgu_w'], p['sgu_bt'], t + "sgu")
    ys = (y_fox, y_conv, y_swa, y_sgu)
    ycat = group_norm_fwd(ys, p['g_group'], t + "group_norm")
    wt.update(fetch('w_out', ycat))
    ymix = mm_nn(ycat, wt['w_out'], F32, t + "proj_out")
    x1 = resid_fwd(x0, ymix, ga1, p['g_post_mix'], t + "resid1")
    h2 = modnorm_fwd(x1, p['g_pre_ffn'], sc2, sh2, t + "modnorm2")
    wt.update(fetch('ffn_w_up', h2))
    u = mm_nn(h2, wt['ffn_w_up'], F32, t + "ffn_up")
    act = ffn_gate_fwd(u, wt['ffn_conv_w'], p['ffn_conv_b'], t + "ffn_gate")
    wt.update(fetch('ffn_w_down', act))
    yffn = mm_nn(act, wt['ffn_w_down'], F32, t + "ffn_down")
    x2 = resid_fwd(x1, yffn, ga2, p['g_post_ffn'], t + "resid2")
    res = dict(x0=x0, h1=h1, z=z, cumc=cumc, cumr=cumr, y_fox=y_fox, lse=lse, hc=hc, ys=ys, ycat=ycat, ymix=ymix,
               x1=x1, h2=h2, u=u, act=act, yffn=yffn, wt=wt)
    return x2, res


def _layer_bwd(l, dx2, mod, p, r, emit):
    sh1, sc1, ga1, sh2, sc2, ga2 = mod
    t = f"l{l}_bwd_"
    g = {}
    wt = r['wt']
    dyffn, dga2, g['g_post_ffn'] = resid_bwd(r['yffn'], ga2, p['g_post_ffn'], dx2, t + "resid2")
    tok = emit({'ffn_w_down': mm_tn(r['act'], dyffn, t + "ffn_down_dw")})
    dact = mm_nt(dyffn, wt['ffn_w_down'], t + "ffn_down_dx")
    dug, duv, dwg, dwv, dbg, dbv = ffn_gate_bwd(r['u'], wt['ffn_conv_w'], p['ffn_conv_b'] + tok, dact, t + "ffn_gate")
    du = jnp.concatenate([dug, duv], axis=1)
    g['ffn_conv_b'] = jnp.concatenate([dbg, dbv], axis=1)
    tok = emit({'ffn_w_up': mm_tn(r['h2'], du, t + "ffn_up_dw", sharded=True)})
    dh2 = mm_nt(du, wt['ffn_w_up'], t + "ffn_up_dx")
    dx1, g['g_pre_ffn'], dsc2, dsh2 = modnorm_bwd(r['x1'], p['g_pre_ffn'] + tok, sc2, sh2, dh2, dx2, t + "modnorm2")
    dymix, dga1, g['g_post_mix'] = resid_bwd(r['ymix'], ga1, p['g_post_mix'], dx1, t + "resid1")
    tok = emit({'w_out': mm_tn(r['ycat'], dymix, t + "proj_out_dw")})
    dcat = mm_nt(dymix, wt['w_out'], t + "proj_out_dx")
    dy_fox, dy_conv, dy_swa, dy_sgu, g['g_group'] = group_norm_bwd(r['ys'], p['g_group'] + tok, dcat, t + "group_norm")
    z = r['z']
    fq, fk, fv, dcq, dck = fox_bwd(z, r['cumc'], r['cumr'], r['lse'], r['y_fox'], dy_fox, t + "fox")
    dzf, g['b_fgate'] = fgate_bwd(z, p['b_fgate'], dcq, dck, t + "fgate")
    dhc, g['conv_ln_g'], g['conv_ln_b'], dpw, g['conv_pw_b'] = conv_pw_bwd(
        r['hc'], p['conv_ln_g'], p['conv_ln_b'], wt['conv_pw_w'], p['conv_pw_b'], dy_conv, t + "conv_pw")
    ca, cg, dcw, g['conv_b'] = conv_dw_bwd(z, wt['conv_w'], dhc, t + "conv_dw")
    sq, sk, sv, g['swa_sinks'] = swa_bwd(z, p['swa_sinks'], dy_swa, t + "swa")
    gu, gv, g['sgu_ln_g'], g['sgu_ln_b'], g['sgu_w'], g['sgu_bt'] = sgu_bwd(
        z, p['sgu_ln_g'], p['sgu_ln_b'], p['sgu_w'], p['sgu_bt'], dy_sgu, t + "sgu")
    dz = jnp.concatenate([fq, fk, fv, ca, cg, sq, sk, sv, gu, gv, dzf], axis=1)
    tok = emit({'w_in': mm_tn(r['h1'], dz, t + "proj_in_dw"), 'conv_w': dcw, 'conv_pw_w': dpw,
                'ffn_conv_w': jnp.concatenate([dwg, dwv], axis=1)})
    dh1 = mm_nt(dz, wt['w_in'], t + "proj_in_dx")
    dx0, g['g_pre_mix'], dsc1, dsh1 = modnorm_bwd(r['x0'], p['g_pre_mix'] + tok, sc1, sh1, dh1, dx1, t + "modnorm1")
    return dx0, g, (dsh1, dsc1, dga1, dsh2, dsc2, dga2)


def _layer_params(l, w):
    def row(name, width=None):
        v = w[name][l].reshape(1, -1)
        return v if width is None else _pad_to(v, 1, width)
    return {
        'g_pre_mix': row('g_pre_mix'), 'g_post_mix': row('g_post_mix'), 'g_pre_ffn': row('g_pre_ffn'),
        'g_post_ffn': row('g_post_ffn'), 'b_fgate': row('b_fgate', 128), 'conv_b': row('conv_b'),
        'conv_ln_g': row('conv_ln_g'), 'conv_ln_b': row('conv_ln_b'), 'conv_pw_b': row('conv_pw_b'),
        'swa_sinks': row('swa_sinks', 128), 'sgu_ln_g': row('sgu_ln_g'), 'sgu_ln_b': row('sgu_ln_b'),
        'sgu_w': w['sgu_w'][l], 'sgu_bt': _pad_to(w['sgu_b'][l].T, 1, 128), 'g_group': row('g_group'),
        'ffn_conv_b': row('ffn_conv_b'),
    }


def _w_in_from_shards(s):
    return _cols_to_z(jnp.transpose(s, (1, 0, 2)).reshape(D, IN_COLS))


def _w_in_to_shards(g):
    return jnp.transpose(_cols_from_z(g).reshape(D, N_CHIP, IN_COLS // N_CHIP), (1, 0, 2))


def _cols_from_shards(s, rows):
    _, r, n = s.shape
    return _pad_to(jnp.transpose(s, (1, 0, 2)).reshape(r, N_CHIP * n), 0, rows)


def _cols_to_shards(g, r):
    n = g.shape[1] // N_CHIP
    return jnp.transpose(g[:r].reshape(r, N_CHIP, n), (1, 0, 2))


_FROM_SHARDS = {
    'w_in': _w_in_from_shards,
    'w_out': lambda s: s.reshape(D, D),
    'ffn_w_up': lambda s: s,
    'ffn_w_down': lambda s: s.reshape(DFF, D),
    'conv_w': lambda s: _cols_from_shards(s, 32),
    'conv_pw_w': lambda s: s.reshape(GW, GW),
    'ffn_conv_w': lambda s: _cols_from_shards(s, 8),
}
_TO_SHARDS = {
    'w_in': _w_in_to_shards,
    'w_out': lambda g: g.reshape(N_CHIP, D // N_CHIP, D),
    'ffn_w_up': lambda g: g,
    'ffn_w_down': lambda g: g.reshape(N_CHIP, DFF // N_CHIP, D),
    'conv_w': lambda g: _cols_to_shards(g, CONV_K),
    'conv_pw_w': lambda g: g.reshape(N_CHIP, GW // N_CHIP, GW),
    'ffn_conv_w': lambda g: _cols_to_shards(g, FFN_K),
}


def _small_grads(g):
    return {
        'g_pre_mix': g['g_pre_mix'][0], 'g_post_mix': g['g_post_mix'][0], 'g_pre_ffn': g['g_pre_ffn'][0],
        'g_post_ffn': g['g_post_ffn'][0], 'b_fgate': g['b_fgate'][0, :4], 'conv_b': g['conv_b'][0],
        'conv_ln_g': g['conv_ln_g'][0], 'conv_ln_b': g['conv_ln_b'][0], 'conv_pw_b': g['conv_pw_b'][0],
        'swa_sinks': g['swa_sinks'][0, :4], 'sgu_ln_g': g['sgu_ln_g'][0], 'sgu_ln_b': g['sgu_ln_b'][0],
        'sgu_w': g['sgu_w'], 'sgu_b': g['sgu_bt'][:, :4].T, 'g_group': g['g_group'][0],
        'ffn_conv_b': g['ffn_conv_b'][0],
    }


def _local_step(xs, target, mod, w, fetch, emit):
    params, resids, mods = [], [], []
    for l in range(DEPTH):
        params.append(_layer_params(l, w))
        mods.append(tuple(mod[l, j] for j in range(6)))
        xs, r = _layer_fwd(l, xs, mods[l], params[l], functools.partial(fetch, l))
        resids.append(r)
    dx, loss_row = loss_grad(xs, target, "loss")
    grads, dmods = [None] * DEPTH, [None] * DEPTH
    for l in reversed(range(DEPTH)):
        dx, grads[l], dmods[l] = _layer_bwd(l, dx, mods[l], params[l], resids[l], functools.partial(emit, l))
    return loss_row, dx, grads, dmods


_MATMUL_WEIGHTS = ('w_in', 'w_out', 'ffn_w_up', 'ffn_w_down')
_CONV_WEIGHTS = ('conv_w', 'conv_pw_w', 'ffn_conv_w')
_FETCH_GROUPS = {'w_in': ('w_in',) + _CONV_WEIGHTS, 'w_out': ('w_out',), 'ffn_w_up': ('ffn_w_up',),
                 'ffn_w_down': ('ffn_w_down',)}


def kernel(x, c, w_ada, b_ada, g_pre_mix, g_post_mix, g_pre_ffn, g_post_ffn, w_in, b_fgate, conv_w, conv_b, conv_ln_g, conv_ln_b, conv_pw_w, conv_pw_b, swa_sinks, sgu_ln_g, sgu_ln_b, sgu_w, sgu_b, g_group, w_out, ffn_w_up, ffn_conv_w, ffn_conv_b, ffn_w_down, loss_target, m_w_ada, m_b_ada, m_g_pre_mix, m_g_post_mix, m_g_pre_ffn, m_g_post_ffn, m_w_in, m_b_fgate, m_conv_w, m_conv_b, m_conv_ln_g, m_conv_ln_b, m_conv_pw_w, m_conv_pw_b, m_swa_sinks, m_sgu_ln_g, m_sgu_ln_b, m_sgu_w, m_sgu_b, m_g_group, m_w_out, m_ffn_w_up, m_ffn_conv_w, m_ffn_conv_b, m_ffn_w_down, v_w_ada, v_b_ada, v_g_pre_mix, v_g_post_mix, v_g_pre_ffn, v_g_post_ffn, v_w_in, v_b_fgate, v_conv_w, v_conv_b, v_conv_ln_g, v_conv_ln_b, v_conv_pw_w, v_conv_pw_b, v_swa_sinks, v_sgu_ln_g, v_sgu_ln_b, v_sgu_w, v_sgu_b, v_g_group, v_w_out, v_ffn_w_up, v_ffn_conv_w, v_ffn_conv_b, v_ffn_w_down):
    args = locals()
    w = {n: args[n] for n in WEIGHTS}
    m = {n: args['m_' + n] for n in WEIGHTS}
    v = {n: args['v_' + n] for n in WEIGHTS}
    xi, yi, ci = _position()
    chip = 2 * xi + yi

    (c_all,) = device_allgather([c], "gather_c")
    c_all = c_all.reshape(N_DEV, D)
    b_loc = lax.dynamic_slice_in_dim(b_ada, chip * ADA_COLS, ADA_COLS, axis=1).reshape(DEPTH, 1, ADA_COLS)
    mod_all = ada_mod(c_all, w_ada, b_loc, "ada_mod")
    mine = lax.dynamic_index_in_dim(mod_all.reshape(DEPTH, N_CHIP, 2, ADA_COLS), ci, axis=2, keepdims=False)
    (mod4,) = chip_alltoall([jnp.transpose(mine, (1, 0, 2))], "scatter_mod")

    keys = [(n, l) for l in range(DEPTH) for n in _CONV_WEIGHTS]
    keys += [(n, l) for l in range(DEPTH) for n in _MATMUL_WEIGHTS]
    srcs = [w[n][l].astype(BF16) if n in _MATMUL_WEIGHTS else w[n][l] for n, l in keys]
    mod4, srcs = lax.optimization_barrier((mod4, srcs))
    lands = [lax.dynamic_update_slice_in_dim(lax.empty((N_CHIP,) + s.shape, s.dtype), s[None], chip, axis=0)
             for s in srcs]
    flights, token = exchange_start(srcs, lands, GATHER, "gather_start")
    gathering = dict(zip(keys, flights))
    mod = jnp.transpose(mod4, (1, 0, 2)).reshape(DEPTH, 6, 1, D) + token[0:1, 0:1]

    def fetch(l, name, after):
        names = _FETCH_GROUPS[name]
        landed = exchange_wait([gathering[(n, l)] for n in names], GATHER, after, f"gather_wait_l{l}_{name}")
        return {n: _FROM_SHARDS[n](land) for n, (_, land) in zip(names, landed)}

    scattering = {}

    def emit(l, grads):
        names = list(grads)
        pieces = [_TO_SHARDS[n](grads[n]) for n in names]
        lands = [lax.empty(p.shape, p.dtype) for p in pieces]
        started, token = exchange_start(pieces, lands, SCATTER, f"scatter_start_l{l}_{names[0]}")
        scattering.update({(n, l): f for n, f in zip(names, started)})
        return token[0:1, 0:1]

    loss_row, dx, grads, dmods = _local_step(x.reshape(S, D), loss_target.reshape(S, D), mod, w, fetch, emit)
    loss = lax.psum(loss_row[0, 0], ("x", "y", "c"))
    grad_x = dx.reshape(1, S, D)

    small = [_small_grads(g) for g in grads]
    local = {n: jnp.stack([small[l][n] for l in range(DEPTH)]) for n in SMALL if n != 'b_ada'}
    local['b_ada'] = jnp.stack([jnp.concatenate(dmods[l], axis=1)[0] for l in range(DEPTH)])
    g_loc = _pack([local[n] for n in SMALL])
    me = 2 * chip + ci
    g_land = lax.dynamic_update_slice_in_dim(lax.empty((N_DEV,) + g_loc.shape, F32), g_loc[None], me, axis=0)
    (small_flight,), small_token = exchange_start([g_loc], [g_land], ALL, "gather_small_start")

    order = list(scattering)
    landed = dict(zip(order, exchange_wait([scattering[k] for k in order], SCATTER, small_token, "scatter_wait")))
    part = {}
    for n in SHARDED:
        cols = w[n].shape[-1]
        for l in range(DEPTH):
            src, land = landed[(n, l)]
            own = lax.dynamic_index_in_dim(src, chip, axis=0, keepdims=True)
            recv = lax.dynamic_update_slice_in_dim(land, own, chip, axis=0)
            part[n] = sum_pieces(recv.reshape(N_CHIP, -1, cols), l, part.get(n), f"sum_{n}_l{l}")
        part[n] = part[n].reshape(-1, cols)
    parts = [part[n] for n in SHARDED]
    swap_flights, swap_token = exchange_start(parts, [lax.empty(p.shape, F32) for p in parts], SWAP, "swap_start")

    ((_, g_all),) = exchange_wait([small_flight], ALL, swap_token, "gather_small_wait")
    shapes = [w[n].shape for n in SMALL]
    packed = adamw_gathered(_pack([w[n] for n in SMALL]), g_all, _pack([m[n] for n in SMALL]),
                            _pack([v[n] for n in SMALL]), "adamw_small")
    out = {n: tuple(a) for n, a in zip(SMALL, zip(*[_unpack(pk, shapes) for pk in packed]))}

    n_ada = DEPTH * 6 * D // 128
    dmod_all = g_all[:, :n_ada, :].reshape(N_DEV, DEPTH, 6 * D)
    dmod_loc = jnp.transpose(lax.dynamic_slice_in_dim(dmod_all, chip * ADA_COLS, ADA_COLS, axis=2), (1, 0, 2))
    out['w_ada'] = tuple(ada_update(c_all.T, dmod_loc, w_ada, m['w_ada'], v['w_ada'], "adamw_ada"))

    swapped = exchange_wait(swap_flights, SWAP, out['w_ada'][0], "swap_wait")
    part = {n: own for n, (own, _) in zip(SHARDED, swapped)}
    other = {n: got for n, (_, got) in zip(SHARDED, swapped)}
    for n in SHARDED:
        cols = w[n].shape[-1]
        res = adamw_pair(w[n].reshape(-1, cols), part[n], other[n], m[n].reshape(-1, cols), v[n].reshape(-1, cols),
                         "adamw_" + n)
        out[n] = tuple(a.reshape(w[n].shape) for a in res)

    return (loss, grad_x, *[out[n][0] for n in WEIGHTS], *[out[n][1] for n in WEIGHTS],
            *[out[n][2] for n in WEIGHTS], *[out[n][3] for n in WEIGHTS])
```

```python
import functools

import jax
import jax.numpy as jnp
from jax import lax
from jax.experimental import pallas as pl
from jax.experimental.pallas import tpu as pltpu

F32 = jnp.float32
BF16 = jnp.bfloat16
SDS = jax.ShapeDtypeStruct
MESH = pl.DeviceIdType.MESH
ANY = pl.BlockSpec(memory_space=pl.ANY)

DEPTH = 2
S = 2048
D = 1024
GW = 256
DFF = 2816
NUP = 2 * DFF
IN_COLS = 2308
ZC = 2432
CONV_K = 31
FFN_K = 3
EPS = 1e-6
SCALE = 0.125
NEG = -1e30
N_CHIP = 4
N_DEV = 8

Z_FOX_Q, Z_FOX_K, Z_FOX_V = 0, 256, 512
Z_CONV_A, Z_CONV_G = 768, 1024
Z_SWA_Q, Z_SWA_K, Z_SWA_V = 1280, 1536, 1664
Z_SGU_U, Z_SGU_V = 1792, 2048
Z_FG = 2304

ADAM_LR, ADAM_B1, ADAM_B2, ADAM_EPS, ADAM_WD, ADAM_STEP = 0.001, 0.9, 0.999, 1e-08, 0.01, 10

TS = 256
TM = 1024
N_SPLIT = 2816
N_BLOCK = 1408

WEIGHTS = ['w_ada', 'b_ada', 'g_pre_mix', 'g_post_mix', 'g_pre_ffn', 'g_post_ffn', 'w_in', 'b_fgate', 'conv_w',
           'conv_b', 'conv_ln_g', 'conv_ln_b', 'conv_pw_w', 'conv_pw_b', 'swa_sinks', 'sgu_ln_g', 'sgu_ln_b',
           'sgu_w', 'sgu_b', 'g_group', 'w_out', 'ffn_w_up', 'ffn_conv_w', 'ffn_conv_b', 'ffn_w_down']
SHARDED = ['w_in', 'conv_w', 'conv_pw_w', 'w_out', 'ffn_w_up', 'ffn_conv_w', 'ffn_w_down']
SMALL = [n for n in WEIGHTS if n not in SHARDED and n != 'w_ada']


def _rms(x, g):
    return x * lax.rsqrt(jnp.mean(x * x, axis=-1, keepdims=True) + EPS) * g


def _modnorm(x, g, sc, sh):
    return _rms(x, g) * (1.0 + sc) + sh


def _resid(x, y, ga, g):
    return x + ga * _rms(y, g)


@functools.partial(jax.custom_vjp, nondiff_argnums=(1,))
def _shift_down(x, n):
    if n == 0:
        return x
    row = lax.broadcasted_iota(jnp.int32, x.shape, 0)
    return jnp.where(row >= n, pltpu.roll(x, n, axis=0), 0.0)


def _shift_up(x, n):
    if n == 0:
        return x
    rows = x.shape[0]
    row = lax.broadcasted_iota(jnp.int32, x.shape, 0)
    return jnp.where(row < rows - n, pltpu.roll(x, rows - n, axis=0), 0.0)


def _shift_down_fwd(x, n):
    return _shift_down(x, n), None


def _shift_down_bwd(n, _, ct):
    return (_shift_up(ct, n),)


_shift_down.defvjp(_shift_down_fwd, _shift_down_bwd)


def _nt(a, b):
    return lax.dot_general(a, b, (((1,), (1,)), ((), ())), preferred_element_type=F32)


def _tn(a, b):
    return lax.dot_general(a, b, (((0,), (0,)), ((), ())), preferred_element_type=F32)


def _nn(a, b):
    return jnp.dot(a, b, preferred_element_type=F32)


def _acc(ref, val, first):
    @pl.when(first)
    def _():
        ref[...] = val

    @pl.when(jnp.logical_not(first))
    def _():
        ref[...] += val


def _row_tile(rows, cols):
    limit = max(8, (1 << 20) // (4 * cols))
    best = None
    for t in range(8, rows + 1, 8):
        if rows % t == 0 and t <= limit:
            best = t
    return best if best is not None else rows


def _ncol(n):
    return n if n <= N_SPLIT else N_BLOCK


def _weight_spec(b, order):
    pick = (lambda j, i: j) if order == 0 else (lambda i, j: j)
    if b.ndim == 3:
        _, k, tn = b.shape
        return pl.BlockSpec((None, k, tn), lambda *g: (pick(*g), 0, 0)), k, N_CHIP * tn, tn
    k, n = b.shape
    tn = _ncol(n)
    return pl.BlockSpec((k, tn), lambda *g: (0, pick(*g))), k, n, tn


def mm_nn(a, b, out_dtype, name):
    m = a.shape[0]
    b_spec, k, n, tn = _weight_spec(b, 0)

    def body(a_ref, b_ref, o_ref):
        o_ref[...] = _nn(a_ref[...], b_ref[...]).astype(out_dtype)

    return pl.pallas_call(
        body, grid=(n // tn, m // TM), name=name,
        in_specs=[pl.BlockSpec((TM, k), lambda j, i: (i, 0)), b_spec],
        out_specs=pl.BlockSpec((TM, tn), lambda j, i: (i, j)),
        out_shape=SDS((m, n), out_dtype),
    )(a, b)


def _offsets(pieces):
    out, c = [], 0
    for p in pieces:
        out.append(c)
        c += p.shape[1]
    return out, c


def mm_nt(pieces, b, name):
    m = pieces[0].shape[0]
    k = b.shape[0]
    offs, n = _offsets(pieces)
    assert n == b.shape[1] <= N_SPLIT

    def body(*refs):
        a_refs, b_ref, o_ref = refs[:-2], refs[-2], refs[-1]
        a = a_refs[0][...] if len(a_refs) == 1 else jnp.concatenate([r[...] for r in a_refs], axis=1)
        o_ref[...] = _nt(a, b_ref[...])

    return pl.pallas_call(
        body, grid=(m // TM,), name=name,
        in_specs=[pl.BlockSpec((TM, p.shape[1]), lambda i: (i, 0)) for p in pieces] + [pl.BlockSpec((k, n), lambda i: (0, 0))],
        out_specs=pl.BlockSpec((TM, k), lambda i: (i, 0)),
        out_shape=SDS((m, k), F32),
    )(*pieces, b)


def mm_nt_halves(a0, a1, b4, name):
    m = a0.shape[0]
    _, k, tc = b4.shape

    def body(a0_ref, a1_ref, b_ref, o_ref):
        c = pl.program_id(1)

        @pl.when(c == 0)
        def _():
            o_ref[...] = _nt(a0_ref[...], b_ref[...])

        @pl.when(c == 1)
        def _():
            o_ref[...] += _nt(a0_ref[...], b_ref[...])

        @pl.when(c >= 2)
        def _():
            o_ref[...] += _nt(a1_ref[...], b_ref[...])

    return pl.pallas_call(
        body, grid=(m // TM, N_CHIP), name=name,
        in_specs=[pl.BlockSpec((TM, tc), lambda i, c: (i, jnp.minimum(c, 1))),
                  pl.BlockSpec((TM, tc), lambda i, c: (i, jnp.maximum(c - 2, 0))),
                  pl.BlockSpec((None, k, tc), lambda i, c: (c, 0, 0))],
        out_specs=pl.BlockSpec((TM, k), lambda i, c: (i, 0)),
        out_shape=SDS((m, k), F32),
    )(a0, a1, b4)


def mm_tn(a, pieces, name):
    m, k = a.shape
    offs, n = _offsets(pieces)
    assert n <= N_SPLIT
    steps = m // TM

    def body(*refs):
        a_ref, b_refs, o_ref, acc_ref = refs[0], refs[1:-2], refs[-2], refs[-1]
        i = pl.program_id(0)
        b = b_refs[0][...] if len(b_refs) == 1 else jnp.concatenate([r[...] for r in b_refs], axis=1)
        _acc(acc_ref, _tn(a_ref[...], b), i == 0)

        @pl.when(i == steps - 1)
        def _():
            o_ref[...] = acc_ref[...].astype(BF16)

    return pl.pallas_call(
        body, grid=(steps,), name=name,
        in_specs=[pl.BlockSpec((TM, k), lambda i: (i, 0))] + [pl.BlockSpec((TM, p.shape[1]), lambda i: (i, 0)) for p in pieces],
        out_specs=pl.BlockSpec((k, n), lambda i: (0, 0)), out_shape=SDS((k, n), BF16),
        scratch_shapes=[pltpu.VMEM((k, n), F32)],
    )(a, *pieces)


def mm_tn_halves(a, b0, b1, name):
    m, k = a.shape
    tn = b0.shape[1] // 2
    steps = m // TM

    def body(a_ref, b0_ref, b1_ref, o_ref, acc_ref):
        j, i = pl.program_id(0), pl.program_id(1)

        @pl.when(j < 2)
        def _():
            _acc(acc_ref, _tn(a_ref[...], b0_ref[...]), i == 0)

        @pl.when(j >= 2)
        def _():
            _acc(acc_ref, _tn(a_ref[...], b1_ref[...]), i == 0)

        @pl.when(i == steps - 1)
        def _():
            o_ref[...] = acc_ref[...].astype(BF16)

    return pl.pallas_call(
        body, grid=(N_CHIP, steps), name=name,
        in_specs=[pl.BlockSpec((TM, k), lambda j, i: (i, 0)),
                  pl.BlockSpec((TM, tn), lambda j, i: (i, jnp.minimum(j, 1))),
                  pl.BlockSpec((TM, tn), lambda j, i: (i, jnp.maximum(j - 2, 0)))],
        out_specs=pl.BlockSpec((None, k, tn), lambda j, i: (j, 0, 0)), out_shape=SDS((N_CHIP, k, tn), BF16),
        scratch_shapes=[pltpu.VMEM((k, tn), F32)],
    )(a, b0, b1)


_ROW = pl.BlockSpec((TS, D), lambda i: (i, 0))
_VEC = pl.BlockSpec((1, D), lambda i: (0, 0))


def modnorm_fwd(x, g, sc, sh, name):
    def body(x_ref, g_ref, sc_ref, sh_ref, o_ref):
        o_ref[...] = _modnorm(x_ref[...], g_ref[...], sc_ref[...], sh_ref[...]).astype(BF16)

    return pl.pallas_call(body, grid=(S // TS,), name=name, in_specs=[_ROW, _VEC, _VEC, _VEC], out_specs=_ROW,
                          out_shape=SDS((S, D), BF16))(x, g, sc, sh)


def modnorm_bwd(x, g, sc, sh, dh, dx_in, name):
    def body(x_ref, g_ref, sc_ref, sh_ref, dh_ref, dxin_ref, dx_ref, dg_ref, dsc_ref, dsh_ref):
        _, vjp = jax.vjp(_modnorm, x_ref[...], g_ref[...], sc_ref[...], sh_ref[...])
        dx, dg, dsc, dsh = vjp(dh_ref[...])
        dx_ref[...] = dxin_ref[...] + dx
        first = pl.program_id(0) == 0
        _acc(dg_ref, dg, first)
        _acc(dsc_ref, dsc, first)
        _acc(dsh_ref, dsh, first)

    vec = SDS((1, D), F32)
    return pl.pallas_call(body, grid=(S // TS,), name=name, in_specs=[_ROW, _VEC, _VEC, _VEC, _ROW, _ROW],
                          out_specs=[_ROW, _VEC, _VEC, _VEC], out_shape=[SDS((S, D), F32), vec, vec, vec])(
                              x, g, sc, sh, dh, dx_in)


def resid_fwd(x, y, ga, g, name):
    def body(x_ref, y_ref, ga_ref, g_ref, o_ref):
        o_ref[...] = _resid(x_ref[...], y_ref[...], ga_ref[...], g_ref[...])

    return pl.pallas_call(body, grid=(S // TS,), name=name, in_specs=[_ROW, _ROW, _VEC, _VEC], out_specs=_ROW,
                          out_shape=SDS((S, D), F32))(x, y, ga, g)


def resid_bwd(y, ga, g, dxo, name):
    def body(y_ref, ga_ref, g_ref, dxo_ref, dy_ref, dga_ref, dg_ref):
        _, vjp = jax.vjp(lambda y, ga, g: ga * _rms(y, g), y_ref[...], ga_ref[...], g_ref[...])
        dy, dga, dg = vjp(dxo_ref[...])
        dy_ref[...] = dy.astype(BF16)
        first = pl.program_id(0) == 0
        _acc(dga_ref, dga, first)
        _acc(dg_ref, dg, first)

    vec = SDS((1, D), F32)
    return pl.pallas_call(body, grid=(S // TS,), name=name, in_specs=[_ROW, _VEC, _VEC, _ROW],
                          out_specs=[_ROW, _VEC, _VEC], out_shape=[SDS((S, D), BF16), vec, vec])(y, ga, g, dxo)


def loss_grad(xf, target, name):
    def body(x_ref, t_ref, dx_ref, l_ref):
        err = x_ref[...] - t_ref[...]
        dx_ref[...] = err * (1.0 / D)
        part = 0.5 * jnp.sum(jnp.mean(err * err, axis=-1, keepdims=True), axis=0, keepdims=True)
        _acc(l_ref, jnp.broadcast_to(part, (1, 128)), pl.program_id(0) == 0)

    return pl.pallas_call(body, grid=(S // TS,), name=name, in_specs=[_ROW, _ROW],
                          out_specs=[_ROW, pl.BlockSpec((1, 128), lambda i: (0, 0))],
                          out_shape=[SDS((S, D), F32), SDS((1, 128), F32)])(xf, target)


_FG_SPEC = pl.BlockSpec((S, 128), lambda i: (0, Z_FG // 128))


def _tri128(lower):
    r = lax.broadcasted_iota(jnp.int32, (128, 128), 0)
    c = lax.broadcasted_iota(jnp.int32, (128, 128), 1)
    return ((r >= c) if lower else (r <= c)).astype(F32)


def fgate_fwd(z, bf, name):
    def body(z_ref, b_ref, cc_ref, cr_ref):
        tri = _tri128(True)
        carry = jnp.zeros((1, 128), F32)
        for i in range(S // 128):
            rows = pl.ds(i * 128, 128)
            lf = jax.nn.log_sigmoid(z_ref[rows, :] + b_ref[...])
            c = jnp.dot(tri, lf, precision=lax.Precision.HIGHEST, preferred_element_type=F32) + carry
            cc_ref[rows, :] = c
            carry = c[127:128, :]
        cr_ref[...] = cc_ref[...].T

    return pl.pallas_call(body, name=name, grid=(1,),
                          in_specs=[_FG_SPEC, pl.BlockSpec((1, 128), lambda i: (0, 0))],
                          out_specs=[pl.BlockSpec((S, 128), lambda i: (0, 0)), pl.BlockSpec((128, S), lambda i: (0, 0))],
                          out_shape=[SDS((S, 128), F32), SDS((128, S), F32)])(z, bf)


def fgate_bwd(z, bf, dcq, dck, name):
    def body(z_ref, b_ref, dcq_ref, dck_ref, dz_ref, db_ref, col_ref):
        col_ref[...] = dcq_ref[...] + jnp.concatenate([dck_ref[...], jnp.zeros((120, S), F32)], axis=0).T
        tri = _tri128(False)
        carry = jnp.zeros((1, 128), F32)
        db = jnp.zeros((1, 128), F32)
        for i in reversed(range(S // 128)):
            rows = pl.ds(i * 128, 128)
            dlf = jnp.dot(tri, col_ref[rows, :], precision=lax.Precision.HIGHEST, preferred_element_type=F32) + carry
            carry = dlf[0:1, :]
            dz = dlf * jax.nn.sigmoid(-(z_ref[rows, :] + b_ref[...]))
            dz_ref[rows, :] = dz.astype(BF16)
            db = db + jnp.sum(dz, axis=0, keepdims=True)
        db_ref[...] = db

    return pl.pallas_call(body, name=name, grid=(1,),
                          in_specs=[_FG_SPEC, pl.BlockSpec((1, 128), lambda i: (0, 0)),
                                    pl.BlockSpec((S, 128), lambda i: (0, 0)), pl.BlockSpec((8, S), lambda i: (0, 0))],
                          out_specs=[pl.BlockSpec((S, 128), lambda i: (0, 0)), pl.BlockSpec((1, 128), lambda i: (0, 0))],
                          out_shape=[SDS((S, 128), BF16), SDS((1, 128), F32)],
                          scratch_shapes=[pltpu.VMEM((S, 128), F32)])(z, bf, dcq, dck)


TQ = 256


def _head_mask(hh):
    lane = lax.broadcasted_iota(jnp.int32, (TQ, 128), 1)
    return (lane >= 64 * hh) & (lane < 64 * hh + 64)


def _fox_specs():
    q = pl.BlockSpec((TQ, 256), lambda i: (i, Z_FOX_Q // 256))
    k = pl.BlockSpec((S, 256), lambda i: (0, Z_FOX_K // 256))
    v = pl.BlockSpec((S, 256), lambda i: (0, Z_FOX_V // 256))
    cc = pl.BlockSpec((TQ, 128), lambda i: (i, 0))
    cr = pl.BlockSpec((8, S), lambda i: (0, 0))
    return q, k, v, cc, cr


FOX_SPAN = 2
FOX_GROUPS = S // (FOX_SPAN * TQ)


def _fox_scores(qm, k, cc_h, cr_h, i):
    klen = k.shape[0]
    s = _nt(qm, k) * SCALE + cc_h - cr_h
    qpos = i * TQ + lax.broadcasted_iota(jnp.int32, (TQ, klen), 0)
    kpos = lax.broadcasted_iota(jnp.int32, (TQ, klen), 1)
    return jnp.where(kpos <= qpos, s, NEG)


def _for_key_length(i, fn):
    for g in range(FOX_GROUPS):
        pl.when(i // FOX_SPAN == g)(functools.partial(fn, (g + 1) * FOX_SPAN * TQ))


def fox_fwd(z, cumc, cumr, name):
    def body(q_ref, k_ref, v_ref, cc_ref, cr_ref, o_ref, l_ref):
        i = pl.program_id(0)

        def block(klen):
            lane = lax.broadcasted_iota(jnp.int32, (TQ, 128), 1)
            cc = cc_ref[...]
            lse = jnp.zeros((TQ, 128), F32)
            for p in range(2):
                cols = pl.ds(128 * p, 128)
                q = q_ref[:, cols]
                k = k_ref[0:klen, cols].astype(BF16)
                v = v_ref[0:klen, cols].astype(BF16)
                o_pair = jnp.zeros((TQ, 128), F32)
                for hh in range(2):
                    h = 2 * p + hh
                    hm = _head_mask(hh)
                    qm = jnp.where(hm, q, 0.0).astype(BF16)
                    s = _fox_scores(qm, k, cc[:, h:h + 1], cr_ref[h:h + 1, 0:klen], i)
                    m = jnp.max(s, axis=1, keepdims=True)
                    e = jnp.exp(s - m)
                    l = jnp.sum(e, axis=1, keepdims=True)
                    o_pair = jnp.where(hm, _nn(e.astype(BF16), v) / l, o_pair)
                    lse = jnp.where(lane == h, m + jnp.log(l), lse)
                o_ref[:, cols] = o_pair
            l_ref[...] = lse

        _for_key_length(i, block)

    q, k, v, cc, cr = _fox_specs()
    return pl.pallas_call(body, grid=(S // TQ,), name=name, in_specs=[q, k, v, cc, cr],
                          out_specs=[pl.BlockSpec((TQ, 256), lambda i: (i, 0)), cc],
                          out_shape=[SDS((S, 256), F32), SDS((S, 128), F32)])(z, z, z, cumc, cumr)


def fox_bwd(z, cumc, cumr, lse, o, do, name):
    steps = S // TQ

    def body(q_ref, k_ref, v_ref, cc_ref, cr_ref, l_ref, o_ref, do_ref, dq_ref, dk_ref, dv_ref, dcq_ref, dck_ref,
             dk_acc, dv_acc):
        i = pl.program_id(0)

        @pl.when(i == 0)
        def _():
            dk_acc[...] = jnp.zeros_like(dk_acc)
            dv_acc[...] = jnp.zeros_like(dv_acc)
            dck_ref[...] = jnp.zeros_like(dck_ref)

        def block(klen):
            lane = lax.broadcasted_iota(jnp.int32, (TQ, 128), 1)
            cc = cc_ref[...]
            lse_all = l_ref[...]
            dcq = jnp.zeros((TQ, 128), F32)
            for p in range(2):
                cols = pl.ds(128 * p, 128)
                q = q_ref[:, cols]
                k = k_ref[0:klen, cols].astype(BF16)
                v = v_ref[0:klen, cols].astype(BF16)
                o_p = o_ref[:, cols]
                do_p = do_ref[:, cols]
                dq_pair = jnp.zeros((TQ, 128), F32)
                dk_pair = jnp.zeros((klen, 128), F32)
                dv_pair = jnp.zeros((klen, 128), F32)
                for hh in range(2):
                    h = 2 * p + hh
                    hm = _head_mask(hh)
                    qm = jnp.where(hm, q, 0.0).astype(BF16)
                    s = _fox_scores(qm, k, cc[:, h:h + 1], cr_ref[h:h + 1, 0:klen], i)
                    pn = jnp.exp(s - lse_all[:, h:h + 1])
                    dom = jnp.where(hm, do_p, 0.0)
                    dl = jnp.sum(dom * o_p, axis=1, keepdims=True)
                    dom = dom.astype(BF16)
                    ds = pn * (_nt(dom, v) - dl)
                    dsb = ds.astype(BF16)
                    dq_pair = jnp.where(hm, _nn(dsb, k) * SCALE, dq_pair)
                    dk_pair = dk_pair + _tn(dsb, qm) * SCALE
                    dv_pair = dv_pair + _tn(pn.astype(BF16), dom)
                    dck_ref[h:h + 1, 0:klen] -= jnp.sum(ds, axis=0, keepdims=True)
                    dcq = jnp.where(lane == h, jnp.sum(ds, axis=1, keepdims=True), dcq)
                dq_ref[:, cols] = dq_pair.astype(BF16)
                dk_acc[0:klen, cols] += dk_pair
                dv_acc[0:klen, cols] += dv_pair
            dcq_ref[...] = dcq

        _for_key_length(i, block)

        @pl.when(i == steps - 1)
        def _():
            dk_ref[...] = dk_acc[...].astype(BF16)
            dv_ref[...] = dv_acc[...].astype(BF16)

    q, k, v, cc, cr = _fox_specs()
    blk = pl.BlockSpec((TQ, 256), lambda i: (i, 0))
    full = pl.BlockSpec((S, 256), lambda i: (0, 0))
    return pl.pallas_call(body, grid=(steps,), name=name, in_specs=[q, k, v, cc, cr, cc, blk, blk],
                          out_specs=[blk, full, full, cc, cr],
                          out_shape=[SDS((S, 256), BF16), SDS((S, 256), BF16), SDS((S, 256), BF16), SDS((S, 128), F32),
                                     SDS((8, S), F32)],
                          scratch_shapes=[pltpu.VMEM((S, 256), F32), pltpu.VMEM((S, 256), F32)])(
                              z, z, z, cumc, cumr, lse, o, do)


W = 128
SWA_HEADS = 4


def _swa_core(first, qs, kcat, vcat, sink):
    r = lax.broadcasted_iota(jnp.int32, (SWA_HEADS * W, 2 * W), 0)
    j = lax.broadcasted_iota(jnp.int32, (SWA_HEADS * W, 2 * W), 1)
    qi = r & (W - 1)
    valid = ((j < W) & (j > qi) & jnp.logical_not(first)) | ((j >= W) & (j - W <= qi))
    s = jnp.where(valid, _nt(qs.astype(BF16), kcat.astype(BF16)) * SCALE, NEG)
    m = lax.stop_gradient(jnp.maximum(jnp.max(s, axis=1, keepdims=True), sink))
    e = jnp.exp(s - m)
    den = jnp.sum(e, axis=1, keepdims=True) + jnp.exp(sink - m)
    return _nn((e / den).astype(BF16), vcat.astype(BF16))


def _kv_lanes(kv):
    lane = lax.broadcasted_iota(jnp.int32, (W, 128), 1)
    return (lane >= 64 * kv) & (lane < 64 * kv + 64)


def _swa_stack(pair0, pair1):
    blocks = []
    for h in range(SWA_HEADS):
        kv, hh = h // 2, h % 2
        a = (pair0, pair1)[kv]
        a = a if hh == kv else pltpu.roll(a, 64, axis=1)
        blocks.append(jnp.where(_kv_lanes(kv), a, 0.0))
    return jnp.concatenate(blocks, axis=0)


def _swa_unstack(stacked):
    pairs = [jnp.zeros((W, 128), F32), jnp.zeros((W, 128), F32)]
    for h in range(SWA_HEADS):
        kv, hh = h // 2, h % 2
        a = jnp.where(_kv_lanes(kv), stacked[h * W:(h + 1) * W], 0.0)
        pairs[kv] = pairs[kv] + (a if hh == kv else pltpu.roll(a, 64, axis=1))
    return pairs


def _head_rows():
    return lax.broadcasted_iota(jnp.int32, (SWA_HEADS * W, 1), 0) // W


def _swa_operands(q_ref, kp_ref, kc_ref, vp_ref, vc_ref, sk_ref):
    qs = _swa_stack(q_ref[:, 0:128], q_ref[:, 128:256])
    kcat = jnp.concatenate([kp_ref[...], kc_ref[...]], axis=0)
    vcat = jnp.concatenate([vp_ref[...], vc_ref[...]], axis=0)
    sink = jnp.zeros((SWA_HEADS * W, 1), F32)
    for h in range(SWA_HEADS):
        sink = jnp.where(_head_rows() == h, sk_ref[:, h:h + 1], sink)
    return qs, kcat, vcat, sink


def _swa_specs():
    q = pl.BlockSpec((W, 256), lambda n: (n, Z_SWA_Q // 256))
    kc = pl.BlockSpec((W, 128), lambda n: (n, Z_SWA_K // 128))
    kp = pl.BlockSpec((W, 128), lambda n: (jnp.maximum(n - 1, 0), Z_SWA_K // 128))
    vc = pl.BlockSpec((W, 128), lambda n: (n, Z_SWA_V // 128))
    vp = pl.BlockSpec((W, 128), lambda n: (jnp.maximum(n - 1, 0), Z_SWA_V // 128))
    sk = pl.BlockSpec((1, 128), lambda n: (0, 0))
    return q, kp, kc, vp, vc, sk


def swa_fwd(z, sinks, name):
    def body(q_ref, kp_ref, kc_ref, vp_ref, vc_ref, sk_ref, o_ref):
        o = _swa_core(pl.program_id(0) == 0, *_swa_operands(q_ref, kp_ref, kc_ref, vp_ref, vc_ref, sk_ref))
        o0, o1 = _swa_unstack(o)
        o_ref[:, 0:128] = o0
        o_ref[:, 128:256] = o1

    return pl.pallas_call(body, grid=(S // W,), name=name, in_specs=list(_swa_specs()),
                          out_specs=pl.BlockSpec((W, 256), lambda n: (n, 0)),
                          out_shape=SDS((S, 256), F32))(z, z, z, z, z, sinks)


def swa_bwd(z, sinks, do, name):
    steps = S // W

    def body(q_ref, kp_ref, kc_ref, vp_ref, vc_ref, sk_ref, do_ref, dq_ref, dk_ref, dv_ref, dsk_ref, dk_acc, dv_acc):
        n = pl.program_id(0)
        first = n == 0
        _, vjp = jax.vjp(functools.partial(_swa_core, first), *_swa_operands(q_ref, kp_ref, kc_ref, vp_ref, vc_ref, sk_ref))
        dqs, dkcat, dvcat, dsink = vjp(_swa_stack(do_ref[:, 0:128], do_ref[:, 128:256]))
        dq0, dq1 = _swa_unstack(dqs)
        dq_ref[:, 0:128] = dq0.astype(BF16)
        dq_ref[:, 128:256] = dq1.astype(BF16)

        @pl.when(first)
        def _():
            dk_acc[...] = jnp.zeros_like(dk_acc)
            dv_acc[...] = jnp.zeros_like(dv_acc)

        cur = pl.ds(pl.multiple_of(n * W, W), W)
        dk_acc[cur, :] += dkcat[W:2 * W]
        dv_acc[cur, :] += dvcat[W:2 * W]

        @pl.when(n > 0)
        def _():
            prev = pl.ds(pl.multiple_of((n - 1) * W, W), W)
            dk_acc[prev, :] += dkcat[0:W]
            dv_acc[prev, :] += dvcat[0:W]

        lane = lax.broadcasted_iota(jnp.int32, (1, 128), 1)
        dsk = jnp.zeros((1, 128), F32)
        for h in range(SWA_HEADS):
            d = jnp.sum(jnp.where(_head_rows() == h, dsink, 0.0), axis=0, keepdims=True)
            dsk = jnp.where(lane == h, d, dsk)
        _acc(dsk_ref, dsk, first)

        @pl.when(n == steps - 1)
        def _():
            dk_ref[...] = dk_acc[...].astype(BF16)
            dv_ref[...] = dv_acc[...].astype(BF16)

    blk = pl.BlockSpec((W, 256), lambda n: (n, 0))
    full = pl.BlockSpec((S, 128), lambda n: (0, 0))
    return pl.pallas_call(body, grid=(steps,), name=name, in_specs=list(_swa_specs()) + [blk],
                          out_specs=[blk, full, full, pl.BlockSpec((1, 128), lambda n: (0, 0))],
                          out_shape=[SDS((S, 256), BF16), SDS((S, 128), BF16), SDS((S, 128), BF16), SDS((1, 128), F32)],
                          scratch_shapes=[pltpu.VMEM((S, 128), F32), pltpu.VMEM((S, 128), F32)])(
                              z, z, z, z, z, sinks, do)


def _glu(a, g):
    return a * jax.nn.sigmoid(g)


def _cv1_specs():
    a = pl.BlockSpec((S, 128), lambda j: (0, Z_CONV_A // 128 + j))
    g = pl.BlockSpec((S, 128), lambda j: (0, Z_CONV_G // 128 + j))
    w = pl.BlockSpec((32, 128), lambda j: (0, j))
    b = pl.BlockSpec((1, 128), lambda j: (0, j))
    h = pl.BlockSpec((S, 128), lambda j: (0, j))
    return a, g, w, b, h


def conv_dw_fwd(z, cw, cb, name):
    def body(a_ref, g_ref, w_ref, b_ref, o_ref):
        hh = _glu(a_ref[...], g_ref[...])
        acc = jnp.zeros((S, 128), F32) + b_ref[...]
        for k in range(CONV_K):
            acc = acc + _shift_down(hh, CONV_K - 1 - k) * w_ref[k:k + 1, :]
        o_ref[...] = acc

    a, g, w, b, h = _cv1_specs()
    return pl.pallas_call(body, grid=(2,), name=name, in_specs=[a, g, w, b], out_specs=h,
                          out_shape=SDS((S, 256), F32))(z, z, cw, cb)


def conv_dw_bwd(z, cw, dhc, name):
    def body(a_ref, g_ref, w_ref, dh_ref, da_ref, dg_ref, dw_ref, db_ref):
        hh, vjp = jax.vjp(_glu, a_ref[...], g_ref[...])
        dh = dh_ref[...]
        dhh = jnp.zeros((S, 128), F32)
        for k in range(CONV_K):
            n = CONV_K - 1 - k
            dhh = dhh + _shift_up(dh, n) * w_ref[k:k + 1, :]
            dw_ref[k:k + 1, :] = jnp.sum(dh * _shift_down(hh, n), axis=0, keepdims=True)
        dw_ref[CONV_K:32, :] = jnp.zeros((32 - CONV_K, 128), F32)
        db_ref[...] = jnp.sum(dh, axis=0, keepdims=True)
        da, dg = vjp(dhh)
        da_ref[...] = da.astype(BF16)
        dg_ref[...] = dg.astype(BF16)

    a, g, w, b, h = _cv1_specs()
    return pl.pallas_call(body, grid=(2,), name=name, in_specs=[a, g, w, h], out_specs=[h, h, w, b],
                          out_shape=[SDS((S, 256), BF16), SDS((S, 256), BF16), SDS((32, 256), F32), SDS((1, 256), F32)])(
                              z, z, cw, dhc)


def _ln(x, g, b):
    mu = jnp.mean(x, axis=-1, keepdims=True)
    xc = x - mu
    var = jnp.mean(xc * xc, axis=-1, keepdims=True)
    return xc * lax.rsqrt(var + EPS) * g + b


def _conv_pw(hc, lg, lb, pw, pb):
    y = jax.nn.silu(_ln(hc, lg, lb))
    return _nn(y.astype(BF16), pw.astype(BF16)) + pb


TS2 = 512
_ROW2 = pl.BlockSpec((TS2, 256), lambda i: (i, 0))
_VEC2 = pl.BlockSpec((1, 256), lambda i: (0, 0))
_MAT2 = pl.BlockSpec((256, 256), lambda i: (0, 0))


def conv_pw_fwd(hc, lg, lb, pw, pb, name):
    def body(h_ref, lg_ref, lb_ref, pw_ref, pb_ref, o_ref):
        o_ref[...] = _conv_pw(h_ref[...], lg_ref[...], lb_ref[...], pw_ref[...], pb_ref[...])

    return pl.pallas_call(body, grid=(S // TS2,), name=name, in_specs=[_ROW2, _VEC2, _VEC2, _MAT2, _VEC2],
                          out_specs=_ROW2, out_shape=SDS((S, 256), F32))(hc, lg, lb, pw, pb)


def conv_pw_bwd(hc, lg, lb, pw, pb, dy, name):
    def body(h_ref, lg_ref, lb_ref, pw_ref, pb_ref, dy_ref, dh_ref, dlg_ref, dlb_ref, dpw_ref, dpb_ref):
        _, vjp = jax.vjp(_conv_pw, h_ref[...], lg_ref[...], lb_ref[...], pw_ref[...], pb_ref[...])
        dh, dlg, dlb, dpw, dpb = vjp(dy_ref[...])
        dh_ref[...] = dh
        first = pl.program_id(0) == 0
        _acc(dlg_ref, dlg, first)
        _acc(dlb_ref, dlb, first)
        _acc(dpw_ref, dpw, first)
        _acc(dpb_ref, dpb, first)

    vec = SDS((1, 256), F32)
    return pl.pallas_call(body, grid=(S // TS2,), name=name, in_specs=[_ROW2, _VEC2, _VEC2, _MAT2, _VEC2, _ROW2],
                          out_specs=[_ROW2, _VEC2, _VEC2, _MAT2, _VEC2],
                          out_shape=[SDS((S, 256), F32), vec, vec, SDS((256, 256), F32), vec])(hc, lg, lb, pw, pb, dy)


def _sgu_block(u0, u1, v0, v1, lg0, lg1, lb0, lb1, w0, w1, w2, w3, bt):
    u0, u1, v0, v1 = (jax.nn.gelu(a) for a in (u0, u1, v0, v1))
    mu = (jnp.sum(v0, axis=1, keepdims=True) + jnp.sum(v1, axis=1, keepdims=True)) * (1.0 / GW)
    c0, c1 = v0 - mu, v1 - mu
    var = (jnp.sum(c0 * c0, axis=1, keepdims=True) + jnp.sum(c1 * c1, axis=1, keepdims=True)) * (1.0 / GW)
    r = lax.rsqrt(var + EPS)
    n0 = c0 * r * lg0 + lb0
    n1 = c1 * r * lg1 + lb1
    row = lax.broadcasted_iota(jnp.int32, (128, 128), 0)
    col = lax.broadcasted_iota(jnp.int32, (128, 128), 1)
    tri = row >= col
    outs = []
    for p, (n, u, wa, wb) in enumerate(((n0, u0, w0, w1), (n1, u1, w2, w3))):
        nb = n.astype(BF16)
        ma = _nn(jnp.where(tri, wa, 0.0).astype(BF16), nb)
        mb = _nn(jnp.where(tri, wb, 0.0).astype(BF16), nb)
        expand = (row == 2 * p + col // 64).astype(F32)
        bias = jnp.dot(bt, expand, precision=lax.Precision.HIGHEST, preferred_element_type=F32)
        outs.append(u * (jnp.where(col < 64, ma, mb) + bias))
    return outs[0], outs[1]


def _sgu_specs():
    def col(c):
        return pl.BlockSpec((128, 128), lambda n, c=c: (n, c))
    zs = [col(Z_SGU_U // 128), col(Z_SGU_U // 128 + 1), col(Z_SGU_V // 128), col(Z_SGU_V // 128 + 1)]
    vec = [pl.BlockSpec((1, 128), lambda n: (0, 0)), pl.BlockSpec((1, 128), lambda n: (0, 1))]
    ws = [pl.BlockSpec((None, 128, 128), lambda n, g=g: (g, 0, 0)) for g in range(4)]
    bt = pl.BlockSpec((128, 128), lambda n: (0, 0))
    return zs + vec + vec + ws + [bt]


def sgu_fwd(z, lg, lb, w, bt, name):
    def body(*refs):
        o_ref = refs[-1]
        y0, y1 = _sgu_block(*[r[...] for r in refs[:-1]])
        o_ref[:, 0:128] = y0
        o_ref[:, 128:256] = y1

    return pl.pallas_call(body, grid=(S // 128,), name=name, in_specs=_sgu_specs(),
                          out_specs=pl.BlockSpec((128, 256), lambda n: (n, 0)),
                          out_shape=SDS((S, 256), F32))(z, z, z, z, lg, lg, lb, lb, w, w, w, w, bt)


def sgu_bwd(z, lg, lb, w, bt, dy, name):
    def body(*refs):
        ins, dy_ref = refs[:13], refs[13]
        du_ref, dv_ref, dlg_ref, dlb_ref, dw_ref, dbt_ref = refs[14:]
        _, vjp = jax.vjp(_sgu_block, *[r[...] for r in ins])
        du0, du1, dv0, dv1, dlg0, dlg1, dlb0, dlb1, dw0, dw1, dw2, dw3, dbt = vjp((dy_ref[:, 0:128], dy_ref[:, 128:256]))
        du_ref[:, 0:128] = du0.astype(BF16)
        du_ref[:, 128:256] = du1.astype(BF16)
        dv_ref[:, 0:128] = dv0.astype(BF16)
        dv_ref[:, 128:256] = dv1.astype(BF16)
        first = pl.program_id(0) == 0

        @pl.when(first)
        def _():
            dlg_ref[...] = jnp.zeros_like(dlg_ref)
            dlb_ref[...] = jnp.zeros_like(dlb_ref)
            dw_ref[...] = jnp.zeros_like(dw_ref)
            dbt_ref[...] = jnp.zeros_like(dbt_ref)

        dlg_ref[:, 0:128] += dlg0
        dlg_ref[:, 128:256] += dlg1
        dlb_ref[:, 0:128] += dlb0
        dlb_ref[:, 128:256] += dlb1
        for g, d in enumerate((dw0, dw1, dw2, dw3)):
            dw_ref[g] += d
        dbt_ref[...] += dbt

    blk = pl.BlockSpec((128, 256), lambda n: (n, 0))
    vec = pl.BlockSpec((1, 256), lambda n: (0, 0))
    return pl.pallas_call(body, grid=(S // 128,), name=name, in_specs=_sgu_specs() + [blk],
                          out_specs=[blk, blk, vec, vec, pl.BlockSpec((4, 128, 128), lambda n: (0, 0, 0)),
                                     pl.BlockSpec((128, 128), lambda n: (0, 0))],
                          out_shape=[SDS((S, 256), BF16), SDS((S, 256), BF16), SDS((1, 256), F32), SDS((1, 256), F32),
                                     SDS((4, 128, 128), F32), SDS((128, 128), F32)])(
                                         z, z, z, z, lg, lg, lb, lb, w, w, w, w, bt, dy)


def _group_norm(y0, y1, y2, y3, g0, g1, g2, g3):
    return tuple(_rms(y, g) for y, g in zip((y0, y1, y2, y3), (g0, g1, g2, g3)))


_GROW = pl.BlockSpec((TS2, 256), lambda i: (i, 0))
_GCAT = pl.BlockSpec((TS2, D), lambda i: (i, 0))
_GVEC = [pl.BlockSpec((1, 256), lambda i, j=j: (0, j)) for j in range(4)]


def group_norm_fwd(ys, gg, name):
    def body(*refs):
        o_ref = refs[-1]
        outs = _group_norm(*[r[...] for r in refs[:-1]])
        for j, c in enumerate(outs):
            o_ref[:, 256 * j:256 * (j + 1)] = c.astype(BF16)

    return pl.pallas_call(body, grid=(S // TS2,), name=name, in_specs=[_GROW] * 4 + _GVEC, out_specs=_GCAT,
                          out_shape=SDS((S, D), BF16))(*ys, gg, gg, gg, gg)


def group_norm_bwd(ys, gg, dcat, name):
    def body(*refs):
        ins, dc_ref = refs[:8], refs[8]
        dy_refs, dg_ref = refs[9:13], refs[13]
        _, vjp = jax.vjp(_group_norm, *[r[...] for r in ins])
        grads = vjp(tuple(dc_ref[:, 256 * j:256 * (j + 1)] for j in range(4)))
        first = pl.program_id(0) == 0

        @pl.when(first)
        def _():
            dg_ref[...] = jnp.zeros_like(dg_ref)

        for j in range(4):
            dy_refs[j][...] = grads[j]
            dg_ref[:, 256 * j:256 * (j + 1)] += grads[4 + j]

    return pl.pallas_call(body, grid=(S // TS2,), name=name, in_specs=[_GROW] * 4 + _GVEC + [_GCAT],
                          out_specs=[_GROW] * 4 + [pl.BlockSpec((1, D), lambda i: (0, 0))],
                          out_shape=[SDS((S, 256), F32)] * 4 + [SDS((1, D), F32)])(*ys, gg, gg, gg, gg, dcat)


FB = 256
N_FB = DFF // FB


def _ffn_gate(ug, uv, wg0, wg1, wg2, wv0, wv1, wv2, bg, bv):
    cg = bg + _shift_down(ug, 2) * wg0 + _shift_down(ug, 1) * wg1 + ug * wg2
    cv = bv + _shift_down(uv, 2) * wv0 + _shift_down(uv, 1) * wv1 + uv * wv2
    return jax.nn.silu(cg) * cv


def _gate_specs():
    ug = pl.BlockSpec((S, FB), lambda j: (0, j))
    uv = pl.BlockSpec((S, FB), lambda j: (0, j + N_FB))
    wg = pl.BlockSpec((8, FB), lambda j: (0, j))
    wv = pl.BlockSpec((8, FB), lambda j: (0, j + N_FB))
    bg = pl.BlockSpec((1, FB), lambda j: (0, j))
    bv = pl.BlockSpec((1, FB), lambda j: (0, j + N_FB))
    return ug, uv, wg, wv, bg, bv


def _gate_args(ug_ref, uv_ref, wg_ref, wv_ref, bg_ref, bv_ref):
    return (ug_ref[...], uv_ref[...], wg_ref[0:1, :], wg_ref[1:2, :], wg_ref[2:3, :],
            wv_ref[0:1, :], wv_ref[1:2, :], wv_ref[2:3, :], bg_ref[...], bv_ref[...])


def ffn_gate_fwd(u, cw, cb, name):
    def body(ug_ref, uv_ref, wg_ref, wv_ref, bg_ref, bv_ref, o_ref):
        o_ref[...] = _ffn_gate(*_gate_args(ug_ref, uv_ref, wg_ref, wv_ref, bg_ref, bv_ref)).astype(BF16)

    return pl.pallas_call(body, grid=(N_FB,), name=name, in_specs=list(_gate_specs()),
                          out_specs=pl.BlockSpec((S, FB), lambda j: (0, j)),
                          out_shape=SDS((S, DFF), BF16))(u, u, cw, cw, cb, cb)


def ffn_gate_bwd(u, cw, cb, da, name):
    def body(ug_ref, uv_ref, wg_ref, wv_ref, bg_ref, bv_ref, da_ref, dug_ref, duv_ref, dwg_ref, dwv_ref, dbg_ref, dbv_ref):
        _, vjp = jax.vjp(_ffn_gate, *_gate_args(ug_ref, uv_ref, wg_ref, wv_ref, bg_ref, bv_ref))
        dug, duv, g0, g1, g2, v0, v1, v2, dbg, dbv = vjp(da_ref[...])
        dug_ref[...] = dug.astype(BF16)
        duv_ref[...] = duv.astype(BF16)
        for k, (a, b) in enumerate(((g0, v0), (g1, v1), (g2, v2))):
            dwg_ref[k:k + 1, :] = a
            dwv_ref[k:k + 1, :] = b
        dwg_ref[FFN_K:8, :] = jnp.zeros((8 - FFN_K, FB), F32)
        dwv_ref[FFN_K:8, :] = jnp.zeros((8 - FFN_K, FB), F32)
        dbg_ref[...] = dbg
        dbv_ref[...] = dbv

    ug, uv, wg, wv, bg, bv = _gate_specs()
    half = pl.BlockSpec((S, FB), lambda j: (0, j))
    whalf = pl.BlockSpec((8, FB), lambda j: (0, j))
    bhalf = pl.BlockSpec((1, FB), lambda j: (0, j))
    return pl.pallas_call(body, grid=(N_FB,), name=name, in_specs=[ug, uv, wg, wv, bg, bv, half],
                          out_specs=[half, half, whalf, whalf, bhalf, bhalf],
                          out_shape=[SDS((S, DFF), BF16), SDS((S, DFF), BF16), SDS((8, DFF), F32), SDS((8, DFF), F32),
                                     SDS((1, DFF), F32), SDS((1, DFF), F32)])(u, u, cw, cw, cb, cb, da)


def _adamw(w, g, m, v):
    m = ADAM_B1 * m + (1.0 - ADAM_B1) * g
    v = ADAM_B2 * v + (1.0 - ADAM_B2) * (g * g)
    m_hat = m / (1.0 - ADAM_B1 ** ADAM_STEP)
    v_hat = v / (1.0 - ADAM_B2 ** ADAM_STEP)
    delta = -ADAM_LR * (m_hat / (jnp.sqrt(v_hat) + ADAM_EPS) + ADAM_WD * w)
    return delta, m, v


def sum_pieces(chip, r, own, layer, base, name):
    n, rows, cols = r.shape
    tr = _row_tile(rows, cols)

    def body(chip_ref, r_ref, own_ref, *rest):
        o_ref = rest[-1]
        acc = jnp.zeros((tr, cols), F32)
        for j in range(n):
            acc = acc + jnp.where(chip_ref[0] == j, own_ref[0], r_ref[j]).astype(F32)
        o_ref[...] = acc

    extra = {} if base is None else dict(input_output_aliases={3: 0})
    grid_spec = pltpu.PrefetchScalarGridSpec(
        num_scalar_prefetch=1, grid=(rows // tr,),
        in_specs=[pl.BlockSpec((n, tr, cols), lambda i, c: (0, i, 0)), pl.BlockSpec((1, tr, cols), lambda i, c: (c[0], i, 0))]
        + ([] if base is None else [ANY]),
        out_specs=pl.BlockSpec((None, tr, cols), lambda i, c: (layer, i, 0)))
    return pl.pallas_call(body, grid_spec=grid_spec, name=name, out_shape=SDS((DEPTH, rows, cols), F32), **extra)(
        *([chip, r, own] if base is None else [chip, r, own, base]))


def adamw_pair(w, p, q, m, v, name):
    rows, cols = w.shape
    tr = _row_tile(rows, cols)

    def body(w_ref, p_ref, q_ref, m_ref, v_ref, g_ref, d_ref, nm_ref, nv_ref):
        g = p_ref[...] + q_ref[...]
        g_ref[...] = g
        d_ref[...], nm_ref[...], nv_ref[...] = _adamw(w_ref[...], g, m_ref[...], v_ref[...])

    spec = pl.BlockSpec((tr, cols), lambda i: (i, 0))
    return pl.pallas_call(body, grid=(rows // tr,), name=name, in_specs=[spec] * 5, out_specs=[spec] * 4,
                          out_shape=[SDS((rows, cols), F32)] * 4)(w, p, q, m, v)


_PACK_LAYOUT = (('b_ada', 6 * D), ('g_pre_mix', D), ('g_post_mix', D), ('g_pre_ffn', D), ('g_post_ffn', D),
                ('b_fgate', 128), ('conv_b', GW), ('conv_ln_g', GW), ('conv_ln_b', GW), ('conv_pw_b', GW),
                ('swa_sinks', 128), ('sgu_ln_g', GW), ('sgu_ln_b', GW), ('sgu_b', 4 * 128), ('g_group', D),
                ('ffn_conv_b', NUP))
_PACK_WIDTH = dict(_PACK_LAYOUT)
_PACK_ROW, _LAYER_ROWS = {}, 0
for _name, _width in _PACK_LAYOUT:
    _PACK_ROW[_name] = _LAYER_ROWS
    _LAYER_ROWS += -(-_width // D)
PACK_ROWS = -(-DEPTH * _LAYER_ROWS // 8) * 8


def _segments(offset, width):
    out, s = [], 0
    while s < width:
        row, col = divmod(offset + s, D)
        n = min(width - s, D - col)
        out.append((s, row, col, n))
        s += n
    return out


def pack_small(grads, dmods, name):
    ops, plan = [], []
    for l in range(DEPTH):
        pieces = [('b_ada', j * D, a) for j, a in enumerate(dmods[l])]
        pieces += [(n, 0, grads[l][n]) for n, _ in _PACK_LAYOUT if n not in ('b_ada', 'sgu_b', 'ffn_conv_b')]
        pieces += [('ffn_conv_b', j * DFF, a) for j, a in enumerate(grads[l]['ffn_conv_b'])]
        for n, off, a in pieces:
            plan.append((len(ops), l, n, off))
            ops.append(a)
    bts = [grads[l]['sgu_bt'] for l in range(DEPTH)]
    sws = [grads[l]['sgu_w'] for l in range(DEPTH)]
    n_vec = len(ops)

    def body(*refs):
        vec, bt, sw = refs[:n_vec], refs[n_vec:n_vec + DEPTH], refs[n_vec + DEPTH:n_vec + 2 * DEPTH]
        o_ref, ow_ref, scr = refs[n_vec + 2 * DEPTH:]
        o_ref[...] = jnp.zeros_like(o_ref)
        for idx, l, n, off in plan:
            base = l * _LAYER_ROWS + _PACK_ROW[n]
            width = min(vec[idx].shape[1], _PACK_WIDTH[n] - off)
            for s, row, col, lanes in _segments(off, width):
                o_ref[base + row:base + row + 1, col:col + lanes] = vec[idx][:, s:s + lanes]
        for l in range(DEPTH):
            scr[...] = bt[l][...].T
            row = l * _LAYER_ROWS + _PACK_ROW['sgu_b']
            for g in range(4):
                o_ref[row:row + 1, 128 * g:128 * (g + 1)] = scr[g:g + 1, :]
            ow_ref[l] = sw[l][...]

    vm = pl.BlockSpec(memory_space=pltpu.VMEM)
    return pl.pallas_call(body, name=name, in_specs=[vm] * (n_vec + 2 * DEPTH), out_specs=[vm, vm],
                          out_shape=[SDS((PACK_ROWS, D), F32), SDS((DEPTH, 4, 128, 128), F32)],
                          scratch_shapes=[pltpu.VMEM((128, 128), F32)])(*ops, *bts, *sws)


def adamw_small(gall, gall_w, w, m, v, name):
    names = [n for n in SMALL]
    n_par = len(names)

    def native(ref, n, l, col, lanes):
        if n == 'sgu_b':
            return ref.at[l, pl.ds(col // 128, 1), :]
        return ref.at[pl.ds(l, 1), pl.ds(col, lanes)]

    def body(*refs):
        ga_ref, gw_ref = refs[0], refs[1]
        w_refs, m_refs, v_refs = (refs[2 + k * n_par:2 + (k + 1) * n_par] for k in range(3))
        outs = refs[2 + 3 * n_par:2 + 7 * n_par]
        scr = refs[-1]
        g = ga_ref[0]
        for j in range(1, N_DEV):
            g = g + ga_ref[j]
        scr[...] = g
        for k, n in enumerate(names):
            o_g, o_d, o_m, o_v = outs[4 * k:4 * k + 4]
            if n == 'sgu_w':
                gw = gw_ref[0]
                for j in range(1, N_DEV):
                    gw = gw + gw_ref[j]
                o_g[...] = gw
                o_d[...], o_m[...], o_v[...] = _adamw(w_refs[k][...], gw, m_refs[k][...], v_refs[k][...])
                continue
            width = w_refs[k].shape[-1] if n != 'sgu_b' else 4 * 128
            for l in range(DEPTH):
                base = l * _LAYER_ROWS + _PACK_ROW[n]
                step = 128 if n == 'sgu_b' else D
                for col in range(0, width, step):
                    lanes = min(step, width - col)
                    row, lane0 = divmod(col, D)
                    gv = scr[base + row:base + row + 1, lane0:lane0 + lanes]
                    at = functools.partial(native, n=n, l=l, col=col, lanes=lanes)
                    at(o_g)[...] = gv
                    at(o_d)[...], at(o_m)[...], at(o_v)[...] = _adamw(at(w_refs[k])[...], gv, at(m_refs[k])[...],
                                                                      at(v_refs[k])[...])

    vm = pl.BlockSpec(memory_space=pltpu.VMEM)
    params = [d[n] for d in (w, m, v) for n in names]
    outs = pl.pallas_call(body, name=name, in_specs=[vm] * (2 + 3 * n_par), out_specs=[vm] * (4 * n_par),
                          out_shape=[SDS(w[n].shape, F32) for n in names for _ in range(4)],
                          scratch_shapes=[pltpu.VMEM((PACK_ROWS, D), F32)])(gall, gall_w, *params)
    return {n: tuple(outs[4 * k:4 * k + 4]) for k, n in enumerate(names)}


ADA_COLS = 6 * D // N_CHIP
ADA_TN = 512


def ada_mod(c_all, w, b, name):
    def body(c_ref, w_ref, b_ref, o_ref):
        ca = jax.nn.silu(c_ref[...])
        o_ref[...] = jnp.dot(ca, w_ref[...], precision=lax.Precision.HIGHEST, preferred_element_type=F32) + b_ref[...]

    return pl.pallas_call(
        body, grid=(DEPTH, ADA_COLS // ADA_TN), name=name,
        in_specs=[pl.BlockSpec((N_DEV, D), lambda l, j: (0, 0)),
                  pl.BlockSpec((None, D, ADA_TN), lambda l, j: (l, 0, j)),
                  pl.BlockSpec((None, 1, ADA_TN), lambda l, j: (l, 0, j))],
        out_specs=pl.BlockSpec((None, N_DEV, ADA_TN), lambda l, j: (l, 0, j)),
        out_shape=SDS((DEPTH, N_DEV, ADA_COLS), F32))(c_all, w, b)


def ada_update(c_all_t, dmod, w, m, v, name):
    def body(c_ref, dm_ref, w_ref, m_ref, v_ref, g_ref, d_ref, nm_ref, nv_ref):
        ca = jax.nn.silu(c_ref[...])
        g = jnp.dot(ca, dm_ref[...], precision=lax.Precision.HIGHEST, preferred_element_type=F32)
        g_ref[...] = g
        d_ref[...], nm_ref[...], nv_ref[...] = _adamw(w_ref[...], g, m_ref[...], v_ref[...])

    wspec = pl.BlockSpec((None, D, ADA_TN), lambda l, j: (l, 0, j))
    return pl.pallas_call(
        body, grid=(DEPTH, ADA_COLS // ADA_TN), name=name,
        in_specs=[pl.BlockSpec((D, N_DEV), lambda l, j: (0, 0)),
                  pl.BlockSpec((None, N_DEV, ADA_TN), lambda l, j: (l, 0, j)), wspec, wspec, wspec],
        out_specs=[wspec] * 4, out_shape=[SDS((DEPTH, D, ADA_COLS), F32)] * 4)(c_all_t, dmod, w, m, v)


_CHIP_FLIPS = ((1, 0), (0, 1), (1, 1))
_DEV_FLIPS = tuple((a, b, c) for a in (0, 1) for b in (0, 1) for c in (0, 1) if (a, b, c) != (0, 0, 0))


def _position():
    return lax.axis_index("x"), lax.axis_index("y"), lax.axis_index("c")


def _hbm_call(body, arrs, out_shapes, n_remote, name):
    n = len(arrs)
    return pl.pallas_call(
        body, name=name, in_specs=[ANY] * n, out_specs=[ANY] * n, out_shape=out_shapes,
        scratch_shapes=[pltpu.SemaphoreType.DMA((n, n_remote)), pltpu.SemaphoreType.DMA((n, n_remote)),
                        pltpu.SemaphoreType.DMA((n,))])(*arrs)


_HBM = pl.BlockSpec(memory_space=pltpu.HBM)
_SEM = pl.BlockSpec(memory_space=pltpu.SEMAPHORE)
_EFFECT = pltpu.SideEffectType.DATAFLOW_SIDE_EFFECTING


GATHER, SCATTER, ALL, SWAP = "gather", "scatter", "all", "swap"
_PEERS = {GATHER: tuple((fx, fy, 0) for fx, fy in _CHIP_FLIPS), SCATTER: tuple((fx, fy, 0) for fx, fy in _CHIP_FLIPS),
          ALL: _DEV_FLIPS, SWAP: ((0, 0, 1),)}


def _peer_copies(kind, src, land, send, recv, arrivals):
    x, y, c = _position()
    out = []
    for k, (fx, fy, fc) in enumerate(_PEERS[kind]):
        tx, ty, tc = x ^ fx, y ^ fy, c ^ fc
        if kind == ALL:
            me, peer = 4 * x + 2 * y + c, 4 * tx + 2 * ty + tc
        else:
            me, peer = 2 * x + y, 2 * tx + ty
        if kind == SWAP:
            dst = land
        else:
            dst = land.at[peer if arrivals else me]
        out.append(pltpu.make_async_remote_copy(
            src_ref=src.at[peer] if kind == SCATTER else src, dst_ref=dst, send_sem=send.at[k], recv_sem=recv.at[k],
            device_id=(tx, ty, tc), device_id_type=MESH))
    return out


def _own_copy(kind, src, land, send):
    x, y, c = _position()
    me = 4 * x + 2 * y + c if kind == ALL else 2 * x + y
    return pltpu.make_async_copy(src, land.at[me], send.at[len(_PEERS[kind])])


def exchange_start(srcs, lands, kind, name):
    n = len(srcs)
    n_peers = len(_PEERS[kind])

    def body(*refs):
        src, land = refs[:n], refs[n:2 * n]
        send, recv = refs[2 * n:3 * n], refs[3 * n:4 * n]
        token = refs[-1]
        for a in range(n):
            for copy in _peer_copies(kind, src[a], land[a], send[a], recv[a], False):
                copy.start()
            if kind in (GATHER, ALL):
                _own_copy(kind, src[a], land[a], send[a]).start()
        token[...] = jnp.zeros_like(token)

    bufs = list(srcs) + list(lands)
    outs = pl.pallas_call(
        body, name=name, in_specs=[_HBM] * (2 * n),
        out_specs=[_SEM] * (2 * n) + [_HBM] * (2 * n) + [pl.BlockSpec(memory_space=pltpu.VMEM)],
        out_shape=[pltpu.SemaphoreType.DMA((n_peers + 1,))] * n + [pltpu.SemaphoreType.DMA((n_peers,))] * n
        + [pltpu.HBM(a.shape, a.dtype) for a in bufs] + [SDS((8, 128), F32)],
        input_output_aliases={i: 2 * n + i for i in range(2 * n)},
        compiler_params=pltpu.CompilerParams(has_side_effects=_EFFECT),
    )(*[pltpu.with_memory_space_constraint(a, pltpu.HBM) for a in bufs])
    flights = [(outs[a], outs[n + a], outs[2 * n + a], outs[3 * n + a]) for a in range(n)]
    return flights, outs[-1]


def exchange_wait(flights, kind, after, name):
    n = len(flights)

    def body(*refs):
        src, land = refs[:n], refs[n:2 * n]
        send, recv = refs[2 * n:3 * n], refs[3 * n:4 * n]
        for a in range(n):
            for arrival in _peer_copies(kind, src[a], land[a], send[a], recv[a], True):
                arrival.wait_send()
                arrival.wait_recv()
            if kind in (GATHER, ALL):
                _own_copy(kind, src[a], land[a], send[a]).wait()

    bufs = [f[2] for f in flights] + [f[3] for f in flights]
    sems = [f[0] for f in flights] + [f[1] for f in flights]
    outs = pl.pallas_call(
        body, name=name, in_specs=[_HBM] * (2 * n) + [_SEM] * (2 * n) + [ANY], out_specs=[_HBM] * (2 * n),
        out_shape=[pltpu.HBM(a.shape, a.dtype) for a in bufs],
        input_output_aliases={i: i for i in range(2 * n)},
        compiler_params=pltpu.CompilerParams(has_side_effects=_EFFECT),
    )(*bufs, *sems, after)
    return [(outs[a], outs[n + a]) for a in range(n)]


def chip_alltoall(arrs, name):
    n = len(arrs)

    def body(*refs):
        ins, outs = refs[:n], refs[n:2 * n]
        send, recv, loc = refs[2 * n:]
        x, y, c = _position()
        me = 2 * x + y
        copies = []
        for a in range(n):
            local = pltpu.make_async_copy(ins[a].at[me], outs[a].at[me], loc.at[a])
            local.start()
            copies.append(local)
            for k, (fx, fy) in enumerate(_CHIP_FLIPS):
                tx, ty = x ^ fx, y ^ fy
                cp = pltpu.make_async_remote_copy(
                    src_ref=ins[a].at[2 * tx + ty], dst_ref=outs[a].at[me], send_sem=send.at[a, k],
                    recv_sem=recv.at[a, k], device_id=(tx, ty, c), device_id_type=MESH)
                cp.start()
                copies.append(cp)
        for cp in copies:
            cp.wait()

    shapes = [SDS(a.shape, a.dtype) for a in arrs]
    return _hbm_call(body, arrs, shapes, 3, name)


def device_allgather(arrs, name):
    n = len(arrs)

    def body(*refs):
        ins, outs = refs[:n], refs[n:2 * n]
        send, recv, loc = refs[2 * n:]
        x, y, c = _position()
        me = 4 * x + 2 * y + c
        copies = []
        for a in range(n):
            local = pltpu.make_async_copy(ins[a], outs[a].at[me], loc.at[a])
            local.start()
            copies.append(local)
            for k, (fx, fy, fc) in enumerate(_DEV_FLIPS):
                cp = pltpu.make_async_remote_copy(
                    src_ref=ins[a], dst_ref=outs[a].at[me], send_sem=send.at[a, k], recv_sem=recv.at[a, k],
                    device_id=(x ^ fx, y ^ fy, c ^ fc), device_id_type=MESH)
                cp.start()
                copies.append(cp)
        for cp in copies:
            cp.wait()

    shapes = [SDS((N_DEV,) + a.shape, a.dtype) for a in arrs]
    return _hbm_call(body, arrs, shapes, 7, name)


def _pad_to(a, axis, size):
    pad = [(0, 0)] * a.ndim
    pad[axis] = (0, size - a.shape[axis])
    return jnp.pad(a, pad)


def _cols_to_z(w):
    return _pad_to(jnp.concatenate([w[..., :768], w[..., 772:], w[..., 768:772]], axis=-1), w.ndim - 1, ZC)


def _cols_from_z(w):
    return jnp.concatenate([w[..., :768], w[..., 2304:2308], w[..., 768:2304]], axis=-1)


def _layer_fwd(l, x0, mod, p, fetch):
    sh1, sc1, ga1, sh2, sc2, ga2 = mod
    t = f"l{l}_"
    h1 = modnorm_fwd(x0, p['g_pre_mix'], sc1, sh1, t + "modnorm1")
    wt = dict(fetch('w_in', h1))
    z = mm_nn(h1, wt['w_in'], F32, t + "proj_in")
    cumc, cumr = fgate_fwd(z, p['b_fgate'], t + "fgate")
    y_fox, lse = fox_fwd(z, cumc, cumr, t + "fox")
    hc = conv_dw_fwd(z, wt['conv_w'], p['conv_b'], t + "conv_dw")
    y_conv = conv_pw_fwd(hc, p['conv_ln_g'], p['conv_ln_b'], wt['conv_pw_w'], p['conv_pw_b'], t + "conv_pw")
    y_swa = swa_fwd(z, p['swa_sinks'], t + "swa")
    y_sgu = sgu_fwd(z, p['sgu_ln_g'], p['sgu_ln_b'], p['sgu_w'], p['sgu_bt'], t + "sgu")
    ys = (y_fox, y_conv, y_swa, y_sgu)
    ycat = group_norm_fwd(ys, p['g_group'], t + "group_norm")
    wt.update(fetch('w_out', ycat))
    ymix = mm_nn(ycat, wt['w_out'], F32, t + "proj_out")
    x1 = resid_fwd(x0, ymix, ga1, p['g_post_mix'], t + "resid1")
    h2 = modnorm_fwd(x1, p['g_pre_ffn'], sc2, sh2, t + "modnorm2")
    wt.update(fetch('ffn_w_up', h2))
    u = mm_nn(h2, wt['ffn_w_up'], F32, t + "ffn_up")
    act = ffn_gate_fwd(u, wt['ffn_conv_w'], p['ffn_conv_b'], t + "ffn_gate")
    wt.update(fetch('ffn_w_down', act))
    yffn = mm_nn(act, wt['ffn_w_down'], F32, t + "ffn_down")
    x2 = resid_fwd(x1, yffn, ga2, p['g_post_ffn'], t + "resid2")
    res = dict(x0=x0, h1=h1, z=z, cumc=cumc, cumr=cumr, y_fox=y_fox, lse=lse, hc=hc, ys=ys, ycat=ycat, ymix=ymix,
               x1=x1, h2=h2, u=u, act=act, yffn=yffn, wt=wt)
    return x2, res


def _layer_bwd(l, dx2, mod, p, r, emit):
    sh1, sc1, ga1, sh2, sc2, ga2 = mod
    t = f"l{l}_bwd_"
    g = {}
    wt = r['wt']
    dyffn, dga2, g['g_post_ffn'] = resid_bwd(r['yffn'], ga2, p['g_post_ffn'], dx2, t + "resid2")
    tok = emit({'ffn_w_down': mm_tn(r['act'], [dyffn], t + "ffn_down_dw")})
    dact = mm_nt([dyffn], wt['ffn_w_down'], t + "ffn_down_dx")
    dug, duv, dwg, dwv, dbg, dbv = ffn_gate_bwd(r['u'], wt['ffn_conv_w'], p['ffn_conv_b'] + tok, dact, t + "ffn_gate")
    g['ffn_conv_b'] = (dbg, dbv)
    tok = emit({'ffn_w_up': mm_tn_halves(r['h2'], dug, duv, t + "ffn_up_dw")})
    dh2 = mm_nt_halves(dug, duv, wt['ffn_w_up'], t + "ffn_up_dx")
    dx1, g['g_pre_ffn'], dsc2, dsh2 = modnorm_bwd(r['x1'], p['g_pre_ffn'] + tok, sc2, sh2, dh2, dx2, t + "modnorm2")
    dymix, dga1, g['g_post_mix'] = resid_bwd(r['ymix'], ga1, p['g_post_mix'], dx1, t + "resid1")
    tok = emit({'w_out': mm_tn(r['ycat'], [dymix], t + "proj_out_dw")})
    dcat = mm_nt([dymix], wt['w_out'], t + "proj_out_dx")
    dy_fox, dy_conv, dy_swa, dy_sgu, g['g_group'] = group_norm_bwd(r['ys'], p['g_group'] + tok, dcat, t + "group_norm")
    z = r['z']
    fq, fk, fv, dcq, dck = fox_bwd(z, r['cumc'], r['cumr'], r['lse'], r['y_fox'], dy_fox, t + "fox")
    dzf, g['b_fgate'] = fgate_bwd(z, p['b_fgate'], dcq, dck, t + "fgate")
    dhc, g['conv_ln_g'], g['conv_ln_b'], dpw, g['conv_pw_b'] = conv_pw_bwd(
        r['hc'], p['conv_ln_g'], p['conv_ln_b'], wt['conv_pw_w'], p['conv_pw_b'], dy_conv, t + "conv_pw")
    ca, cg, dcw, g['conv_b'] = conv_dw_bwd(z, wt['conv_w'], dhc, t + "conv_dw")
    sq, sk, sv, g['swa_sinks'] = swa_bwd(z, p['swa_sinks'], dy_swa, t + "swa")
    gu, gv, g['sgu_ln_g'], g['sgu_ln_b'], g['sgu_w'], g['sgu_bt'] = sgu_bwd(
        z, p['sgu_ln_g'], p['sgu_ln_b'], p['sgu_w'], p['sgu_bt'], dy_sgu, t + "sgu")
    dz = [fq, fk, fv, ca, cg, sq, sk, sv, gu, gv, dzf]
    tok = emit({'w_in': mm_tn(r['h1'], dz, t + "proj_in_dw"), 'conv_w': dcw, 'conv_pw_w': dpw,
                'ffn_conv_w': jnp.concatenate([dwg, dwv], axis=1)})
    dh1 = mm_nt(dz, wt['w_in'], t + "proj_in_dx")
    dx0, g['g_pre_mix'], dsc1, dsh1 = modnorm_bwd(r['x0'], p['g_pre_mix'] + tok, sc1, sh1, dh1, dx1, t + "modnorm1")
    return dx0, g, (dsh1, dsc1, dga1, dsh2, dsc2, dga2)


def _layer_params(l, w):
    def row(name, width=None):
        v = w[name][l].reshape(1, -1)
        return v if width is None else _pad_to(v, 1, width)
    return {
        'g_pre_mix': row('g_pre_mix'), 'g_post_mix': row('g_post_mix'), 'g_pre_ffn': row('g_pre_ffn'),
        'g_post_ffn': row('g_post_ffn'), 'b_fgate': row('b_fgate', 128), 'conv_b': row('conv_b'),
        'conv_ln_g': row('conv_ln_g'), 'conv_ln_b': row('conv_ln_b'), 'conv_pw_b': row('conv_pw_b'),
        'swa_sinks': row('swa_sinks', 128), 'sgu_ln_g': row('sgu_ln_g'), 'sgu_ln_b': row('sgu_ln_b'),
        'sgu_w': w['sgu_w'][l], 'sgu_bt': _pad_to(w['sgu_b'][l].T, 1, 128), 'g_group': row('g_group'),
        'ffn_conv_b': row('ffn_conv_b'),
    }


def _w_in_from_shards(s):
    return _cols_to_z(jnp.transpose(s, (1, 0, 2)).reshape(D, IN_COLS))


def _w_in_to_shards(g):
    return jnp.transpose(_cols_from_z(g).reshape(D, N_CHIP, IN_COLS // N_CHIP), (1, 0, 2))


def _cols_from_shards(s, rows):
    _, r, n = s.shape
    return _pad_to(jnp.transpose(s, (1, 0, 2)).reshape(r, N_CHIP * n), 0, rows)


def _cols_to_shards(g, r):
    n = g.shape[1] // N_CHIP
    return jnp.transpose(g[:r].reshape(r, N_CHIP, n), (1, 0, 2))


_FROM_SHARDS = {
    'w_in': _w_in_from_shards,
    'w_out': lambda s: s.reshape(D, D),
    'ffn_w_up': lambda s: s,
    'ffn_w_down': lambda s: s.reshape(DFF, D),
    'conv_w': lambda s: _cols_from_shards(s, 32),
    'conv_pw_w': lambda s: s.reshape(GW, GW),
    'ffn_conv_w': lambda s: _cols_from_shards(s, 8),
}
_TO_SHARDS = {
    'w_in': _w_in_to_shards,
    'w_out': lambda g: g.reshape(N_CHIP, D // N_CHIP, D),
    'ffn_w_up': lambda g: g,
    'ffn_w_down': lambda g: g.reshape(N_CHIP, DFF // N_CHIP, D),
    'conv_w': lambda g: _cols_to_shards(g, CONV_K),
    'conv_pw_w': lambda g: g.reshape(N_CHIP, GW // N_CHIP, GW),
    'ffn_conv_w': lambda g: _cols_to_shards(g, FFN_K),
}


def _local_step(xs, target, mod, w, fetch, emit):
    params, resids, mods = [], [], []
    for l in range(DEPTH):
        params.append(_layer_params(l, w))
        mods.append(tuple(mod[l, j] for j in range(6)))
        xs, r = _layer_fwd(l, xs, mods[l], params[l], functools.partial(fetch, l))
        resids.append(r)
    dx, loss_row = loss_grad(xs, target, "loss")
    grads, dmods = [None] * DEPTH, [None] * DEPTH
    for l in reversed(range(DEPTH)):
        dx, grads[l], dmods[l] = _layer_bwd(l, dx, mods[l], params[l], resids[l], functools.partial(emit, l))
    return loss_row, dx, grads, dmods


_MATMUL_WEIGHTS = ('w_in', 'w_out', 'ffn_w_up', 'ffn_w_down')
_CONV_WEIGHTS = ('conv_w', 'conv_pw_w', 'ffn_conv_w')
_FETCH_GROUPS = {'w_in': ('w_in',) + _CONV_WEIGHTS, 'w_out': ('w_out',), 'ffn_w_up': ('ffn_w_up',),
                 'ffn_w_down': ('ffn_w_down',)}


def kernel(x, c, w_ada, b_ada, g_pre_mix, g_post_mix, g_pre_ffn, g_post_ffn, w_in, b_fgate, conv_w, conv_b, conv_ln_g, conv_ln_b, conv_pw_w, conv_pw_b, swa_sinks, sgu_ln_g, sgu_ln_b, sgu_w, sgu_b, g_group, w_out, ffn_w_up, ffn_conv_w, ffn_conv_b, ffn_w_down, loss_target, m_w_ada, m_b_ada, m_g_pre_mix, m_g_post_mix, m_g_pre_ffn, m_g_post_ffn, m_w_in, m_b_fgate, m_conv_w, m_conv_b, m_conv_ln_g, m_conv_ln_b, m_conv_pw_w, m_conv_pw_b, m_swa_sinks, m_sgu_ln_g, m_sgu_ln_b, m_sgu_w, m_sgu_b, m_g_group, m_w_out, m_ffn_w_up, m_ffn_conv_w, m_ffn_conv_b, m_ffn_w_down, v_w_ada, v_b_ada, v_g_pre_mix, v_g_post_mix, v_g_pre_ffn, v_g_post_ffn, v_w_in, v_b_fgate, v_conv_w, v_conv_b, v_conv_ln_g, v_conv_ln_b, v_conv_pw_w, v_conv_pw_b, v_swa_sinks, v_sgu_ln_g, v_sgu_ln_b, v_sgu_w, v_sgu_b, v_g_group, v_w_out, v_ffn_w_up, v_ffn_conv_w, v_ffn_conv_b, v_ffn_w_down):
    args = locals()
    w = {n: args[n] for n in WEIGHTS}
    m = {n: args['m_' + n] for n in WEIGHTS}
    v = {n: args['v_' + n] for n in WEIGHTS}
    xi, yi, ci = _position()
    chip = 2 * xi + yi

    (c_all,) = device_allgather([c], "gather_c")
    c_all = c_all.reshape(N_DEV, D)
    b_loc = lax.dynamic_slice_in_dim(b_ada, chip * ADA_COLS, ADA_COLS, axis=1).reshape(DEPTH, 1, ADA_COLS)
    mod_all = ada_mod(c_all, w_ada, b_loc, "ada_mod")
    mine = lax.dynamic_index_in_dim(mod_all.reshape(DEPTH, N_CHIP, 2, ADA_COLS), ci, axis=2, keepdims=False)
    (mod4,) = chip_alltoall([jnp.transpose(mine, (1, 0, 2))], "scatter_mod")

    keys = [(n, l) for l in range(DEPTH) for n in _CONV_WEIGHTS]
    keys += [(n, l) for l in range(DEPTH) for n in _MATMUL_WEIGHTS]
    srcs = [w[n][l].astype(BF16) if n in _MATMUL_WEIGHTS else w[n][l] for n, l in keys]
    mod4, srcs = lax.optimization_barrier((mod4, srcs))
    lands = [lax.empty((N_CHIP,) + s.shape, s.dtype) for s in srcs]
    flights, token = exchange_start(srcs, lands, GATHER, "gather_start")
    gathering = dict(zip(keys, flights))
    mod = jnp.transpose(mod4, (1, 0, 2)).reshape(DEPTH, 6, 1, D) + token[0:1, 0:1]

    def fetch(l, name, after):
        names = _FETCH_GROUPS[name]
        landed = exchange_wait([gathering[(n, l)] for n in names], GATHER, after, f"gather_wait_l{l}_{name}")
        return {n: _FROM_SHARDS[n](land) for n, (_, land) in zip(names, landed)}

    scattering = {}

    def emit(l, grads):
        names = list(grads)
        pieces = [_TO_SHARDS[n](grads[n]) for n in names]
        lands = [lax.empty(p.shape, p.dtype) for p in pieces]
        started, token = exchange_start(pieces, lands, SCATTER, f"scatter_start_l{l}_{names[0]}")
        scattering.update({(n, l): f for n, f in zip(names, started)})
        return token[0:1, 0:1]

    loss_row, dx, grads, dmods = _local_step(x.reshape(S, D), loss_target.reshape(S, D), mod, w, fetch, emit)
    loss = lax.psum(loss_row[0, 0], ("x", "y", "c"))
    grad_x = dx.reshape(1, S, D)

    small_srcs = pack_small(grads, dmods, "pack_small")
    small_flights, small_token = exchange_start(small_srcs, [lax.empty((N_DEV,) + a.shape, F32) for a in small_srcs], ALL,
                                                "gather_small_start")

    order = list(scattering)
    landed = dict(zip(order, exchange_wait([scattering[k] for k in order], SCATTER, small_token, "scatter_wait")))
    chip1 = chip.astype(jnp.int32).reshape(1)
    part = {}
    for n in SHARDED:
        cols = w[n].shape[-1]
        for l in range(DEPTH):
            src, land = landed[(n, l)]
            part[n] = sum_pieces(chip1, land.reshape(N_CHIP, -1, cols), src.reshape(N_CHIP, -1, cols), l, part.get(n),
                                 f"sum_{n}_l{l}")
        part[n] = part[n].reshape(-1, cols)
    parts = [part[n] for n in SHARDED]
    swap_flights, swap_token = exchange_start(parts, [lax.empty(p.shape, F32) for p in parts], SWAP, "swap_start")

    g_all, gw_all = (land for _, land in exchange_wait(small_flights, ALL, swap_token, "gather_small_wait"))
    out = adamw_small(g_all, gw_all, w, m, v, "adamw_small")

    dmod_all = g_all[:, :DEPTH * _LAYER_ROWS].reshape(N_DEV, DEPTH, _LAYER_ROWS * D)[:, :, :6 * D]
    dmod_loc = jnp.transpose(lax.dynamic_slice_in_dim(dmod_all, chip * ADA_COLS, ADA_COLS, axis=2), (1, 0, 2))
    out['w_ada'] = tuple(ada_update(c_all.T, dmod_loc, w_ada, m['w_ada'], v['w_ada'], "adamw_ada"))

    swapped = exchange_wait(swap_flights, SWAP, out['w_ada'][0], "swap_wait")
    part = {n: own for n, (own, _) in zip(SHARDED, swapped)}
    other = {n: got for n, (_, got) in zip(SHARDED, swapped)}
    for n in SHARDED:
        cols = w[n].shape[-1]
        res = adamw_pair(w[n].reshape(-1, cols), part[n], other[n], m[n].reshape(-1, cols), v[n].reshape(-1, cols),
                         "adamw_" + n)
        out[n] = tuple(a.reshape(w[n].shape) for a in res)

    return (loss, grad_x, *[out[n][0] for n in WEIGHTS], *[out[n][1] for n in WEIGHTS],
            *[out[n][2] for n in WEIGHTS], *[out[n][3] for n in WEIGHTS])
```

```python
import functools

import jax
import jax.numpy as jnp
from jax import lax
from jax.experimental import pallas as pl
from jax.experimental.pallas import tpu as pltpu

F32 = jnp.float32
BF16 = jnp.bfloat16
SDS = jax.ShapeDtypeStruct
MESH = pl.DeviceIdType.MESH
ANY = pl.BlockSpec(memory_space=pl.ANY)

DEPTH = 2
S = 2048
D = 1024
GW = 256
DFF = 2816
NUP = 2 * DFF
IN_COLS = 2308
ZC = 2432
CONV_K = 31
FFN_K = 3
EPS = 1e-6
SCALE = 0.125
NEG = -1e30
N_CHIP = 4
N_DEV = 8

Z_FOX_Q, Z_FOX_K, Z_FOX_V = 0, 256, 512
Z_FG = 768
Z_CONV_A, Z_CONV_G = 896, 1152
Z_SWA_Q, Z_SWA_K, Z_SWA_V = 1408, 1664, 1792
Z_SGU_U, Z_SGU_V = 1920, 2176
FG_END = 772

ADAM_LR, ADAM_B1, ADAM_B2, ADAM_EPS, ADAM_WD, ADAM_STEP = 0.001, 0.9, 0.999, 1e-08, 0.01, 10

TS = 256
TM = 1024
N_SPLIT = 2816
N_BLOCK = 1408

WEIGHTS = ['w_ada', 'b_ada', 'g_pre_mix', 'g_post_mix', 'g_pre_ffn', 'g_post_ffn', 'w_in', 'b_fgate', 'conv_w',
           'conv_b', 'conv_ln_g', 'conv_ln_b', 'conv_pw_w', 'conv_pw_b', 'swa_sinks', 'sgu_ln_g', 'sgu_ln_b',
           'sgu_w', 'sgu_b', 'g_group', 'w_out', 'ffn_w_up', 'ffn_conv_w', 'ffn_conv_b', 'ffn_w_down']
SHARDED = ['w_in', 'conv_w', 'conv_pw_w', 'w_out', 'ffn_w_up', 'ffn_conv_w', 'ffn_w_down']
SMALL = [n for n in WEIGHTS if n not in SHARDED and n != 'w_ada']


def _rms(x, g):
    return x * lax.rsqrt(jnp.mean(x * x, axis=-1, keepdims=True) + EPS) * g


def _modnorm(x, g, sc, sh):
    return _rms(x, g) * (1.0 + sc) + sh


def _resid(x, y, ga, g):
    return x + ga * _rms(y, g)


@functools.partial(jax.custom_vjp, nondiff_argnums=(1,))
def _shift_down(x, n):
    if n == 0:
        return x
    row = lax.broadcasted_iota(jnp.int32, x.shape, 0)
    return jnp.where(row >= n, pltpu.roll(x, n, axis=0), 0.0)


def _shift_up(x, n):
    if n == 0:
        return x
    rows = x.shape[0]
    row = lax.broadcasted_iota(jnp.int32, x.shape, 0)
    return jnp.where(row < rows - n, pltpu.roll(x, rows - n, axis=0), 0.0)


def _shift_down_fwd(x, n):
    return _shift_down(x, n), None


def _shift_down_bwd(n, _, ct):
    return (_shift_up(ct, n),)


_shift_down.defvjp(_shift_down_fwd, _shift_down_bwd)


def _nt(a, b):
    return lax.dot_general(a, b, (((1,), (1,)), ((), ())), preferred_element_type=F32)


def _tn(a, b):
    return lax.dot_general(a, b, (((0,), (0,)), ((), ())), preferred_element_type=F32)


def _nn(a, b):
    return jnp.dot(a, b, preferred_element_type=F32)


def _acc(ref, val, first):
    @pl.when(first)
    def _():
        ref[...] = val

    @pl.when(jnp.logical_not(first))
    def _():
        ref[...] += val


def _row_tile(rows, cols):
    limit = max(8, (1 << 20) // (4 * cols))
    best = None
    for t in range(8, rows + 1, 8):
        if rows % t == 0 and t <= limit:
            best = t
    return best if best is not None else rows


def _ncol(n):
    return n if n <= N_SPLIT else N_BLOCK


def _weight_spec(b, order):
    pick = (lambda j, i: j) if order == 0 else (lambda i, j: j)
    if b.ndim == 3:
        _, k, tn = b.shape
        return pl.BlockSpec((None, k, tn), lambda *g: (pick(*g), 0, 0)), k, N_CHIP * tn, tn
    k, n = b.shape
    tn = _ncol(n)
    return pl.BlockSpec((k, tn), lambda *g: (0, pick(*g))), k, n, tn


def _side_by_side(refs):
    return refs[0][...] if len(refs) == 1 else jnp.concatenate([r[...] for r in refs], axis=1)


def mm_nn(pieces, b, out_dtype, name):
    m = pieces[0].shape[0]
    b_spec, k, n, tn = _weight_spec(b, 0)

    def body(*refs):
        a_refs, b_ref, o_ref = refs[:-2], refs[-2], refs[-1]
        o_ref[...] = _nn(_side_by_side(a_refs), b_ref[...]).astype(out_dtype)

    return pl.pallas_call(
        body, grid=(n // tn, m // TM), name=name,
        in_specs=[pl.BlockSpec((TM, p.shape[1]), lambda j, i: (i, 0)) for p in pieces] + [b_spec],
        out_specs=pl.BlockSpec((TM, tn), lambda j, i: (i, j)),
        out_shape=SDS((m, n), out_dtype),
    )(*pieces, b)


def mm_nt(pieces, b, name):
    m = pieces[0].shape[0]
    k, n = b.shape
    assert n == sum(p.shape[1] for p in pieces) <= N_SPLIT

    def body(*refs):
        a_refs, b_ref, o_ref = refs[:-2], refs[-2], refs[-1]
        o_ref[...] = _nt(_side_by_side(a_refs), b_ref[...])

    return pl.pallas_call(
        body, grid=(m // TM,), name=name,
        in_specs=[pl.BlockSpec((TM, p.shape[1]), lambda i: (i, 0)) for p in pieces] + [pl.BlockSpec((k, n), lambda i: (0, 0))],
        out_specs=pl.BlockSpec((TM, k), lambda i: (i, 0)),
        out_shape=SDS((m, k), F32),
    )(*pieces, b)


def mm_nt_halves(a0, a1, b4, name):
    m = a0.shape[0]
    _, k, tc = b4.shape

    def body(a0_ref, a1_ref, b_ref, o_ref):
        c = pl.program_id(1)

        @pl.when(c == 0)
        def _():
            o_ref[...] = _nt(a0_ref[...], b_ref[...])

        @pl.when(c == 1)
        def _():
            o_ref[...] += _nt(a0_ref[...], b_ref[...])

        @pl.when(c >= 2)
        def _():
            o_ref[...] += _nt(a1_ref[...], b_ref[...])

    return pl.pallas_call(
        body, grid=(m // TM, N_CHIP), name=name,
        in_specs=[pl.BlockSpec((TM, tc), lambda i, c: (i, jnp.minimum(c, 1))),
                  pl.BlockSpec((TM, tc), lambda i, c: (i, jnp.maximum(c - 2, 0))),
                  pl.BlockSpec((None, k, tc), lambda i, c: (c, 0, 0))],
        out_specs=pl.BlockSpec((TM, k), lambda i, c: (i, 0)),
        out_shape=SDS((m, k), F32),
    )(a0, a1, b4)


def mm_tn(a_pieces, b_pieces, name):
    m = a_pieces[0].shape[0]
    k = sum(p.shape[1] for p in a_pieces)
    n = sum(p.shape[1] for p in b_pieces)
    assert n <= N_SPLIT
    steps = m // TM
    n_a = len(a_pieces)

    def body(*refs):
        a_refs, b_refs, o_ref, acc_ref = refs[:n_a], refs[n_a:-2], refs[-2], refs[-1]
        i = pl.program_id(0)
        _acc(acc_ref, _tn(_side_by_side(a_refs), _side_by_side(b_refs)), i == 0)

        @pl.when(i == steps - 1)
        def _():
            o_ref[...] = acc_ref[...].astype(BF16)

    return pl.pallas_call(
        body, grid=(steps,), name=name,
        in_specs=[pl.BlockSpec((TM, p.shape[1]), lambda i: (i, 0)) for p in list(a_pieces) + list(b_pieces)],
        out_specs=pl.BlockSpec((k, n), lambda i: (0, 0)), out_shape=SDS((k, n), BF16),
        scratch_shapes=[pltpu.VMEM((k, n), F32)],
    )(*a_pieces, *b_pieces)


def mm_tn_halves(a, b0, b1, name):
    m, k = a.shape
    tn = b0.shape[1] // 2
    steps = m // TM

    def body(a_ref, b0_ref, b1_ref, o_ref, acc_ref):
        j, i = pl.program_id(0), pl.program_id(1)

        @pl.when(j < 2)
        def _():
            _acc(acc_ref, _tn(a_ref[...], b0_ref[...]), i == 0)

        @pl.when(j >= 2)
        def _():
            _acc(acc_ref, _tn(a_ref[...], b1_ref[...]), i == 0)

        @pl.when(i == steps - 1)
        def _():
            o_ref[...] = acc_ref[...].astype(BF16)

    return pl.pallas_call(
        body, grid=(N_CHIP, steps), name=name,
        in_specs=[pl.BlockSpec((TM, k), lambda j, i: (i, 0)),
                  pl.BlockSpec((TM, tn), lambda j, i: (i, jnp.minimum(j, 1))),
                  pl.BlockSpec((TM, tn), lambda j, i: (i, jnp.maximum(j - 2, 0)))],
        out_specs=pl.BlockSpec((None, k, tn), lambda j, i: (j, 0, 0)), out_shape=SDS((N_CHIP, k, tn), BF16),
        scratch_shapes=[pltpu.VMEM((k, tn), F32)],
    )(a, b0, b1)


_ROW = pl.BlockSpec((TS, D), lambda i: (i, 0))
_VEC = pl.BlockSpec((1, D), lambda i: (0, 0))


def modnorm_fwd(x, g, sc, sh, name):
    def body(x_ref, g_ref, sc_ref, sh_ref, o_ref):
        o_ref[...] = _modnorm(x_ref[...], g_ref[...], sc_ref[...], sh_ref[...]).astype(BF16)

    return pl.pallas_call(body, grid=(S // TS,), name=name, in_specs=[_ROW, _VEC, _VEC, _VEC], out_specs=_ROW,
                          out_shape=SDS((S, D), BF16))(x, g, sc, sh)


def modnorm_bwd(x, g, sc, sh, dh, dx_in, name):
    def body(x_ref, g_ref, sc_ref, sh_ref, dh_ref, dxin_ref, dx_ref, dg_ref, dsc_ref, dsh_ref):
        _, vjp = jax.vjp(_modnorm, x_ref[...], g_ref[...], sc_ref[...], sh_ref[...])
        dx, dg, dsc, dsh = vjp(dh_ref[...])
        dx_ref[...] = dxin_ref[...] + dx
        first = pl.program_id(0) == 0
        _acc(dg_ref, dg, first)
        _acc(dsc_ref, dsc, first)
        _acc(dsh_ref, dsh, first)

    vec = SDS((1, D), F32)
    return pl.pallas_call(body, grid=(S // TS,), name=name, in_specs=[_ROW, _VEC, _VEC, _VEC, _ROW, _ROW],
                          out_specs=[_ROW, _VEC, _VEC, _VEC], out_shape=[SDS((S, D), F32), vec, vec, vec])(
                              x, g, sc, sh, dh, dx_in)


def resid_fwd(x, y, ga, g, name):
    def body(x_ref, y_ref, ga_ref, g_ref, o_ref):
        o_ref[...] = _resid(x_ref[...], y_ref[...], ga_ref[...], g_ref[...])

    return pl.pallas_call(body, grid=(S // TS,), name=name, in_specs=[_ROW, _ROW, _VEC, _VEC], out_specs=_ROW,
                          out_shape=SDS((S, D), F32))(x, y, ga, g)


def resid_bwd(y, ga, g, dxo, name):
    def body(y_ref, ga_ref, g_ref, dxo_ref, dy_ref, dga_ref, dg_ref):
        _, vjp = jax.vjp(lambda y, ga, g: ga * _rms(y, g), y_ref[...], ga_ref[...], g_ref[...])
        dy, dga, dg = vjp(dxo_ref[...])
        dy_ref[...] = dy.astype(BF16)
        first = pl.program_id(0) == 0
        _acc(dga_ref, dga, first)
        _acc(dg_ref, dg, first)

    vec = SDS((1, D), F32)
    return pl.pallas_call(body, grid=(S // TS,), name=name, in_specs=[_ROW, _VEC, _VEC, _ROW],
                          out_specs=[_ROW, _VEC, _VEC], out_shape=[SDS((S, D), BF16), vec, vec])(y, ga, g, dxo)


def loss_grad(xf, target, name):
    def body(x_ref, t_ref, dx_ref, l_ref):
        err = x_ref[...] - t_ref[...]
        dx_ref[...] = err * (1.0 / D)
        part = 0.5 * jnp.sum(jnp.mean(err * err, axis=-1, keepdims=True), axis=0, keepdims=True)
        _acc(l_ref, jnp.broadcast_to(part, (1, 128)), pl.program_id(0) == 0)

    return pl.pallas_call(body, grid=(S // TS,), name=name, in_specs=[_ROW, _ROW],
                          out_specs=[_ROW, pl.BlockSpec((1, 128), lambda i: (0, 0))],
                          out_shape=[SDS((S, D), F32), SDS((1, 128), F32)])(xf, target)


_FG_SPEC = pl.BlockSpec((S, 128), lambda i: (0, Z_FG // 128))


def _tri128(lower):
    r = lax.broadcasted_iota(jnp.int32, (128, 128), 0)
    c = lax.broadcasted_iota(jnp.int32, (128, 128), 1)
    return ((r >= c) if lower else (r <= c)).astype(F32)


def fgate_fwd(z, bf, name):
    def body(z_ref, b_ref, cc_ref, cr_ref):
        tri = _tri128(True)
        carry = jnp.zeros((1, 128), F32)
        for i in range(S // 128):
            rows = pl.ds(i * 128, 128)
            lf = jax.nn.log_sigmoid(z_ref[rows, :] + b_ref[...])
            c = jnp.dot(tri, lf, precision=lax.Precision.HIGHEST, preferred_element_type=F32) + carry
            cc_ref[rows, :] = c
            carry = c[127:128, :]
        cr_ref[...] = cc_ref[...].T

    return pl.pallas_call(body, name=name, grid=(1,),
                          in_specs=[_FG_SPEC, pl.BlockSpec((1, 128), lambda i: (0, 0))],
                          out_specs=[pl.BlockSpec((S, 128), lambda i: (0, 0)), pl.BlockSpec((128, S), lambda i: (0, 0))],
                          out_shape=[SDS((S, 128), F32), SDS((128, S), F32)])(z, bf)


def fgate_bwd(z, bf, dcq, dck, name):
    def body(z_ref, b_ref, dcq_ref, dck_ref, dz_ref, db_ref, col_ref):
        col_ref[...] = dcq_ref[...] + jnp.concatenate([dck_ref[...], jnp.zeros((120, S), F32)], axis=0).T
        tri = _tri128(False)
        carry = jnp.zeros((1, 128), F32)
        db = jnp.zeros((1, 128), F32)
        for i in reversed(range(S // 128)):
            rows = pl.ds(i * 128, 128)
            dlf = jnp.dot(tri, col_ref[rows, :], precision=lax.Precision.HIGHEST, preferred_element_type=F32) + carry
            carry = dlf[0:1, :]
            dz = dlf * jax.nn.sigmoid(-(z_ref[rows, :] + b_ref[...]))
            dz_ref[rows, :] = dz.astype(BF16)
            db = db + jnp.sum(dz, axis=0, keepdims=True)
        db_ref[...] = db

    return pl.pallas_call(body, name=name, grid=(1,),
                          in_specs=[_FG_SPEC, pl.BlockSpec((1, 128), lambda i: (0, 0)),
                                    pl.BlockSpec((S, 128), lambda i: (0, 0)), pl.BlockSpec((8, S), lambda i: (0, 0))],
                          out_specs=[pl.BlockSpec((S, 128), lambda i: (0, 0)), pl.BlockSpec((1, 128), lambda i: (0, 0))],
                          out_shape=[SDS((S, 128), BF16), SDS((1, 128), F32)],
                          scratch_shapes=[pltpu.VMEM((S, 128), F32)])(z, bf, dcq, dck)


TQ = 256


def _head_mask(hh):
    lane = lax.broadcasted_iota(jnp.int32, (TQ, 128), 1)
    return (lane >= 64 * hh) & (lane < 64 * hh + 64)


def _fox_specs():
    q = pl.BlockSpec((TQ, 256), lambda i: (i, Z_FOX_Q // 256))
    k = pl.BlockSpec((S, 256), lambda i: (0, Z_FOX_K // 256))
    v = pl.BlockSpec((S, 256), lambda i: (0, Z_FOX_V // 256))
    cc = pl.BlockSpec((TQ, 128), lambda i: (i, 0))
    cr = pl.BlockSpec((8, S), lambda i: (0, 0))
    return q, k, v, cc, cr


FOX_SPAN = 2
FOX_GROUPS = S // (FOX_SPAN * TQ)


def _fox_scores(qm, k, cc_h, cr_h, i):
    klen = k.shape[0]
    s = _nt(qm, k) * SCALE + cc_h - cr_h
    qpos = i * TQ + lax.broadcasted_iota(jnp.int32, (TQ, klen), 0)
    kpos = lax.broadcasted_iota(jnp.int32, (TQ, klen), 1)
    return jnp.where(kpos <= qpos, s, NEG)


def _for_key_length(i, fn):
    for g in range(FOX_GROUPS):
        pl.when(i // FOX_SPAN == g)(functools.partial(fn, (g + 1) * FOX_SPAN * TQ))


def fox_fwd(z, cumc, cumr, name):
    def body(q_ref, k_ref, v_ref, cc_ref, cr_ref, o_ref, l_ref):
        i = pl.program_id(0)

        def block(klen):
            lane = lax.broadcasted_iota(jnp.int32, (TQ, 128), 1)
            cc = cc_ref[...]
            lse = jnp.zeros((TQ, 128), F32)
            for p in range(2):
                cols = pl.ds(128 * p, 128)
                q = q_ref[:, cols]
                k = k_ref[0:klen, cols].astype(BF16)
                v = v_ref[0:klen, cols].astype(BF16)
                o_pair = jnp.zeros((TQ, 128), F32)
                for hh in range(2):
                    h = 2 * p + hh
                    hm = _head_mask(hh)
                    qm = jnp.where(hm, q, 0.0).astype(BF16)
                    s = _fox_scores(qm, k, cc[:, h:h + 1], cr_ref[h:h + 1, 0:klen], i)
                    m = jnp.max(s, axis=1, keepdims=True)
                    e = jnp.exp(s - m)
                    l = jnp.sum(e, axis=1, keepdims=True)
                    o_pair = jnp.where(hm, _nn(e.astype(BF16), v) / l, o_pair)
                    lse = jnp.where(lane == h, m + jnp.log(l), lse)
                o_ref[:, cols] = o_pair
            l_ref[...] = lse

        _for_key_length(i, block)

    q, k, v, cc, cr = _fox_specs()
    return pl.pallas_call(body, grid=(S // TQ,), name=name, in_specs=[q, k, v, cc, cr],
                          out_specs=[pl.BlockSpec((TQ, 256), lambda i: (i, 0)), cc],
                          out_shape=[SDS((S, 256), F32), SDS((S, 128), F32)])(z, z, z, cumc, cumr)


def fox_bwd(z, cumc, cumr, lse, o, do, name):
    steps = S // TQ

    def body(q_ref, k_ref, v_ref, cc_ref, cr_ref, l_ref, o_ref, do_ref, dq_ref, dk_ref, dv_ref, dcq_ref, dck_ref,
             dk_acc, dv_acc):
        i = pl.program_id(0)

        @pl.when(i == 0)
        def _():
            dk_acc[...] = jnp.zeros_like(dk_acc)
            dv_acc[...] = jnp.zeros_like(dv_acc)
            dck_ref[...] = jnp.zeros_like(dck_ref)

        def block(klen):
            lane = lax.broadcasted_iota(jnp.int32, (TQ, 128), 1)
            cc = cc_ref[...]
            lse_all = l_ref[...]
            dcq = jnp.zeros((TQ, 128), F32)
            for p in range(2):
                cols = pl.ds(128 * p, 128)
                q = q_ref[:, cols]
                k = k_ref[0:klen, cols].astype(BF16)
                v = v_ref[0:klen, cols].astype(BF16)
                o_p = o_ref[:, cols]
                do_p = do_ref[:, cols]
                dq_pair = jnp.zeros((TQ, 128), F32)
                dk_pair = jnp.zeros((klen, 128), F32)
                dv_pair = jnp.zeros((klen, 128), F32)
                for hh in range(2):
                    h = 2 * p + hh
                    hm = _head_mask(hh)
                    qm = jnp.where(hm, q, 0.0).astype(BF16)
                    s = _fox_scores(qm, k, cc[:, h:h + 1], cr_ref[h:h + 1, 0:klen], i)
                    pn = jnp.exp(s - lse_all[:, h:h + 1])
                    dom = jnp.where(hm, do_p, 0.0)
                    dl = jnp.sum(dom * o_p, axis=1, keepdims=True)
                    dom = dom.astype(BF16)
                    ds = pn * (_nt(dom, v) - dl)
                    dsb = ds.astype(BF16)
                    dq_pair = jnp.where(hm, _nn(dsb, k) * SCALE, dq_pair)
                    dk_pair = dk_pair + _tn(dsb, qm) * SCALE
                    dv_pair = dv_pair + _tn(pn.astype(BF16), dom)
                    dck_ref[h:h + 1, 0:klen] -= jnp.sum(ds, axis=0, keepdims=True)
                    dcq = jnp.where(lane == h, jnp.sum(ds, axis=1, keepdims=True), dcq)
                dq_ref[:, cols] = dq_pair.astype(BF16)
                dk_acc[0:klen, cols] += dk_pair
                dv_acc[0:klen, cols] += dv_pair
            dcq_ref[...] = dcq

        _for_key_length(i, block)

        @pl.when(i == steps - 1)
        def _():
            dk_ref[...] = dk_acc[...].astype(BF16)
            dv_ref[...] = dv_acc[...].astype(BF16)

    q, k, v, cc, cr = _fox_specs()
    blk = pl.BlockSpec((TQ, 256), lambda i: (i, 0))
    full = pl.BlockSpec((S, 256), lambda i: (0, 0))
    return pl.pallas_call(body, grid=(steps,), name=name, in_specs=[q, k, v, cc, cr, cc, blk, blk],
                          out_specs=[blk, full, full, cc, cr],
                          out_shape=[SDS((S, 256), BF16), SDS((S, 256), BF16), SDS((S, 256), BF16), SDS((S, 128), F32),
                                     SDS((8, S), F32)],
                          scratch_shapes=[pltpu.VMEM((S, 256), F32), pltpu.VMEM((S, 256), F32)])(
                              z, z, z, cumc, cumr, lse, o, do)


W = 128
SWA_HEADS = 4


def _swa_core(first, qs, kcat, vcat, sink):
    r = lax.broadcasted_iota(jnp.int32, (SWA_HEADS * W, 2 * W), 0)
    j = lax.broadcasted_iota(jnp.int32, (SWA_HEADS * W, 2 * W), 1)
    qi = r & (W - 1)
    valid = ((j < W) & (j > qi) & jnp.logical_not(first)) | ((j >= W) & (j - W <= qi))
    s = jnp.where(valid, _nt(qs.astype(BF16), kcat.astype(BF16)) * SCALE, NEG)
    m = lax.stop_gradient(jnp.maximum(jnp.max(s, axis=1, keepdims=True), sink))
    e = jnp.exp(s - m)
    den = jnp.sum(e, axis=1, keepdims=True) + jnp.exp(sink - m)
    return _nn((e / den).astype(BF16), vcat.astype(BF16))


def _kv_lanes(kv):
    lane = lax.broadcasted_iota(jnp.int32, (W, 128), 1)
    return (lane >= 64 * kv) & (lane < 64 * kv + 64)


def _swa_stack(pair0, pair1):
    blocks = []
    for h in range(SWA_HEADS):
        kv, hh = h // 2, h % 2
        a = (pair0, pair1)[kv]
        a = a if hh == kv else pltpu.roll(a, 64, axis=1)
        blocks.append(jnp.where(_kv_lanes(kv), a, 0.0))
    return jnp.concatenate(blocks, axis=0)


def _swa_unstack(stacked):
    pairs = [jnp.zeros((W, 128), F32), jnp.zeros((W, 128), F32)]
    for h in range(SWA_HEADS):
        kv, hh = h // 2, h % 2
        a = jnp.where(_kv_lanes(kv), stacked[h * W:(h + 1) * W], 0.0)
        pairs[kv] = pairs[kv] + (a if hh == kv else pltpu.roll(a, 64, axis=1))
    return pairs


def _head_rows():
    return lax.broadcasted_iota(jnp.int32, (SWA_HEADS * W, 1), 0) // W


def _swa_operands(q0_ref, q1_ref, kp_ref, kc_ref, vp_ref, vc_ref, sk_ref):
    qs = _swa_stack(q0_ref[...], q1_ref[...])
    kcat = jnp.concatenate([kp_ref[...], kc_ref[...]], axis=0)
    vcat = jnp.concatenate([vp_ref[...], vc_ref[...]], axis=0)
    sink = jnp.zeros((SWA_HEADS * W, 1), F32)
    for h in range(SWA_HEADS):
        sink = jnp.where(_head_rows() == h, sk_ref[:, h:h + 1], sink)
    return qs, kcat, vcat, sink


def _swa_specs():
    q0 = pl.BlockSpec((W, 128), lambda n: (n, Z_SWA_Q // 128))
    q1 = pl.BlockSpec((W, 128), lambda n: (n, Z_SWA_Q // 128 + 1))
    kc = pl.BlockSpec((W, 128), lambda n: (n, Z_SWA_K // 128))
    kp = pl.BlockSpec((W, 128), lambda n: (jnp.maximum(n - 1, 0), Z_SWA_K // 128))
    vc = pl.BlockSpec((W, 128), lambda n: (n, Z_SWA_V // 128))
    vp = pl.BlockSpec((W, 128), lambda n: (jnp.maximum(n - 1, 0), Z_SWA_V // 128))
    sk = pl.BlockSpec((1, 128), lambda n: (0, 0))
    return q0, q1, kp, kc, vp, vc, sk


def swa_fwd(z, sinks, name):
    def body(*refs):
        o_ref = refs[-1]
        o = _swa_core(pl.program_id(0) == 0, *_swa_operands(*refs[:-1]))
        o0, o1 = _swa_unstack(o)
        o_ref[:, 0:128] = o0
        o_ref[:, 128:256] = o1

    return pl.pallas_call(body, grid=(S // W,), name=name, in_specs=list(_swa_specs()),
                          out_specs=pl.BlockSpec((W, 256), lambda n: (n, 0)),
                          out_shape=SDS((S, 256), F32))(z, z, z, z, z, z, sinks)


def swa_bwd(z, sinks, do, name):
    steps = S // W

    def body(*refs):
        do_ref, dq_ref, dk_ref, dv_ref, dsk_ref, dk_acc, dv_acc = refs[7:]
        n = pl.program_id(0)
        first = n == 0
        _, vjp = jax.vjp(functools.partial(_swa_core, first), *_swa_operands(*refs[:7]))
        dqs, dkcat, dvcat, dsink = vjp(_swa_stack(do_ref[:, 0:128], do_ref[:, 128:256]))
        dq0, dq1 = _swa_unstack(dqs)
        dq_ref[:, 0:128] = dq0.astype(BF16)
        dq_ref[:, 128:256] = dq1.astype(BF16)

        @pl.when(first)
        def _():
            dk_acc[...] = jnp.zeros_like(dk_acc)
            dv_acc[...] = jnp.zeros_like(dv_acc)

        cur = pl.ds(pl.multiple_of(n * W, W), W)
        dk_acc[cur, :] += dkcat[W:2 * W]
        dv_acc[cur, :] += dvcat[W:2 * W]

        @pl.when(n > 0)
        def _():
            prev = pl.ds(pl.multiple_of((n - 1) * W, W), W)
            dk_acc[prev, :] += dkcat[0:W]
            dv_acc[prev, :] += dvcat[0:W]

        lane = lax.broadcasted_iota(jnp.int32, (1, 128), 1)
        dsk = jnp.zeros((1, 128), F32)
        for h in range(SWA_HEADS):
            d = jnp.sum(jnp.where(_head_rows() == h, dsink, 0.0), axis=0, keepdims=True)
            dsk = jnp.where(lane == h, d, dsk)
        _acc(dsk_ref, dsk, first)

        @pl.when(n == steps - 1)
        def _():
            dk_ref[...] = dk_acc[...].astype(BF16)
            dv_ref[...] = dv_acc[...].astype(BF16)

    blk = pl.BlockSpec((W, 256), lambda n: (n, 0))
    full = pl.BlockSpec((S, 128), lambda n: (0, 0))
    return pl.pallas_call(body, grid=(steps,), name=name, in_specs=list(_swa_specs()) + [blk],
                          out_specs=[blk, full, full, pl.BlockSpec((1, 128), lambda n: (0, 0))],
                          out_shape=[SDS((S, 256), BF16), SDS((S, 128), BF16), SDS((S, 128), BF16), SDS((1, 128), F32)],
                          scratch_shapes=[pltpu.VMEM((S, 128), F32), pltpu.VMEM((S, 128), F32)])(
                              z, z, z, z, z, z, sinks, do)


def _glu(a, g):
    return a * jax.nn.sigmoid(g)


def _cv1_specs():
    a = pl.BlockSpec((S, 128), lambda j: (0, Z_CONV_A // 128 + j))
    g = pl.BlockSpec((S, 128), lambda j: (0, Z_CONV_G // 128 + j))
    w = pl.BlockSpec((32, 128), lambda j: (0, j))
    b = pl.BlockSpec((1, 128), lambda j: (0, j))
    h = pl.BlockSpec((S, 128), lambda j: (0, j))
    return a, g, w, b, h


def conv_dw_fwd(z, cw, cb, name):
    def body(a_ref, g_ref, w_ref, b_ref, o_ref):
        hh = _glu(a_ref[...], g_ref[...])
        acc = jnp.zeros((S, 128), F32) + b_ref[...]
        for k in range(CONV_K):
            acc = acc + _shift_down(hh, CONV_K - 1 - k) * w_ref[k:k + 1, :]
        o_ref[...] = acc

    a, g, w, b, h = _cv1_specs()
    return pl.pallas_call(body, grid=(2,), name=name, in_specs=[a, g, w, b], out_specs=h,
                          out_shape=SDS((S, 256), F32))(z, z, cw, cb)


def conv_dw_bwd(z, cw, dhc, name):
    def body(a_ref, g_ref, w_ref, dh_ref, da_ref, dg_ref, dw_ref, db_ref):
        hh, vjp = jax.vjp(_glu, a_ref[...], g_ref[...])
        dh = dh_ref[...]
        dhh = jnp.zeros((S, 128), F32)
        for k in range(CONV_K):
            n = CONV_K - 1 - k
            dhh = dhh + _shift_up(dh, n) * w_ref[k:k + 1, :]
            dw_ref[k:k + 1, :] = jnp.sum(dh * _shift_down(hh, n), axis=0, keepdims=True)
        dw_ref[CONV_K:32, :] = jnp.zeros((32 - CONV_K, 128), F32)
        db_ref[...] = jnp.sum(dh, axis=0, keepdims=True)
        da, dg = vjp(dhh)
        da_ref[...] = da.astype(BF16)
        dg_ref[...] = dg.astype(BF16)

    a, g, w, b, h = _cv1_specs()
    return pl.pallas_call(body, grid=(2,), name=name, in_specs=[a, g, w, h], out_specs=[h, h, w, b],
                          out_shape=[SDS((S, 256), BF16), SDS((S, 256), BF16), SDS((32, 256), F32), SDS((1, 256), F32)])(
                              z, z, cw, dhc)


def _ln(x, g, b):
    mu = jnp.mean(x, axis=-1, keepdims=True)
    xc = x - mu
    var = jnp.mean(xc * xc, axis=-1, keepdims=True)
    return xc * lax.rsqrt(var + EPS) * g + b


def _conv_pw(hc, lg, lb, pw, pb):
    y = jax.nn.silu(_ln(hc, lg, lb))
    return _nn(y.astype(BF16), pw.astype(BF16)) + pb


TS2 = 512
_ROW2 = pl.BlockSpec((TS2, 256), lambda i: (i, 0))
_VEC2 = pl.BlockSpec((1, 256), lambda i: (0, 0))
_MAT2 = pl.BlockSpec((256, 256), lambda i: (0, 0))


def conv_pw_fwd(hc, lg, lb, pw, pb, name):
    def body(h_ref, lg_ref, lb_ref, pw_ref, pb_ref, o_ref):
        o_ref[...] = _conv_pw(h_ref[...], lg_ref[...], lb_ref[...], pw_ref[...], pb_ref[...])

    return pl.pallas_call(body, grid=(S // TS2,), name=name, in_specs=[_ROW2, _VEC2, _VEC2, _MAT2, _VEC2],
                          out_specs=_ROW2, out_shape=SDS((S, 256), F32))(hc, lg, lb, pw, pb)


def conv_pw_bwd(hc, lg, lb, pw, pb, dy, name):
    def body(h_ref, lg_ref, lb_ref, pw_ref, pb_ref, dy_ref, dh_ref, dlg_ref, dlb_ref, dpw_ref, dpb_ref):
        _, vjp = jax.vjp(_conv_pw, h_ref[...], lg_ref[...], lb_ref[...], pw_ref[...], pb_ref[...])
        dh, dlg, dlb, dpw, dpb = vjp(dy_ref[...])
        dh_ref[...] = dh
        first = pl.program_id(0) == 0
        _acc(dlg_ref, dlg, first)
        _acc(dlb_ref, dlb, first)
        _acc(dpw_ref, dpw, first)
        _acc(dpb_ref, dpb, first)

    vec = SDS((1, 256), F32)
    return pl.pallas_call(body, grid=(S // TS2,), name=name, in_specs=[_ROW2, _VEC2, _VEC2, _MAT2, _VEC2, _ROW2],
                          out_specs=[_ROW2, _VEC2, _VEC2, _MAT2, _VEC2],
                          out_shape=[SDS((S, 256), F32), vec, vec, SDS((256, 256), F32), vec])(hc, lg, lb, pw, pb, dy)


def _sgu_block(u0, u1, v0, v1, lg0, lg1, lb0, lb1, w0, w1, w2, w3, bt):
    u0, u1, v0, v1 = (jax.nn.gelu(a) for a in (u0, u1, v0, v1))
    mu = (jnp.sum(v0, axis=1, keepdims=True) + jnp.sum(v1, axis=1, keepdims=True)) * (1.0 / GW)
    c0, c1 = v0 - mu, v1 - mu
    var = (jnp.sum(c0 * c0, axis=1, keepdims=True) + jnp.sum(c1 * c1, axis=1, keepdims=True)) * (1.0 / GW)
    r = lax.rsqrt(var + EPS)
    n0 = c0 * r * lg0 + lb0
    n1 = c1 * r * lg1 + lb1
    row = lax.broadcasted_iota(jnp.int32, (128, 128), 0)
    col = lax.broadcasted_iota(jnp.int32, (128, 128), 1)
    tri = row >= col
    outs = []
    for p, (n, u, wa, wb) in enumerate(((n0, u0, w0, w1), (n1, u1, w2, w3))):
        nb = n.astype(BF16)
        ma = _nn(jnp.where(tri, wa, 0.0).astype(BF16), nb)
        mb = _nn(jnp.where(tri, wb, 0.0).astype(BF16), nb)
        expand = (row == 2 * p + col // 64).astype(F32)
        bias = jnp.dot(bt, expand, precision=lax.Precision.HIGHEST, preferred_element_type=F32)
        outs.append(u * (jnp.where(col < 64, ma, mb) + bias))
    return outs[0], outs[1]


def _sgu_specs():
    def col(c):
        return pl.BlockSpec((128, 128), lambda n, c=c: (n, c))
    zs = [col(Z_SGU_U // 128), col(Z_SGU_U // 128 + 1), col(Z_SGU_V // 128), col(Z_SGU_V // 128 + 1)]
    vec = [pl.BlockSpec((1, 128), lambda n: (0, 0)), pl.BlockSpec((1, 128), lambda n: (0, 1))]
    ws = [pl.BlockSpec((None, 128, 128), lambda n, g=g: (g, 0, 0)) for g in range(4)]
    bt = pl.BlockSpec((128, 128), lambda n: (0, 0))
    return zs + vec + vec + ws + [bt]


def sgu_fwd(z, lg, lb, w, bt, name):
    def body(*refs):
        o_ref = refs[-1]
        y0, y1 = _sgu_block(*[r[...] for r in refs[:-1]])
        o_ref[:, 0:128] = y0
        o_ref[:, 128:256] = y1

    return pl.pallas_call(body, grid=(S // 128,), name=name, in_specs=_sgu_specs(),
                          out_specs=pl.BlockSpec((128, 256), lambda n: (n, 0)),
                          out_shape=SDS((S, 256), F32))(z, z, z, z, lg, lg, lb, lb, w, w, w, w, bt)


def sgu_bwd(z, lg, lb, w, bt, dy, name):
    def body(*refs):
        ins, dy_ref = refs[:13], refs[13]
        du_ref, dv_ref, dlg_ref, dlb_ref, dw_ref, dbt_ref = refs[14:]
        _, vjp = jax.vjp(_sgu_block, *[r[...] for r in ins])
        du0, du1, dv0, dv1, dlg0, dlg1, dlb0, dlb1, dw0, dw1, dw2, dw3, dbt = vjp((dy_ref[:, 0:128], dy_ref[:, 128:256]))
        du_ref[:, 0:128] = du0.astype(BF16)
        du_ref[:, 128:256] = du1.astype(BF16)
        dv_ref[:, 0:128] = dv0.astype(BF16)
        dv_ref[:, 128:256] = dv1.astype(BF16)
        first = pl.program_id(0) == 0

        @pl.when(first)
        def _():
            dlg_ref[...] = jnp.zeros_like(dlg_ref)
            dlb_ref[...] = jnp.zeros_like(dlb_ref)
            dw_ref[...] = jnp.zeros_like(dw_ref)
            dbt_ref[...] = jnp.zeros_like(dbt_ref)

        dlg_ref[:, 0:128] += dlg0
        dlg_ref[:, 128:256] += dlg1
        dlb_ref[:, 0:128] += dlb0
        dlb_ref[:, 128:256] += dlb1
        for g, d in enumerate((dw0, dw1, dw2, dw3)):
            dw_ref[g] += d
        dbt_ref[...] += dbt

    blk = pl.BlockSpec((128, 256), lambda n: (n, 0))
    vec = pl.BlockSpec((1, 256), lambda n: (0, 0))
    return pl.pallas_call(body, grid=(S // 128,), name=name, in_specs=_sgu_specs() + [blk],
                          out_specs=[blk, blk, vec, vec, pl.BlockSpec((4, 128, 128), lambda n: (0, 0, 0)),
                                     pl.BlockSpec((128, 128), lambda n: (0, 0))],
                          out_shape=[SDS((S, 256), BF16), SDS((S, 256), BF16), SDS((1, 256), F32), SDS((1, 256), F32),
                                     SDS((4, 128, 128), F32), SDS((128, 128), F32)])(
                                         z, z, z, z, lg, lg, lb, lb, w, w, w, w, bt, dy)


def _group_norm(y0, y1, y2, y3, g0, g1, g2, g3):
    return tuple(_rms(y, g) for y, g in zip((y0, y1, y2, y3), (g0, g1, g2, g3)))


_GROW = pl.BlockSpec((TS2, 256), lambda i: (i, 0))
_GCAT = pl.BlockSpec((TS2, D), lambda i: (i, 0))
_GVEC = [pl.BlockSpec((1, 256), lambda i, j=j: (0, j)) for j in range(4)]


def group_norm_fwd(ys, gg, name):
    def body(*refs):
        o_ref = refs[-1]
        outs = _group_norm(*[r[...] for r in refs[:-1]])
        for j, c in enumerate(outs):
            o_ref[:, 256 * j:256 * (j + 1)] = c.astype(BF16)

    return pl.pallas_call(body, grid=(S // TS2,), name=name, in_specs=[_GROW] * 4 + _GVEC, out_specs=_GCAT,
                          out_shape=SDS((S, D), BF16))(*ys, gg, gg, gg, gg)


def group_norm_bwd(ys, gg, dcat, name):
    def body(*refs):
        ins, dc_ref = refs[:8], refs[8]
        dy_refs, dg_ref = refs[9:13], refs[13]
        _, vjp = jax.vjp(_group_norm, *[r[...] for r in ins])
        grads = vjp(tuple(dc_ref[:, 256 * j:256 * (j + 1)] for j in range(4)))
        first = pl.program_id(0) == 0

        @pl.when(first)
        def _():
            dg_ref[...] = jnp.zeros_like(dg_ref)

        for j in range(4):
            dy_refs[j][...] = grads[j]
            dg_ref[:, 256 * j:256 * (j + 1)] += grads[4 + j]

    return pl.pallas_call(body, grid=(S // TS2,), name=name, in_specs=[_GROW] * 4 + _GVEC + [_GCAT],
                          out_specs=[_GROW] * 4 + [pl.BlockSpec((1, D), lambda i: (0, 0))],
                          out_shape=[SDS((S, 256), F32)] * 4 + [SDS((1, D), F32)])(*ys, gg, gg, gg, gg, dcat)


FB = 256
N_FB = DFF // FB


def _ffn_gate(ug, uv, wg0, wg1, wg2, wv0, wv1, wv2, bg, bv):
    cg = bg + _shift_down(ug, 2) * wg0 + _shift_down(ug, 1) * wg1 + ug * wg2
    cv = bv + _shift_down(uv, 2) * wv0 + _shift_down(uv, 1) * wv1 + uv * wv2
    return jax.nn.silu(cg) * cv


def _gate_specs():
    ug = pl.BlockSpec((S, FB), lambda j: (0, j))
    uv = pl.BlockSpec((S, FB), lambda j: (0, j + N_FB))
    wg = pl.BlockSpec((8, FB), lambda j: (0, j))
    wv = pl.BlockSpec((8, FB), lambda j: (0, j + N_FB))
    bg = pl.BlockSpec((1, FB), lambda j: (0, j))
    bv = pl.BlockSpec((1, FB), lambda j: (0, j + N_FB))
    return ug, uv, wg, wv, bg, bv


def _gate_args(ug_ref, uv_ref, wg_ref, wv_ref, bg_ref, bv_ref):
    return (ug_ref[...], uv_ref[...], wg_ref[0:1, :], wg_ref[1:2, :], wg_ref[2:3, :],
            wv_ref[0:1, :], wv_ref[1:2, :], wv_ref[2:3, :], bg_ref[...], bv_ref[...])


def ffn_gate_fwd(u, cw, cb, name):
    def body(ug_ref, uv_ref, wg_ref, wv_ref, bg_ref, bv_ref, o_ref):
        o_ref[...] = _ffn_gate(*_gate_args(ug_ref, uv_ref, wg_ref, wv_ref, bg_ref, bv_ref)).astype(BF16)

    return pl.pallas_call(body, grid=(N_FB,), name=name, in_specs=list(_gate_specs()),
                          out_specs=pl.BlockSpec((S, FB), lambda j: (0, j)),
                          out_shape=SDS((S, DFF), BF16))(u, u, cw, cw, cb, cb)


def ffn_gate_bwd(u, cw, cb, da, name):
    def body(ug_ref, uv_ref, wg_ref, wv_ref, bg_ref, bv_ref, da_ref, dug_ref, duv_ref, dwg_ref, dwv_ref, dbg_ref, dbv_ref):
        _, vjp = jax.vjp(_ffn_gate, *_gate_args(ug_ref, uv_ref, wg_ref, wv_ref, bg_ref, bv_ref))
        dug, duv, g0, g1, g2, v0, v1, v2, dbg, dbv = vjp(da_ref[...])
        dug_ref[...] = dug.astype(BF16)
        duv_ref[...] = duv.astype(BF16)
        for k, (a, b) in enumerate(((g0, v0), (g1, v1), (g2, v2))):
            dwg_ref[k:k + 1, :] = a
            dwv_ref[k:k + 1, :] = b
        dwg_ref[FFN_K:8, :] = jnp.zeros((8 - FFN_K, FB), F32)
        dwv_ref[FFN_K:8, :] = jnp.zeros((8 - FFN_K, FB), F32)
        dbg_ref[...] = dbg
        dbv_ref[...] = dbv

    ug, uv, wg, wv, bg, bv = _gate_specs()
    half = pl.BlockSpec((S, FB), lambda j: (0, j))
    whalf = pl.BlockSpec((8, FB), lambda j: (0, j))
    bhalf = pl.BlockSpec((1, FB), lambda j: (0, j))
    return pl.pallas_call(body, grid=(N_FB,), name=name, in_specs=[ug, uv, wg, wv, bg, bv, half],
                          out_specs=[half, half, whalf, whalf, bhalf, bhalf],
                          out_shape=[SDS((S, DFF), BF16), SDS((S, DFF), BF16), SDS((8, DFF), F32), SDS((8, DFF), F32),
                                     SDS((1, DFF), F32), SDS((1, DFF), F32)])(u, u, cw, cw, cb, cb, da)


def _adamw(w, g, m, v):
    m = ADAM_B1 * m + (1.0 - ADAM_B1) * g
    v = ADAM_B2 * v + (1.0 - ADAM_B2) * (g * g)
    m_hat = m / (1.0 - ADAM_B1 ** ADAM_STEP)
    v_hat = v / (1.0 - ADAM_B2 ** ADAM_STEP)
    delta = -ADAM_LR * (m_hat / (jnp.sqrt(v_hat) + ADAM_EPS) + ADAM_WD * w)
    return delta, m, v


def sum_pieces(chip, r, own, layer, base, name):
    n, rows, cols = r.shape
    tr = _row_tile(rows, cols)

    def body(chip_ref, r_ref, own_ref, *rest):
        o_ref = rest[-1]
        acc = jnp.zeros((tr, cols), F32)
        for j in range(n):
            acc = acc + jnp.where(chip_ref[0] == j, own_ref[0], r_ref[j]).astype(F32)
        o_ref[...] = acc

    extra = {} if base is None else dict(input_output_aliases={3: 0})
    grid_spec = pltpu.PrefetchScalarGridSpec(
        num_scalar_prefetch=1, grid=(rows // tr,),
        in_specs=[pl.BlockSpec((n, tr, cols), lambda i, c: (0, i, 0)), pl.BlockSpec((1, tr, cols), lambda i, c: (c[0], i, 0))]
        + ([] if base is None else [ANY]),
        out_specs=pl.BlockSpec((None, tr, cols), lambda i, c: (layer, i, 0)))
    return pl.pallas_call(body, grid_spec=grid_spec, name=name, out_shape=SDS((DEPTH, rows, cols), F32), **extra)(
        *([chip, r, own] if base is None else [chip, r, own, base]))


LANE_BLOCK = 128


def sum_pieces_t(chip, rs, owns, name):
    n, rows, cols = rs[0].shape

    def body(chip_ref, *refs):
        o_ref = refs[-1]
        for l in range(DEPTH):
            r_ref, own_ref = refs[2 * l], refs[2 * l + 1]
            acc = jnp.zeros((rows, LANE_BLOCK), F32)
            for j in range(n):
                acc = acc + jnp.where(chip_ref[0] == j, own_ref[0], r_ref[j]).astype(F32)
            o_ref[:, l, :] = acc

    r_spec = pl.BlockSpec((n, rows, LANE_BLOCK), lambda i, c: (0, 0, i))
    own_spec = pl.BlockSpec((1, rows, LANE_BLOCK), lambda i, c: (c[0], 0, i))
    grid_spec = pltpu.PrefetchScalarGridSpec(
        num_scalar_prefetch=1, grid=(cols // LANE_BLOCK,), in_specs=[r_spec, own_spec] * DEPTH,
        out_specs=pl.BlockSpec((rows, DEPTH, LANE_BLOCK), lambda i, c: (0, 0, i)))
    ops = [a for pair in zip(rs, owns) for a in pair]
    return pl.pallas_call(body, grid_spec=grid_spec, name=name, out_shape=SDS((rows, DEPTH, cols), F32))(chip, *ops)


def adamw_pair_t(w, p, q, m, v, name):
    rows, depth, cols = w.shape

    def body(w_ref, p_ref, q_ref, m_ref, v_ref, g_ref, d_ref, nm_ref, nv_ref):
        g = p_ref[...] + q_ref[...]
        g_ref[...] = g
        d_ref[...], nm_ref[...], nv_ref[...] = _adamw(w_ref[...], g, m_ref[...], v_ref[...])

    spec = pl.BlockSpec((rows, depth, LANE_BLOCK), lambda i: (0, 0, i))
    return pl.pallas_call(body, grid=(cols // LANE_BLOCK,), name=name, in_specs=[spec] * 5, out_specs=[spec] * 4,
                          out_shape=[SDS(w.shape, F32)] * 4)(w, p, q, m, v)


def adamw_pair(w, p, q, m, v, name):
    rows, cols = w.shape
    tr = _row_tile(rows, cols)

    def body(w_ref, p_ref, q_ref, m_ref, v_ref, g_ref, d_ref, nm_ref, nv_ref):
        g = p_ref[...] + q_ref[...]
        g_ref[...] = g
        d_ref[...], nm_ref[...], nv_ref[...] = _adamw(w_ref[...], g, m_ref[...], v_ref[...])

    spec = pl.BlockSpec((tr, cols), lambda i: (i, 0))
    return pl.pallas_call(body, grid=(rows // tr,), name=name, in_specs=[spec] * 5, out_specs=[spec] * 4,
                          out_shape=[SDS((rows, cols), F32)] * 4)(w, p, q, m, v)


_PACK_LAYOUT = (('b_ada', 6 * D), ('g_pre_mix', D), ('g_post_mix', D), ('g_pre_ffn', D), ('g_post_ffn', D),
                ('b_fgate', 128), ('conv_b', GW), ('conv_ln_g', GW), ('conv_ln_b', GW), ('conv_pw_b', GW),
                ('swa_sinks', 128), ('sgu_ln_g', GW), ('sgu_ln_b', GW), ('sgu_b', 4 * 128), ('g_group', D),
                ('ffn_conv_b', NUP))
_PACK_WIDTH = dict(_PACK_LAYOUT)
_PACK_ROW, _LAYER_ROWS = {}, 0
for _name, _width in _PACK_LAYOUT:
    _PACK_ROW[_name] = _LAYER_ROWS
    _LAYER_ROWS += -(-_width // D)
PACK_ROWS = -(-DEPTH * _LAYER_ROWS // 8) * 8


def _segments(offset, width):
    out, s = [], 0
    while s < width:
        row, col = divmod(offset + s, D)
        n = min(width - s, D - col)
        out.append((s, row, col, n))
        s += n
    return out


def pack_small(grads, dmods, name):
    ops, plan = [], []
    for l in range(DEPTH):
        pieces = [('b_ada', j * D, a) for j, a in enumerate(dmods[l])]
        pieces += [(n, 0, grads[l][n]) for n, _ in _PACK_LAYOUT if n not in ('b_ada', 'sgu_b', 'ffn_conv_b')]
        pieces += [('ffn_conv_b', j * DFF, a) for j, a in enumerate(grads[l]['ffn_conv_b'])]
        for n, off, a in pieces:
            plan.append((len(ops), l, n, off))
            ops.append(a)
    bts = [grads[l]['sgu_bt'] for l in range(DEPTH)]
    sws = [grads[l]['sgu_w'] for l in range(DEPTH)]
    n_vec = len(ops)

    def body(*refs):
        vec, bt, sw = refs[:n_vec], refs[n_vec:n_vec + DEPTH], refs[n_vec + DEPTH:n_vec + 2 * DEPTH]
        o_ref, ow_ref, scr = refs[n_vec + 2 * DEPTH:]
        o_ref[...] = jnp.zeros_like(o_ref)
        for idx, l, n, off in plan:
            base = l * _LAYER_ROWS + _PACK_ROW[n]
            width = min(vec[idx].shape[1], _PACK_WIDTH[n] - off)
            for s, row, col, lanes in _segments(off, width):
                o_ref[base + row:base + row + 1, col:col + lanes] = vec[idx][:, s:s + lanes]
        for l in range(DEPTH):
            scr[...] = bt[l][...].T
            row = l * _LAYER_ROWS + _PACK_ROW['sgu_b']
            for g in range(4):
                o_ref[row:row + 1, 128 * g:128 * (g + 1)] = scr[g:g + 1, :]
            ow_ref[l] = sw[l][...]

    vm = pl.BlockSpec(memory_space=pltpu.VMEM)
    return pl.pallas_call(body, name=name, in_specs=[vm] * (n_vec + 2 * DEPTH), out_specs=[vm, vm],
                          out_shape=[SDS((PACK_ROWS, D), F32), SDS((DEPTH, 4, 128, 128), F32)],
                          scratch_shapes=[pltpu.VMEM((128, 128), F32)])(*ops, *bts, *sws)


def adamw_small(gall, gall_w, w, m, v, name):
    names = [n for n in SMALL]
    n_par = len(names)

    def native(ref, n, l, col, lanes):
        if n == 'sgu_b':
            return ref.at[l, pl.ds(col // 128, 1), :]
        return ref.at[pl.ds(l, 1), pl.ds(col, lanes)]

    def body(*refs):
        ga_ref, gw_ref = refs[0], refs[1]
        w_refs, m_refs, v_refs = (refs[2 + k * n_par:2 + (k + 1) * n_par] for k in range(3))
        outs = refs[2 + 3 * n_par:2 + 7 * n_par]
        scr = refs[-1]
        g = ga_ref[0]
        for j in range(1, N_DEV):
            g = g + ga_ref[j]
        scr[...] = g
        for k, n in enumerate(names):
            o_g, o_d, o_m, o_v = outs[4 * k:4 * k + 4]
            if n == 'sgu_w':
                gw = gw_ref[0]
                for j in range(1, N_DEV):
                    gw = gw + gw_ref[j]
                o_g[...] = gw
                o_d[...], o_m[...], o_v[...] = _adamw(w_refs[k][...], gw, m_refs[k][...], v_refs[k][...])
                continue
            width = w_refs[k].shape[-1] if n != 'sgu_b' else 4 * 128
            for l in range(DEPTH):
                base = l * _LAYER_ROWS + _PACK_ROW[n]
                step = 128 if n == 'sgu_b' else D
                for col in range(0, width, step):
                    lanes = min(step, width - col)
                    row, lane0 = divmod(col, D)
                    gv = scr[base + row:base + row + 1, lane0:lane0 + lanes]
                    at = functools.partial(native, n=n, l=l, col=col, lanes=lanes)
                    at(o_g)[...] = gv
                    at(o_d)[...], at(o_m)[...], at(o_v)[...] = _adamw(at(w_refs[k])[...], gv, at(m_refs[k])[...],
                                                                      at(v_refs[k])[...])

    vm = pl.BlockSpec(memory_space=pltpu.VMEM)
    params = [d[n] for d in (w, m, v) for n in names]
    outs = pl.pallas_call(body, name=name, in_specs=[vm] * (2 + 3 * n_par), out_specs=[vm] * (4 * n_par),
                          out_shape=[SDS(w[n].shape, F32) for n in names for _ in range(4)],
                          scratch_shapes=[pltpu.VMEM((PACK_ROWS, D), F32)])(gall, gall_w, *params)
    return {n: tuple(outs[4 * k:4 * k + 4]) for k, n in enumerate(names)}


ADA_COLS = 6 * D // N_CHIP
ADA_TN = 512


def ada_mod(c_all, w, b, name):
    def body(c_ref, w_ref, b_ref, o_ref):
        ca = jax.nn.silu(c_ref[...])
        o_ref[...] = jnp.dot(ca, w_ref[...], precision=lax.Precision.HIGHEST, preferred_element_type=F32) + b_ref[...]

    return pl.pallas_call(
        body, grid=(DEPTH, ADA_COLS // ADA_TN), name=name,
        in_specs=[pl.BlockSpec((N_DEV, D), lambda l, j: (0, 0)),
                  pl.BlockSpec((None, D, ADA_TN), lambda l, j: (l, 0, j)),
                  pl.BlockSpec((None, 1, ADA_TN), lambda l, j: (l, 0, j))],
        out_specs=pl.BlockSpec((None, N_DEV, ADA_TN), lambda l, j: (l, 0, j)),
        out_shape=SDS((DEPTH, N_DEV, ADA_COLS), F32))(c_all, w, b)


def ada_update(c_all_t, dmod, w, m, v, name):
    def body(c_ref, dm_ref, w_ref, m_ref, v_ref, g_ref, d_ref, nm_ref, nv_ref):
        ca = jax.nn.silu(c_ref[...])
        g = jnp.dot(ca, dm_ref[...], precision=lax.Precision.HIGHEST, preferred_element_type=F32)
        g_ref[...] = g
        d_ref[...], nm_ref[...], nv_ref[...] = _adamw(w_ref[...], g, m_ref[...], v_ref[...])

    wspec = pl.BlockSpec((None, D, ADA_TN), lambda l, j: (l, 0, j))
    return pl.pallas_call(
        body, grid=(DEPTH, ADA_COLS // ADA_TN), name=name,
        in_specs=[pl.BlockSpec((D, N_DEV), lambda l, j: (0, 0)),
                  pl.BlockSpec((None, N_DEV, ADA_TN), lambda l, j: (l, 0, j)), wspec, wspec, wspec],
        out_specs=[wspec] * 4, out_shape=[SDS((DEPTH, D, ADA_COLS), F32)] * 4)(c_all_t, dmod, w, m, v)


_CHIP_FLIPS = ((1, 0), (0, 1), (1, 1))
_DEV_FLIPS = tuple((a, b, c) for a in (0, 1) for b in (0, 1) for c in (0, 1) if (a, b, c) != (0, 0, 0))


def _position():
    return lax.axis_index("x"), lax.axis_index("y"), lax.axis_index("c")


def _hbm_call(body, arrs, out_shapes, n_remote, name):
    n = len(arrs)
    return pl.pallas_call(
        body, name=name, in_specs=[ANY] * n, out_specs=[ANY] * n, out_shape=out_shapes,
        scratch_shapes=[pltpu.SemaphoreType.DMA((n, n_remote)), pltpu.SemaphoreType.DMA((n, n_remote)),
                        pltpu.SemaphoreType.DMA((n,))])(*arrs)


_HBM = pl.BlockSpec(memory_space=pltpu.HBM)
_SEM = pl.BlockSpec(memory_space=pltpu.SEMAPHORE)
_EFFECT = pltpu.SideEffectType.DATAFLOW_SIDE_EFFECTING


GATHER, SCATTER, ALL, SWAP = "gather", "scatter", "all", "swap"
_PEERS = {GATHER: tuple((fx, fy, 0) for fx, fy in _CHIP_FLIPS), SCATTER: tuple((fx, fy, 0) for fx, fy in _CHIP_FLIPS),
          ALL: _DEV_FLIPS, SWAP: ((0, 0, 1),)}


def _peer_copies(kind, src, land, send, recv, arrivals):
    x, y, c = _position()
    out = []
    for k, (fx, fy, fc) in enumerate(_PEERS[kind]):
        tx, ty, tc = x ^ fx, y ^ fy, c ^ fc
        if kind == ALL:
            me, peer = 4 * x + 2 * y + c, 4 * tx + 2 * ty + tc
        else:
            me, peer = 2 * x + y, 2 * tx + ty
        if kind == SWAP:
            dst = land
        else:
            dst = land.at[peer if arrivals else me]
        out.append(pltpu.make_async_remote_copy(
            src_ref=src.at[peer] if kind == SCATTER else src, dst_ref=dst, send_sem=send.at[k], recv_sem=recv.at[k],
            device_id=(tx, ty, tc), device_id_type=MESH))
    return out


def _own_copy(kind, src, land, send):
    x, y, c = _position()
    me = 4 * x + 2 * y + c if kind == ALL else 2 * x + y
    return pltpu.make_async_copy(src, land.at[me], send.at[len(_PEERS[kind])])


def exchange_start(srcs, lands, kind, name):
    n = len(srcs)
    n_peers = len(_PEERS[kind])

    def body(*refs):
        src, land = refs[:n], refs[n:2 * n]
        send, recv = refs[2 * n:3 * n], refs[3 * n:4 * n]
        token = refs[-1]
        for a in range(n):
            for copy in _peer_copies(kind, src[a], land[a], send[a], recv[a], False):
                copy.start()
            if kind in (GATHER, ALL):
                _own_copy(kind, src[a], land[a], send[a]).start()
        token[...] = jnp.zeros_like(token)

    bufs = list(srcs) + list(lands)
    outs = pl.pallas_call(
        body, name=name, in_specs=[_HBM] * (2 * n),
        out_specs=[_SEM] * (2 * n) + [_HBM] * (2 * n) + [pl.BlockSpec(memory_space=pltpu.VMEM)],
        out_shape=[pltpu.SemaphoreType.DMA((n_peers + 1,))] * n + [pltpu.SemaphoreType.DMA((n_peers,))] * n
        + [pltpu.HBM(a.shape, a.dtype) for a in bufs] + [SDS((8, 128), F32)],
        input_output_aliases={i: 2 * n + i for i in range(2 * n)},
        compiler_params=pltpu.CompilerParams(has_side_effects=_EFFECT),
    )(*[pltpu.with_memory_space_constraint(a, pltpu.HBM) for a in bufs])
    flights = [(outs[a], outs[n + a], outs[2 * n + a], outs[3 * n + a]) for a in range(n)]
    return flights, outs[-1]


def exchange_wait(flights, kind, after, name):
    n = len(flights)

    def body(*refs):
        src, land = refs[:n], refs[n:2 * n]
        send, recv = refs[2 * n:3 * n], refs[3 * n:4 * n]
        for a in range(n):
            for arrival in _peer_copies(kind, src[a], land[a], send[a], recv[a], True):
                arrival.wait_send()
                arrival.wait_recv()
            if kind in (GATHER, ALL):
                _own_copy(kind, src[a], land[a], send[a]).wait()

    bufs = [f[2] for f in flights] + [f[3] for f in flights]
    sems = [f[0] for f in flights] + [f[1] for f in flights]
    outs = pl.pallas_call(
        body, name=name, in_specs=[_HBM] * (2 * n) + [_SEM] * (2 * n) + [ANY], out_specs=[_HBM] * (2 * n),
        out_shape=[pltpu.HBM(a.shape, a.dtype) for a in bufs],
        input_output_aliases={i: i for i in range(2 * n)},
        compiler_params=pltpu.CompilerParams(has_side_effects=_EFFECT),
    )(*bufs, *sems, after)
    return [(outs[a], outs[n + a]) for a in range(n)]


def chip_alltoall(arrs, name):
    n = len(arrs)

    def body(*refs):
        ins, outs = refs[:n], refs[n:2 * n]
        send, recv, loc = refs[2 * n:]
        x, y, c = _position()
        me = 2 * x + y
        copies = []
        for a in range(n):
            local = pltpu.make_async_copy(ins[a].at[me], outs[a].at[me], loc.at[a])
            local.start()
            copies.append(local)
            for k, (fx, fy) in enumerate(_CHIP_FLIPS):
                tx, ty = x ^ fx, y ^ fy
                cp = pltpu.make_async_remote_copy(
                    src_ref=ins[a].at[2 * tx + ty], dst_ref=outs[a].at[me], send_sem=send.at[a, k],
                    recv_sem=recv.at[a, k], device_id=(tx, ty, c), device_id_type=MESH)
                cp.start()
                copies.append(cp)
        for cp in copies:
            cp.wait()

    shapes = [SDS(a.shape, a.dtype) for a in arrs]
    return _hbm_call(body, arrs, shapes, 3, name)


def device_allgather(arrs, name):
    n = len(arrs)

    def body(*refs):
        ins, outs = refs[:n], refs[n:2 * n]
        send, recv, loc = refs[2 * n:]
        x, y, c = _position()
        me = 4 * x + 2 * y + c
        copies = []
        for a in range(n):
            local = pltpu.make_async_copy(ins[a], outs[a].at[me], loc.at[a])
            local.start()
            copies.append(local)
            for k, (fx, fy, fc) in enumerate(_DEV_FLIPS):
                cp = pltpu.make_async_remote_copy(
                    src_ref=ins[a], dst_ref=outs[a].at[me], send_sem=send.at[a, k], recv_sem=recv.at[a, k],
                    device_id=(x ^ fx, y ^ fy, c ^ fc), device_id_type=MESH)
                cp.start()
                copies.append(cp)
        for cp in copies:
            cp.wait()

    shapes = [SDS((N_DEV,) + a.shape, a.dtype) for a in arrs]
    return _hbm_call(body, arrs, shapes, 7, name)


def _pad_to(a, axis, size):
    pad = [(0, 0)] * a.ndim
    pad[axis] = (0, size - a.shape[axis])
    return jnp.pad(a, pad)


def _rows_to_z(wt):
    gap = Z_CONV_A - FG_END
    row = lax.broadcasted_iota(jnp.int32, (ZC, 1), 0)
    low = jnp.pad(wt, ((0, gap), (0, 0)))
    high = jnp.pad(wt, ((gap, 0), (0, 0)))
    return jnp.where(row < FG_END, low, jnp.where(row < Z_CONV_A, jnp.zeros_like(low), high))


def _rows_from_z(wt):
    row = lax.broadcasted_iota(jnp.int32, (IN_COLS, 1), 0)
    return jnp.where(row < FG_END, wt[:IN_COLS], wt[Z_CONV_A - FG_END:])


def _layer_fwd(l, x0, mod, p, fetch):
    sh1, sc1, ga1, sh2, sc2, ga2 = mod
    t = f"l{l}_"
    h1 = modnorm_fwd(x0, p['g_pre_mix'], sc1, sh1, t + "modnorm1")
    wt = dict(fetch('w_in', h1))
    z = mm_nt([h1], wt['w_in'], t + "proj_in")
    cumc, cumr = fgate_fwd(z, p['b_fgate'], t + "fgate")
    y_fox, lse = fox_fwd(z, cumc, cumr, t + "fox")
    hc = conv_dw_fwd(z, wt['conv_w'], p['conv_b'], t + "conv_dw")
    y_conv = conv_pw_fwd(hc, p['conv_ln_g'], p['conv_ln_b'], wt['conv_pw_w'], p['conv_pw_b'], t + "conv_pw")
    y_swa = swa_fwd(z, p['swa_sinks'], t + "swa")
    y_sgu = sgu_fwd(z, p['sgu_ln_g'], p['sgu_ln_b'], p['sgu_w'], p['sgu_bt'], t + "sgu")
    ys = (y_fox, y_conv, y_swa, y_sgu)
    ycat = group_norm_fwd(ys, p['g_group'], t + "group_norm")
    wt.update(fetch('w_out', ycat))
    ymix = mm_nn([ycat], wt['w_out'], F32, t + "proj_out")
    x1 = resid_fwd(x0, ymix, ga1, p['g_post_mix'], t + "resid1")
    h2 = modnorm_fwd(x1, p['g_pre_ffn'], sc2, sh2, t + "modnorm2")
    wt.update(fetch('ffn_w_up', h2))
    u = mm_nn([h2], wt['ffn_w_up'], F32, t + "ffn_up")
    act = ffn_gate_fwd(u, wt['ffn_conv_w'], p['ffn_conv_b'], t + "ffn_gate")
    wt.update(fetch('ffn_w_down', act))
    yffn = mm_nn([act], wt['ffn_w_down'], F32, t + "ffn_down")
    x2 = resid_fwd(x1, yffn, ga2, p['g_post_ffn'], t + "resid2")
    res = dict(x0=x0, h1=h1, z=z, cumc=cumc, cumr=cumr, y_fox=y_fox, lse=lse, hc=hc, ys=ys, ycat=ycat, ymix=ymix,
               x1=x1, h2=h2, u=u, act=act, yffn=yffn, wt=wt)
    return x2, res


def _layer_bwd(l, dx2, mod, p, r, emit):
    sh1, sc1, ga1, sh2, sc2, ga2 = mod
    t = f"l{l}_bwd_"
    g = {}
    wt = r['wt']
    dyffn, dga2, g['g_post_ffn'] = resid_bwd(r['yffn'], ga2, p['g_post_ffn'], dx2, t + "resid2")
    tok = emit({'ffn_w_down': mm_tn([r['act']], [dyffn], t + "ffn_down_dw")})
    dact = mm_nt([dyffn], wt['ffn_w_down'], t + "ffn_down_dx")
    dug, duv, dwg, dwv, dbg, dbv = ffn_gate_bwd(r['u'], wt['ffn_conv_w'], p['ffn_conv_b'] + tok, dact, t + "ffn_gate")
    g['ffn_conv_b'] = (dbg, dbv)
    tok = emit({'ffn_w_up': mm_tn_halves(r['h2'], dug, duv, t + "ffn_up_dw")})
    dh2 = mm_nt_halves(dug, duv, wt['ffn_w_up'], t + "ffn_up_dx")
    dx1, g['g_pre_ffn'], dsc2, dsh2 = modnorm_bwd(r['x1'], p['g_pre_ffn'] + tok, sc2, sh2, dh2, dx2, t + "modnorm2")
    dymix, dga1, g['g_post_mix'] = resid_bwd(r['ymix'], ga1, p['g_post_mix'], dx1, t + "resid1")
    tok = emit({'w_out': mm_tn([r['ycat']], [dymix], t + "proj_out_dw")})
    dcat = mm_nt([dymix], wt['w_out'], t + "proj_out_dx")
    dy_fox, dy_conv, dy_swa, dy_sgu, g['g_group'] = group_norm_bwd(r['ys'], p['g_group'] + tok, dcat, t + "group_norm")
    z = r['z']
    fq, fk, fv, dcq, dck = fox_bwd(z, r['cumc'], r['cumr'], r['lse'], r['y_fox'], dy_fox, t + "fox")
    dzf, g['b_fgate'] = fgate_bwd(z, p['b_fgate'], dcq, dck, t + "fgate")
    dhc, g['conv_ln_g'], g['conv_ln_b'], dpw, g['conv_pw_b'] = conv_pw_bwd(
        r['hc'], p['conv_ln_g'], p['conv_ln_b'], wt['conv_pw_w'], p['conv_pw_b'], dy_conv, t + "conv_pw")
    ca, cg, dcw, g['conv_b'] = conv_dw_bwd(z, wt['conv_w'], dhc, t + "conv_dw")
    sq, sk, sv, g['swa_sinks'] = swa_bwd(z, p['swa_sinks'], dy_swa, t + "swa")
    gu, gv, g['sgu_ln_g'], g['sgu_ln_b'], g['sgu_w'], g['sgu_bt'] = sgu_bwd(
        z, p['sgu_ln_g'], p['sgu_ln_b'], p['sgu_w'], p['sgu_bt'], dy_sgu, t + "sgu")
    dz = [fq, fk, fv, dzf, ca, cg, sq, sk, sv, gu, gv]
    tok = emit({'w_in': mm_tn(dz, [r['h1']], t + "proj_in_dw"), 'conv_w': dcw, 'conv_pw_w': dpw,
                'ffn_conv_w': jnp.concatenate([dwg, dwv], axis=1)})
    dh1 = mm_nn(dz, wt['w_in'], F32, t + "proj_in_dx")
    dx0, g['g_pre_mix'], dsc1, dsh1 = modnorm_bwd(r['x0'], p['g_pre_mix'] + tok, sc1, sh1, dh1, dx1, t + "modnorm1")
    return dx0, g, (dsh1, dsc1, dga1, dsh2, dsc2, dga2)


def _layer_params(l, w):
    def row(name, width=None):
        v = w[name][l].reshape(1, -1)
        return v if width is None else _pad_to(v, 1, width)
    return {
        'g_pre_mix': row('g_pre_mix'), 'g_post_mix': row('g_post_mix'), 'g_pre_ffn': row('g_pre_ffn'),
        'g_post_ffn': row('g_post_ffn'), 'b_fgate': row('b_fgate', 128), 'conv_b': row('conv_b'),
        'conv_ln_g': row('conv_ln_g'), 'conv_ln_b': row('conv_ln_b'), 'conv_pw_b': row('conv_pw_b'),
        'swa_sinks': row('swa_sinks', 128), 'sgu_ln_g': row('sgu_ln_g'), 'sgu_ln_b': row('sgu_ln_b'),
        'sgu_w': w['sgu_w'][l], 'sgu_bt': _pad_to(w['sgu_b'][l].T, 1, 128), 'g_group': row('g_group'),
        'ffn_conv_b': row('ffn_conv_b'),
    }


def _w_in_from_shards(s):
    return _rows_to_z(s.reshape(IN_COLS, D))


def _w_in_to_shards(g):
    return _rows_from_z(g).reshape(N_CHIP, IN_COLS // N_CHIP, D)


def _cols_from_shards(s, rows):
    _, r, n = s.shape
    return _pad_to(jnp.transpose(s, (1, 0, 2)).reshape(r, N_CHIP * n), 0, rows)


def _cols_to_shards(g, r):
    n = g.shape[1] // N_CHIP
    return jnp.transpose(g[:r].reshape(r, N_CHIP, n), (1, 0, 2))


_FROM_SHARDS = {
    'w_in': _w_in_from_shards,
    'w_out': lambda s: s.reshape(D, D),
    'ffn_w_up': lambda s: s,
    'ffn_w_down': lambda s: s.reshape(DFF, D),
    'conv_w': lambda s: _cols_from_shards(s, 32),
    'conv_pw_w': lambda s: s.reshape(GW, GW),
    'ffn_conv_w': lambda s: _cols_from_shards(s, 8),
}
_TO_SHARDS = {
    'w_in': _w_in_to_shards,
    'w_out': lambda g: g.reshape(N_CHIP, D // N_CHIP, D),
    'ffn_w_up': lambda g: g,
    'ffn_w_down': lambda g: g.reshape(N_CHIP, DFF // N_CHIP, D),
    'conv_w': lambda g: _cols_to_shards(g, CONV_K),
    'conv_pw_w': lambda g: g.reshape(N_CHIP, GW // N_CHIP, GW),
    'ffn_conv_w': lambda g: _cols_to_shards(g, FFN_K),
}


def _local_step(xs, target, mod, w, fetch, emit):
    params, resids, mods = [], [], []
    for l in range(DEPTH):
        params.append(_layer_params(l, w))
        mods.append(tuple(mod[l, j] for j in range(6)))
        xs, r = _layer_fwd(l, xs, mods[l], params[l], functools.partial(fetch, l))
        resids.append(r)
    dx, loss_row = loss_grad(xs, target, "loss")
    grads, dmods = [None] * DEPTH, [None] * DEPTH
    for l in reversed(range(DEPTH)):
        dx, grads[l], dmods[l] = _layer_bwd(l, dx, mods[l], params[l], resids[l], functools.partial(emit, l))
    return loss_row, dx, grads, dmods


_MATMUL_WEIGHTS = ('w_in', 'w_out', 'ffn_w_up', 'ffn_w_down')
_CONV_WEIGHTS = ('conv_w', 'conv_pw_w', 'ffn_conv_w')
_FETCH_GROUPS = {'w_in': ('w_in',) + _CONV_WEIGHTS, 'w_out': ('w_out',), 'ffn_w_up': ('ffn_w_up',),
                 'ffn_w_down': ('ffn_w_down',)}


def kernel(x, c, w_ada, b_ada, g_pre_mix, g_post_mix, g_pre_ffn, g_post_ffn, w_in, b_fgate, conv_w, conv_b, conv_ln_g, conv_ln_b, conv_pw_w, conv_pw_b, swa_sinks, sgu_ln_g, sgu_ln_b, sgu_w, sgu_b, g_group, w_out, ffn_w_up, ffn_conv_w, ffn_conv_b, ffn_w_down, loss_target, m_w_ada, m_b_ada, m_g_pre_mix, m_g_post_mix, m_g_pre_ffn, m_g_post_ffn, m_w_in, m_b_fgate, m_conv_w, m_conv_b, m_conv_ln_g, m_conv_ln_b, m_conv_pw_w, m_conv_pw_b, m_swa_sinks, m_sgu_ln_g, m_sgu_ln_b, m_sgu_w, m_sgu_b, m_g_group, m_w_out, m_ffn_w_up, m_ffn_conv_w, m_ffn_conv_b, m_ffn_w_down, v_w_ada, v_b_ada, v_g_pre_mix, v_g_post_mix, v_g_pre_ffn, v_g_post_ffn, v_w_in, v_b_fgate, v_conv_w, v_conv_b, v_conv_ln_g, v_conv_ln_b, v_conv_pw_w, v_conv_pw_b, v_swa_sinks, v_sgu_ln_g, v_sgu_ln_b, v_sgu_w, v_sgu_b, v_g_group, v_w_out, v_ffn_w_up, v_ffn_conv_w, v_ffn_conv_b, v_ffn_w_down):
    args = locals()
    w = {n: args[n] for n in WEIGHTS}
    m = {n: args['m_' + n] for n in WEIGHTS}
    v = {n: args['v_' + n] for n in WEIGHTS}
    xi, yi, ci = _position()
    chip = 2 * xi + yi

    (c_all,) = device_allgather([c], "gather_c")
    c_all = c_all.reshape(N_DEV, D)
    b_loc = lax.dynamic_slice_in_dim(b_ada, chip * ADA_COLS, ADA_COLS, axis=1).reshape(DEPTH, 1, ADA_COLS)
    mod_all = ada_mod(c_all, w_ada, b_loc, "ada_mod")
    mine = lax.dynamic_index_in_dim(mod_all.reshape(DEPTH, N_CHIP, 2, ADA_COLS), ci, axis=2, keepdims=False)
    (mod4,) = chip_alltoall([jnp.transpose(mine, (1, 0, 2))], "scatter_mod")

    w_in_t, m_in_t, v_in_t = (jnp.transpose(a, (2, 0, 1)) for a in (w_in, m_w_in, v_w_in))

    def shard(n, l):
        a = w_in_t[:, l] if n == 'w_in' else w[n][l]
        return a.astype(BF16) if n in _MATMUL_WEIGHTS else a

    keys = [(n, l) for l in range(DEPTH) for n in _CONV_WEIGHTS]
    keys += [(n, l) for l in range(DEPTH) for n in _MATMUL_WEIGHTS]
    srcs = [shard(n, l) for n, l in keys]
    mod4, srcs = lax.optimization_barrier((mod4, srcs))
    lands = [lax.empty((N_CHIP,) + s.shape, s.dtype) for s in srcs]
    flights, token = exchange_start(srcs, lands, GATHER, "gather_start")
    gathering = dict(zip(keys, flights))
    mod = jnp.transpose(mod4, (1, 0, 2)).reshape(DEPTH, 6, 1, D) + token[0:1, 0:1]

    def fetch(l, name, after):
        names = _FETCH_GROUPS[name]
        landed = exchange_wait([gathering[(n, l)] for n in names], GATHER, after, f"gather_wait_l{l}_{name}")
        return {n: _FROM_SHARDS[n](land) for n, (_, land) in zip(names, landed)}

    scattering = {}

    def emit(l, grads):
        names = list(grads)
        pieces = [_TO_SHARDS[n](grads[n]) for n in names]
        lands = [lax.empty(p.shape, p.dtype) for p in pieces]
        started, token = exchange_start(pieces, lands, SCATTER, f"scatter_start_l{l}_{names[0]}")
        scattering.update({(n, l): f for n, f in zip(names, started)})
        return token[0:1, 0:1]

    loss_row, dx, grads, dmods = _local_step(x.reshape(S, D), loss_target.reshape(S, D), mod, w, fetch, emit)
    loss = lax.psum(loss_row[0, 0], ("x", "y", "c"))
    grad_x = dx.reshape(1, S, D)

    small_srcs = pack_small(grads, dmods, "pack_small")
    small_flights, small_token = exchange_start(small_srcs, [lax.empty((N_DEV,) + a.shape, F32) for a in small_srcs], ALL,
                                                "gather_small_start")

    order = list(scattering)
    landed = dict(zip(order, exchange_wait([scattering[k] for k in order], SCATTER, small_token, "scatter_wait")))
    chip1 = chip.astype(jnp.int32).reshape(1)
    part = {'w_in': sum_pieces_t(chip1, [landed[('w_in', l)][1] for l in range(DEPTH)],
                                 [landed[('w_in', l)][0] for l in range(DEPTH)], "sum_w_in")}
    for n in SHARDED[1:]:
        cols = w[n].shape[-1]
        for l in range(DEPTH):
            src, land = landed[(n, l)]
            part[n] = sum_pieces(chip1, land.reshape(N_CHIP, -1, cols), src.reshape(N_CHIP, -1, cols), l, part.get(n),
                                 f"sum_{n}_l{l}")
        part[n] = part[n].reshape(-1, cols)
    parts = [part[n] for n in SHARDED]
    swap_flights, swap_token = exchange_start(parts, [lax.empty(p.shape, F32) for p in parts], SWAP, "swap_start")

    g_all, gw_all = (land for _, land in exchange_wait(small_flights, ALL, swap_token, "gather_small_wait"))
    out = adamw_small(g_all, gw_all, w, m, v, "adamw_small")

    dmod_all = g_all[:, :DEPTH * _LAYER_ROWS].reshape(N_DEV, DEPTH, _LAYER_ROWS * D)[:, :, :6 * D]
    dmod_loc = jnp.transpose(lax.dynamic_slice_in_dim(dmod_all, chip * ADA_COLS, ADA_COLS, axis=2), (1, 0, 2))
    out['w_ada'] = tuple(ada_update(c_all.T, dmod_loc, w_ada, m['w_ada'], v['w_ada'], "adamw_ada"))

    swapped = exchange_wait(swap_flights, SWAP, out['w_ada'][0], "swap_wait")
    part = {n: own for n, (own, _) in zip(SHARDED, swapped)}
    other = {n: got for n, (_, got) in zip(SHARDED, swapped)}
    res = adamw_pair_t(w_in_t, part['w_in'], other['w_in'], m_in_t, v_in_t, "adamw_w_in")
    out['w_in'] = tuple(jnp.transpose(a, (1, 2, 0)) for a in res)
    for n in SHARDED[1:]:
        cols = w[n].shape[-1]
        res = adamw_pair(w[n].reshape(-1, cols), part[n], other[n], m[n].reshape(-1, cols), v[n].reshape(-1, cols),
                         "adamw_" + n)
        out[n] = tuple(a.reshape(w[n].shape) for a in res)

    return (loss, grad_x, *[out[n][0] for n in WEIGHTS], *[out[n][1] for n in WEIGHTS],
            *[out[n][2] for n in WEIGHTS], *[out[n][3] for n in WEIGHTS])
```

```python
import functools

import jax
import jax.numpy as jnp
from jax import lax
from jax.experimental import pallas as pl
from jax.experimental.pallas import tpu as pltpu

F32 = jnp.float32
BF16 = jnp.bfloat16
SDS = jax.ShapeDtypeStruct
MESH = pl.DeviceIdType.MESH
ANY = pl.BlockSpec(memory_space=pl.ANY)

DEPTH = 2
S = 2048
D = 1024
GW = 256
DFF = 2816
NUP = 2 * DFF
IN_COLS = 2308
ZC = 2432
CONV_K = 31
FFN_K = 3
EPS = 1e-6
SCALE = 0.125
NEG = -1e30
N_CHIP = 4
N_DEV = 8

Z_FOX_Q, Z_FOX_K, Z_FOX_V = 0, 256, 512
Z_FG = 768
Z_CONV_A, Z_CONV_G = 896, 1152
Z_SWA_Q, Z_SWA_K, Z_SWA_V = 1408, 1664, 1792
Z_SGU_U, Z_SGU_V = 1920, 2176
FG_END = 772

ADAM_LR, ADAM_B1, ADAM_B2, ADAM_EPS, ADAM_WD, ADAM_STEP = 0.001, 0.9, 0.999, 1e-08, 0.01, 10

TS = 256
TM = 1024
N_SPLIT = 2816
N_BLOCK = 1408

WEIGHTS = ['w_ada', 'b_ada', 'g_pre_mix', 'g_post_mix', 'g_pre_ffn', 'g_post_ffn', 'w_in', 'b_fgate', 'conv_w',
           'conv_b', 'conv_ln_g', 'conv_ln_b', 'conv_pw_w', 'conv_pw_b', 'swa_sinks', 'sgu_ln_g', 'sgu_ln_b',
           'sgu_w', 'sgu_b', 'g_group', 'w_out', 'ffn_w_up', 'ffn_conv_w', 'ffn_conv_b', 'ffn_w_down']
SHARDED = ['w_in', 'conv_w', 'conv_pw_w', 'w_out', 'ffn_w_up', 'ffn_conv_w', 'ffn_w_down']
SMALL = [n for n in WEIGHTS if n not in SHARDED and n != 'w_ada']


def _rms(x, g):
    return x * lax.rsqrt(jnp.mean(x * x, axis=-1, keepdims=True) + EPS) * g


def _modnorm(x, g, sc, sh):
    return _rms(x, g) * (1.0 + sc) + sh


def _resid(x, y, ga, g):
    return x + ga * _rms(y, g)


@functools.partial(jax.custom_vjp, nondiff_argnums=(1,))
def _shift_down(x, n):
    if n == 0:
        return x
    row = lax.broadcasted_iota(jnp.int32, x.shape, 0)
    return jnp.where(row >= n, pltpu.roll(x, n, axis=0), 0.0)


def _shift_up(x, n):
    if n == 0:
        return x
    rows = x.shape[0]
    row = lax.broadcasted_iota(jnp.int32, x.shape, 0)
    return jnp.where(row < rows - n, pltpu.roll(x, rows - n, axis=0), 0.0)


def _shift_down_fwd(x, n):
    return _shift_down(x, n), None


def _shift_down_bwd(n, _, ct):
    return (_shift_up(ct, n),)


_shift_down.defvjp(_shift_down_fwd, _shift_down_bwd)


def _nt(a, b):
    return lax.dot_general(a, b, (((1,), (1,)), ((), ())), preferred_element_type=F32)


def _tn(a, b):
    return lax.dot_general(a, b, (((0,), (0,)), ((), ())), preferred_element_type=F32)


def _nn(a, b):
    return jnp.dot(a, b, preferred_element_type=F32)


def _acc(ref, val, first):
    @pl.when(first)
    def _():
        ref[...] = val

    @pl.when(jnp.logical_not(first))
    def _():
        ref[...] += val


def _row_tile(rows, cols):
    limit = max(8, (1 << 21) // (4 * cols))
    best = None
    for t in range(8, rows + 1, 8):
        if rows % t == 0 and t <= limit:
            best = t
    return best if best is not None else rows


def _ncol(n):
    return n if n <= N_SPLIT else N_BLOCK


def _weight_spec(b, order):
    pick = (lambda j, i: j) if order == 0 else (lambda i, j: j)
    if b.ndim == 3:
        _, k, tn = b.shape
        return pl.BlockSpec((None, k, tn), lambda *g: (pick(*g), 0, 0)), k, N_CHIP * tn, tn
    k, n = b.shape
    tn = _ncol(n)
    return pl.BlockSpec((k, tn), lambda *g: (0, pick(*g))), k, n, tn


def _side_by_side(refs):
    return refs[0][...] if len(refs) == 1 else jnp.concatenate([r[...] for r in refs], axis=1)


def mm_nn(pieces, b, out_dtype, name, tm=TM):
    m = pieces[0].shape[0]
    b_spec, k, n, tn = _weight_spec(b, 0)

    def body(*refs):
        a_refs, b_ref, o_ref = refs[:-2], refs[-2], refs[-1]
        o_ref[...] = _nn(_side_by_side(a_refs), b_ref[...]).astype(out_dtype)

    return pl.pallas_call(
        body, grid=(n // tn, m // tm), name=name,
        in_specs=[pl.BlockSpec((tm, p.shape[1]), lambda j, i: (i, 0)) for p in pieces] + [b_spec],
        out_specs=pl.BlockSpec((tm, tn), lambda j, i: (i, j)),
        out_shape=SDS((m, n), out_dtype),
    )(*pieces, b)


def mm_nt(pieces, b, name):
    m = pieces[0].shape[0]
    k, n = b.shape
    assert n == sum(p.shape[1] for p in pieces) <= N_SPLIT

    def body(*refs):
        a_refs, b_ref, o_ref = refs[:-2], refs[-2], refs[-1]
        o_ref[...] = _nt(_side_by_side(a_refs), b_ref[...])

    return pl.pallas_call(
        body, grid=(m // TM,), name=name,
        in_specs=[pl.BlockSpec((TM, p.shape[1]), lambda i: (i, 0)) for p in pieces] + [pl.BlockSpec((k, n), lambda i: (0, 0))],
        out_specs=pl.BlockSpec((TM, k), lambda i: (i, 0)),
        out_shape=SDS((m, k), F32),
    )(*pieces, b)


def mm_nt_halves(a0, a1, b4, name):
    m = a0.shape[0]
    _, k, tc = b4.shape

    def body(a0_ref, a1_ref, b_ref, o_ref):
        c = pl.program_id(1)

        @pl.when(c == 0)
        def _():
            o_ref[...] = _nt(a0_ref[...], b_ref[...])

        @pl.when(c == 1)
        def _():
            o_ref[...] += _nt(a0_ref[...], b_ref[...])

        @pl.when(c >= 2)
        def _():
            o_ref[...] += _nt(a1_ref[...], b_ref[...])

    return pl.pallas_call(
        body, grid=(m // TM, N_CHIP), name=name,
        in_specs=[pl.BlockSpec((TM, tc), lambda i, c: (i, jnp.minimum(c, 1))),
                  pl.BlockSpec((TM, tc), lambda i, c: (i, jnp.maximum(c - 2, 0))),
                  pl.BlockSpec((None, k, tc), lambda i, c: (c, 0, 0))],
        out_specs=pl.BlockSpec((TM, k), lambda i, c: (i, 0)),
        out_shape=SDS((m, k), F32),
    )(a0, a1, b4)


def mm_tn(a_pieces, b_pieces, name):
    m = a_pieces[0].shape[0]
    k = sum(p.shape[1] for p in a_pieces)
    n = sum(p.shape[1] for p in b_pieces)
    assert n <= N_SPLIT
    n_a = len(a_pieces)

    def body(*refs):
        a_refs, b_refs, o_ref = refs[:n_a], refs[n_a:-1], refs[-1]
        o_ref[...] = _tn(_side_by_side(a_refs), _side_by_side(b_refs)).astype(BF16)

    return pl.pallas_call(
        body, grid=(1,), name=name,
        in_specs=[pl.BlockSpec((m, p.shape[1]), lambda i: (0, 0)) for p in list(a_pieces) + list(b_pieces)],
        out_specs=pl.BlockSpec((k, n), lambda i: (0, 0)), out_shape=SDS((k, n), BF16),
    )(*a_pieces, *b_pieces)


def mm_tn_halves(a, b0, b1, name):
    m, k = a.shape
    tn = b0.shape[1] // 2

    def body(a_ref, b0_ref, b1_ref, o_ref):
        j = pl.program_id(0)

        @pl.when(j < 2)
        def _():
            o_ref[...] = _tn(a_ref[...], b0_ref[...]).astype(BF16)

        @pl.when(j >= 2)
        def _():
            o_ref[...] = _tn(a_ref[...], b1_ref[...]).astype(BF16)

    return pl.pallas_call(
        body, grid=(N_CHIP,), name=name,
        in_specs=[pl.BlockSpec((m, k), lambda j: (0, 0)),
                  pl.BlockSpec((m, tn), lambda j: (0, jnp.minimum(j, 1))),
                  pl.BlockSpec((m, tn), lambda j: (0, jnp.maximum(j - 2, 0)))],
        out_specs=pl.BlockSpec((None, k, tn), lambda j: (j, 0, 0)), out_shape=SDS((N_CHIP, k, tn), BF16),
    )(a, b0, b1)


_ROW = pl.BlockSpec((TS, D), lambda i: (i, 0))
_VEC = pl.BlockSpec((1, D), lambda i: (0, 0))


def modnorm_fwd(x, g, sc, sh, name):
    def body(x_ref, g_ref, sc_ref, sh_ref, o_ref):
        o_ref[...] = _modnorm(x_ref[...], g_ref[...], sc_ref[...], sh_ref[...]).astype(BF16)

    return pl.pallas_call(body, grid=(S // TS,), name=name, in_specs=[_ROW, _VEC, _VEC, _VEC], out_specs=_ROW,
                          out_shape=SDS((S, D), BF16))(x, g, sc, sh)


def modnorm_bwd(x, g, sc, sh, dh, dx_in, name):
    def body(x_ref, g_ref, sc_ref, sh_ref, dh_ref, dxin_ref, dx_ref, dg_ref, dsc_ref, dsh_ref):
        _, vjp = jax.vjp(_modnorm, x_ref[...], g_ref[...], sc_ref[...], sh_ref[...])
        dx, dg, dsc, dsh = vjp(dh_ref[...])
        dx_ref[...] = dxin_ref[...] + dx
        first = pl.program_id(0) == 0
        _acc(dg_ref, dg, first)
        _acc(dsc_ref, dsc, first)
        _acc(dsh_ref, dsh, first)

    vec = SDS((1, D), F32)
    return pl.pallas_call(body, grid=(S // TS,), name=name, in_specs=[_ROW, _VEC, _VEC, _VEC, _ROW, _ROW],
                          out_specs=[_ROW, _VEC, _VEC, _VEC], out_shape=[SDS((S, D), F32), vec, vec, vec])(
                              x, g, sc, sh, dh, dx_in)


def resid_fwd(x, y, ga, g, name):
    def body(x_ref, y_ref, ga_ref, g_ref, o_ref):
        o_ref[...] = _resid(x_ref[...], y_ref[...], ga_ref[...], g_ref[...])

    return pl.pallas_call(body, grid=(S // TS,), name=name, in_specs=[_ROW, _ROW, _VEC, _VEC], out_specs=_ROW,
                          out_shape=SDS((S, D), F32))(x, y, ga, g)


def resid_bwd(y, ga, g, dxo, name):
    def body(y_ref, ga_ref, g_ref, dxo_ref, dy_ref, dga_ref, dg_ref):
        _, vjp = jax.vjp(_gated_norm, y_ref[...], ga_ref[...], g_ref[...])
        dy, dga, dg = vjp(dxo_ref[...])
        dy_ref[...] = dy.astype(BF16)
        first = pl.program_id(0) == 0
        _acc(dga_ref, dga, first)
        _acc(dg_ref, dg, first)

    vec = SDS((1, D), F32)
    return pl.pallas_call(body, grid=(S // TS,), name=name, in_specs=[_ROW, _VEC, _VEC, _ROW],
                          out_specs=[_ROW, _VEC, _VEC], out_shape=[SDS((S, D), BF16), vec, vec])(y, ga, g, dxo)


def _gated_norm(y, ga, g):
    return ga * _rms(y, g)


def resid_modnorm_fwd(x, y, ga, g_post, g_pre, sc, sh, name):
    def body(x_ref, y_ref, ga_ref, gp_ref, g_ref, sc_ref, sh_ref, xo_ref, h_ref):
        xn = _resid(x_ref[...], y_ref[...], ga_ref[...], gp_ref[...])
        xo_ref[...] = xn
        h_ref[...] = _modnorm(xn, g_ref[...], sc_ref[...], sh_ref[...]).astype(BF16)

    return pl.pallas_call(body, grid=(S // TS,), name=name, in_specs=[_ROW, _ROW] + [_VEC] * 5, out_specs=[_ROW, _ROW],
                          out_shape=[SDS((S, D), F32), SDS((S, D), BF16)])(x, y, ga, g_post, g_pre, sc, sh)


def modnorm_resid_bwd(x, g, sc, sh, dh, dx_in, y, ga, g_post, name):
    def body(x_ref, g_ref, sc_ref, sh_ref, dh_ref, dxin_ref, y_ref, ga_ref, gp_ref,
             dx_ref, dg_ref, dsc_ref, dsh_ref, dy_ref, dga_ref, dgp_ref):
        _, vjp = jax.vjp(_modnorm, x_ref[...], g_ref[...], sc_ref[...], sh_ref[...])
        dx, dg, dsc, dsh = vjp(dh_ref[...])
        dx = dxin_ref[...] + dx
        dx_ref[...] = dx
        _, vjp = jax.vjp(_gated_norm, y_ref[...], ga_ref[...], gp_ref[...])
        dy, dga, dgp = vjp(dx)
        dy_ref[...] = dy.astype(BF16)
        first = pl.program_id(0) == 0
        for ref, val in ((dg_ref, dg), (dsc_ref, dsc), (dsh_ref, dsh), (dga_ref, dga), (dgp_ref, dgp)):
            _acc(ref, val, first)

    vec = SDS((1, D), F32)
    return pl.pallas_call(body, grid=(S // TS,), name=name,
                          in_specs=[_ROW, _VEC, _VEC, _VEC, _ROW, _ROW, _ROW, _VEC, _VEC],
                          out_specs=[_ROW, _VEC, _VEC, _VEC, _ROW, _VEC, _VEC],
                          out_shape=[SDS((S, D), F32), vec, vec, vec, SDS((S, D), BF16), vec, vec])(
                              x, g, sc, sh, dh, dx_in, y, ga, g_post)


def loss_grad(xf, target, name):
    def body(x_ref, t_ref, dx_ref, l_ref):
        err = x_ref[...] - t_ref[...]
        dx_ref[...] = err * (1.0 / D)
        part = 0.5 * jnp.sum(jnp.mean(err * err, axis=-1, keepdims=True), axis=0, keepdims=True)
        _acc(l_ref, jnp.broadcast_to(part, (1, 128)), pl.program_id(0) == 0)

    return pl.pallas_call(body, grid=(S // TS,), name=name, in_specs=[_ROW, _ROW],
                          out_specs=[_ROW, pl.BlockSpec((1, 128), lambda i: (0, 0))],
                          out_shape=[SDS((S, D), F32), SDS((1, 128), F32)])(xf, target)


_FG_SPEC = pl.BlockSpec((S, 128), lambda i: (0, Z_FG // 128))


def _tri128(lower):
    r = lax.broadcasted_iota(jnp.int32, (128, 128), 0)
    c = lax.broadcasted_iota(jnp.int32, (128, 128), 1)
    return ((r >= c) if lower else (r <= c)).astype(F32)


def fgate_fwd(z, bf, name):
    def body(z_ref, b_ref, cc_ref, cr_ref):
        tri = _tri128(True)
        carry = jnp.zeros((1, 128), F32)
        for i in range(S // 128):
            rows = pl.ds(i * 128, 128)
            lf = jax.nn.log_sigmoid(z_ref[rows, :] + b_ref[...])
            c = jnp.dot(tri, lf, precision=lax.Precision.HIGHEST, preferred_element_type=F32) + carry
            cc_ref[rows, :] = c
            carry = c[127:128, :]
        cr_ref[...] = cc_ref[...].T

    return pl.pallas_call(body, name=name, grid=(1,),
                          in_specs=[_FG_SPEC, pl.BlockSpec((1, 128), lambda i: (0, 0))],
                          out_specs=[pl.BlockSpec((S, 128), lambda i: (0, 0)), pl.BlockSpec((128, S), lambda i: (0, 0))],
                          out_shape=[SDS((S, 128), F32), SDS((128, S), F32)])(z, bf)


def fgate_bwd(z, bf, dcq, dck, name):
    def body(z_ref, b_ref, dcq_ref, dck_ref, dz_ref, db_ref, col_ref):
        col_ref[...] = dcq_ref[...] + jnp.concatenate([dck_ref[...], jnp.zeros((120, S), F32)], axis=0).T
        tri = _tri128(False)
        carry = jnp.zeros((1, 128), F32)
        db = jnp.zeros((1, 128), F32)
        for i in reversed(range(S // 128)):
            rows = pl.ds(i * 128, 128)
            dlf = jnp.dot(tri, col_ref[rows, :], precision=lax.Precision.HIGHEST, preferred_element_type=F32) + carry
            carry = dlf[0:1, :]
            dz = dlf * jax.nn.sigmoid(-(z_ref[rows, :] + b_ref[...]))
            dz_ref[rows, :] = dz.astype(BF16)
            db = db + jnp.sum(dz, axis=0, keepdims=True)
        db_ref[...] = db

    return pl.pallas_call(body, name=name, grid=(1,),
                          in_specs=[_FG_SPEC, pl.BlockSpec((1, 128), lambda i: (0, 0)),
                                    pl.BlockSpec((S, 128), lambda i: (0, 0)), pl.BlockSpec((8, S), lambda i: (0, 0))],
                          out_specs=[pl.BlockSpec((S, 128), lambda i: (0, 0)), pl.BlockSpec((1, 128), lambda i: (0, 0))],
                          out_shape=[SDS((S, 128), BF16), SDS((1, 128), F32)],
                          scratch_shapes=[pltpu.VMEM((S, 128), F32)])(z, bf, dcq, dck)


TQ = 256


def _head_mask(hh):
    lane = lax.broadcasted_iota(jnp.int32, (TQ, 128), 1)
    return (lane >= 64 * hh) & (lane < 64 * hh + 64)


def _fox_specs():
    q = pl.BlockSpec((TQ, 256), lambda i: (i, Z_FOX_Q // 256))
    k = pl.BlockSpec((S, 256), lambda i: (0, Z_FOX_K // 256))
    v = pl.BlockSpec((S, 256), lambda i: (0, Z_FOX_V // 256))
    cc = pl.BlockSpec((TQ, 128), lambda i: (i, 0))
    cr = pl.BlockSpec((8, S), lambda i: (0, 0))
    return q, k, v, cc, cr


FOX_SPAN = 2
FOX_GROUPS = S // (FOX_SPAN * TQ)


def _fox_scores(qm, k, cc_h, cr_h, i):
    klen = k.shape[0]
    s = _nt(qm, k) + cc_h - cr_h
    qpos = i * TQ + lax.broadcasted_iota(jnp.int32, (TQ, klen), 0)
    kpos = lax.broadcasted_iota(jnp.int32, (TQ, klen), 1)
    return jnp.where(kpos <= qpos, s, NEG)


def _for_key_length(i, fn):
    for g in range(FOX_GROUPS):
        pl.when(i // FOX_SPAN == g)(functools.partial(fn, (g + 1) * FOX_SPAN * TQ))


def fox_fwd(z, cumc, cumr, name):
    def body(q_ref, k_ref, v_ref, cc_ref, cr_ref, o_ref, l_ref):
        i = pl.program_id(0)

        def block(klen):
            lane = lax.broadcasted_iota(jnp.int32, (TQ, 128), 1)
            cc = cc_ref[...]
            lse = jnp.zeros((TQ, 128), F32)
            for p in range(2):
                cols = pl.ds(128 * p, 128)
                q = q_ref[:, cols]
                k = k_ref[0:klen, cols].astype(BF16)
                v = v_ref[0:klen, cols].astype(BF16)
                o_pair = jnp.zeros((TQ, 128), F32)
                for hh in range(2):
                    h = 2 * p + hh
                    hm = _head_mask(hh)
                    qm = jnp.where(hm, q * SCALE, 0.0).astype(BF16)
                    s = _fox_scores(qm, k, cc[:, h:h + 1], cr_ref[h:h + 1, 0:klen], i)
                    m = jnp.max(s, axis=1, keepdims=True)
                    e = jnp.exp(s - m)
                    l = jnp.sum(e, axis=1, keepdims=True)
                    o_pair = jnp.where(hm, _nn(e.astype(BF16), v) / l, o_pair)
                    lse = jnp.where(lane == h, m + jnp.log(l), lse)
                o_ref[:, cols] = o_pair
            l_ref[...] = lse

        _for_key_length(i, block)

    q, k, v, cc, cr = _fox_specs()
    return pl.pallas_call(body, grid=(S // TQ,), name=name, in_specs=[q, k, v, cc, cr],
                          out_specs=[pl.BlockSpec((TQ, 256), lambda i: (i, 0)), cc],
                          out_shape=[SDS((S, 256), F32), SDS((S, 128), F32)])(z, z, z, cumc, cumr)


def fox_bwd(z, cumc, cumr, lse, o, do, name):
    steps = S // TQ

    def body(q_ref, k_ref, v_ref, cc_ref, cr_ref, l_ref, o_ref, do_ref, dq_ref, dk_ref, dv_ref, dcq_ref, dck_ref,
             dk_acc, dv_acc):
        i = pl.program_id(0)

        @pl.when(i == 0)
        def _():
            dk_acc[...] = jnp.zeros_like(dk_acc)
            dv_acc[...] = jnp.zeros_like(dv_acc)
            dck_ref[...] = jnp.zeros_like(dck_ref)

        def block(klen):
            lane = lax.broadcasted_iota(jnp.int32, (TQ, 128), 1)
            cc = cc_ref[...]
            lse_all = l_ref[...]
            dcq = jnp.zeros((TQ, 128), F32)
            for p in range(2):
                cols = pl.ds(128 * p, 128)
                q = q_ref[:, cols]
                k = k_ref[0:klen, cols].astype(BF16)
                v = v_ref[0:klen, cols].astype(BF16)
                o_p = o_ref[:, cols]
                do_p = do_ref[:, cols]
                dq_pair = jnp.zeros((TQ, 128), F32)
                dk_pair = jnp.zeros((klen, 128), F32)
                dv_pair = jnp.zeros((klen, 128), F32)
                for hh in range(2):
                    h = 2 * p + hh
                    hm = _head_mask(hh)
                    qm = jnp.where(hm, q * SCALE, 0.0).astype(BF16)
                    s = _fox_scores(qm, k, cc[:, h:h + 1], cr_ref[h:h + 1, 0:klen], i)
                    pn = jnp.exp(s - lse_all[:, h:h + 1])
                    dom = jnp.where(hm, do_p, 0.0)
                    dl = jnp.sum(dom * o_p, axis=1, keepdims=True)
                    dom = dom.astype(BF16)
                    ds = pn * (_nt(dom, v) - dl)
                    dsb = ds.astype(BF16)
                    dq_pair = jnp.where(hm, _nn(dsb, k) * SCALE, dq_pair)
                    dk_pair = dk_pair + _tn(dsb, qm)
                    dv_pair = dv_pair + _tn(pn.astype(BF16), dom)
                    dck_ref[h:h + 1, 0:klen] -= jnp.sum(ds, axis=0, keepdims=True)
                    dcq = jnp.where(lane == h, jnp.sum(ds, axis=1, keepdims=True), dcq)
                dq_ref[:, cols] = dq_pair.astype(BF16)
                dk_acc[0:klen, cols] += dk_pair
                dv_acc[0:klen, cols] += dv_pair
            dcq_ref[...] = dcq

        _for_key_length(i, block)

        @pl.when(i == steps - 1)
        def _():
            dk_ref[...] = dk_acc[...].astype(BF16)
            dv_ref[...] = dv_acc[...].astype(BF16)

    q, k, v, cc, cr = _fox_specs()
    blk = pl.BlockSpec((TQ, 256), lambda i: (i, 0))
    full = pl.BlockSpec((S, 256), lambda i: (0, 0))
    return pl.pallas_call(body, grid=(steps,), name=name, in_specs=[q, k, v, cc, cr, cc, blk, blk],
                          out_specs=[blk, full, full, cc, cr],
                          out_shape=[SDS((S, 256), BF16), SDS((S, 256), BF16), SDS((S, 256), BF16), SDS((S, 128), F32),
                                     SDS((8, S), F32)],
                          scratch_shapes=[pltpu.VMEM((S, 256), F32), pltpu.VMEM((S, 256), F32)])(
                              z, z, z, cumc, cumr, lse, o, do)


W = 128
SWA_HEADS = 4


def _swa_core(first, qs, kcat, vcat, sink):
    r = lax.broadcasted_iota(jnp.int32, (SWA_HEADS * W, 2 * W), 0)
    j = lax.broadcasted_iota(jnp.int32, (SWA_HEADS * W, 2 * W), 1)
    qi = r & (W - 1)
    valid = ((j < W) & (j > qi) & jnp.logical_not(first)) | ((j >= W) & (j - W <= qi))
    s = jnp.where(valid, _nt((qs * SCALE).astype(BF16), kcat.astype(BF16)), NEG)
    m = lax.stop_gradient(jnp.maximum(jnp.max(s, axis=1, keepdims=True), sink))
    e = jnp.exp(s - m)
    den = jnp.sum(e, axis=1, keepdims=True) + jnp.exp(sink - m)
    return _nn((e / den).astype(BF16), vcat.astype(BF16))


def _kv_lanes(kv):
    lane = lax.broadcasted_iota(jnp.int32, (W, 128), 1)
    return (lane >= 64 * kv) & (lane < 64 * kv + 64)


def _swa_stack(pair0, pair1):
    blocks = []
    for h in range(SWA_HEADS):
        kv, hh = h // 2, h % 2
        a = (pair0, pair1)[kv]
        a = a if hh == kv else pltpu.roll(a, 64, axis=1)
        blocks.append(jnp.where(_kv_lanes(kv), a, 0.0))
    return jnp.concatenate(blocks, axis=0)


def _swa_unstack(stacked):
    pairs = [jnp.zeros((W, 128), F32), jnp.zeros((W, 128), F32)]
    for h in range(SWA_HEADS):
        kv, hh = h // 2, h % 2
        a = jnp.where(_kv_lanes(kv), stacked[h * W:(h + 1) * W], 0.0)
        pairs[kv] = pairs[kv] + (a if hh == kv else pltpu.roll(a, 64, axis=1))
    return pairs


def _head_rows():
    return lax.broadcasted_iota(jnp.int32, (SWA_HEADS * W, 1), 0) // W


def _swa_operands(q0_ref, q1_ref, kp_ref, kc_ref, vp_ref, vc_ref, sk_ref):
    qs = _swa_stack(q0_ref[...], q1_ref[...])
    kcat = jnp.concatenate([kp_ref[...], kc_ref[...]], axis=0)
    vcat = jnp.concatenate([vp_ref[...], vc_ref[...]], axis=0)
    sink = jnp.zeros((SWA_HEADS * W, 1), F32)
    for h in range(SWA_HEADS):
        sink = jnp.where(_head_rows() == h, sk_ref[:, h:h + 1], sink)
    return qs, kcat, vcat, sink


def _swa_specs():
    q0 = pl.BlockSpec((W, 128), lambda n: (n, Z_SWA_Q // 128))
    q1 = pl.BlockSpec((W, 128), lambda n: (n, Z_SWA_Q // 128 + 1))
    kc = pl.BlockSpec((W, 128), lambda n: (n, Z_SWA_K // 128))
    kp = pl.BlockSpec((W, 128), lambda n: (jnp.maximum(n - 1, 0), Z_SWA_K // 128))
    vc = pl.BlockSpec((W, 128), lambda n: (n, Z_SWA_V // 128))
    vp = pl.BlockSpec((W, 128), lambda n: (jnp.maximum(n - 1, 0), Z_SWA_V // 128))
    sk = pl.BlockSpec((1, 128), lambda n: (0, 0))
    return q0, q1, kp, kc, vp, vc, sk


def swa_fwd(z, sinks, name):
    def body(*refs):
        o_ref = refs[-1]
        o = _swa_core(pl.program_id(0) == 0, *_swa_operands(*refs[:-1]))
        o0, o1 = _swa_unstack(o)
        o_ref[:, 0:128] = o0
        o_ref[:, 128:256] = o1

    return pl.pallas_call(body, grid=(S // W,), name=name, in_specs=list(_swa_specs()),
                          out_specs=pl.BlockSpec((W, 256), lambda n: (n, 0)),
                          out_shape=SDS((S, 256), F32))(z, z, z, z, z, z, sinks)


def swa_bwd(z, sinks, do, name):
    steps = S // W

    def body(*refs):
        do_ref, dq_ref, dk_ref, dv_ref, dsk_ref, dk_acc, dv_acc = refs[7:]
        n = pl.program_id(0)
        first = n == 0
        _, vjp = jax.vjp(functools.partial(_swa_core, first), *_swa_operands(*refs[:7]))
        dqs, dkcat, dvcat, dsink = vjp(_swa_stack(do_ref[:, 0:128], do_ref[:, 128:256]))
        dq0, dq1 = _swa_unstack(dqs)
        dq_ref[:, 0:128] = dq0.astype(BF16)
        dq_ref[:, 128:256] = dq1.astype(BF16)

        @pl.when(first)
        def _():
            dk_acc[...] = jnp.zeros_like(dk_acc)
            dv_acc[...] = jnp.zeros_like(dv_acc)

        cur = pl.ds(pl.multiple_of(n * W, W), W)
        dk_acc[cur, :] += dkcat[W:2 * W]
        dv_acc[cur, :] += dvcat[W:2 * W]

        @pl.when(n > 0)
        def _():
            prev = pl.ds(pl.multiple_of((n - 1) * W, W), W)
            dk_acc[prev, :] += dkcat[0:W]
            dv_acc[prev, :] += dvcat[0:W]

        lane = lax.broadcasted_iota(jnp.int32, (1, 128), 1)
        dsk = jnp.zeros((1, 128), F32)
        for h in range(SWA_HEADS):
            d = jnp.sum(jnp.where(_head_rows() == h, dsink, 0.0), axis=0, keepdims=True)
            dsk = jnp.where(lane == h, d, dsk)
        _acc(dsk_ref, dsk, first)

        @pl.when(n == steps - 1)
        def _():
            dk_ref[...] = dk_acc[...].astype(BF16)
            dv_ref[...] = dv_acc[...].astype(BF16)

    blk = pl.BlockSpec((W, 256), lambda n: (n, 0))
    full = pl.BlockSpec((S, 128), lambda n: (0, 0))
    return pl.pallas_call(body, grid=(steps,), name=name, in_specs=list(_swa_specs()) + [blk],
                          out_specs=[blk, full, full, pl.BlockSpec((1, 128), lambda n: (0, 0))],
                          out_shape=[SDS((S, 256), BF16), SDS((S, 128), BF16), SDS((S, 128), BF16), SDS((1, 128), F32)],
                          scratch_shapes=[pltpu.VMEM((S, 128), F32), pltpu.VMEM((S, 128), F32)])(
                              z, z, z, z, z, z, sinks, do)


def _glu(a, g):
    return a * jax.nn.sigmoid(g)


def _cv1_specs():
    a = pl.BlockSpec((S, 128), lambda j: (0, Z_CONV_A // 128 + j))
    g = pl.BlockSpec((S, 128), lambda j: (0, Z_CONV_G // 128 + j))
    w = pl.BlockSpec((32, 128), lambda j: (0, j))
    b = pl.BlockSpec((1, 128), lambda j: (0, j))
    h = pl.BlockSpec((S, 128), lambda j: (0, j))
    return a, g, w, b, h


def conv_dw_fwd(z, cw, cb, name):
    def body(a_ref, g_ref, w_ref, b_ref, o_ref):
        hh = _glu(a_ref[...], g_ref[...])
        acc = jnp.zeros((S, 128), F32) + b_ref[...]
        for k in range(CONV_K):
            acc = acc + _shift_down(hh, CONV_K - 1 - k) * w_ref[k:k + 1, :]
        o_ref[...] = acc

    a, g, w, b, h = _cv1_specs()
    return pl.pallas_call(body, grid=(2,), name=name, in_specs=[a, g, w, b], out_specs=h,
                          out_shape=SDS((S, 256), F32))(z, z, cw, cb)


def conv_dw_bwd(z, cw, dhc, name):
    def body(a_ref, g_ref, w_ref, dh_ref, da_ref, dg_ref, dw_ref, db_ref):
        hh, vjp = jax.vjp(_glu, a_ref[...], g_ref[...])
        dh = dh_ref[...]
        dhh = jnp.zeros((S, 128), F32)
        for k in range(CONV_K):
            n = CONV_K - 1 - k
            dhh = dhh + _shift_up(dh, n) * w_ref[k:k + 1, :]
            dw_ref[k:k + 1, :] = jnp.sum(dh * _shift_down(hh, n), axis=0, keepdims=True)
        dw_ref[CONV_K:32, :] = jnp.zeros((32 - CONV_K, 128), F32)
        db_ref[...] = jnp.sum(dh, axis=0, keepdims=True)
        da, dg = vjp(dhh)
        da_ref[...] = da.astype(BF16)
        dg_ref[...] = dg.astype(BF16)

    a, g, w, b, h = _cv1_specs()
    return pl.pallas_call(body, grid=(2,), name=name, in_specs=[a, g, w, h], out_specs=[h, h, w, b],
                          out_shape=[SDS((S, 256), BF16), SDS((S, 256), BF16), SDS((32, 256), F32), SDS((1, 256), F32)])(
                              z, z, cw, dhc)


def _ln(x, g, b):
    mu = jnp.mean(x, axis=-1, keepdims=True)
    xc = x - mu
    var = jnp.mean(xc * xc, axis=-1, keepdims=True)
    return xc * lax.rsqrt(var + EPS) * g + b


def _conv_pw(hc, lg, lb, pw, pb):
    y = jax.nn.silu(_ln(hc, lg, lb))
    return _nn(y.astype(BF16), pw.astype(BF16)) + pb


TS2 = 512
_ROW2 = pl.BlockSpec((TS2, 256), lambda i: (i, 0))
_VEC2 = pl.BlockSpec((1, 256), lambda i: (0, 0))
_MAT2 = pl.BlockSpec((256, 256), lambda i: (0, 0))


def conv_pw_fwd(hc, lg, lb, pw, pb, name):
    def body(h_ref, lg_ref, lb_ref, pw_ref, pb_ref, o_ref):
        o_ref[...] = _conv_pw(h_ref[...], lg_ref[...], lb_ref[...], pw_ref[...], pb_ref[...])

    return pl.pallas_call(body, grid=(S // TS2,), name=name, in_specs=[_ROW2, _VEC2, _VEC2, _MAT2, _VEC2],
                          out_specs=_ROW2, out_shape=SDS((S, 256), F32))(hc, lg, lb, pw, pb)


def conv_pw_bwd(hc, lg, lb, pw, pb, dy, name):
    def body(h_ref, lg_ref, lb_ref, pw_ref, pb_ref, dy_ref, dh_ref, dlg_ref, dlb_ref, dpw_ref, dpb_ref):
        _, vjp = jax.vjp(_conv_pw, h_ref[...], lg_ref[...], lb_ref[...], pw_ref[...], pb_ref[...])
        dh, dlg, dlb, dpw, dpb = vjp(dy_ref[...])
        dh_ref[...] = dh
        first = pl.program_id(0) == 0
        _acc(dlg_ref, dlg, first)
        _acc(dlb_ref, dlb, first)
        _acc(dpw_ref, dpw, first)
        _acc(dpb_ref, dpb, first)

    vec = SDS((1, 256), F32)
    return pl.pallas_call(body, grid=(S // TS2,), name=name, in_specs=[_ROW2, _VEC2, _VEC2, _MAT2, _VEC2, _ROW2],
                          out_specs=[_ROW2, _VEC2, _VEC2, _MAT2, _VEC2],
                          out_shape=[SDS((S, 256), F32), vec, vec, SDS((256, 256), F32), vec])(hc, lg, lb, pw, pb, dy)


def _sgu_block(u0, u1, v0, v1, lg0, lg1, lb0, lb1, w0, w1, w2, w3, bt):
    u0, u1, v0, v1 = (jax.nn.gelu(a) for a in (u0, u1, v0, v1))
    mu = (jnp.sum(v0, axis=1, keepdims=True) + jnp.sum(v1, axis=1, keepdims=True)) * (1.0 / GW)
    c0, c1 = v0 - mu, v1 - mu
    var = (jnp.sum(c0 * c0, axis=1, keepdims=True) + jnp.sum(c1 * c1, axis=1, keepdims=True)) * (1.0 / GW)
    r = lax.rsqrt(var + EPS)
    n0 = c0 * r * lg0 + lb0
    n1 = c1 * r * lg1 + lb1
    row = lax.broadcasted_iota(jnp.int32, (128, 128), 0)
    col = lax.broadcasted_iota(jnp.int32, (128, 128), 1)
    tri = row >= col
    outs = []
    for p, (n, u, wa, wb) in enumerate(((n0, u0, w0, w1), (n1, u1, w2, w3))):
        nb = n.astype(BF16)
        ma = _nn(jnp.where(tri, wa, 0.0).astype(BF16), nb)
        mb = _nn(jnp.where(tri, wb, 0.0).astype(BF16), nb)
        expand = (row == 2 * p + col // 64).astype(F32)
        bias = jnp.dot(bt, expand, precision=lax.Precision.HIGHEST, preferred_element_type=F32)
        outs.append(u * (jnp.where(col < 64, ma, mb) + bias))
    return outs[0], outs[1]


def _sgu_specs():
    def col(c):
        return pl.BlockSpec((128, 128), lambda n, c=c: (n, c))
    zs = [col(Z_SGU_U // 128), col(Z_SGU_U // 128 + 1), col(Z_SGU_V // 128), col(Z_SGU_V // 128 + 1)]
    vec = [pl.BlockSpec((1, 128), lambda n: (0, 0)), pl.BlockSpec((1, 128), lambda n: (0, 1))]
    ws = [pl.BlockSpec((None, 128, 128), lambda n, g=g: (g, 0, 0)) for g in range(4)]
    bt = pl.BlockSpec((128, 128), lambda n: (0, 0))
    return zs + vec + vec + ws + [bt]


def sgu_fwd(z, lg, lb, w, bt, name):
    def body(*refs):
        o_ref = refs[-1]
        y0, y1 = _sgu_block(*[r[...] for r in refs[:-1]])
        o_ref[:, 0:128] = y0
        o_ref[:, 128:256] = y1

    return pl.pallas_call(body, grid=(S // 128,), name=name, in_specs=_sgu_specs(),
                          out_specs=pl.BlockSpec((128, 256), lambda n: (n, 0)),
                          out_shape=SDS((S, 256), F32))(z, z, z, z, lg, lg, lb, lb, w, w, w, w, bt)


def sgu_bwd(z, lg, lb, w, bt, dy, name):
    def body(*refs):
        ins, dy_ref = refs[:13], refs[13]
        du_ref, dv_ref, dlg_ref, dlb_ref, dw_ref, dbt_ref = refs[14:]
        _, vjp = jax.vjp(_sgu_block, *[r[...] for r in ins])
        du0, du1, dv0, dv1, dlg0, dlg1, dlb0, dlb1, dw0, dw1, dw2, dw3, dbt = vjp((dy_ref[:, 0:128], dy_ref[:, 128:256]))
        du_ref[:, 0:128] = du0.astype(BF16)
        du_ref[:, 128:256] = du1.astype(BF16)
        dv_ref[:, 0:128] = dv0.astype(BF16)
        dv_ref[:, 128:256] = dv1.astype(BF16)
        first = pl.program_id(0) == 0

        @pl.when(first)
        def _():
            dlg_ref[...] = jnp.zeros_like(dlg_ref)
            dlb_ref[...] = jnp.zeros_like(dlb_ref)
            dw_ref[...] = jnp.zeros_like(dw_ref)
            dbt_ref[...] = jnp.zeros_like(dbt_ref)

        dlg_ref[:, 0:128] += dlg0
        dlg_ref[:, 128:256] += dlg1
        dlb_ref[:, 0:128] += dlb0
        dlb_ref[:, 128:256] += dlb1
        for g, d in enumerate((dw0, dw1, dw2, dw3)):
            dw_ref[g] += d
        dbt_ref[...] += dbt

    blk = pl.BlockSpec((128, 256), lambda n: (n, 0))
    vec = pl.BlockSpec((1, 256), lambda n: (0, 0))
    return pl.pallas_call(body, grid=(S // 128,), name=name, in_specs=_sgu_specs() + [blk],
                          out_specs=[blk, blk, vec, vec, pl.BlockSpec((4, 128, 128), lambda n: (0, 0, 0)),
                                     pl.BlockSpec((128, 128), lambda n: (0, 0))],
                          out_shape=[SDS((S, 256), BF16), SDS((S, 256), BF16), SDS((1, 256), F32), SDS((1, 256), F32),
                                     SDS((4, 128, 128), F32), SDS((128, 128), F32)])(
                                         z, z, z, z, lg, lg, lb, lb, w, w, w, w, bt, dy)


def _group_norm(y0, y1, y2, y3, g0, g1, g2, g3):
    return tuple(_rms(y, g) for y, g in zip((y0, y1, y2, y3), (g0, g1, g2, g3)))


_GROW = pl.BlockSpec((TS2, 256), lambda i: (i, 0))
_GCAT = pl.BlockSpec((TS2, D), lambda i: (i, 0))
_GVEC = [pl.BlockSpec((1, 256), lambda i, j=j: (0, j)) for j in range(4)]


def group_norm_fwd(ys, gg, name):
    def body(*refs):
        o_ref = refs[-1]
        outs = _group_norm(*[r[...] for r in refs[:-1]])
        for j, c in enumerate(outs):
            o_ref[:, 256 * j:256 * (j + 1)] = c.astype(BF16)

    return pl.pallas_call(body, grid=(S // TS2,), name=name, in_specs=[_GROW] * 4 + _GVEC, out_specs=_GCAT,
                          out_shape=SDS((S, D), BF16))(*ys, gg, gg, gg, gg)


def group_norm_bwd(ys, gg, dcat, name):
    def body(*refs):
        ins, dc_ref = refs[:8], refs[8]
        dy_refs, dg_ref = refs[9:13], refs[13]
        _, vjp = jax.vjp(_group_norm, *[r[...] for r in ins])
        grads = vjp(tuple(dc_ref[:, 256 * j:256 * (j + 1)] for j in range(4)))
        first = pl.program_id(0) == 0

        @pl.when(first)
        def _():
            dg_ref[...] = jnp.zeros_like(dg_ref)

        for j in range(4):
            dy_refs[j][...] = grads[j]
            dg_ref[:, 256 * j:256 * (j + 1)] += grads[4 + j]

    return pl.pallas_call(body, grid=(S // TS2,), name=name, in_specs=[_GROW] * 4 + _GVEC + [_GCAT],
                          out_specs=[_GROW] * 4 + [pl.BlockSpec((1, D), lambda i: (0, 0))],
                          out_shape=[SDS((S, 256), F32)] * 4 + [SDS((1, D), F32)])(*ys, gg, gg, gg, gg, dcat)


FB = 256
N_FB = DFF // FB


def _ffn_gate(ug, uv, wg0, wg1, wg2, wv0, wv1, wv2, bg, bv):
    cg = bg + _shift_down(ug, 2) * wg0 + _shift_down(ug, 1) * wg1 + ug * wg2
    cv = bv + _shift_down(uv, 2) * wv0 + _shift_down(uv, 1) * wv1 + uv * wv2
    return jax.nn.silu(cg) * cv


def _gate_specs():
    ug = pl.BlockSpec((S, FB), lambda j: (0, j))
    uv = pl.BlockSpec((S, FB), lambda j: (0, j + N_FB))
    wg = pl.BlockSpec((8, FB), lambda j: (0, j))
    wv = pl.BlockSpec((8, FB), lambda j: (0, j + N_FB))
    bg = pl.BlockSpec((1, FB), lambda j: (0, j))
    bv = pl.BlockSpec((1, FB), lambda j: (0, j + N_FB))
    return ug, uv, wg, wv, bg, bv


def _gate_args(ug_ref, uv_ref, wg_ref, wv_ref, bg_ref, bv_ref):
    return (ug_ref[...], uv_ref[...], wg_ref[0:1, :], wg_ref[1:2, :], wg_ref[2:3, :],
            wv_ref[0:1, :], wv_ref[1:2, :], wv_ref[2:3, :], bg_ref[...], bv_ref[...])


def ffn_gate_fwd(u, cw, cb, name):
    def body(ug_ref, uv_ref, wg_ref, wv_ref, bg_ref, bv_ref, o_ref):
        o_ref[...] = _ffn_gate(*_gate_args(ug_ref, uv_ref, wg_ref, wv_ref, bg_ref, bv_ref)).astype(BF16)

    return pl.pallas_call(body, grid=(N_FB,), name=name, in_specs=list(_gate_specs()),
                          out_specs=pl.BlockSpec((S, FB), lambda j: (0, j)),
                          out_shape=SDS((S, DFF), BF16))(u, u, cw, cw, cb, cb)


def ffn_gate_bwd(u, cw, cb, da, name):
    def body(ug_ref, uv_ref, wg_ref, wv_ref, bg_ref, bv_ref, da_ref, dug_ref, duv_ref, dwg_ref, dwv_ref, dbg_ref, dbv_ref):
        _, vjp = jax.vjp(_ffn_gate, *_gate_args(ug_ref, uv_ref, wg_ref, wv_ref, bg_ref, bv_ref))
        dug, duv, g0, g1, g2, v0, v1, v2, dbg, dbv = vjp(da_ref[...])
        dug_ref[...] = dug.astype(BF16)
        duv_ref[...] = duv.astype(BF16)
        for k, (a, b) in enumerate(((g0, v0), (g1, v1), (g2, v2))):
            dwg_ref[k:k + 1, :] = a
            dwv_ref[k:k + 1, :] = b
        dwg_ref[FFN_K:8, :] = jnp.zeros((8 - FFN_K, FB), F32)
        dwv_ref[FFN_K:8, :] = jnp.zeros((8 - FFN_K, FB), F32)
        dbg_ref[...] = dbg
        dbv_ref[...] = dbv

    ug, uv, wg, wv, bg, bv = _gate_specs()
    half = pl.BlockSpec((S, FB), lambda j: (0, j))
    whalf = pl.BlockSpec((8, FB), lambda j: (0, j))
    bhalf = pl.BlockSpec((1, FB), lambda j: (0, j))
    return pl.pallas_call(body, grid=(N_FB,), name=name, in_specs=[ug, uv, wg, wv, bg, bv, half],
                          out_specs=[half, half, whalf, whalf, bhalf, bhalf],
                          out_shape=[SDS((S, DFF), BF16), SDS((S, DFF), BF16), SDS((8, DFF), F32), SDS((8, DFF), F32),
                                     SDS((1, DFF), F32), SDS((1, DFF), F32)])(u, u, cw, cw, cb, cb, da)


def _adamw(w, g, m, v):
    m = ADAM_B1 * m + (1.0 - ADAM_B1) * g
    v = ADAM_B2 * v + (1.0 - ADAM_B2) * (g * g)
    m_hat = m / (1.0 - ADAM_B1 ** ADAM_STEP)
    v_hat = v / (1.0 - ADAM_B2 ** ADAM_STEP)
    delta = -ADAM_LR * (m_hat / (jnp.sqrt(v_hat) + ADAM_EPS) + ADAM_WD * w)
    return delta, m, v


def sum_pieces(chip, r, own, layer, base, name):
    n, rows, cols = r.shape
    tr = _row_tile(rows, cols)

    def body(chip_ref, r_ref, own_ref, *rest):
        o_ref = rest[-1]
        acc = jnp.zeros((tr, cols), F32)
        for j in range(n):
            acc = acc + jnp.where(chip_ref[0] == j, own_ref[0], r_ref[j]).astype(F32)
        o_ref[...] = acc

    extra = {} if base is None else dict(input_output_aliases={3: 0})
    grid_spec = pltpu.PrefetchScalarGridSpec(
        num_scalar_prefetch=1, grid=(rows // tr,),
        in_specs=[pl.BlockSpec((n, tr, cols), lambda i, c: (0, i, 0)), pl.BlockSpec((1, tr, cols), lambda i, c: (c[0], i, 0))]
        + ([] if base is None else [ANY]),
        out_specs=pl.BlockSpec((None, tr, cols), lambda i, c: (layer, i, 0)))
    return pl.pallas_call(body, grid_spec=grid_spec, name=name, out_shape=SDS((DEPTH, rows, cols), F32), **extra)(
        *([chip, r, own] if base is None else [chip, r, own, base]))


LANE_BLOCK = 128


def sum_pieces_t(chip, rs, owns, name):
    n, rows, cols = rs[0].shape

    def body(chip_ref, *refs):
        o_ref = refs[-1]
        for l in range(DEPTH):
            r_ref, own_ref = refs[2 * l], refs[2 * l + 1]
            acc = jnp.zeros((rows, LANE_BLOCK), F32)
            for j in range(n):
                acc = acc + jnp.where(chip_ref[0] == j, own_ref[0], r_ref[j]).astype(F32)
            o_ref[:, l, :] = acc

    r_spec = pl.BlockSpec((n, rows, LANE_BLOCK), lambda i, c: (0, 0, i))
    own_spec = pl.BlockSpec((1, rows, LANE_BLOCK), lambda i, c: (c[0], 0, i))
    grid_spec = pltpu.PrefetchScalarGridSpec(
        num_scalar_prefetch=1, grid=(cols // LANE_BLOCK,), in_specs=[r_spec, own_spec] * DEPTH,
        out_specs=pl.BlockSpec((rows, DEPTH, LANE_BLOCK), lambda i, c: (0, 0, i)))
    ops = [a for pair in zip(rs, owns) for a in pair]
    return pl.pallas_call(body, grid_spec=grid_spec, name=name, out_shape=SDS((rows, DEPTH, cols), F32))(chip, *ops)


def adamw_pair_t(w, p, q, m, v, name):
    rows, depth, cols = w.shape

    def body(w_ref, p_ref, q_ref, m_ref, v_ref, g_ref, d_ref, nm_ref, nv_ref):
        g = p_ref[...] + q_ref[...]
        g_ref[...] = g
        d_ref[...], nm_ref[...], nv_ref[...] = _adamw(w_ref[...], g, m_ref[...], v_ref[...])

    spec = pl.BlockSpec((rows, depth, LANE_BLOCK), lambda i: (0, 0, i))
    return pl.pallas_call(body, grid=(cols // LANE_BLOCK,), name=name, in_specs=[spec] * 5, out_specs=[spec] * 4,
                          out_shape=[SDS(w.shape, F32)] * 4)(w, p, q, m, v)


def adamw_pair(w, p, q, m, v, name):
    rows, cols = w.shape
    tr = _row_tile(rows, cols)

    def body(w_ref, p_ref, q_ref, m_ref, v_ref, g_ref, d_ref, nm_ref, nv_ref):
        g = p_ref[...] + q_ref[...]
        g_ref[...] = g
        d_ref[...], nm_ref[...], nv_ref[...] = _adamw(w_ref[...], g, m_ref[...], v_ref[...])

    spec = pl.BlockSpec((tr, cols), lambda i: (i, 0))
    return pl.pallas_call(body, grid=(rows // tr,), name=name, in_specs=[spec] * 5, out_specs=[spec] * 4,
                          out_shape=[SDS((rows, cols), F32)] * 4)(w, p, q, m, v)


_PACK_LAYOUT = (('b_ada', 6 * D), ('g_pre_mix', D), ('g_post_mix', D), ('g_pre_ffn', D), ('g_post_ffn', D),
                ('b_fgate', 128), ('conv_b', GW), ('conv_ln_g', GW), ('conv_ln_b', GW), ('conv_pw_b', GW),
                ('swa_sinks', 128), ('sgu_ln_g', GW), ('sgu_ln_b', GW), ('sgu_b', 4 * 128), ('g_group', D),
                ('ffn_conv_b', NUP))
_PACK_WIDTH = dict(_PACK_LAYOUT)
_PACK_ROW, _LAYER_ROWS = {}, 0
for _name, _width in _PACK_LAYOUT:
    _PACK_ROW[_name] = _LAYER_ROWS
    _LAYER_ROWS += -(-_width // D)
PACK_ROWS = -(-DEPTH * _LAYER_ROWS // 8) * 8


def _segments(offset, width):
    out, s = [], 0
    while s < width:
        row, col = divmod(offset + s, D)
        n = min(width - s, D - col)
        out.append((s, row, col, n))
        s += n
    return out


def pack_small(grads, dmods, name):
    ops, plan = [], []
    for l in range(DEPTH):
        pieces = [('b_ada', j * D, a) for j, a in enumerate(dmods[l])]
        pieces += [(n, 0, grads[l][n]) for n, _ in _PACK_LAYOUT if n not in ('b_ada', 'sgu_b', 'ffn_conv_b')]
        pieces += [('ffn_conv_b', j * DFF, a) for j, a in enumerate(grads[l]['ffn_conv_b'])]
        for n, off, a in pieces:
            plan.append((len(ops), l, n, off))
            ops.append(a)
    bts = [grads[l]['sgu_bt'] for l in range(DEPTH)]
    sws = [grads[l]['sgu_w'] for l in range(DEPTH)]
    n_vec = len(ops)

    def body(*refs):
        vec, bt, sw = refs[:n_vec], refs[n_vec:n_vec + DEPTH], refs[n_vec + DEPTH:n_vec + 2 * DEPTH]
        o_ref, ow_ref, scr = refs[n_vec + 2 * DEPTH:]
        o_ref[...] = jnp.zeros_like(o_ref)
        for idx, l, n, off in plan:
            base = l * _LAYER_ROWS + _PACK_ROW[n]
            width = min(vec[idx].shape[1], _PACK_WIDTH[n] - off)
            for s, row, col, lanes in _segments(off, width):
                o_ref[base + row:base + row + 1, col:col + lanes] = vec[idx][:, s:s + lanes]
        for l in range(DEPTH):
            scr[...] = bt[l][...].T
            row = l * _LAYER_ROWS + _PACK_ROW['sgu_b']
            for g in range(4):
                o_ref[row:row + 1, 128 * g:128 * (g + 1)] = scr[g:g + 1, :]
            ow_ref[l] = sw[l][...]

    vm = pl.BlockSpec(memory_space=pltpu.VMEM)
    return pl.pallas_call(body, name=name, in_specs=[vm] * (n_vec + 2 * DEPTH), out_specs=[vm, vm],
                          out_shape=[SDS((PACK_ROWS, D), F32), SDS((DEPTH, 4, 128, 128), F32)],
                          scratch_shapes=[pltpu.VMEM((128, 128), F32)])(*ops, *bts, *sws)


def adamw_small(gall, gall_w, w, m, v, name):
    names = [n for n in SMALL]
    n_par = len(names)

    def native(ref, n, l, col, lanes):
        if n == 'sgu_b':
            return ref.at[l, pl.ds(col // 128, 1), :]
        return ref.at[pl.ds(l, 1), pl.ds(col, lanes)]

    def body(*refs):
        ga_ref, gw_ref = refs[0], refs[1]
        w_refs, m_refs, v_refs = (refs[2 + k * n_par:2 + (k + 1) * n_par] for k in range(3))
        outs = refs[2 + 3 * n_par:2 + 7 * n_par]
        scr = refs[-1]
        g = ga_ref[0]
        for j in range(1, N_DEV):
            g = g + ga_ref[j]
        scr[...] = g
        for k, n in enumerate(names):
            o_g, o_d, o_m, o_v = outs[4 * k:4 * k + 4]
            if n == 'sgu_w':
                gw = gw_ref[0]
                for j in range(1, N_DEV):
                    gw = gw + gw_ref[j]
                o_g[...] = gw
                o_d[...], o_m[...], o_v[...] = _adamw(w_refs[k][...], gw, m_refs[k][...], v_refs[k][...])
                continue
            width = w_refs[k].shape[-1] if n != 'sgu_b' else 4 * 128
            for l in range(DEPTH):
                base = l * _LAYER_ROWS + _PACK_ROW[n]
                step = 128 if n == 'sgu_b' else D
                for col in range(0, width, step):
                    lanes = min(step, width - col)
                    row, lane0 = divmod(col, D)
                    gv = scr[base + row:base + row + 1, lane0:lane0 + lanes]
                    at = functools.partial(native, n=n, l=l, col=col, lanes=lanes)
                    at(o_g)[...] = gv
                    at(o_d)[...], at(o_m)[...], at(o_v)[...] = _adamw(at(w_refs[k])[...], gv, at(m_refs[k])[...],
                                                                      at(v_refs[k])[...])

    vm = pl.BlockSpec(memory_space=pltpu.VMEM)
    params = [d[n] for d in (w, m, v) for n in names]
    outs = pl.pallas_call(body, name=name, in_specs=[vm] * (2 + 3 * n_par), out_specs=[vm] * (4 * n_par),
                          out_shape=[SDS(w[n].shape, F32) for n in names for _ in range(4)],
                          scratch_shapes=[pltpu.VMEM((PACK_ROWS, D), F32)])(gall, gall_w, *params)
    return {n: tuple(outs[4 * k:4 * k + 4]) for k, n in enumerate(names)}


ADA_COLS = 6 * D // N_CHIP
ADA_TN = 512


def ada_mod(c_all, w, b, name):
    def body(c_ref, w_ref, b_ref, o_ref):
        ca = jax.nn.silu(c_ref[...])
        o_ref[...] = jnp.dot(ca, w_ref[...], precision=lax.Precision.HIGHEST, preferred_element_type=F32) + b_ref[...]

    return pl.pallas_call(
        body, grid=(DEPTH, ADA_COLS // ADA_TN), name=name,
        in_specs=[pl.BlockSpec((N_DEV, D), lambda l, j: (0, 0)),
                  pl.BlockSpec((None, D, ADA_TN), lambda l, j: (l, 0, j)),
                  pl.BlockSpec((None, 1, ADA_TN), lambda l, j: (l, 0, j))],
        out_specs=pl.BlockSpec((None, N_DEV, ADA_TN), lambda l, j: (l, 0, j)),
        out_shape=SDS((DEPTH, N_DEV, ADA_COLS), F32))(c_all, w, b)


def ada_update(c_all_t, dmod, w, m, v, name):
    def body(c_ref, dm_ref, w_ref, m_ref, v_ref, g_ref, d_ref, nm_ref, nv_ref):
        ca = jax.nn.silu(c_ref[...])
        g = jnp.dot(ca, dm_ref[...], precision=lax.Precision.HIGHEST, preferred_element_type=F32)
        g_ref[...] = g
        d_ref[...], nm_ref[...], nv_ref[...] = _adamw(w_ref[...], g, m_ref[...], v_ref[...])

    wspec = pl.BlockSpec((None, D, ADA_TN), lambda l, j: (l, 0, j))
    return pl.pallas_call(
        body, grid=(DEPTH, ADA_COLS // ADA_TN), name=name,
        in_specs=[pl.BlockSpec((D, N_DEV), lambda l, j: (0, 0)),
                  pl.BlockSpec((None, N_DEV, ADA_TN), lambda l, j: (l, 0, j)), wspec, wspec, wspec],
        out_specs=[wspec] * 4, out_shape=[SDS((DEPTH, D, ADA_COLS), F32)] * 4)(c_all_t, dmod, w, m, v)


_CHIP_FLIPS = ((1, 0), (0, 1), (1, 1))
_DEV_FLIPS = tuple((a, b, c) for a in (0, 1) for b in (0, 1) for c in (0, 1) if (a, b, c) != (0, 0, 0))


def _position():
    return lax.axis_index("x"), lax.axis_index("y"), lax.axis_index("c")


def _hbm_call(body, arrs, out_shapes, n_remote, name):
    n = len(arrs)
    return pl.pallas_call(
        body, name=name, in_specs=[ANY] * n, out_specs=[ANY] * n, out_shape=out_shapes,
        scratch_shapes=[pltpu.SemaphoreType.DMA((n, n_remote)), pltpu.SemaphoreType.DMA((n, n_remote)),
                        pltpu.SemaphoreType.DMA((n,))])(*arrs)


_HBM = pl.BlockSpec(memory_space=pltpu.HBM)
_SEM = pl.BlockSpec(memory_space=pltpu.SEMAPHORE)
_EFFECT = pltpu.SideEffectType.DATAFLOW_SIDE_EFFECTING


GATHER, SCATTER, ALL, SWAP = "gather", "scatter", "all", "swap"
_PEERS = {GATHER: tuple((fx, fy, 0) for fx, fy in _CHIP_FLIPS), SCATTER: tuple((fx, fy, 0) for fx, fy in _CHIP_FLIPS),
          ALL: _DEV_FLIPS, SWAP: ((0, 0, 1),)}


def _peer_copies(kind, src, land, send, recv, arrivals):
    x, y, c = _position()
    out = []
    for k, (fx, fy, fc) in enumerate(_PEERS[kind]):
        tx, ty, tc = x ^ fx, y ^ fy, c ^ fc
        if kind == ALL:
            me, peer = 4 * x + 2 * y + c, 4 * tx + 2 * ty + tc
        else:
            me, peer = 2 * x + y, 2 * tx + ty
        if kind == SWAP:
            dst = land
        else:
            dst = land.at[peer if arrivals else me]
        out.append(pltpu.make_async_remote_copy(
            src_ref=src.at[peer] if kind == SCATTER else src, dst_ref=dst, send_sem=send.at[k], recv_sem=recv.at[k],
            device_id=(tx, ty, tc), device_id_type=MESH))
    return out


def _own_copy(kind, src, land, send):
    x, y, c = _position()
    me = 4 * x + 2 * y + c if kind == ALL else 2 * x + y
    return pltpu.make_async_copy(src, land.at[me], send.at[len(_PEERS[kind])])


def exchange_start(srcs, lands, kind, name):
    n = len(srcs)
    n_peers = len(_PEERS[kind])

    def body(*refs):
        src, land = refs[:n], refs[n:2 * n]
        send, recv = refs[2 * n:3 * n], refs[3 * n:4 * n]
        token = refs[-1]
        for a in range(n):
            for copy in _peer_copies(kind, src[a], land[a], send[a], recv[a], False):
                copy.start()
            if kind in (GATHER, ALL):
                _own_copy(kind, src[a], land[a], send[a]).start()
        token[...] = jnp.zeros_like(token)

    bufs = list(srcs) + list(lands)
    outs = pl.pallas_call(
        body, name=name, in_specs=[_HBM] * (2 * n),
        out_specs=[_SEM] * (2 * n) + [_HBM] * (2 * n) + [pl.BlockSpec(memory_space=pltpu.VMEM)],
        out_shape=[pltpu.SemaphoreType.DMA((n_peers + 1,))] * n + [pltpu.SemaphoreType.DMA((n_peers,))] * n
        + [pltpu.HBM(a.shape, a.dtype) for a in bufs] + [SDS((8, 128), F32)],
        input_output_aliases={i: 2 * n + i for i in range(2 * n)},
        compiler_params=pltpu.CompilerParams(has_side_effects=_EFFECT),
    )(*[pltpu.with_memory_space_constraint(a, pltpu.HBM) for a in bufs])
    flights = [(outs[a], outs[n + a], outs[2 * n + a], outs[3 * n + a]) for a in range(n)]
    return flights, outs[-1]


def exchange_wait(flights, kind, after, name):
    n = len(flights)

    def body(*refs):
        src, land = refs[:n], refs[n:2 * n]
        send, recv = refs[2 * n:3 * n], refs[3 * n:4 * n]
        for a in range(n):
            for arrival in _peer_copies(kind, src[a], land[a], send[a], recv[a], True):
                arrival.wait_send()
                arrival.wait_recv()
            if kind in (GATHER, ALL):
                _own_copy(kind, src[a], land[a], send[a]).wait()

    bufs = [f[2] for f in flights] + [f[3] for f in flights]
    sems = [f[0] for f in flights] + [f[1] for f in flights]
    outs = pl.pallas_call(
        body, name=name, in_specs=[_HBM] * (2 * n) + [_SEM] * (2 * n) + [ANY], out_specs=[_HBM] * (2 * n),
        out_shape=[pltpu.HBM(a.shape, a.dtype) for a in bufs],
        input_output_aliases={i: i for i in range(2 * n)},
        compiler_params=pltpu.CompilerParams(has_side_effects=_EFFECT),
    )(*bufs, *sems, after)
    return [(outs[a], outs[n + a]) for a in range(n)]


def chip_alltoall(arrs, name):
    n = len(arrs)

    def body(*refs):
        ins, outs = refs[:n], refs[n:2 * n]
        send, recv, loc = refs[2 * n:]
        x, y, c = _position()
        me = 2 * x + y
        copies = []
        for a in range(n):
            local = pltpu.make_async_copy(ins[a].at[me], outs[a].at[me], loc.at[a])
            local.start()
            copies.append(local)
            for k, (fx, fy) in enumerate(_CHIP_FLIPS):
                tx, ty = x ^ fx, y ^ fy
                cp = pltpu.make_async_remote_copy(
                    src_ref=ins[a].at[2 * tx + ty], dst_ref=outs[a].at[me], send_sem=send.at[a, k],
                    recv_sem=recv.at[a, k], device_id=(tx, ty, c), device_id_type=MESH)
                cp.start()
                copies.append(cp)
        for cp in copies:
            cp.wait()

    shapes = [SDS(a.shape, a.dtype) for a in arrs]
    return _hbm_call(body, arrs, shapes, 3, name)


def device_allgather(arrs, name):
    n = len(arrs)

    def body(*refs):
        ins, outs = refs[:n], refs[n:2 * n]
        send, recv, loc = refs[2 * n:]
        x, y, c = _position()
        me = 4 * x + 2 * y + c
        copies = []
        for a in range(n):
            local = pltpu.make_async_copy(ins[a], outs[a].at[me], loc.at[a])
            local.start()
            copies.append(local)
            for k, (fx, fy, fc) in enumerate(_DEV_FLIPS):
                cp = pltpu.make_async_remote_copy(
                    src_ref=ins[a], dst_ref=outs[a].at[me], send_sem=send.at[a, k], recv_sem=recv.at[a, k],
                    device_id=(x ^ fx, y ^ fy, c ^ fc), device_id_type=MESH)
                cp.start()
                copies.append(cp)
        for cp in copies:
            cp.wait()

    shapes = [SDS((N_DEV,) + a.shape, a.dtype) for a in arrs]
    return _hbm_call(body, arrs, shapes, 7, name)


def _pad_to(a, axis, size):
    pad = [(0, 0)] * a.ndim
    pad[axis] = (0, size - a.shape[axis])
    return jnp.pad(a, pad)


def _rows_to_z(wt):
    gap = Z_CONV_A - FG_END
    row = lax.broadcasted_iota(jnp.int32, (ZC, 1), 0)
    low = jnp.pad(wt, ((0, gap), (0, 0)))
    high = jnp.pad(wt, ((gap, 0), (0, 0)))
    return jnp.where(row < FG_END, low, jnp.where(row < Z_CONV_A, jnp.zeros_like(low), high))


def _rows_from_z(wt):
    row = lax.broadcasted_iota(jnp.int32, (IN_COLS, 1), 0)
    return jnp.where(row < FG_END, wt[:IN_COLS], wt[Z_CONV_A - FG_END:])


def _layer_fwd(l, x0, h1, mod, p, fetch, nxt):
    sh1, sc1, ga1, sh2, sc2, ga2 = mod
    t = f"l{l}_"
    wt = dict(fetch('w_in', h1))
    z = mm_nt([h1], wt['w_in'], t + "proj_in")
    cumc, cumr = fgate_fwd(z, p['b_fgate'], t + "fgate")
    y_fox, lse = fox_fwd(z, cumc, cumr, t + "fox")
    hc = conv_dw_fwd(z, wt['conv_w'], p['conv_b'], t + "conv_dw")
    y_conv = conv_pw_fwd(hc, p['conv_ln_g'], p['conv_ln_b'], wt['conv_pw_w'], p['conv_pw_b'], t + "conv_pw")
    y_swa = swa_fwd(z, p['swa_sinks'], t + "swa")
    y_sgu = sgu_fwd(z, p['sgu_ln_g'], p['sgu_ln_b'], p['sgu_w'], p['sgu_bt'], t + "sgu")
    ys = (y_fox, y_conv, y_swa, y_sgu)
    ycat = group_norm_fwd(ys, p['g_group'], t + "group_norm")
    wt.update(fetch('w_out', ycat))
    ymix = mm_nn([ycat], wt['w_out'], F32, t + "proj_out")
    x1, h2 = resid_modnorm_fwd(x0, ymix, ga1, p['g_post_mix'], p['g_pre_ffn'], sc2, sh2, t + "resid1_modnorm2")
    wt.update(fetch('ffn_w_up', h2))
    u = mm_nn([h2], wt['ffn_w_up'], F32, t + "ffn_up", tm=S)
    act = ffn_gate_fwd(u, wt['ffn_conv_w'], p['ffn_conv_b'], t + "ffn_gate")
    wt.update(fetch('ffn_w_down', act))
    yffn = mm_nn([act], wt['ffn_w_down'], F32, t + "ffn_down")
    if nxt is None:
        x2, h1_next = resid_fwd(x1, yffn, ga2, p['g_post_ffn'], t + "resid2"), None
    else:
        x2, h1_next = resid_modnorm_fwd(x1, yffn, ga2, p['g_post_ffn'], *nxt, t + "resid2_modnorm1")
    res = dict(x0=x0, h1=h1, z=z, cumc=cumc, cumr=cumr, y_fox=y_fox, lse=lse, hc=hc, ys=ys, ycat=ycat, ymix=ymix,
               x1=x1, h2=h2, u=u, act=act, yffn=yffn, wt=wt)
    return x2, h1_next, res


def _layer_bwd(l, dx2, head, mod, p, r, emit, prev):
    sh1, sc1, ga1, sh2, sc2, ga2 = mod
    t = f"l{l}_bwd_"
    g = {}
    wt = r['wt']
    dyffn, dga2, g['g_post_ffn'] = head
    tok = emit({'ffn_w_down': mm_tn([r['act']], [dyffn], t + "ffn_down_dw")})
    dact = mm_nt([dyffn], wt['ffn_w_down'], t + "ffn_down_dx")
    dug, duv, dwg, dwv, dbg, dbv = ffn_gate_bwd(r['u'], wt['ffn_conv_w'], p['ffn_conv_b'] + tok, dact, t + "ffn_gate")
    g['ffn_conv_b'] = (dbg, dbv)
    tok = emit({'ffn_w_up': mm_tn_halves(r['h2'], dug, duv, t + "ffn_up_dw")})
    dh2 = mm_nt_halves(dug, duv, wt['ffn_w_up'], t + "ffn_up_dx")
    dx1, g['g_pre_ffn'], dsc2, dsh2, dymix, dga1, g['g_post_mix'] = modnorm_resid_bwd(
        r['x1'], p['g_pre_ffn'] + tok, sc2, sh2, dh2, dx2, r['ymix'], ga1, p['g_post_mix'], t + "modnorm2_resid1")
    tok = emit({'w_out': mm_tn([r['ycat']], [dymix], t + "proj_out_dw")})
    dcat = mm_nt([dymix], wt['w_out'], t + "proj_out_dx")
    dy_fox, dy_conv, dy_swa, dy_sgu, g['g_group'] = group_norm_bwd(r['ys'], p['g_group'] + tok, dcat, t + "group_norm")
    z = r['z']
    fq, fk, fv, dcq, dck = fox_bwd(z, r['cumc'], r['cumr'], r['lse'], r['y_fox'], dy_fox, t + "fox")
    dzf, g['b_fgate'] = fgate_bwd(z, p['b_fgate'], dcq, dck, t + "fgate")
    dhc, g['conv_ln_g'], g['conv_ln_b'], dpw, g['conv_pw_b'] = conv_pw_bwd(
        r['hc'], p['conv_ln_g'], p['conv_ln_b'], wt['conv_pw_w'], p['conv_pw_b'], dy_conv, t + "conv_pw")
    ca, cg, dcw, g['conv_b'] = conv_dw_bwd(z, wt['conv_w'], dhc, t + "conv_dw")
    sq, sk, sv, g['swa_sinks'] = swa_bwd(z, p['swa_sinks'], dy_swa, t + "swa")
    gu, gv, g['sgu_ln_g'], g['sgu_ln_b'], g['sgu_w'], g['sgu_bt'] = sgu_bwd(
        z, p['sgu_ln_g'], p['sgu_ln_b'], p['sgu_w'], p['sgu_bt'], dy_sgu, t + "sgu")
    dz = [fq, fk, fv, dzf, ca, cg, sq, sk, sv, gu, gv]
    tok = emit({'w_in': mm_tn(dz, [r['h1']], t + "proj_in_dw"), 'conv_w': dcw, 'conv_pw_w': dpw,
                'ffn_conv_w': jnp.concatenate([dwg, dwv], axis=1)})
    dh1 = mm_nn(dz, wt['w_in'], F32, t + "proj_in_dx")
    if prev is None:
        dx0, g['g_pre_mix'], dsc1, dsh1 = modnorm_bwd(r['x0'], p['g_pre_mix'] + tok, sc1, sh1, dh1, dx1, t + "modnorm1")
        head = None
    else:
        dx0, g['g_pre_mix'], dsc1, dsh1, *head = modnorm_resid_bwd(
            r['x0'], p['g_pre_mix'] + tok, sc1, sh1, dh1, dx1, *prev, t + "modnorm1_resid2")
    return dx0, head, g, (dsh1, dsc1, dga1, dsh2, dsc2, dga2)


def _layer_params(l, w):
    def row(name, width=None):
        v = w[name][l].reshape(1, -1)
        return v if width is None else _pad_to(v, 1, width)
    return {
        'g_pre_mix': row('g_pre_mix'), 'g_post_mix': row('g_post_mix'), 'g_pre_ffn': row('g_pre_ffn'),
        'g_post_ffn': row('g_post_ffn'), 'b_fgate': row('b_fgate', 128), 'conv_b': row('conv_b'),
        'conv_ln_g': row('conv_ln_g'), 'conv_ln_b': row('conv_ln_b'), 'conv_pw_b': row('conv_pw_b'),
        'swa_sinks': row('swa_sinks', 128), 'sgu_ln_g': row('sgu_ln_g'), 'sgu_ln_b': row('sgu_ln_b'),
        'sgu_w': w['sgu_w'][l], 'sgu_bt': _pad_to(w['sgu_b'][l].T, 1, 128), 'g_group': row('g_group'),
        'ffn_conv_b': row('ffn_conv_b'),
    }


def _w_in_from_shards(s):
    return _rows_to_z(s.reshape(IN_COLS, D))


def _w_in_to_shards(g):
    return _rows_from_z(g).reshape(N_CHIP, IN_COLS // N_CHIP, D)


def _cols_from_shards(s, rows):
    _, r, n = s.shape
    return _pad_to(jnp.transpose(s, (1, 0, 2)).reshape(r, N_CHIP * n), 0, rows)


def _cols_to_shards(g, r):
    n = g.shape[1] // N_CHIP
    return jnp.transpose(g[:r].reshape(r, N_CHIP, n), (1, 0, 2))


_FROM_SHARDS = {
    'w_in': _w_in_from_shards,
    'w_out': lambda s: s.reshape(D, D),
    'ffn_w_up': lambda s: s,
    'ffn_w_down': lambda s: s.reshape(DFF, D),
    'conv_w': lambda s: _cols_from_shards(s, 32),
    'conv_pw_w': lambda s: s.reshape(GW, GW),
    'ffn_conv_w': lambda s: _cols_from_shards(s, 8),
}
_TO_SHARDS = {
    'w_in': _w_in_to_shards,
    'w_out': lambda g: g.reshape(N_CHIP, D // N_CHIP, D),
    'ffn_w_up': lambda g: g,
    'ffn_w_down': lambda g: g.reshape(N_CHIP, DFF // N_CHIP, D),
    'conv_w': lambda g: _cols_to_shards(g, CONV_K),
    'conv_pw_w': lambda g: g.reshape(N_CHIP, GW // N_CHIP, GW),
    'ffn_conv_w': lambda g: _cols_to_shards(g, FFN_K),
}


def _local_step(xs, target, mod, w, fetch, emit):
    params = [_layer_params(l, w) for l in range(DEPTH)]
    mods = [tuple(mod[l, j] for j in range(6)) for l in range(DEPTH)]
    pre = [(params[l]['g_pre_mix'], mods[l][1], mods[l][0]) for l in range(DEPTH)]
    post = [(mods[l][5], params[l]['g_post_ffn']) for l in range(DEPTH)]
    resids = []
    h1 = modnorm_fwd(xs, *pre[0], "l0_modnorm1")
    for l in range(DEPTH):
        nxt = pre[l + 1] if l + 1 < DEPTH else None
        xs, h1, r = _layer_fwd(l, xs, h1, mods[l], params[l], functools.partial(fetch, l), nxt)
        resids.append(r)
    dx, loss_row = loss_grad(xs, target, "loss")
    last = DEPTH - 1
    head = resid_bwd(resids[last]['yffn'], *post[last], dx, f"l{last}_bwd_resid2")
    grads, dmods = [None] * DEPTH, [None] * DEPTH
    for l in reversed(range(DEPTH)):
        prev = (resids[l - 1]['yffn'],) + post[l - 1] if l > 0 else None
        dx, head, grads[l], dmods[l] = _layer_bwd(l, dx, head, mods[l], params[l], resids[l], functools.partial(emit, l),
                                                  prev)
    return loss_row, dx, grads, dmods


_MATMUL_WEIGHTS = ('w_in', 'w_out', 'ffn_w_up', 'ffn_w_down')
_CONV_WEIGHTS = ('conv_w', 'conv_pw_w', 'ffn_conv_w')
_FETCH_GROUPS = {'w_in': ('w_in',) + _CONV_WEIGHTS, 'w_out': ('w_out',), 'ffn_w_up': ('ffn_w_up',),
                 'ffn_w_down': ('ffn_w_down',)}


def kernel(x, c, w_ada, b_ada, g_pre_mix, g_post_mix, g_pre_ffn, g_post_ffn, w_in, b_fgate, conv_w, conv_b, conv_ln_g, conv_ln_b, conv_pw_w, conv_pw_b, swa_sinks, sgu_ln_g, sgu_ln_b, sgu_w, sgu_b, g_group, w_out, ffn_w_up, ffn_conv_w, ffn_conv_b, ffn_w_down, loss_target, m_w_ada, m_b_ada, m_g_pre_mix, m_g_post_mix, m_g_pre_ffn, m_g_post_ffn, m_w_in, m_b_fgate, m_conv_w, m_conv_b, m_conv_ln_g, m_conv_ln_b, m_conv_pw_w, m_conv_pw_b, m_swa_sinks, m_sgu_ln_g, m_sgu_ln_b, m_sgu_w, m_sgu_b, m_g_group, m_w_out, m_ffn_w_up, m_ffn_conv_w, m_ffn_conv_b, m_ffn_w_down, v_w_ada, v_b_ada, v_g_pre_mix, v_g_post_mix, v_g_pre_ffn, v_g_post_ffn, v_w_in, v_b_fgate, v_conv_w, v_conv_b, v_conv_ln_g, v_conv_ln_b, v_conv_pw_w, v_conv_pw_b, v_swa_sinks, v_sgu_ln_g, v_sgu_ln_b, v_sgu_w, v_sgu_b, v_g_group, v_w_out, v_ffn_w_up, v_ffn_conv_w, v_ffn_conv_b, v_ffn_w_down):
    args = locals()
    w = {n: args[n] for n in WEIGHTS}
    m = {n: args['m_' + n] for n in WEIGHTS}
    v = {n: args['v_' + n] for n in WEIGHTS}
    xi, yi, ci = _position()
    chip = 2 * xi + yi

    (c_all,) = device_allgather([c], "gather_c")
    c_all = c_all.reshape(N_DEV, D)
    b_loc = lax.dynamic_slice_in_dim(b_ada, chip * ADA_COLS, ADA_COLS, axis=1).reshape(DEPTH, 1, ADA_COLS)
    mod_all = ada_mod(c_all, w_ada, b_loc, "ada_mod")
    mine = lax.dynamic_index_in_dim(mod_all.reshape(DEPTH, N_CHIP, 2, ADA_COLS), ci, axis=2, keepdims=False)
    (mod4,) = chip_alltoall([jnp.transpose(mine, (1, 0, 2))], "scatter_mod")

    w_in_t, m_in_t, v_in_t = (jnp.transpose(a, (2, 0, 1)) for a in (w_in, m_w_in, v_w_in))

    def shard(n, l):
        a = w_in_t[:, l] if n == 'w_in' else w[n][l]
        return a.astype(BF16) if n in _MATMUL_WEIGHTS else a

    keys = [(n, l) for l in range(DEPTH) for n in _CONV_WEIGHTS]
    keys += [(n, l) for l in range(DEPTH) for n in _MATMUL_WEIGHTS]
    srcs = [shard(n, l) for n, l in keys]
    mod4, srcs = lax.optimization_barrier((mod4, srcs))
    lands = [lax.empty((N_CHIP,) + s.shape, s.dtype) for s in srcs]
    flights, token = exchange_start(srcs, lands, GATHER, "gather_start")
    gathering = dict(zip(keys, flights))
    mod = jnp.transpose(mod4, (1, 0, 2)).reshape(DEPTH, 6, 1, D) + token[0:1, 0:1]

    def fetch(l, name, after):
        names = _FETCH_GROUPS[name]
        landed = exchange_wait([gathering[(n, l)] for n in names], GATHER, after, f"gather_wait_l{l}_{name}")
        return {n: _FROM_SHARDS[n](land) for n, (_, land) in zip(names, landed)}

    scattering = {}

    def emit(l, grads):
        names = list(grads)
        pieces = [_TO_SHARDS[n](grads[n]) for n in names]
        lands = [lax.empty(p.shape, p.dtype) for p in pieces]
        started, token = exchange_start(pieces, lands, SCATTER, f"scatter_start_l{l}_{names[0]}")
        scattering.update({(n, l): f for n, f in zip(names, started)})
        return token[0:1, 0:1]

    loss_row, dx, grads, dmods = _local_step(x.reshape(S, D), loss_target.reshape(S, D), mod, w, fetch, emit)
    loss = lax.psum(loss_row[0, 0], ("x", "y", "c"))
    grad_x = dx.reshape(1, S, D)

    small_srcs = pack_small(grads, dmods, "pack_small")
    small_flights, small_token = exchange_start(small_srcs, [lax.empty((N_DEV,) + a.shape, F32) for a in small_srcs], ALL,
                                                "gather_small_start")

    order = list(scattering)
    landed = dict(zip(order, exchange_wait([scattering[k] for k in order], SCATTER, small_token, "scatter_wait")))
    chip1 = chip.astype(jnp.int32).reshape(1)
    part = {'w_in': sum_pieces_t(chip1, [landed[('w_in', l)][1] for l in range(DEPTH)],
                                 [landed[('w_in', l)][0] for l in range(DEPTH)], "sum_w_in")}
    for n in SHARDED[1:]:
        cols = w[n].shape[-1]
        for l in range(DEPTH):
            src, land = landed[(n, l)]
            part[n] = sum_pieces(chip1, land.reshape(N_CHIP, -1, cols), src.reshape(N_CHIP, -1, cols), l, part.get(n),
                                 f"sum_{n}_l{l}")
        part[n] = part[n].reshape(-1, cols)
    parts = [part[n] for n in SHARDED]
    swap_flights, swap_token = exchange_start(parts, [lax.empty(p.shape, F32) for p in parts], SWAP, "swap_start")

    g_all, gw_all = (land for _, land in exchange_wait(small_flights, ALL, swap_token, "gather_small_wait"))
    out = adamw_small(g_all, gw_all, w, m, v, "adamw_small")

    dmod_all = g_all[:, :DEPTH * _LAYER_ROWS].reshape(N_DEV, DEPTH, _LAYER_ROWS * D)[:, :, :6 * D]
    dmod_loc = jnp.transpose(lax.dynamic_slice_in_dim(dmod_all, chip * ADA_COLS, ADA_COLS, axis=2), (1, 0, 2))
    out['w_ada'] = tuple(ada_update(c_all.T, dmod_loc, w_ada, m['w_ada'], v['w_ada'], "adamw_ada"))

    swapped = exchange_wait(swap_flights, SWAP, out['w_ada'][0], "swap_wait")
    part = {n: own for n, (own, _) in zip(SHARDED, swapped)}
    other = {n: got for n, (_, got) in zip(SHARDED, swapped)}
    res = adamw_pair_t(w_in_t, part['w_in'], other['w_in'], m_in_t, v_in_t, "adamw_w_in")
    out['w_in'] = tuple(jnp.transpose(a, (1, 2, 0)) for a in res)
    for n in SHARDED[1:]:
        cols = w[n].shape[-1]
        res = adamw_pair(w[n].reshape(-1, cols), part[n], other[n], m[n].reshape(-1, cols), v[n].reshape(-1, cols),
                         "adamw_" + n)
        out[n] = tuple(a.reshape(w[n].shape) for a in res)

    return (loss, grad_x, *[out[n][0] for n in WEIGHTS], *[out[n][1] for n in WEIGHTS],
            *[out[n][2] for n in WEIGHTS], *[out[n][3] for n in WEIGHTS])
```

```python
import functools

import jax
import jax.numpy as jnp
from jax import lax
from jax.experimental import pallas as pl
from jax.experimental.pallas import tpu as pltpu

F32 = jnp.float32
BF16 = jnp.bfloat16
SDS = jax.ShapeDtypeStruct
MESH = pl.DeviceIdType.MESH
ANY = pl.BlockSpec(memory_space=pl.ANY)

DEPTH = 2
S = 2048
D = 1024
GW = 256
DFF = 2816
NUP = 2 * DFF
IN_COLS = 2308
ZC = 2432
CONV_K = 31
FFN_K = 3
EPS = 1e-6
SCALE = 0.125
NEG = -1e30
N_CHIP = 4
N_DEV = 8

Z_FOX_Q, Z_FOX_K, Z_FOX_V = 0, 256, 512
Z_FG = 768
Z_CONV_A, Z_CONV_G = 896, 1152
Z_SWA_Q, Z_SWA_K, Z_SWA_V = 1408, 1664, 1792
Z_SGU_U, Z_SGU_V = 1920, 2176
FG_END = 772

ADAM_LR, ADAM_B1, ADAM_B2, ADAM_EPS, ADAM_WD, ADAM_STEP = 0.001, 0.9, 0.999, 1e-08, 0.01, 10

TS = 256
TM = 1024
N_SPLIT = 2816
N_BLOCK = 1408

WEIGHTS = ['w_ada', 'b_ada', 'g_pre_mix', 'g_post_mix', 'g_pre_ffn', 'g_post_ffn', 'w_in', 'b_fgate', 'conv_w',
           'conv_b', 'conv_ln_g', 'conv_ln_b', 'conv_pw_w', 'conv_pw_b', 'swa_sinks', 'sgu_ln_g', 'sgu_ln_b',
           'sgu_w', 'sgu_b', 'g_group', 'w_out', 'ffn_w_up', 'ffn_conv_w', 'ffn_conv_b', 'ffn_w_down']
SHARDED = ['w_in', 'conv_w', 'conv_pw_w', 'w_out', 'ffn_w_up', 'ffn_conv_w', 'ffn_w_down']
SMALL = [n for n in WEIGHTS if n not in SHARDED and n != 'w_ada']


def _rms(x, g):
    return x * lax.rsqrt(jnp.mean(x * x, axis=-1, keepdims=True) + EPS) * g


def _modnorm(x, g, sc, sh):
    return _rms(x, g) * (1.0 + sc) + sh


def _resid(x, y, ga, g):
    return x + ga * _rms(y, g)


@functools.partial(jax.custom_vjp, nondiff_argnums=(1,))
def _shift_down(x, n):
    if n == 0:
        return x
    row = lax.broadcasted_iota(jnp.int32, x.shape, 0)
    return jnp.where(row >= n, pltpu.roll(x, n, axis=0), 0.0)


def _shift_up(x, n):
    if n == 0:
        return x
    rows = x.shape[0]
    row = lax.broadcasted_iota(jnp.int32, x.shape, 0)
    return jnp.where(row < rows - n, pltpu.roll(x, rows - n, axis=0), 0.0)


def _shift_down_fwd(x, n):
    return _shift_down(x, n), None


def _shift_down_bwd(n, _, ct):
    return (_shift_up(ct, n),)


_shift_down.defvjp(_shift_down_fwd, _shift_down_bwd)


def _nt(a, b):
    return lax.dot_general(a, b, (((1,), (1,)), ((), ())), preferred_element_type=F32)


def _tn(a, b):
    return lax.dot_general(a, b, (((0,), (0,)), ((), ())), preferred_element_type=F32)


def _nn(a, b):
    return jnp.dot(a, b, preferred_element_type=F32)


def _acc(ref, val, first):
    @pl.when(first)
    def _():
        ref[...] = val

    @pl.when(jnp.logical_not(first))
    def _():
        ref[...] += val


def _row_tile(rows, cols):
    limit = max(8, (1 << 21) // (4 * cols))
    best = None
    for t in range(8, rows + 1, 8):
        if rows % t == 0 and t <= limit:
            best = t
    return best if best is not None else rows


def _ncol(n):
    return n if n <= N_SPLIT else N_BLOCK


def _weight_spec(b, order):
    pick = (lambda j, i: j) if order == 0 else (lambda i, j: j)
    if b.ndim == 3:
        _, k, tn = b.shape
        return pl.BlockSpec((None, k, tn), lambda *g: (pick(*g), 0, 0)), k, N_CHIP * tn, tn
    k, n = b.shape
    tn = _ncol(n)
    return pl.BlockSpec((k, tn), lambda *g: (0, pick(*g))), k, n, tn


def _side_by_side(refs):
    return refs[0][...] if len(refs) == 1 else jnp.concatenate([r[...] for r in refs], axis=1)


def mm_nn(pieces, b, out_dtype, name, tm=TM):
    m = pieces[0].shape[0]
    b_spec, k, n, tn = _weight_spec(b, 0)

    def body(*refs):
        a_refs, b_ref, o_ref = refs[:-2], refs[-2], refs[-1]
        o_ref[...] = _nn(_side_by_side(a_refs), b_ref[...]).astype(out_dtype)

    return pl.pallas_call(
        body, grid=(n // tn, m // tm), name=name,
        in_specs=[pl.BlockSpec((tm, p.shape[1]), lambda j, i: (i, 0)) for p in pieces] + [b_spec],
        out_specs=pl.BlockSpec((tm, tn), lambda j, i: (i, j)),
        out_shape=SDS((m, n), out_dtype),
    )(*pieces, b)


def mm_nt(pieces, b, name):
    m = pieces[0].shape[0]
    k, n = b.shape
    assert n == sum(p.shape[1] for p in pieces) <= N_SPLIT

    def body(*refs):
        a_refs, b_ref, o_ref = refs[:-2], refs[-2], refs[-1]
        o_ref[...] = _nt(_side_by_side(a_refs), b_ref[...])

    return pl.pallas_call(
        body, grid=(m // TM,), name=name,
        in_specs=[pl.BlockSpec((TM, p.shape[1]), lambda i: (i, 0)) for p in pieces] + [pl.BlockSpec((k, n), lambda i: (0, 0))],
        out_specs=pl.BlockSpec((TM, k), lambda i: (i, 0)),
        out_shape=SDS((m, k), F32),
    )(*pieces, b)


def mm_nt_halves(a0, a1, b4, name):
    m = a0.shape[0]
    _, k, tc = b4.shape

    def body(a0_ref, a1_ref, b_ref, o_ref):
        c = pl.program_id(1)

        @pl.when(c == 0)
        def _():
            o_ref[...] = _nt(a0_ref[...], b_ref[...])

        @pl.when(c == 1)
        def _():
            o_ref[...] += _nt(a0_ref[...], b_ref[...])

        @pl.when(c >= 2)
        def _():
            o_ref[...] += _nt(a1_ref[...], b_ref[...])

    return pl.pallas_call(
        body, grid=(m // TM, N_CHIP), name=name,
        in_specs=[pl.BlockSpec((TM, tc), lambda i, c: (i, jnp.minimum(c, 1))),
                  pl.BlockSpec((TM, tc), lambda i, c: (i, jnp.maximum(c - 2, 0))),
                  pl.BlockSpec((None, k, tc), lambda i, c: (c, 0, 0))],
        out_specs=pl.BlockSpec((TM, k), lambda i, c: (i, 0)),
        out_shape=SDS((m, k), F32),
    )(a0, a1, b4)


def mm_tn(a_pieces, b_pieces, name):
    m = a_pieces[0].shape[0]
    k = sum(p.shape[1] for p in a_pieces)
    n = sum(p.shape[1] for p in b_pieces)
    assert n <= N_SPLIT
    n_a = len(a_pieces)

    def body(*refs):
        a_refs, b_refs, o_ref = refs[:n_a], refs[n_a:-1], refs[-1]
        o_ref[...] = _tn(_side_by_side(a_refs), _side_by_side(b_refs)).astype(BF16)

    return pl.pallas_call(
        body, grid=(1,), name=name,
        in_specs=[pl.BlockSpec((m, p.shape[1]), lambda i: (0, 0)) for p in list(a_pieces) + list(b_pieces)],
        out_specs=pl.BlockSpec((k, n), lambda i: (0, 0)), out_shape=SDS((k, n), BF16),
    )(*a_pieces, *b_pieces)


def mm_tn_halves(a, b0, b1, name):
    m, k = a.shape
    tn = b0.shape[1] // 2

    def body(a_ref, b0_ref, b1_ref, o_ref):
        j = pl.program_id(0)

        @pl.when(j < 2)
        def _():
            o_ref[...] = _tn(a_ref[...], b0_ref[...]).astype(BF16)

        @pl.when(j >= 2)
        def _():
            o_ref[...] = _tn(a_ref[...], b1_ref[...]).astype(BF16)

    return pl.pallas_call(
        body, grid=(N_CHIP,), name=name,
        in_specs=[pl.BlockSpec((m, k), lambda j: (0, 0)),
                  pl.BlockSpec((m, tn), lambda j: (0, jnp.minimum(j, 1))),
                  pl.BlockSpec((m, tn), lambda j: (0, jnp.maximum(j - 2, 0)))],
        out_specs=pl.BlockSpec((None, k, tn), lambda j: (j, 0, 0)), out_shape=SDS((N_CHIP, k, tn), BF16),
    )(a, b0, b1)


_ROW = pl.BlockSpec((TS, D), lambda i: (i, 0))
_VEC = pl.BlockSpec((1, D), lambda i: (0, 0))


def modnorm_fwd(x, g, sc, sh, name):
    def body(x_ref, g_ref, sc_ref, sh_ref, o_ref):
        o_ref[...] = _modnorm(x_ref[...], g_ref[...], sc_ref[...], sh_ref[...]).astype(BF16)

    return pl.pallas_call(body, grid=(S // TS,), name=name, in_specs=[_ROW, _VEC, _VEC, _VEC], out_specs=_ROW,
                          out_shape=SDS((S, D), BF16))(x, g, sc, sh)


def modnorm_bwd(x, g, sc, sh, dh, dx_in, name):
    def body(x_ref, g_ref, sc_ref, sh_ref, dh_ref, dxin_ref, dx_ref, dg_ref, dsc_ref, dsh_ref):
        _, vjp = jax.vjp(_modnorm, x_ref[...], g_ref[...], sc_ref[...], sh_ref[...])
        dx, dg, dsc, dsh = vjp(dh_ref[...])
        dx_ref[...] = dxin_ref[...] + dx
        first = pl.program_id(0) == 0
        _acc(dg_ref, dg, first)
        _acc(dsc_ref, dsc, first)
        _acc(dsh_ref, dsh, first)

    vec = SDS((1, D), F32)
    return pl.pallas_call(body, grid=(S // TS,), name=name, in_specs=[_ROW, _VEC, _VEC, _VEC, _ROW, _ROW],
                          out_specs=[_ROW, _VEC, _VEC, _VEC], out_shape=[SDS((S, D), F32), vec, vec, vec])(
                              x, g, sc, sh, dh, dx_in)


def resid_fwd(x, y, ga, g, name):
    def body(x_ref, y_ref, ga_ref, g_ref, o_ref):
        o_ref[...] = _resid(x_ref[...], y_ref[...], ga_ref[...], g_ref[...])

    return pl.pallas_call(body, grid=(S // TS,), name=name, in_specs=[_ROW, _ROW, _VEC, _VEC], out_specs=_ROW,
                          out_shape=SDS((S, D), F32))(x, y, ga, g)


def resid_bwd(y, ga, g, dxo, name):
    def body(y_ref, ga_ref, g_ref, dxo_ref, dy_ref, dga_ref, dg_ref):
        _, vjp = jax.vjp(_gated_norm, y_ref[...], ga_ref[...], g_ref[...])
        dy, dga, dg = vjp(dxo_ref[...])
        dy_ref[...] = dy.astype(BF16)
        first = pl.program_id(0) == 0
        _acc(dga_ref, dga, first)
        _acc(dg_ref, dg, first)

    vec = SDS((1, D), F32)
    return pl.pallas_call(body, grid=(S // TS,), name=name, in_specs=[_ROW, _VEC, _VEC, _ROW],
                          out_specs=[_ROW, _VEC, _VEC], out_shape=[SDS((S, D), BF16), vec, vec])(y, ga, g, dxo)


def _gated_norm(y, ga, g):
    return ga * _rms(y, g)


def resid_modnorm_fwd(x, y, ga, g_post, g_pre, sc, sh, name):
    def body(x_ref, y_ref, ga_ref, gp_ref, g_ref, sc_ref, sh_ref, xo_ref, h_ref):
        xn = _resid(x_ref[...], y_ref[...], ga_ref[...], gp_ref[...])
        xo_ref[...] = xn
        h_ref[...] = _modnorm(xn, g_ref[...], sc_ref[...], sh_ref[...]).astype(BF16)

    return pl.pallas_call(body, grid=(S // TS,), name=name, in_specs=[_ROW, _ROW] + [_VEC] * 5, out_specs=[_ROW, _ROW],
                          out_shape=[SDS((S, D), F32), SDS((S, D), BF16)])(x, y, ga, g_post, g_pre, sc, sh)


def modnorm_resid_bwd(x, g, sc, sh, dh, dx_in, y, ga, g_post, name):
    def body(x_ref, g_ref, sc_ref, sh_ref, dh_ref, dxin_ref, y_ref, ga_ref, gp_ref,
             dx_ref, dg_ref, dsc_ref, dsh_ref, dy_ref, dga_ref, dgp_ref):
        _, vjp = jax.vjp(_modnorm, x_ref[...], g_ref[...], sc_ref[...], sh_ref[...])
        dx, dg, dsc, dsh = vjp(dh_ref[...])
        dx = dxin_ref[...] + dx
        dx_ref[...] = dx
        _, vjp = jax.vjp(_gated_norm, y_ref[...], ga_ref[...], gp_ref[...])
        dy, dga, dgp = vjp(dx)
        dy_ref[...] = dy.astype(BF16)
        first = pl.program_id(0) == 0
        for ref, val in ((dg_ref, dg), (dsc_ref, dsc), (dsh_ref, dsh), (dga_ref, dga), (dgp_ref, dgp)):
            _acc(ref, val, first)

    vec = SDS((1, D), F32)
    return pl.pallas_call(body, grid=(S // TS,), name=name,
                          in_specs=[_ROW, _VEC, _VEC, _VEC, _ROW, _ROW, _ROW, _VEC, _VEC],
                          out_specs=[_ROW, _VEC, _VEC, _VEC, _ROW, _VEC, _VEC],
                          out_shape=[SDS((S, D), F32), vec, vec, vec, SDS((S, D), BF16), vec, vec])(
                              x, g, sc, sh, dh, dx_in, y, ga, g_post)


def loss_grad(xf, target, name):
    def body(x_ref, t_ref, dx_ref, l_ref):
        err = x_ref[...] - t_ref[...]
        dx_ref[...] = err * (1.0 / D)
        part = 0.5 * jnp.sum(jnp.mean(err * err, axis=-1, keepdims=True), axis=0, keepdims=True)
        _acc(l_ref, jnp.broadcast_to(part, (1, 128)), pl.program_id(0) == 0)

    return pl.pallas_call(body, grid=(S // TS,), name=name, in_specs=[_ROW, _ROW],
                          out_specs=[_ROW, pl.BlockSpec((1, 128), lambda i: (0, 0))],
                          out_shape=[SDS((S, D), F32), SDS((1, 128), F32)])(xf, target)


_FG_SPEC = pl.BlockSpec((S, 128), lambda i: (0, Z_FG // 128))


def _tri128(lower):
    r = lax.broadcasted_iota(jnp.int32, (128, 128), 0)
    c = lax.broadcasted_iota(jnp.int32, (128, 128), 1)
    return ((r >= c) if lower else (r <= c)).astype(F32)


def fgate_fwd(z, bf, name):
    def body(z_ref, b_ref, cc_ref, cr_ref):
        tri = _tri128(True)
        carry = jnp.zeros((1, 128), F32)
        for i in range(S // 128):
            rows = pl.ds(i * 128, 128)
            lf = jax.nn.log_sigmoid(z_ref[rows, :] + b_ref[...])
            c = jnp.dot(tri, lf, precision=lax.Precision.HIGHEST, preferred_element_type=F32) + carry
            cc_ref[rows, :] = c
            carry = c[127:128, :]
        cr_ref[...] = cc_ref[...].T

    return pl.pallas_call(body, name=name, grid=(1,),
                          in_specs=[_FG_SPEC, pl.BlockSpec((1, 128), lambda i: (0, 0))],
                          out_specs=[pl.BlockSpec((S, 128), lambda i: (0, 0)), pl.BlockSpec((128, S), lambda i: (0, 0))],
                          out_shape=[SDS((S, 128), F32), SDS((128, S), F32)])(z, bf)


def fgate_bwd(z, bf, dcq, dck, name):
    def body(z_ref, b_ref, dcq_ref, dck_ref, dz_ref, db_ref, col_ref):
        col_ref[...] = dcq_ref[...] + jnp.concatenate([dck_ref[...], jnp.zeros((120, S), F32)], axis=0).T
        tri = _tri128(False)
        carry = jnp.zeros((1, 128), F32)
        db = jnp.zeros((1, 128), F32)
        for i in reversed(range(S // 128)):
            rows = pl.ds(i * 128, 128)
            dlf = jnp.dot(tri, col_ref[rows, :], precision=lax.Precision.HIGHEST, preferred_element_type=F32) + carry
            carry = dlf[0:1, :]
            dz = dlf * jax.nn.sigmoid(-(z_ref[rows, :] + b_ref[...]))
            dz_ref[rows, :] = dz.astype(BF16)
            db = db + jnp.sum(dz, axis=0, keepdims=True)
        db_ref[...] = db

    return pl.pallas_call(body, name=name, grid=(1,),
                          in_specs=[_FG_SPEC, pl.BlockSpec((1, 128), lambda i: (0, 0)),
                                    pl.BlockSpec((S, 128), lambda i: (0, 0)), pl.BlockSpec((8, S), lambda i: (0, 0))],
                          out_specs=[pl.BlockSpec((S, 128), lambda i: (0, 0)), pl.BlockSpec((1, 128), lambda i: (0, 0))],
                          out_shape=[SDS((S, 128), BF16), SDS((1, 128), F32)],
                          scratch_shapes=[pltpu.VMEM((S, 128), F32)])(z, bf, dcq, dck)


TQ = 256


def _head_mask(hh):
    lane = lax.broadcasted_iota(jnp.int32, (TQ, 128), 1)
    return (lane >= 64 * hh) & (lane < 64 * hh + 64)


def _fox_specs():
    q = pl.BlockSpec((TQ, 256), lambda i: (i, Z_FOX_Q // 256))
    k = pl.BlockSpec((S, 256), lambda i: (0, Z_FOX_K // 256))
    v = pl.BlockSpec((S, 256), lambda i: (0, Z_FOX_V // 256))
    cc = pl.BlockSpec((TQ, 128), lambda i: (i, 0))
    cr = pl.BlockSpec((8, S), lambda i: (0, 0))
    return q, k, v, cc, cr


FOX_SPAN = 2
FOX_GROUPS = S // (FOX_SPAN * TQ)


def _fox_scores(qm, k, cc_h, cr_h, i):
    klen = k.shape[0]
    s = _nt(qm, k) + cc_h - cr_h
    qpos = i * TQ + lax.broadcasted_iota(jnp.int32, (TQ, klen), 0)
    kpos = lax.broadcasted_iota(jnp.int32, (TQ, klen), 1)
    return jnp.where(kpos <= qpos, s, NEG)


def _for_key_length(i, fn):
    for g in range(FOX_GROUPS):
        pl.when(i // FOX_SPAN == g)(functools.partial(fn, (g + 1) * FOX_SPAN * TQ))


def fox_fwd(z, cumc, cumr, name):
    def body(q_ref, k_ref, v_ref, cc_ref, cr_ref, o_ref, l_ref):
        i = pl.program_id(0)

        def block(klen):
            lane = lax.broadcasted_iota(jnp.int32, (TQ, 128), 1)
            cc = cc_ref[...]
            lse = jnp.zeros((TQ, 128), F32)
            for p in range(2):
                cols = pl.ds(128 * p, 128)
                q = q_ref[:, cols]
                k = k_ref[0:klen, cols].astype(BF16)
                v = v_ref[0:klen, cols].astype(BF16)
                o_pair = jnp.zeros((TQ, 128), F32)
                for hh in range(2):
                    h = 2 * p + hh
                    hm = _head_mask(hh)
                    qm = jnp.where(hm, q * SCALE, 0.0).astype(BF16)
                    s = _fox_scores(qm, k, cc[:, h:h + 1], cr_ref[h:h + 1, 0:klen], i)
                    m = jnp.max(s, axis=1, keepdims=True)
                    e = jnp.exp(s - m)
                    l = jnp.sum(e, axis=1, keepdims=True)
                    o_pair = jnp.where(hm, _nn(e.astype(BF16), v) / l, o_pair)
                    lse = jnp.where(lane == h, m + jnp.log(l), lse)
                o_ref[:, cols] = o_pair
            l_ref[...] = lse

        _for_key_length(i, block)

    q, k, v, cc, cr = _fox_specs()
    return pl.pallas_call(body, grid=(S // TQ,), name=name, in_specs=[q, k, v, cc, cr],
                          out_specs=[pl.BlockSpec((TQ, 256), lambda i: (i, 0)), cc],
                          out_shape=[SDS((S, 256), F32), SDS((S, 128), F32)])(z, z, z, cumc, cumr)


def fox_bwd(z, cumc, cumr, lse, o, do, name):
    steps = S // TQ

    def body(q_ref, k_ref, v_ref, cc_ref, cr_ref, l_ref, o_ref, do_ref, dq_ref, dk_ref, dv_ref, dcq_ref, dck_ref,
             dk_acc, dv_acc):
        i = pl.program_id(0)

        @pl.when(i == 0)
        def _():
            dk_acc[...] = jnp.zeros_like(dk_acc)
            dv_acc[...] = jnp.zeros_like(dv_acc)
            dck_ref[...] = jnp.zeros_like(dck_ref)

        def block(klen):
            lane = lax.broadcasted_iota(jnp.int32, (TQ, 128), 1)
            cc = cc_ref[...]
            lse_all = l_ref[...]
            dcq = jnp.zeros((TQ, 128), F32)
            for p in range(2):
                cols = pl.ds(128 * p, 128)
                q = q_ref[:, cols]
                k = k_ref[0:klen, cols].astype(BF16)
                v = v_ref[0:klen, cols].astype(BF16)
                o_p = o_ref[:, cols]
                do_p = do_ref[:, cols]
                dq_pair = jnp.zeros((TQ, 128), F32)
                dk_pair = jnp.zeros((klen, 128), F32)
                dv_pair = jnp.zeros((klen, 128), F32)
                for hh in range(2):
                    h = 2 * p + hh
                    hm = _head_mask(hh)
                    qm = jnp.where(hm, q * SCALE, 0.0).astype(BF16)
                    s = _fox_scores(qm, k, cc[:, h:h + 1], cr_ref[h:h + 1, 0:klen], i)
                    pn = jnp.exp(s - lse_all[:, h:h + 1])
                    dom = jnp.where(hm, do_p, 0.0)
                    dl = jnp.sum(dom * o_p, axis=1, keepdims=True)
                    dom = dom.astype(BF16)
                    ds = pn * (_nt(dom, v) - dl)
                    dsb = ds.astype(BF16)
                    dq_pair = jnp.where(hm, _nn(dsb, k) * SCALE, dq_pair)
                    dk_pair = dk_pair + _tn(dsb, qm)
                    dv_pair = dv_pair + _tn(pn.astype(BF16), dom)
                    dck_ref[h:h + 1, 0:klen] -= jnp.sum(ds, axis=0, keepdims=True)
                    dcq = jnp.where(lane == h, jnp.sum(ds, axis=1, keepdims=True), dcq)
                dq_ref[:, cols] = dq_pair.astype(BF16)
                dk_acc[0:klen, cols] += dk_pair
                dv_acc[0:klen, cols] += dv_pair
            dcq_ref[...] = dcq

        _for_key_length(i, block)

        @pl.when(i == steps - 1)
        def _():
            dk_ref[...] = dk_acc[...].astype(BF16)
            dv_ref[...] = dv_acc[...].astype(BF16)

    q, k, v, cc, cr = _fox_specs()
    blk = pl.BlockSpec((TQ, 256), lambda i: (i, 0))
    full = pl.BlockSpec((S, 256), lambda i: (0, 0))
    return pl.pallas_call(body, grid=(steps,), name=name, in_specs=[q, k, v, cc, cr, cc, blk, blk],
                          out_specs=[blk, full, full, cc, cr],
                          out_shape=[SDS((S, 256), BF16), SDS((S, 256), BF16), SDS((S, 256), BF16), SDS((S, 128), F32),
                                     SDS((8, S), F32)],
                          scratch_shapes=[pltpu.VMEM((S, 256), F32), pltpu.VMEM((S, 256), F32)])(
                              z, z, z, cumc, cumr, lse, o, do)


W = 128
SWA_HEADS = 4


def _swa_core(first, qs, kcat, vcat, sink):
    r = lax.broadcasted_iota(jnp.int32, (SWA_HEADS * W, 2 * W), 0)
    j = lax.broadcasted_iota(jnp.int32, (SWA_HEADS * W, 2 * W), 1)
    qi = r & (W - 1)
    valid = ((j < W) & (j > qi) & jnp.logical_not(first)) | ((j >= W) & (j - W <= qi))
    s = jnp.where(valid, _nt((qs * SCALE).astype(BF16), kcat.astype(BF16)), NEG)
    m = lax.stop_gradient(jnp.maximum(jnp.max(s, axis=1, keepdims=True), sink))
    e = jnp.exp(s - m)
    den = jnp.sum(e, axis=1, keepdims=True) + jnp.exp(sink - m)
    return _nn((e / den).astype(BF16), vcat.astype(BF16))


def _kv_lanes(kv):
    lane = lax.broadcasted_iota(jnp.int32, (W, 128), 1)
    return (lane >= 64 * kv) & (lane < 64 * kv + 64)


def _swa_stack(pair0, pair1):
    blocks = []
    for h in range(SWA_HEADS):
        kv, hh = h // 2, h % 2
        a = (pair0, pair1)[kv]
        a = a if hh == kv else pltpu.roll(a, 64, axis=1)
        blocks.append(jnp.where(_kv_lanes(kv), a, 0.0))
    return jnp.concatenate(blocks, axis=0)


def _swa_unstack(stacked):
    pairs = [jnp.zeros((W, 128), F32), jnp.zeros((W, 128), F32)]
    for h in range(SWA_HEADS):
        kv, hh = h // 2, h % 2
        a = jnp.where(_kv_lanes(kv), stacked[h * W:(h + 1) * W], 0.0)
        pairs[kv] = pairs[kv] + (a if hh == kv else pltpu.roll(a, 64, axis=1))
    return pairs


def _head_rows():
    return lax.broadcasted_iota(jnp.int32, (SWA_HEADS * W, 1), 0) // W


SWA_PAIR = 2 * W


def _swa_operands(sub, q0_ref, q1_ref, kp_ref, kc_ref, vp_ref, vc_ref, sk_ref):
    rows = pl.ds(sub * W, W)
    qs = _swa_stack(q0_ref[rows, :], q1_ref[rows, :])
    k_prev, v_prev = (kp_ref[...], vp_ref[...]) if sub == 0 else (kc_ref[0:W, :], vc_ref[0:W, :])
    kcat = jnp.concatenate([k_prev, kc_ref[rows, :]], axis=0)
    vcat = jnp.concatenate([v_prev, vc_ref[rows, :]], axis=0)
    sink = jnp.zeros((SWA_HEADS * W, 1), F32)
    for h in range(SWA_HEADS):
        sink = jnp.where(_head_rows() == h, sk_ref[:, h:h + 1], sink)
    return qs, kcat, vcat, sink


def _swa_specs(blocks):
    rows = blocks * W
    q0 = pl.BlockSpec((rows, 128), lambda i: (i, Z_SWA_Q // 128))
    q1 = pl.BlockSpec((rows, 128), lambda i: (i, Z_SWA_Q // 128 + 1))
    kc = pl.BlockSpec((rows, 128), lambda i: (i, Z_SWA_K // 128))
    kp = pl.BlockSpec((W, 128), lambda i: (jnp.maximum(blocks * i - 1, 0), Z_SWA_K // 128))
    vc = pl.BlockSpec((rows, 128), lambda i: (i, Z_SWA_V // 128))
    vp = pl.BlockSpec((W, 128), lambda i: (jnp.maximum(blocks * i - 1, 0), Z_SWA_V // 128))
    sk = pl.BlockSpec((1, 128), lambda i: (0, 0))
    return q0, q1, kp, kc, vp, vc, sk


def swa_fwd(z, sinks, name):
    def body(*refs):
        o_ref = refs[-1]
        o0, o1 = _swa_unstack(_swa_core(pl.program_id(0) == 0, *_swa_operands(0, *refs[:-1])))
        o_ref[:, 0:128] = o0
        o_ref[:, 128:256] = o1

    return pl.pallas_call(body, grid=(S // W,), name=name, in_specs=list(_swa_specs(1)),
                          out_specs=pl.BlockSpec((W, 256), lambda i: (i, 0)),
                          out_shape=SDS((S, 256), F32))(z, z, z, z, z, z, sinks)


def swa_bwd(z, sinks, do, name):
    steps = S // SWA_PAIR

    def body(*refs):
        do_ref, dq_ref, dk_ref, dv_ref, dsk_ref, dk_acc, dv_acc = refs[7:]
        i = pl.program_id(0)

        @pl.when(i == 0)
        def _():
            dk_acc[...] = jnp.zeros_like(dk_acc)
            dv_acc[...] = jnp.zeros_like(dv_acc)

        lane = lax.broadcasted_iota(jnp.int32, (1, 128), 1)
        dsk = jnp.zeros((1, 128), F32)
        for sub in range(2):
            rows = pl.ds(sub * W, W)
            first = (i == 0) if sub == 0 else False
            _, vjp = jax.vjp(functools.partial(_swa_core, first), *_swa_operands(sub, *refs[:7]))
            dqs, dkcat, dvcat, dsink = vjp(_swa_stack(do_ref[rows, 0:128], do_ref[rows, 128:256]))
            dq0, dq1 = _swa_unstack(dqs)
            dq_ref[rows, 0:128] = dq0.astype(BF16)
            dq_ref[rows, 128:256] = dq1.astype(BF16)
            cur = pl.ds(pl.multiple_of((2 * i + sub) * W, W), W)
            dk_acc[cur, :] += dkcat[W:2 * W]
            dv_acc[cur, :] += dvcat[W:2 * W]

            def to_prev(dkcat=dkcat, dvcat=dvcat, sub=sub):
                prev = pl.ds(pl.multiple_of((2 * i + sub - 1) * W, W), W)
                dk_acc[prev, :] += dkcat[0:W]
                dv_acc[prev, :] += dvcat[0:W]

            if sub == 0:
                pl.when(i > 0)(to_prev)
            else:
                to_prev()
            for h in range(SWA_HEADS):
                d = jnp.sum(jnp.where(_head_rows() == h, dsink, 0.0), axis=0, keepdims=True)
                dsk = dsk + jnp.where(lane == h, d, 0.0)
        _acc(dsk_ref, dsk, i == 0)

        @pl.when(i == steps - 1)
        def _():
            dk_ref[...] = dk_acc[...].astype(BF16)
            dv_ref[...] = dv_acc[...].astype(BF16)

    blk = pl.BlockSpec((SWA_PAIR, 256), lambda i: (i, 0))
    full = pl.BlockSpec((S, 128), lambda i: (0, 0))
    return pl.pallas_call(body, grid=(steps,), name=name, in_specs=list(_swa_specs(2)) + [blk],
                          out_specs=[blk, full, full, pl.BlockSpec((1, 128), lambda n: (0, 0))],
                          out_shape=[SDS((S, 256), BF16), SDS((S, 128), BF16), SDS((S, 128), BF16), SDS((1, 128), F32)],
                          scratch_shapes=[pltpu.VMEM((S, 128), F32), pltpu.VMEM((S, 128), F32)])(
                              z, z, z, z, z, z, sinks, do)


def _glu(a, g):
    return a * jax.nn.sigmoid(g)


def _cv1_specs():
    a = pl.BlockSpec((S, 128), lambda j: (0, Z_CONV_A // 128 + j))
    g = pl.BlockSpec((S, 128), lambda j: (0, Z_CONV_G // 128 + j))
    w = pl.BlockSpec((32, 128), lambda j: (0, j))
    b = pl.BlockSpec((1, 128), lambda j: (0, j))
    h = pl.BlockSpec((S, 128), lambda j: (0, j))
    return a, g, w, b, h


def conv_dw_fwd(z, cw, cb, name):
    def body(a_ref, g_ref, w_ref, b_ref, o_ref):
        hh = _glu(a_ref[...], g_ref[...])
        acc = jnp.zeros((S, 128), F32) + b_ref[...]
        for k in range(CONV_K):
            acc = acc + _shift_down(hh, CONV_K - 1 - k) * w_ref[k:k + 1, :]
        o_ref[...] = acc

    a, g, w, b, h = _cv1_specs()
    return pl.pallas_call(body, grid=(2,), name=name, in_specs=[a, g, w, b], out_specs=h,
                          out_shape=SDS((S, 256), F32))(z, z, cw, cb)


def conv_dw_bwd(z, cw, dhc, name):
    def body(a_ref, g_ref, w_ref, dh_ref, da_ref, dg_ref, dw_ref, db_ref):
        hh, vjp = jax.vjp(_glu, a_ref[...], g_ref[...])
        dh = dh_ref[...]
        dhh = jnp.zeros((S, 128), F32)
        for k in range(CONV_K):
            n = CONV_K - 1 - k
            dhh = dhh + _shift_up(dh, n) * w_ref[k:k + 1, :]
            dw_ref[k:k + 1, :] = jnp.sum(dh * _shift_down(hh, n), axis=0, keepdims=True)
        dw_ref[CONV_K:32, :] = jnp.zeros((32 - CONV_K, 128), F32)
        db_ref[...] = jnp.sum(dh, axis=0, keepdims=True)
        da, dg = vjp(dhh)
        da_ref[...] = da.astype(BF16)
        dg_ref[...] = dg.astype(BF16)

    a, g, w, b, h = _cv1_specs()
    return pl.pallas_call(body, grid=(2,), name=name, in_specs=[a, g, w, h], out_specs=[h, h, w, b],
                          out_shape=[SDS((S, 256), BF16), SDS((S, 256), BF16), SDS((32, 256), F32), SDS((1, 256), F32)])(
                              z, z, cw, dhc)


def _ln(x, g, b):
    mu = jnp.mean(x, axis=-1, keepdims=True)
    xc = x - mu
    var = jnp.mean(xc * xc, axis=-1, keepdims=True)
    return xc * lax.rsqrt(var + EPS) * g + b


def _conv_pw(hc, lg, lb, pw, pb):
    y = jax.nn.silu(_ln(hc, lg, lb))
    return _nn(y.astype(BF16), pw.astype(BF16)) + pb


TS2 = 512
_ROW2 = pl.BlockSpec((TS2, 256), lambda i: (i, 0))
_VEC2 = pl.BlockSpec((1, 256), lambda i: (0, 0))
_MAT2 = pl.BlockSpec((256, 256), lambda i: (0, 0))


def conv_pw_fwd(hc, lg, lb, pw, pb, name):
    def body(h_ref, lg_ref, lb_ref, pw_ref, pb_ref, o_ref):
        o_ref[...] = _conv_pw(h_ref[...], lg_ref[...], lb_ref[...], pw_ref[...], pb_ref[...])

    return pl.pallas_call(body, grid=(S // TS2,), name=name, in_specs=[_ROW2, _VEC2, _VEC2, _MAT2, _VEC2],
                          out_specs=_ROW2, out_shape=SDS((S, 256), F32))(hc, lg, lb, pw, pb)


def conv_pw_bwd(hc, lg, lb, pw, pb, dy, name):
    def body(h_ref, lg_ref, lb_ref, pw_ref, pb_ref, dy_ref, dh_ref, dlg_ref, dlb_ref, dpw_ref, dpb_ref):
        _, vjp = jax.vjp(_conv_pw, h_ref[...], lg_ref[...], lb_ref[...], pw_ref[...], pb_ref[...])
        dh, dlg, dlb, dpw, dpb = vjp(dy_ref[...])
        dh_ref[...] = dh
        first = pl.program_id(0) == 0
        _acc(dlg_ref, dlg, first)
        _acc(dlb_ref, dlb, first)
        _acc(dpw_ref, dpw, first)
        _acc(dpb_ref, dpb, first)

    vec = SDS((1, 256), F32)
    return pl.pallas_call(body, grid=(S // TS2,), name=name, in_specs=[_ROW2, _VEC2, _VEC2, _MAT2, _VEC2, _ROW2],
                          out_specs=[_ROW2, _VEC2, _VEC2, _MAT2, _VEC2],
                          out_shape=[SDS((S, 256), F32), vec, vec, SDS((256, 256), F32), vec])(hc, lg, lb, pw, pb, dy)


def _sgu_block(u0, u1, v0, v1, lg0, lg1, lb0, lb1, w0, w1, w2, w3, bt):
    u0, u1, v0, v1 = (jax.nn.gelu(a) for a in (u0, u1, v0, v1))
    mu = (jnp.sum(v0, axis=1, keepdims=True) + jnp.sum(v1, axis=1, keepdims=True)) * (1.0 / GW)
    c0, c1 = v0 - mu, v1 - mu
    var = (jnp.sum(c0 * c0, axis=1, keepdims=True) + jnp.sum(c1 * c1, axis=1, keepdims=True)) * (1.0 / GW)
    r = lax.rsqrt(var + EPS)
    n0 = c0 * r * lg0 + lb0
    n1 = c1 * r * lg1 + lb1
    row = lax.broadcasted_iota(jnp.int32, (128, 128), 0)
    col = lax.broadcasted_iota(jnp.int32, (128, 128), 1)
    tri = row >= col
    outs = []
    for p, (n, u, wa, wb) in enumerate(((n0, u0, w0, w1), (n1, u1, w2, w3))):
        nb = n.astype(BF16)
        ma = _nn(jnp.where(tri, wa, 0.0).astype(BF16), nb)
        mb = _nn(jnp.where(tri, wb, 0.0).astype(BF16), nb)
        expand = (row == 2 * p + col // 64).astype(F32)
        bias = jnp.dot(bt, expand, precision=lax.Precision.HIGHEST, preferred_element_type=F32)
        outs.append(u * (jnp.where(col < 64, ma, mb) + bias))
    return outs[0], outs[1]


def _sgu_specs():
    def col(c):
        return pl.BlockSpec((128, 128), lambda n, c=c: (n, c))
    zs = [col(Z_SGU_U // 128), col(Z_SGU_U // 128 + 1), col(Z_SGU_V // 128), col(Z_SGU_V // 128 + 1)]
    vec = [pl.BlockSpec((1, 128), lambda n: (0, 0)), pl.BlockSpec((1, 128), lambda n: (0, 1))]
    ws = [pl.BlockSpec((None, 128, 128), lambda n, g=g: (g, 0, 0)) for g in range(4)]
    bt = pl.BlockSpec((128, 128), lambda n: (0, 0))
    return zs + vec + vec + ws + [bt]


def sgu_fwd(z, lg, lb, w, bt, name):
    def body(*refs):
        o_ref = refs[-1]
        y0, y1 = _sgu_block(*[r[...] for r in refs[:-1]])
        o_ref[:, 0:128] = y0
        o_ref[:, 128:256] = y1

    return pl.pallas_call(body, grid=(S // 128,), name=name, in_specs=_sgu_specs(),
                          out_specs=pl.BlockSpec((128, 256), lambda n: (n, 0)),
                          out_shape=SDS((S, 256), F32))(z, z, z, z, lg, lg, lb, lb, w, w, w, w, bt)


def sgu_bwd(z, lg, lb, w, bt, dy, name):
    def body(*refs):
        ins, dy_ref = refs[:13], refs[13]
        du_ref, dv_ref, dlg_ref, dlb_ref, dw_ref, dbt_ref = refs[14:]
        _, vjp = jax.vjp(_sgu_block, *[r[...] for r in ins])
        du0, du1, dv0, dv1, dlg0, dlg1, dlb0, dlb1, dw0, dw1, dw2, dw3, dbt = vjp((dy_ref[:, 0:128], dy_ref[:, 128:256]))
        du_ref[:, 0:128] = du0.astype(BF16)
        du_ref[:, 128:256] = du1.astype(BF16)
        dv_ref[:, 0:128] = dv0.astype(BF16)
        dv_ref[:, 128:256] = dv1.astype(BF16)
        first = pl.program_id(0) == 0

        @pl.when(first)
        def _():
            dlg_ref[...] = jnp.zeros_like(dlg_ref)
            dlb_ref[...] = jnp.zeros_like(dlb_ref)
            dw_ref[...] = jnp.zeros_like(dw_ref)
            dbt_ref[...] = jnp.zeros_like(dbt_ref)

        dlg_ref[:, 0:128] += dlg0
        dlg_ref[:, 128:256] += dlg1
        dlb_ref[:, 0:128] += dlb0
        dlb_ref[:, 128:256] += dlb1
        for g, d in enumerate((dw0, dw1, dw2, dw3)):
            dw_ref[g] += d
        dbt_ref[...] += dbt

    blk = pl.BlockSpec((128, 256), lambda n: (n, 0))
    vec = pl.BlockSpec((1, 256), lambda n: (0, 0))
    return pl.pallas_call(body, grid=(S // 128,), name=name, in_specs=_sgu_specs() + [blk],
                          out_specs=[blk, blk, vec, vec, pl.BlockSpec((4, 128, 128), lambda n: (0, 0, 0)),
                                     pl.BlockSpec((128, 128), lambda n: (0, 0))],
                          out_shape=[SDS((S, 256), BF16), SDS((S, 256), BF16), SDS((1, 256), F32), SDS((1, 256), F32),
                                     SDS((4, 128, 128), F32), SDS((128, 128), F32)])(
                                         z, z, z, z, lg, lg, lb, lb, w, w, w, w, bt, dy)


def _group_norm(y0, y1, y2, y3, g0, g1, g2, g3):
    return tuple(_rms(y, g) for y, g in zip((y0, y1, y2, y3), (g0, g1, g2, g3)))


_GROW = pl.BlockSpec((TS2, 256), lambda i: (i, 0))
_GCAT = pl.BlockSpec((TS2, D), lambda i: (i, 0))
_GVEC = [pl.BlockSpec((1, 256), lambda i, j=j: (0, j)) for j in range(4)]


def group_norm_fwd(ys, gg, name):
    def body(*refs):
        o_ref = refs[-1]
        outs = _group_norm(*[r[...] for r in refs[:-1]])
        for j, c in enumerate(outs):
            o_ref[:, 256 * j:256 * (j + 1)] = c.astype(BF16)

    return pl.pallas_call(body, grid=(S // TS2,), name=name, in_specs=[_GROW] * 4 + _GVEC, out_specs=_GCAT,
                          out_shape=SDS((S, D), BF16))(*ys, gg, gg, gg, gg)


def group_norm_bwd(ys, gg, dcat, name):
    def body(*refs):
        ins, dc_ref = refs[:8], refs[8]
        dy_refs, dg_ref = refs[9:13], refs[13]
        _, vjp = jax.vjp(_group_norm, *[r[...] for r in ins])
        grads = vjp(tuple(dc_ref[:, 256 * j:256 * (j + 1)] for j in range(4)))
        first = pl.program_id(0) == 0

        @pl.when(first)
        def _():
            dg_ref[...] = jnp.zeros_like(dg_ref)

        for j in range(4):
            dy_refs[j][...] = grads[j]
            dg_ref[:, 256 * j:256 * (j + 1)] += grads[4 + j]

    return pl.pallas_call(body, grid=(S // TS2,), name=name, in_specs=[_GROW] * 4 + _GVEC + [_GCAT],
                          out_specs=[_GROW] * 4 + [pl.BlockSpec((1, D), lambda i: (0, 0))],
                          out_shape=[SDS((S, 256), F32)] * 4 + [SDS((1, D), F32)])(*ys, gg, gg, gg, gg, dcat)


FB = 256
N_FB = DFF // FB


def _ffn_gate(ug, uv, wg0, wg1, wg2, wv0, wv1, wv2, bg, bv):
    cg = bg + _shift_down(ug, 2) * wg0 + _shift_down(ug, 1) * wg1 + ug * wg2
    cv = bv + _shift_down(uv, 2) * wv0 + _shift_down(uv, 1) * wv1 + uv * wv2
    return jax.nn.silu(cg) * cv


def _gate_specs():
    ug = pl.BlockSpec((S, FB), lambda j: (0, j))
    uv = pl.BlockSpec((S, FB), lambda j: (0, j + N_FB))
    wg = pl.BlockSpec((8, FB), lambda j: (0, j))
    wv = pl.BlockSpec((8, FB), lambda j: (0, j + N_FB))
    bg = pl.BlockSpec((1, FB), lambda j: (0, j))
    bv = pl.BlockSpec((1, FB), lambda j: (0, j + N_FB))
    return ug, uv, wg, wv, bg, bv


def _gate_args(ug_ref, uv_ref, wg_ref, wv_ref, bg_ref, bv_ref):
    return (ug_ref[...], uv_ref[...], wg_ref[0:1, :], wg_ref[1:2, :], wg_ref[2:3, :],
            wv_ref[0:1, :], wv_ref[1:2, :], wv_ref[2:3, :], bg_ref[...], bv_ref[...])


def ffn_gate_fwd(u, cw, cb, name):
    def body(ug_ref, uv_ref, wg_ref, wv_ref, bg_ref, bv_ref, o_ref):
        o_ref[...] = _ffn_gate(*_gate_args(ug_ref, uv_ref, wg_ref, wv_ref, bg_ref, bv_ref)).astype(BF16)

    return pl.pallas_call(body, grid=(N_FB,), name=name, in_specs=list(_gate_specs()),
                          out_specs=pl.BlockSpec((S, FB), lambda j: (0, j)),
                          out_shape=SDS((S, DFF), BF16))(u, u, cw, cw, cb, cb)


def ffn_gate_bwd(u, cw, cb, da, name):
    def body(ug_ref, uv_ref, wg_ref, wv_ref, bg_ref, bv_ref, da_ref, dug_ref, duv_ref, dwg_ref, dwv_ref, dbg_ref, dbv_ref):
        _, vjp = jax.vjp(_ffn_gate, *_gate_args(ug_ref, uv_ref, wg_ref, wv_ref, bg_ref, bv_ref))
        dug, duv, g0, g1, g2, v0, v1, v2, dbg, dbv = vjp(da_ref[...])
        dug_ref[...] = dug.astype(BF16)
        duv_ref[...] = duv.astype(BF16)
        for k, (a, b) in enumerate(((g0, v0), (g1, v1), (g2, v2))):
            dwg_ref[k:k + 1, :] = a
            dwv_ref[k:k + 1, :] = b
        dwg_ref[FFN_K:8, :] = jnp.zeros((8 - FFN_K, FB), F32)
        dwv_ref[FFN_K:8, :] = jnp.zeros((8 - FFN_K, FB), F32)
        dbg_ref[...] = dbg
        dbv_ref[...] = dbv

    ug, uv, wg, wv, bg, bv = _gate_specs()
    half = pl.BlockSpec((S, FB), lambda j: (0, j))
    whalf = pl.BlockSpec((8, FB), lambda j: (0, j))
    bhalf = pl.BlockSpec((1, FB), lambda j: (0, j))
    return pl.pallas_call(body, grid=(N_FB,), name=name, in_specs=[ug, uv, wg, wv, bg, bv, half],
                          out_specs=[half, half, whalf, whalf, bhalf, bhalf],
                          out_shape=[SDS((S, DFF), BF16), SDS((S, DFF), BF16), SDS((8, DFF), F32), SDS((8, DFF), F32),
                                     SDS((1, DFF), F32), SDS((1, DFF), F32)])(u, u, cw, cw, cb, cb, da)


def _adamw(w, g, m, v):
    m = ADAM_B1 * m + (1.0 - ADAM_B1) * g
    v = ADAM_B2 * v + (1.0 - ADAM_B2) * (g * g)
    m_hat = m / (1.0 - ADAM_B1 ** ADAM_STEP)
    v_hat = v / (1.0 - ADAM_B2 ** ADAM_STEP)
    delta = -ADAM_LR * (m_hat / (jnp.sqrt(v_hat) + ADAM_EPS) + ADAM_WD * w)
    return delta, m, v


def sum_pieces(chip, r, own, layer, base, name):
    n, rows, cols = r.shape
    tr = _row_tile(rows, cols)

    def body(chip_ref, r_ref, own_ref, *rest):
        o_ref = rest[-1]
        acc = jnp.zeros((tr, cols), F32)
        for j in range(n):
            acc = acc + jnp.where(chip_ref[0] == j, own_ref[0], r_ref[j]).astype(F32)
        o_ref[...] = acc

    extra = {} if base is None else dict(input_output_aliases={3: 0})
    grid_spec = pltpu.PrefetchScalarGridSpec(
        num_scalar_prefetch=1, grid=(rows // tr,),
        in_specs=[pl.BlockSpec((n, tr, cols), lambda i, c: (0, i, 0)), pl.BlockSpec((1, tr, cols), lambda i, c: (c[0], i, 0))]
        + ([] if base is None else [ANY]),
        out_specs=pl.BlockSpec((None, tr, cols), lambda i, c: (layer, i, 0)))
    return pl.pallas_call(body, grid_spec=grid_spec, name=name, out_shape=SDS((DEPTH, rows, cols), F32), **extra)(
        *([chip, r, own] if base is None else [chip, r, own, base]))


LANE_BLOCK = 128


def sum_pieces_t(chip, rs, owns, name):
    n, rows, cols = rs[0].shape

    def body(chip_ref, *refs):
        o_ref = refs[-1]
        for l in range(DEPTH):
            r_ref, own_ref = refs[2 * l], refs[2 * l + 1]
            acc = jnp.zeros((rows, LANE_BLOCK), F32)
            for j in range(n):
                acc = acc + jnp.where(chip_ref[0] == j, own_ref[0], r_ref[j]).astype(F32)
            o_ref[:, l, :] = acc

    r_spec = pl.BlockSpec((n, rows, LANE_BLOCK), lambda i, c: (0, 0, i))
    own_spec = pl.BlockSpec((1, rows, LANE_BLOCK), lambda i, c: (c[0], 0, i))
    grid_spec = pltpu.PrefetchScalarGridSpec(
        num_scalar_prefetch=1, grid=(cols // LANE_BLOCK,), in_specs=[r_spec, own_spec] * DEPTH,
        out_specs=pl.BlockSpec((rows, DEPTH, LANE_BLOCK), lambda i, c: (0, 0, i)))
    ops = [a for pair in zip(rs, owns) for a in pair]
    return pl.pallas_call(body, grid_spec=grid_spec, name=name, out_shape=SDS((rows, DEPTH, cols), F32))(chip, *ops)


def adamw_pair_t(w, p, q, m, v, name):
    rows, depth, cols = w.shape

    def body(w_ref, p_ref, q_ref, m_ref, v_ref, g_ref, d_ref, nm_ref, nv_ref):
        g = p_ref[...] + q_ref[...]
        g_ref[...] = g
        d_ref[...], nm_ref[...], nv_ref[...] = _adamw(w_ref[...], g, m_ref[...], v_ref[...])

    spec = pl.BlockSpec((rows, depth, LANE_BLOCK), lambda i: (0, 0, i))
    return pl.pallas_call(body, grid=(cols // LANE_BLOCK,), name=name, in_specs=[spec] * 5, out_specs=[spec] * 4,
                          out_shape=[SDS(w.shape, F32)] * 4)(w, p, q, m, v)


def adamw_pair(w, p, q, m, v, name):
    rows, cols = w.shape
    tr = _row_tile(rows, cols)

    def body(w_ref, p_ref, q_ref, m_ref, v_ref, g_ref, d_ref, nm_ref, nv_ref):
        g = p_ref[...] + q_ref[...]
        g_ref[...] = g
        d_ref[...], nm_ref[...], nv_ref[...] = _adamw(w_ref[...], g, m_ref[...], v_ref[...])

    spec = pl.BlockSpec((tr, cols), lambda i: (i, 0))
    return pl.pallas_call(body, grid=(rows // tr,), name=name, in_specs=[spec] * 5, out_specs=[spec] * 4,
                          out_shape=[SDS((rows, cols), F32)] * 4)(w, p, q, m, v)


_PACK_LAYOUT = (('b_ada', 6 * D), ('g_pre_mix', D), ('g_post_mix', D), ('g_pre_ffn', D), ('g_post_ffn', D),
                ('b_fgate', 128), ('conv_b', GW), ('conv_ln_g', GW), ('conv_ln_b', GW), ('conv_pw_b', GW),
                ('swa_sinks', 128), ('sgu_ln_g', GW), ('sgu_ln_b', GW), ('sgu_b', 4 * 128), ('g_group', D),
                ('ffn_conv_b', NUP))
_PACK_WIDTH = dict(_PACK_LAYOUT)
_PACK_ROW, _LAYER_ROWS = {}, 0
for _name, _width in _PACK_LAYOUT:
    _PACK_ROW[_name] = _LAYER_ROWS
    _LAYER_ROWS += -(-_width // D)
LOSS_ROW = DEPTH * _LAYER_ROWS
PACK_ROWS = -(-(LOSS_ROW + 1) // 8) * 8


def _segments(offset, width):
    out, s = [], 0
    while s < width:
        row, col = divmod(offset + s, D)
        n = min(width - s, D - col)
        out.append((s, row, col, n))
        s += n
    return out


def pack_small(grads, dmods, loss_row, name):
    ops, plan = [], []
    for l in range(DEPTH):
        pieces = [('b_ada', j * D, a) for j, a in enumerate(dmods[l])]
        pieces += [(n, 0, grads[l][n]) for n, _ in _PACK_LAYOUT if n not in ('b_ada', 'sgu_b', 'ffn_conv_b')]
        pieces += [('ffn_conv_b', j * DFF, a) for j, a in enumerate(grads[l]['ffn_conv_b'])]
        for n, off, a in pieces:
            plan.append((len(ops), l, n, off))
            ops.append(a)
    bts = [grads[l]['sgu_bt'] for l in range(DEPTH)]
    sws = [grads[l]['sgu_w'] for l in range(DEPTH)]
    n_vec = len(ops)

    def body(*refs):
        vec, bt, sw = refs[:n_vec], refs[n_vec:n_vec + DEPTH], refs[n_vec + DEPTH:n_vec + 2 * DEPTH]
        loss_ref, o_ref, ow_ref, scr = refs[n_vec + 2 * DEPTH:]
        o_ref[...] = jnp.zeros_like(o_ref)
        o_ref[LOSS_ROW:LOSS_ROW + 1, 0:128] = loss_ref[...]
        for idx, l, n, off in plan:
            base = l * _LAYER_ROWS + _PACK_ROW[n]
            width = min(vec[idx].shape[1], _PACK_WIDTH[n] - off)
            for s, row, col, lanes in _segments(off, width):
                o_ref[base + row:base + row + 1, col:col + lanes] = vec[idx][:, s:s + lanes]
        for l in range(DEPTH):
            scr[...] = bt[l][...].T
            row = l * _LAYER_ROWS + _PACK_ROW['sgu_b']
            for g in range(4):
                o_ref[row:row + 1, 128 * g:128 * (g + 1)] = scr[g:g + 1, :]
            ow_ref[l] = sw[l][...]

    vm = pl.BlockSpec(memory_space=pltpu.VMEM)
    return pl.pallas_call(body, name=name, in_specs=[vm] * (n_vec + 2 * DEPTH + 1), out_specs=[vm, vm],
                          out_shape=[SDS((PACK_ROWS, D), F32), SDS((DEPTH, 4, 128, 128), F32)],
                          scratch_shapes=[pltpu.VMEM((128, 128), F32)])(*ops, *bts, *sws, loss_row)


def adamw_small(gall, gall_w, w, m, v, name):
    names = [n for n in SMALL]
    n_par = len(names)

    def native(ref, n, l, col, lanes):
        if n == 'sgu_b':
            return ref.at[l, pl.ds(col // 128, 1), :]
        return ref.at[pl.ds(l, 1), pl.ds(col, lanes)]

    def body(*refs):
        ga_ref, gw_ref = refs[0], refs[1]
        w_refs, m_refs, v_refs = (refs[2 + k * n_par:2 + (k + 1) * n_par] for k in range(3))
        outs = refs[2 + 3 * n_par:2 + 7 * n_par]
        loss_ref, scr = refs[-2], refs[-1]
        g = ga_ref[0]
        for j in range(1, N_DEV):
            g = g + ga_ref[j]
        scr[...] = g
        loss_ref[...] = scr[LOSS_ROW:LOSS_ROW + 1, 0:128]
        for k, n in enumerate(names):
            o_g, o_d, o_m, o_v = outs[4 * k:4 * k + 4]
            if n == 'sgu_w':
                gw = gw_ref[0]
                for j in range(1, N_DEV):
                    gw = gw + gw_ref[j]
                o_g[...] = gw
                o_d[...], o_m[...], o_v[...] = _adamw(w_refs[k][...], gw, m_refs[k][...], v_refs[k][...])
                continue
            width = w_refs[k].shape[-1] if n != 'sgu_b' else 4 * 128
            for l in range(DEPTH):
                base = l * _LAYER_ROWS + _PACK_ROW[n]
                step = 128 if n == 'sgu_b' else D
                for col in range(0, width, step):
                    lanes = min(step, width - col)
                    row, lane0 = divmod(col, D)
                    gv = scr[base + row:base + row + 1, lane0:lane0 + lanes]
                    at = functools.partial(native, n=n, l=l, col=col, lanes=lanes)
                    at(o_g)[...] = gv
                    at(o_d)[...], at(o_m)[...], at(o_v)[...] = _adamw(at(w_refs[k])[...], gv, at(m_refs[k])[...],
                                                                      at(v_refs[k])[...])

    vm = pl.BlockSpec(memory_space=pltpu.VMEM)
    params = [d[n] for d in (w, m, v) for n in names]
    outs = pl.pallas_call(body, name=name, in_specs=[vm] * (2 + 3 * n_par), out_specs=[vm] * (4 * n_par + 1),
                          out_shape=[SDS(w[n].shape, F32) for n in names for _ in range(4)] + [SDS((1, 128), F32)],
                          scratch_shapes=[pltpu.VMEM((PACK_ROWS, D), F32)])(gall, gall_w, *params)
    return {n: tuple(outs[4 * k:4 * k + 4]) for k, n in enumerate(names)}, outs[-1]


ADA_COLS = 6 * D // N_CHIP
ADA_TN = 768


def ada_mod(c_all, w, b, name):
    def body(c_ref, w_ref, b_ref, o_ref):
        ca = jax.nn.silu(c_ref[...])
        o_ref[...] = jnp.dot(ca, w_ref[...], precision=lax.Precision.HIGHEST, preferred_element_type=F32) + b_ref[...]

    return pl.pallas_call(
        body, grid=(DEPTH, ADA_COLS // ADA_TN), name=name,
        in_specs=[pl.BlockSpec((N_DEV, D), lambda l, j: (0, 0)),
                  pl.BlockSpec((None, D, ADA_TN), lambda l, j: (l, 0, j)),
                  pl.BlockSpec((None, 1, ADA_TN), lambda l, j: (l, 0, j))],
        out_specs=pl.BlockSpec((None, N_DEV, ADA_TN), lambda l, j: (l, 0, j)),
        out_shape=SDS((DEPTH, N_DEV, ADA_COLS), F32))(c_all, w, b)


def ada_update(c_all_t, dmod, w, m, v, name):
    def body(c_ref, dm_ref, w_ref, m_ref, v_ref, g_ref, d_ref, nm_ref, nv_ref):
        ca = jax.nn.silu(c_ref[...])
        g = jnp.dot(ca, dm_ref[...], precision=lax.Precision.HIGHEST, preferred_element_type=F32)
        g_ref[...] = g
        d_ref[...], nm_ref[...], nv_ref[...] = _adamw(w_ref[...], g, m_ref[...], v_ref[...])

    wspec = pl.BlockSpec((None, D, ADA_TN), lambda l, j: (l, 0, j))
    return pl.pallas_call(
        body, grid=(DEPTH, ADA_COLS // ADA_TN), name=name,
        in_specs=[pl.BlockSpec((D, N_DEV), lambda l, j: (0, 0)),
                  pl.BlockSpec((None, N_DEV, ADA_TN), lambda l, j: (l, 0, j)), wspec, wspec, wspec],
        out_specs=[wspec] * 4, out_shape=[SDS((DEPTH, D, ADA_COLS), F32)] * 4)(c_all_t, dmod, w, m, v)


_CHIP_FLIPS = ((1, 0), (0, 1), (1, 1))
_DEV_FLIPS = tuple((a, b, c) for a in (0, 1) for b in (0, 1) for c in (0, 1) if (a, b, c) != (0, 0, 0))


def _position():
    return lax.axis_index("x"), lax.axis_index("y"), lax.axis_index("c")


def _hbm_call(body, arrs, out_shapes, n_remote, name):
    n = len(arrs)
    return pl.pallas_call(
        body, name=name, in_specs=[ANY] * n, out_specs=[ANY] * n, out_shape=out_shapes,
        scratch_shapes=[pltpu.SemaphoreType.DMA((n, n_remote)), pltpu.SemaphoreType.DMA((n, n_remote)),
                        pltpu.SemaphoreType.DMA((n,))])(*arrs)


_HBM = pl.BlockSpec(memory_space=pltpu.HBM)
_SEM = pl.BlockSpec(memory_space=pltpu.SEMAPHORE)
_EFFECT = pltpu.SideEffectType.DATAFLOW_SIDE_EFFECTING


GATHER, SCATTER, ALL, SWAP = "gather", "scatter", "all", "swap"
_PEERS = {GATHER: tuple((fx, fy, 0) for fx, fy in _CHIP_FLIPS), SCATTER: tuple((fx, fy, 0) for fx, fy in _CHIP_FLIPS),
          ALL: _DEV_FLIPS, SWAP: ((0, 0, 1),)}


def _peer_copies(kind, src, land, send, recv, arrivals):
    x, y, c = _position()
    out = []
    for k, (fx, fy, fc) in enumerate(_PEERS[kind]):
        tx, ty, tc = x ^ fx, y ^ fy, c ^ fc
        if kind == ALL:
            me, peer = 4 * x + 2 * y + c, 4 * tx + 2 * ty + tc
        else:
            me, peer = 2 * x + y, 2 * tx + ty
        if kind == SWAP:
            dst = land
        else:
            dst = land.at[peer if arrivals else me]
        out.append(pltpu.make_async_remote_copy(
            src_ref=src.at[peer] if kind == SCATTER else src, dst_ref=dst, send_sem=send.at[k], recv_sem=recv.at[k],
            device_id=(tx, ty, tc), device_id_type=MESH))
    return out


def _own_copy(kind, src, land, send):
    x, y, c = _position()
    me = 4 * x + 2 * y + c if kind == ALL else 2 * x + y
    return pltpu.make_async_copy(src, land.at[me], send.at[len(_PEERS[kind])])


def exchange_start(srcs, lands, kind, name):
    n = len(srcs)
    n_peers = len(_PEERS[kind])

    def body(*refs):
        src, land = refs[:n], refs[n:2 * n]
        send, recv = refs[2 * n:3 * n], refs[3 * n:4 * n]
        token = refs[-1]
        for a in range(n):
            for copy in _peer_copies(kind, src[a], land[a], send[a], recv[a], False):
                copy.start()
            if kind in (GATHER, ALL):
                _own_copy(kind, src[a], land[a], send[a]).start()
        token[...] = jnp.zeros_like(token)

    bufs = list(srcs) + list(lands)
    outs = pl.pallas_call(
        body, name=name, in_specs=[_HBM] * (2 * n),
        out_specs=[_SEM] * (2 * n) + [_HBM] * (2 * n) + [pl.BlockSpec(memory_space=pltpu.VMEM)],
        out_shape=[pltpu.SemaphoreType.DMA((n_peers + 1,))] * n + [pltpu.SemaphoreType.DMA((n_peers,))] * n
        + [pltpu.HBM(a.shape, a.dtype) for a in bufs] + [SDS((8, 128), F32)],
        input_output_aliases={i: 2 * n + i for i in range(2 * n)},
        compiler_params=pltpu.CompilerParams(has_side_effects=_EFFECT),
    )(*[pltpu.with_memory_space_constraint(a, pltpu.HBM) for a in bufs])
    flights = [(outs[a], outs[n + a], outs[2 * n + a], outs[3 * n + a]) for a in range(n)]
    return flights, outs[-1]


def exchange_wait(flights, kind, after, name):
    n = len(flights)

    def body(*refs):
        src, land = refs[:n], refs[n:2 * n]
        send, recv = refs[2 * n:3 * n], refs[3 * n:4 * n]
        for a in range(n):
            for arrival in _peer_copies(kind, src[a], land[a], send[a], recv[a], True):
                arrival.wait_send()
                arrival.wait_recv()
            if kind in (GATHER, ALL):
                _own_copy(kind, src[a], land[a], send[a]).wait()

    bufs = [f[2] for f in flights] + [f[3] for f in flights]
    sems = [f[0] for f in flights] + [f[1] for f in flights]
    outs = pl.pallas_call(
        body, name=name, in_specs=[_HBM] * (2 * n) + [_SEM] * (2 * n) + [ANY], out_specs=[_HBM] * (2 * n),
        out_shape=[pltpu.HBM(a.shape, a.dtype) for a in bufs],
        input_output_aliases={i: i for i in range(2 * n)},
        compiler_params=pltpu.CompilerParams(has_side_effects=_EFFECT),
    )(*bufs, *sems, after)
    return [(outs[a], outs[n + a]) for a in range(n)]


def chip_alltoall(arrs, name):
    n = len(arrs)

    def body(*refs):
        ins, outs = refs[:n], refs[n:2 * n]
        send, recv, loc = refs[2 * n:]
        x, y, c = _position()
        me = 2 * x + y
        copies = []
        for a in range(n):
            local = pltpu.make_async_copy(ins[a].at[me], outs[a].at[me], loc.at[a])
            local.start()
            copies.append(local)
            for k, (fx, fy) in enumerate(_CHIP_FLIPS):
                tx, ty = x ^ fx, y ^ fy
                cp = pltpu.make_async_remote_copy(
                    src_ref=ins[a].at[2 * tx + ty], dst_ref=outs[a].at[me], send_sem=send.at[a, k],
                    recv_sem=recv.at[a, k], device_id=(tx, ty, c), device_id_type=MESH)
                cp.start()
                copies.append(cp)
        for cp in copies:
            cp.wait()

    shapes = [SDS(a.shape, a.dtype) for a in arrs]
    return _hbm_call(body, arrs, shapes, 3, name)


def device_allgather(arrs, name):
    n = len(arrs)

    def body(*refs):
        ins, outs = refs[:n], refs[n:2 * n]
        send, recv, loc = refs[2 * n:]
        x, y, c = _position()
        me = 4 * x + 2 * y + c
        copies = []
        for a in range(n):
            local = pltpu.make_async_copy(ins[a], outs[a].at[me], loc.at[a])
            local.start()
            copies.append(local)
            for k, (fx, fy, fc) in enumerate(_DEV_FLIPS):
                cp = pltpu.make_async_remote_copy(
                    src_ref=ins[a], dst_ref=outs[a].at[me], send_sem=send.at[a, k], recv_sem=recv.at[a, k],
                    device_id=(x ^ fx, y ^ fy, c ^ fc), device_id_type=MESH)
                cp.start()
                copies.append(cp)
        for cp in copies:
            cp.wait()

    shapes = [SDS((N_DEV,) + a.shape, a.dtype) for a in arrs]
    return _hbm_call(body, arrs, shapes, 7, name)


def _pad_to(a, axis, size):
    pad = [(0, 0)] * a.ndim
    pad[axis] = (0, size - a.shape[axis])
    return jnp.pad(a, pad)


def _rows_to_z(wt):
    gap = Z_CONV_A - FG_END
    row = lax.broadcasted_iota(jnp.int32, (ZC, 1), 0)
    low = jnp.pad(wt, ((0, gap), (0, 0)))
    high = jnp.pad(wt, ((gap, 0), (0, 0)))
    return jnp.where(row < FG_END, low, jnp.where(row < Z_CONV_A, jnp.zeros_like(low), high))


def _rows_from_z(wt):
    row = lax.broadcasted_iota(jnp.int32, (IN_COLS, 1), 0)
    return jnp.where(row < FG_END, wt[:IN_COLS], wt[Z_CONV_A - FG_END:])


def _layer_fwd(l, x0, h1, mod, p, fetch, nxt):
    sh1, sc1, ga1, sh2, sc2, ga2 = mod
    t = f"l{l}_"
    wt = dict(fetch('w_in', h1))
    z = mm_nt([h1], wt['w_in'], t + "proj_in")
    cumc, cumr = fgate_fwd(z, p['b_fgate'], t + "fgate")
    y_fox, lse = fox_fwd(z, cumc, cumr, t + "fox")
    hc = conv_dw_fwd(z, wt['conv_w'], p['conv_b'], t + "conv_dw")
    y_conv = conv_pw_fwd(hc, p['conv_ln_g'], p['conv_ln_b'], wt['conv_pw_w'], p['conv_pw_b'], t + "conv_pw")
    y_swa = swa_fwd(z, p['swa_sinks'], t + "swa")
    y_sgu = sgu_fwd(z, p['sgu_ln_g'], p['sgu_ln_b'], p['sgu_w'], p['sgu_bt'], t + "sgu")
    ys = (y_fox, y_conv, y_swa, y_sgu)
    ycat = group_norm_fwd(ys, p['g_group'], t + "group_norm")
    wt.update(fetch('w_out', ycat))
    ymix = mm_nn([ycat], wt['w_out'], F32, t + "proj_out")
    x1, h2 = resid_modnorm_fwd(x0, ymix, ga1, p['g_post_mix'], p['g_pre_ffn'], sc2, sh2, t + "resid1_modnorm2")
    wt.update(fetch('ffn_w_up', h2))
    u = mm_nn([h2], wt['ffn_w_up'], F32, t + "ffn_up", tm=S)
    act = ffn_gate_fwd(u, wt['ffn_conv_w'], p['ffn_conv_b'], t + "ffn_gate")
    wt.update(fetch('ffn_w_down', act))
    yffn = mm_nn([act], wt['ffn_w_down'], F32, t + "ffn_down")
    if nxt is None:
        x2, h1_next = resid_fwd(x1, yffn, ga2, p['g_post_ffn'], t + "resid2"), None
    else:
        x2, h1_next = resid_modnorm_fwd(x1, yffn, ga2, p['g_post_ffn'], *nxt, t + "resid2_modnorm1")
    res = dict(x0=x0, h1=h1, z=z, cumc=cumc, cumr=cumr, y_fox=y_fox, lse=lse, hc=hc, ys=ys, ycat=ycat, ymix=ymix,
               x1=x1, h2=h2, u=u, act=act, yffn=yffn, wt=wt)
    return x2, h1_next, res


def _layer_bwd(l, dx2, head, mod, p, r, emit, prev):
    sh1, sc1, ga1, sh2, sc2, ga2 = mod
    t = f"l{l}_bwd_"
    g = {}
    wt = r['wt']
    dyffn, dga2, g['g_post_ffn'] = head
    tok = emit({'ffn_w_down': mm_tn([r['act']], [dyffn], t + "ffn_down_dw")})
    dact = mm_nt([dyffn], wt['ffn_w_down'], t + "ffn_down_dx")
    dug, duv, dwg, dwv, dbg, dbv = ffn_gate_bwd(r['u'], wt['ffn_conv_w'], p['ffn_conv_b'] + tok, dact, t + "ffn_gate")
    g['ffn_conv_b'] = (dbg, dbv)
    tok = emit({'ffn_w_up': mm_tn_halves(r['h2'], dug, duv, t + "ffn_up_dw")})
    dh2 = mm_nt_halves(dug, duv, wt['ffn_w_up'], t + "ffn_up_dx")
    dx1, g['g_pre_ffn'], dsc2, dsh2, dymix, dga1, g['g_post_mix'] = modnorm_resid_bwd(
        r['x1'], p['g_pre_ffn'] + tok, sc2, sh2, dh2, dx2, r['ymix'], ga1, p['g_post_mix'], t + "modnorm2_resid1")
    tok = emit({'w_out': mm_tn([r['ycat']], [dymix], t + "proj_out_dw")})
    dcat = mm_nt([dymix], wt['w_out'], t + "proj_out_dx")
    dy_fox, dy_conv, dy_swa, dy_sgu, g['g_group'] = group_norm_bwd(r['ys'], p['g_group'] + tok, dcat, t + "group_norm")
    z = r['z']
    fq, fk, fv, dcq, dck = fox_bwd(z, r['cumc'], r['cumr'], r['lse'], r['y_fox'], dy_fox, t + "fox")
    dzf, g['b_fgate'] = fgate_bwd(z, p['b_fgate'], dcq, dck, t + "fgate")
    dhc, g['conv_ln_g'], g['conv_ln_b'], dpw, g['conv_pw_b'] = conv_pw_bwd(
        r['hc'], p['conv_ln_g'], p['conv_ln_b'], wt['conv_pw_w'], p['conv_pw_b'], dy_conv, t + "conv_pw")
    ca, cg, dcw, g['conv_b'] = conv_dw_bwd(z, wt['conv_w'], dhc, t + "conv_dw")
    sq, sk, sv, g['swa_sinks'] = swa_bwd(z, p['swa_sinks'], dy_swa, t + "swa")
    gu, gv, g['sgu_ln_g'], g['sgu_ln_b'], g['sgu_w'], g['sgu_bt'] = sgu_bwd(
        z, p['sgu_ln_g'], p['sgu_ln_b'], p['sgu_w'], p['sgu_bt'], dy_sgu, t + "sgu")
    dz = [fq, fk, fv, dzf, ca, cg, sq, sk, sv, gu, gv]
    tok = emit({'w_in': mm_tn(dz, [r['h1']], t + "proj_in_dw"), 'conv_w': dcw, 'conv_pw_w': dpw,
                'ffn_conv_w': jnp.concatenate([dwg, dwv], axis=1)})
    dh1 = mm_nn(dz, wt['w_in'], F32, t + "proj_in_dx")
    if prev is None:
        dx0, g['g_pre_mix'], dsc1, dsh1 = modnorm_bwd(r['x0'], p['g_pre_mix'] + tok, sc1, sh1, dh1, dx1, t + "modnorm1")
        head = None
    else:
        dx0, g['g_pre_mix'], dsc1, dsh1, *head = modnorm_resid_bwd(
            r['x0'], p['g_pre_mix'] + tok, sc1, sh1, dh1, dx1, *prev, t + "modnorm1_resid2")
    return dx0, head, g, (dsh1, dsc1, dga1, dsh2, dsc2, dga2)


def _layer_params(l, w):
    def row(name, width=None):
        v = w[name][l].reshape(1, -1)
        return v if width is None else _pad_to(v, 1, width)
    return {
        'g_pre_mix': row('g_pre_mix'), 'g_post_mix': row('g_post_mix'), 'g_pre_ffn': row('g_pre_ffn'),
        'g_post_ffn': row('g_post_ffn'), 'b_fgate': row('b_fgate', 128), 'conv_b': row('conv_b'),
        'conv_ln_g': row('conv_ln_g'), 'conv_ln_b': row('conv_ln_b'), 'conv_pw_b': row('conv_pw_b'),
        'swa_sinks': row('swa_sinks', 128), 'sgu_ln_g': row('sgu_ln_g'), 'sgu_ln_b': row('sgu_ln_b'),
        'sgu_w': w['sgu_w'][l], 'sgu_bt': _pad_to(w['sgu_b'][l].T, 1, 128), 'g_group': row('g_group'),
        'ffn_conv_b': row('ffn_conv_b'),
    }


def _w_in_from_shards(s):
    return _rows_to_z(s.reshape(IN_COLS, D))


def _w_in_to_shards(g):
    return _rows_from_z(g).reshape(N_CHIP, IN_COLS // N_CHIP, D)


def _cols_from_shards(s, rows):
    _, r, n = s.shape
    return _pad_to(jnp.transpose(s, (1, 0, 2)).reshape(r, N_CHIP * n), 0, rows)


def _cols_to_shards(g, r):
    n = g.shape[1] // N_CHIP
    return jnp.transpose(g[:r].reshape(r, N_CHIP, n), (1, 0, 2))


_FROM_SHARDS = {
    'w_in': _w_in_from_shards,
    'w_out': lambda s: s.reshape(D, D),
    'ffn_w_up': lambda s: s,
    'ffn_w_down': lambda s: s.reshape(DFF, D),
    'conv_w': lambda s: _cols_from_shards(s, 32),
    'conv_pw_w': lambda s: s.reshape(GW, GW),
    'ffn_conv_w': lambda s: _cols_from_shards(s, 8),
}
_TO_SHARDS = {
    'w_in': _w_in_to_shards,
    'w_out': lambda g: g.reshape(N_CHIP, D // N_CHIP, D),
    'ffn_w_up': lambda g: g,
    'ffn_w_down': lambda g: g.reshape(N_CHIP, DFF // N_CHIP, D),
    'conv_w': lambda g: _cols_to_shards(g, CONV_K),
    'conv_pw_w': lambda g: g.reshape(N_CHIP, GW // N_CHIP, GW),
    'ffn_conv_w': lambda g: _cols_to_shards(g, FFN_K),
}


def _local_step(xs, target, mod, w, fetch, emit):
    params = [_layer_params(l, w) for l in range(DEPTH)]
    mods = [tuple(mod[l, j] for j in range(6)) for l in range(DEPTH)]
    pre = [(params[l]['g_pre_mix'], mods[l][1], mods[l][0]) for l in range(DEPTH)]
    post = [(mods[l][5], params[l]['g_post_ffn']) for l in range(DEPTH)]
    resids = []
    h1 = modnorm_fwd(xs, *pre[0], "l0_modnorm1")
    for l in range(DEPTH):
        nxt = pre[l + 1] if l + 1 < DEPTH else None
        xs, h1, r = _layer_fwd(l, xs, h1, mods[l], params[l], functools.partial(fetch, l), nxt)
        resids.append(r)
    dx, loss_row = loss_grad(xs, target, "loss")
    last = DEPTH - 1
    head = resid_bwd(resids[last]['yffn'], *post[last], dx, f"l{last}_bwd_resid2")
    grads, dmods = [None] * DEPTH, [None] * DEPTH
    for l in reversed(range(DEPTH)):
        prev = (resids[l - 1]['yffn'],) + post[l - 1] if l > 0 else None
        dx, head, grads[l], dmods[l] = _layer_bwd(l, dx, head, mods[l], params[l], resids[l], functools.partial(emit, l),
                                                  prev)
    return loss_row, dx, grads, dmods


_MATMUL_WEIGHTS = ('w_in', 'w_out', 'ffn_w_up', 'ffn_w_down')
_CONV_WEIGHTS = ('conv_w', 'conv_pw_w', 'ffn_conv_w')
_FETCH_GROUPS = {'w_in': ('w_in',) + _CONV_WEIGHTS, 'w_out': ('w_out',), 'ffn_w_up': ('ffn_w_up',),
                 'ffn_w_down': ('ffn_w_down',)}


def kernel(x, c, w_ada, b_ada, g_pre_mix, g_post_mix, g_pre_ffn, g_post_ffn, w_in, b_fgate, conv_w, conv_b, conv_ln_g, conv_ln_b, conv_pw_w, conv_pw_b, swa_sinks, sgu_ln_g, sgu_ln_b, sgu_w, sgu_b, g_group, w_out, ffn_w_up, ffn_conv_w, ffn_conv_b, ffn_w_down, loss_target, m_w_ada, m_b_ada, m_g_pre_mix, m_g_post_mix, m_g_pre_ffn, m_g_post_ffn, m_w_in, m_b_fgate, m_conv_w, m_conv_b, m_conv_ln_g, m_conv_ln_b, m_conv_pw_w, m_conv_pw_b, m_swa_sinks, m_sgu_ln_g, m_sgu_ln_b, m_sgu_w, m_sgu_b, m_g_group, m_w_out, m_ffn_w_up, m_ffn_conv_w, m_ffn_conv_b, m_ffn_w_down, v_w_ada, v_b_ada, v_g_pre_mix, v_g_post_mix, v_g_pre_ffn, v_g_post_ffn, v_w_in, v_b_fgate, v_conv_w, v_conv_b, v_conv_ln_g, v_conv_ln_b, v_conv_pw_w, v_conv_pw_b, v_swa_sinks, v_sgu_ln_g, v_sgu_ln_b, v_sgu_w, v_sgu_b, v_g_group, v_w_out, v_ffn_w_up, v_ffn_conv_w, v_ffn_conv_b, v_ffn_w_down):
    args = locals()
    w = {n: args[n] for n in WEIGHTS}
    m = {n: args['m_' + n] for n in WEIGHTS}
    v = {n: args['v_' + n] for n in WEIGHTS}
    xi, yi, ci = _position()
    chip = 2 * xi + yi

    (c_all,) = device_allgather([c], "gather_c")
    c_all = c_all.reshape(N_DEV, D)
    b_loc = lax.dynamic_slice_in_dim(b_ada, chip * ADA_COLS, ADA_COLS, axis=1).reshape(DEPTH, 1, ADA_COLS)
    mod_all = ada_mod(c_all, w_ada, b_loc, "ada_mod")
    mine = lax.dynamic_index_in_dim(mod_all.reshape(DEPTH, N_CHIP, 2, ADA_COLS), ci, axis=2, keepdims=False)
    (mod4,) = chip_alltoall([jnp.transpose(mine, (1, 0, 2))], "scatter_mod")

    w_in_t, m_in_t, v_in_t = (jnp.transpose(a, (2, 0, 1)) for a in (w_in, m_w_in, v_w_in))

    def shard(n, l):
        a = w_in_t[:, l] if n == 'w_in' else w[n][l]
        return a.astype(BF16) if n in _MATMUL_WEIGHTS else a

    keys = [(n, l) for l in range(DEPTH) for n in _CONV_WEIGHTS]
    keys += [(n, l) for l in range(DEPTH) for n in _MATMUL_WEIGHTS]
    srcs = [shard(n, l) for n, l in keys]
    mod4, srcs = lax.optimization_barrier((mod4, srcs))
    lands = [lax.empty((N_CHIP,) + s.shape, s.dtype) for s in srcs]
    flights, token = exchange_start(srcs, lands, GATHER, "gather_start")
    gathering = dict(zip(keys, flights))
    mod = jnp.transpose(mod4, (1, 0, 2)).reshape(DEPTH, 6, 1, D) + token[0:1, 0:1]

    def fetch(l, name, after):
        names = _FETCH_GROUPS[name]
        landed = exchange_wait([gathering[(n, l)] for n in names], GATHER, after, f"gather_wait_l{l}_{name}")
        return {n: _FROM_SHARDS[n](land) for n, (_, land) in zip(names, landed)}

    scattering = {}

    def emit(l, grads):
        names = list(grads)
        pieces = [_TO_SHARDS[n](grads[n]) for n in names]
        lands = [lax.empty(p.shape, p.dtype) for p in pieces]
        started, token = exchange_start(pieces, lands, SCATTER, f"scatter_start_l{l}_{names[0]}")
        scattering.update({(n, l): f for n, f in zip(names, started)})
        return token[0:1, 0:1]

    loss_row, dx, grads, dmods = _local_step(x.reshape(S, D), loss_target.reshape(S, D), mod, w, fetch, emit)
    grad_x = dx.reshape(1, S, D)

    small_srcs = pack_small(grads, dmods, loss_row, "pack_small")
    small_flights, small_token = exchange_start(small_srcs, [lax.empty((N_DEV,) + a.shape, F32) for a in small_srcs], ALL,
                                                "gather_small_start")

    order = list(scattering)
    landed = dict(zip(order, exchange_wait([scattering[k] for k in order], SCATTER, small_token, "scatter_wait")))
    chip1 = chip.astype(jnp.int32).reshape(1)
    part = {'w_in': sum_pieces_t(chip1, [landed[('w_in', l)][1] for l in range(DEPTH)],
                                 [landed[('w_in', l)][0] for l in range(DEPTH)], "sum_w_in")}
    for n in SHARDED[1:]:
        cols = w[n].shape[-1]
        for l in range(DEPTH):
            src, land = landed[(n, l)]
            part[n] = sum_pieces(chip1, land.reshape(N_CHIP, -1, cols), src.reshape(N_CHIP, -1, cols), l, part.get(n),
                                 f"sum_{n}_l{l}")
        part[n] = part[n].reshape(-1, cols)
    parts = [part[n] for n in SHARDED]
    swap_flights, swap_token = exchange_start(parts, [lax.empty(p.shape, F32) for p in parts], SWAP, "swap_start")

    g_all, gw_all = (land for _, land in exchange_wait(small_flights, ALL, swap_token, "gather_small_wait"))
    out, loss_sum = adamw_small(g_all, gw_all, w, m, v, "adamw_small")
    loss = loss_sum[0, 0]

    dmod_all = g_all[:, :DEPTH * _LAYER_ROWS].reshape(N_DEV, DEPTH, _LAYER_ROWS * D)[:, :, :6 * D]
    dmod_loc = jnp.transpose(lax.dynamic_slice_in_dim(dmod_all, chip * ADA_COLS, ADA_COLS, axis=2), (1, 0, 2))
    out['w_ada'] = tuple(ada_update(c_all.T, dmod_loc, w_ada, m['w_ada'], v['w_ada'], "adamw_ada"))

    swapped = exchange_wait(swap_flights, SWAP, out['w_ada'][0], "swap_wait")
    part = {n: own for n, (own, _) in zip(SHARDED, swapped)}
    other = {n: got for n, (_, got) in zip(SHARDED, swapped)}
    res = adamw_pair_t(w_in_t, part['w_in'], other['w_in'], m_in_t, v_in_t, "adamw_w_in")
    out['w_in'] = tuple(jnp.transpose(a, (1, 2, 0)) for a in res)
    for n in SHARDED[1:]:
        cols = w[n].shape[-1]
        res = adamw_pair(w[n].reshape(-1, cols), part[n], other[n], m[n].reshape(-1, cols), v[n].reshape(-1, cols),
                         "adamw_" + n)
        out[n] = tuple(a.reshape(w[n].shape) for a in res)

    return (loss, grad_x, *[out[n][0] for n in WEIGHTS], *[out[n][1] for n in WEIGHTS],
            *[out[n][2] for n in WEIGHTS], *[out[n][3] for n in WEIGHTS])
```

```python
import functools

import jax
import jax.numpy as jnp
from jax import lax
from jax.experimental import pallas as pl
from jax.experimental.pallas import tpu as pltpu

F32 = jnp.float32
BF16 = jnp.bfloat16
SDS = jax.ShapeDtypeStruct
MESH = pl.DeviceIdType.MESH
ANY = pl.BlockSpec(memory_space=pl.ANY)

DEPTH = 2
S = 2048
D = 1024
GW = 256
DFF = 2816
NUP = 2 * DFF
IN_COLS = 2308
ZC = 2432
CONV_K = 31
FFN_K = 3
EPS = 1e-6
SCALE = 0.125
NEG = -1e30
N_CHIP = 4
N_DEV = 8

Z_FOX_Q, Z_FOX_K, Z_FOX_V = 0, 256, 512
Z_FG = 768
Z_CONV_A, Z_CONV_G = 896, 1152
Z_SWA_Q, Z_SWA_K, Z_SWA_V = 1408, 1664, 1792
Z_SGU_U, Z_SGU_V = 1920, 2176
FG_END = 772

ADAM_LR, ADAM_B1, ADAM_B2, ADAM_EPS, ADAM_WD, ADAM_STEP = 0.001, 0.9, 0.999, 1e-08, 0.01, 10

TS = 256
TM = 1024
N_SPLIT = 2816
N_BLOCK = 1408

WEIGHTS = ['w_ada', 'b_ada', 'g_pre_mix', 'g_post_mix', 'g_pre_ffn', 'g_post_ffn', 'w_in', 'b_fgate', 'conv_w',
           'conv_b', 'conv_ln_g', 'conv_ln_b', 'conv_pw_w', 'conv_pw_b', 'swa_sinks', 'sgu_ln_g', 'sgu_ln_b',
           'sgu_w', 'sgu_b', 'g_group', 'w_out', 'ffn_w_up', 'ffn_conv_w', 'ffn_conv_b', 'ffn_w_down']
SHARDED = ['w_in', 'conv_w', 'conv_pw_w', 'w_out', 'ffn_w_up', 'ffn_conv_w', 'ffn_w_down']
SMALL = [n for n in WEIGHTS if n not in SHARDED and n != 'w_ada']


def _rms(x, g):
    return x * lax.rsqrt(jnp.mean(x * x, axis=-1, keepdims=True) + EPS) * g


def _modnorm(x, g, sc, sh):
    return _rms(x, g) * (1.0 + sc) + sh


def _resid(x, y, ga, g):
    return x + ga * _rms(y, g)


@functools.partial(jax.custom_vjp, nondiff_argnums=(1,))
def _shift_down(x, n):
    if n == 0:
        return x
    row = lax.broadcasted_iota(jnp.int32, x.shape, 0)
    return jnp.where(row >= n, pltpu.roll(x, n, axis=0), 0.0)


def _shift_up(x, n):
    if n == 0:
        return x
    rows = x.shape[0]
    row = lax.broadcasted_iota(jnp.int32, x.shape, 0)
    return jnp.where(row < rows - n, pltpu.roll(x, rows - n, axis=0), 0.0)


def _shift_down_fwd(x, n):
    return _shift_down(x, n), None


def _shift_down_bwd(n, _, ct):
    return (_shift_up(ct, n),)


_shift_down.defvjp(_shift_down_fwd, _shift_down_bwd)


def _nt(a, b):
    return lax.dot_general(a, b, (((1,), (1,)), ((), ())), preferred_element_type=F32)


def _tn(a, b):
    return lax.dot_general(a, b, (((0,), (0,)), ((), ())), preferred_element_type=F32)


def _nn(a, b):
    return jnp.dot(a, b, preferred_element_type=F32)


def _acc(ref, val, first):
    @pl.when(first)
    def _():
        ref[...] = val

    @pl.when(jnp.logical_not(first))
    def _():
        ref[...] += val


def _row_tile(rows, cols):
    limit = max(8, (1 << 21) // (4 * cols))
    best = None
    for t in range(8, rows + 1, 8):
        if rows % t == 0 and t <= limit:
            best = t
    return best if best is not None else rows


def _ncol(n):
    return n if n <= N_SPLIT else N_BLOCK


def _weight_spec(b, order):
    pick = (lambda j, i: j) if order == 0 else (lambda i, j: j)
    if b.ndim == 3:
        _, k, tn = b.shape
        return pl.BlockSpec((None, k, tn), lambda *g: (pick(*g), 0, 0)), k, N_CHIP * tn, tn
    k, n = b.shape
    tn = _ncol(n)
    return pl.BlockSpec((k, tn), lambda *g: (0, pick(*g))), k, n, tn


def _side_by_side(refs):
    return refs[0][...] if len(refs) == 1 else jnp.concatenate([r[...] for r in refs], axis=1)


def mm_nn(pieces, b, out_dtype, name, tm=TM):
    m = pieces[0].shape[0]
    b_spec, k, n, tn = _weight_spec(b, 0)

    def body(*refs):
        a_refs, b_ref, o_ref = refs[:-2], refs[-2], refs[-1]
        o_ref[...] = _nn(_side_by_side(a_refs), b_ref[...]).astype(out_dtype)

    return pl.pallas_call(
        body, grid=(n // tn, m // tm), name=name,
        in_specs=[pl.BlockSpec((tm, p.shape[1]), lambda j, i: (i, 0)) for p in pieces] + [b_spec],
        out_specs=pl.BlockSpec((tm, tn), lambda j, i: (i, j)),
        out_shape=SDS((m, n), out_dtype),
    )(*pieces, b)


def mm_nt(pieces, b, name):
    m = pieces[0].shape[0]
    k, n = b.shape
    assert n == sum(p.shape[1] for p in pieces) <= N_SPLIT

    def body(*refs):
        a_refs, b_ref, o_ref = refs[:-2], refs[-2], refs[-1]
        o_ref[...] = _nt(_side_by_side(a_refs), b_ref[...])

    return pl.pallas_call(
        body, grid=(m // TM,), name=name,
        in_specs=[pl.BlockSpec((TM, p.shape[1]), lambda i: (i, 0)) for p in pieces] + [pl.BlockSpec((k, n), lambda i: (0, 0))],
        out_specs=pl.BlockSpec((TM, k), lambda i: (i, 0)),
        out_shape=SDS((m, k), F32),
    )(*pieces, b)


def mm_nt_halves(a0, a1, b4, name):
    m = a0.shape[0]
    _, k, tc = b4.shape

    def body(a0_ref, a1_ref, b_ref, o_ref):
        c = pl.program_id(1)

        @pl.when(c == 0)
        def _():
            o_ref[...] = _nt(a0_ref[...], b_ref[...])

        @pl.when(c == 1)
        def _():
            o_ref[...] += _nt(a0_ref[...], b_ref[...])

        @pl.when(c >= 2)
        def _():
            o_ref[...] += _nt(a1_ref[...], b_ref[...])

    return pl.pallas_call(
        body, grid=(m // TM, N_CHIP), name=name,
        in_specs=[pl.BlockSpec((TM, tc), lambda i, c: (i, jnp.minimum(c, 1))),
                  pl.BlockSpec((TM, tc), lambda i, c: (i, jnp.maximum(c - 2, 0))),
                  pl.BlockSpec((None, k, tc), lambda i, c: (c, 0, 0))],
        out_specs=pl.BlockSpec((TM, k), lambda i, c: (i, 0)),
        out_shape=SDS((m, k), F32),
    )(a0, a1, b4)


def mm_tn(a_pieces, b_pieces, name):
    m = a_pieces[0].shape[0]
    k = sum(p.shape[1] for p in a_pieces)
    n = sum(p.shape[1] for p in b_pieces)
    assert n <= N_SPLIT
    n_a = len(a_pieces)

    def body(*refs):
        a_refs, b_refs, o_ref = refs[:n_a], refs[n_a:-1], refs[-1]
        o_ref[...] = _tn(_side_by_side(a_refs), _side_by_side(b_refs)).astype(BF16)

    return pl.pallas_call(
        body, grid=(1,), name=name,
        in_specs=[pl.BlockSpec((m, p.shape[1]), lambda i: (0, 0)) for p in list(a_pieces) + list(b_pieces)],
        out_specs=pl.BlockSpec((k, n), lambda i: (0, 0)), out_shape=SDS((k, n), BF16),
    )(*a_pieces, *b_pieces)


def mm_tn_halves(a, b0, b1, name):
    m, k = a.shape
    tn = b0.shape[1] // 2

    def body(a_ref, b0_ref, b1_ref, o_ref):
        j = pl.program_id(0)

        @pl.when(j < 2)
        def _():
            o_ref[...] = _tn(a_ref[...], b0_ref[...]).astype(BF16)

        @pl.when(j >= 2)
        def _():
            o_ref[...] = _tn(a_ref[...], b1_ref[...]).astype(BF16)

    return pl.pallas_call(
        body, grid=(N_CHIP,), name=name,
        in_specs=[pl.BlockSpec((m, k), lambda j: (0, 0)),
                  pl.BlockSpec((m, tn), lambda j: (0, jnp.minimum(j, 1))),
                  pl.BlockSpec((m, tn), lambda j: (0, jnp.maximum(j - 2, 0)))],
        out_specs=pl.BlockSpec((None, k, tn), lambda j: (j, 0, 0)), out_shape=SDS((N_CHIP, k, tn), BF16),
    )(a, b0, b1)


_ROW = pl.BlockSpec((TS, D), lambda i: (i, 0))
_VEC = pl.BlockSpec((1, D), lambda i: (0, 0))


def modnorm_fwd(x, g, sc, sh, name):
    def body(x_ref, g_ref, sc_ref, sh_ref, o_ref):
        o_ref[...] = _modnorm(x_ref[...], g_ref[...], sc_ref[...], sh_ref[...]).astype(BF16)

    return pl.pallas_call(body, grid=(S // TS,), name=name, in_specs=[_ROW, _VEC, _VEC, _VEC], out_specs=_ROW,
                          out_shape=SDS((S, D), BF16))(x, g, sc, sh)


def modnorm_bwd(x, g, sc, sh, dh, dx_in, name):
    def body(x_ref, g_ref, sc_ref, sh_ref, dh_ref, dxin_ref, dx_ref, dg_ref, dsc_ref, dsh_ref):
        _, vjp = jax.vjp(_modnorm, x_ref[...], g_ref[...], sc_ref[...], sh_ref[...])
        dx, dg, dsc, dsh = vjp(dh_ref[...])
        dx_ref[...] = dxin_ref[...] + dx
        first = pl.program_id(0) == 0
        _acc(dg_ref, dg, first)
        _acc(dsc_ref, dsc, first)
        _acc(dsh_ref, dsh, first)

    vec = SDS((1, D), F32)
    return pl.pallas_call(body, grid=(S // TS,), name=name, in_specs=[_ROW, _VEC, _VEC, _VEC, _ROW, _ROW],
                          out_specs=[_ROW, _VEC, _VEC, _VEC], out_shape=[SDS((S, D), F32), vec, vec, vec])(
                              x, g, sc, sh, dh, dx_in)


def resid_fwd(x, y, ga, g, name):
    def body(x_ref, y_ref, ga_ref, g_ref, o_ref):
        o_ref[...] = _resid(x_ref[...], y_ref[...], ga_ref[...], g_ref[...])

    return pl.pallas_call(body, grid=(S // TS,), name=name, in_specs=[_ROW, _ROW, _VEC, _VEC], out_specs=_ROW,
                          out_shape=SDS((S, D), F32))(x, y, ga, g)


def resid_bwd(y, ga, g, dxo, name):
    def body(y_ref, ga_ref, g_ref, dxo_ref, dy_ref, dga_ref, dg_ref):
        _, vjp = jax.vjp(_gated_norm, y_ref[...], ga_ref[...], g_ref[...])
        dy, dga, dg = vjp(dxo_ref[...])
        dy_ref[...] = dy.astype(BF16)
        first = pl.program_id(0) == 0
        _acc(dga_ref, dga, first)
        _acc(dg_ref, dg, first)

    vec = SDS((1, D), F32)
    return pl.pallas_call(body, grid=(S // TS,), name=name, in_specs=[_ROW, _VEC, _VEC, _ROW],
                          out_specs=[_ROW, _VEC, _VEC], out_shape=[SDS((S, D), BF16), vec, vec])(y, ga, g, dxo)


def _gated_norm(y, ga, g):
    return ga * _rms(y, g)


def resid_modnorm_fwd(x, y, ga, g_post, g_pre, sc, sh, name):
    def body(x_ref, y_ref, ga_ref, gp_ref, g_ref, sc_ref, sh_ref, xo_ref, h_ref):
        xn = _resid(x_ref[...], y_ref[...], ga_ref[...], gp_ref[...])
        xo_ref[...] = xn
        h_ref[...] = _modnorm(xn, g_ref[...], sc_ref[...], sh_ref[...]).astype(BF16)

    return pl.pallas_call(body, grid=(S // TS,), name=name, in_specs=[_ROW, _ROW] + [_VEC] * 5, out_specs=[_ROW, _ROW],
                          out_shape=[SDS((S, D), F32), SDS((S, D), BF16)])(x, y, ga, g_post, g_pre, sc, sh)


def modnorm_resid_bwd(x, g, sc, sh, dh, dx_in, y, ga, g_post, name):
    def body(x_ref, g_ref, sc_ref, sh_ref, dh_ref, dxin_ref, y_ref, ga_ref, gp_ref,
             dx_ref, dg_ref, dsc_ref, dsh_ref, dy_ref, dga_ref, dgp_ref):
        _, vjp = jax.vjp(_modnorm, x_ref[...], g_ref[...], sc_ref[...], sh_ref[...])
        dx, dg, dsc, dsh = vjp(dh_ref[...])
        dx = dxin_ref[...] + dx
        dx_ref[...] = dx
        _, vjp = jax.vjp(_gated_norm, y_ref[...], ga_ref[...], gp_ref[...])
        dy, dga, dgp = vjp(dx)
        dy_ref[...] = dy.astype(BF16)
        first = pl.program_id(0) == 0
        for ref, val in ((dg_ref, dg), (dsc_ref, dsc), (dsh_ref, dsh), (dga_ref, dga), (dgp_ref, dgp)):
            _acc(ref, val, first)

    vec = SDS((1, D), F32)
    return pl.pallas_call(body, grid=(S // TS,), name=name,
                          in_specs=[_ROW, _VEC, _VEC, _VEC, _ROW, _ROW, _ROW, _VEC, _VEC],
                          out_specs=[_ROW, _VEC, _VEC, _VEC, _ROW, _VEC, _VEC],
                          out_shape=[SDS((S, D), F32), vec, vec, vec, SDS((S, D), BF16), vec, vec])(
                              x, g, sc, sh, dh, dx_in, y, ga, g_post)


def loss_grad(xf, target, name):
    def body(x_ref, t_ref, dx_ref, l_ref):
        err = x_ref[...] - t_ref[...]
        dx_ref[...] = err * (1.0 / D)
        part = 0.5 * jnp.sum(jnp.mean(err * err, axis=-1, keepdims=True), axis=0, keepdims=True)
        _acc(l_ref, jnp.broadcast_to(part, (1, 128)), pl.program_id(0) == 0)

    return pl.pallas_call(body, grid=(S // TS,), name=name, in_specs=[_ROW, _ROW],
                          out_specs=[_ROW, pl.BlockSpec((1, 128), lambda i: (0, 0))],
                          out_shape=[SDS((S, D), F32), SDS((1, 128), F32)])(xf, target)


_FG_SPEC = pl.BlockSpec((S, 128), lambda i: (0, Z_FG // 128))


def _tri128(lower):
    r = lax.broadcasted_iota(jnp.int32, (128, 128), 0)
    c = lax.broadcasted_iota(jnp.int32, (128, 128), 1)
    return ((r >= c) if lower else (r <= c)).astype(F32)


def fgate_fwd(z, bf, name):
    def body(z_ref, b_ref, cc_ref, cr_ref):
        tri = _tri128(True)
        carry = jnp.zeros((1, 128), F32)
        for i in range(S // 128):
            rows = pl.ds(i * 128, 128)
            lf = jax.nn.log_sigmoid(z_ref[rows, :] + b_ref[...])
            c = jnp.dot(tri, lf, precision=lax.Precision.HIGHEST, preferred_element_type=F32) + carry
            cc_ref[rows, :] = c
            carry = c[127:128, :]
        cr_ref[...] = cc_ref[...].T

    return pl.pallas_call(body, name=name, grid=(1,),
                          in_specs=[_FG_SPEC, pl.BlockSpec((1, 128), lambda i: (0, 0))],
                          out_specs=[pl.BlockSpec((S, 128), lambda i: (0, 0)), pl.BlockSpec((128, S), lambda i: (0, 0))],
                          out_shape=[SDS((S, 128), F32), SDS((128, S), F32)])(z, bf)


def fgate_bwd(z, bf, dcq, dck, name):
    def body(z_ref, b_ref, dcq_ref, dck_ref, dz_ref, db_ref, col_ref):
        col_ref[...] = dcq_ref[...] + jnp.concatenate([dck_ref[...], jnp.zeros((120, S), F32)], axis=0).T
        tri = _tri128(False)
        carry = jnp.zeros((1, 128), F32)
        db = jnp.zeros((1, 128), F32)
        for i in reversed(range(S // 128)):
            rows = pl.ds(i * 128, 128)
            dlf = jnp.dot(tri, col_ref[rows, :], precision=lax.Precision.HIGHEST, preferred_element_type=F32) + carry
            carry = dlf[0:1, :]
            dz = dlf * jax.nn.sigmoid(-(z_ref[rows, :] + b_ref[...]))
            dz_ref[rows, :] = dz.astype(BF16)
            db = db + jnp.sum(dz, axis=0, keepdims=True)
        db_ref[...] = db

    return pl.pallas_call(body, name=name, grid=(1,),
                          in_specs=[_FG_SPEC, pl.BlockSpec((1, 128), lambda i: (0, 0)),
                                    pl.BlockSpec((S, 128), lambda i: (0, 0)), pl.BlockSpec((8, S), lambda i: (0, 0))],
                          out_specs=[pl.BlockSpec((S, 128), lambda i: (0, 0)), pl.BlockSpec((1, 128), lambda i: (0, 0))],
                          out_shape=[SDS((S, 128), BF16), SDS((1, 128), F32)],
                          scratch_shapes=[pltpu.VMEM((S, 128), F32)])(z, bf, dcq, dck)


TQ = 256


def _head_mask(hh):
    lane = lax.broadcasted_iota(jnp.int32, (TQ, 128), 1)
    return (lane >= 64 * hh) & (lane < 64 * hh + 64)


def _fox_specs():
    q = pl.BlockSpec((TQ, 256), lambda i: (i, Z_FOX_Q // 256))
    k = pl.BlockSpec((S, 256), lambda i: (0, Z_FOX_K // 256))
    v = pl.BlockSpec((S, 256), lambda i: (0, Z_FOX_V // 256))
    cc = pl.BlockSpec((TQ, 128), lambda i: (i, 0))
    cr = pl.BlockSpec((8, S), lambda i: (0, 0))
    return q, k, v, cc, cr


FOX_SPAN = 2
FOX_GROUPS = S // (FOX_SPAN * TQ)


def _fox_scores(qm, k, cc_h, cr_h, i):
    klen = k.shape[0]
    s = _nt(qm, k) + cc_h - cr_h
    qpos = i * TQ + lax.broadcasted_iota(jnp.int32, (TQ, klen), 0)
    kpos = lax.broadcasted_iota(jnp.int32, (TQ, klen), 1)
    return jnp.where(kpos <= qpos, s, NEG)


def _for_key_length(i, fn):
    for g in range(FOX_GROUPS):
        pl.when(i // FOX_SPAN == g)(functools.partial(fn, (g + 1) * FOX_SPAN * TQ))


def fox_fwd(z, cumc, cumr, name):
    def body(q_ref, k_ref, v_ref, cc_ref, cr_ref, o_ref, l_ref):
        i = pl.program_id(0)

        def block(klen):
            lane = lax.broadcasted_iota(jnp.int32, (TQ, 128), 1)
            cc = cc_ref[...]
            lse = jnp.zeros((TQ, 128), F32)
            for p in range(2):
                cols = pl.ds(128 * p, 128)
                q = q_ref[:, cols]
                k = k_ref[0:klen, cols].astype(BF16)
                v = v_ref[0:klen, cols].astype(BF16)
                o_pair = jnp.zeros((TQ, 128), F32)
                for hh in range(2):
                    h = 2 * p + hh
                    hm = _head_mask(hh)
                    qm = jnp.where(hm, q * SCALE, 0.0).astype(BF16)
                    s = _fox_scores(qm, k, cc[:, h:h + 1], cr_ref[h:h + 1, 0:klen], i)
                    m = jnp.max(s, axis=1, keepdims=True)
                    e = jnp.exp(s - m)
                    l = jnp.sum(e, axis=1, keepdims=True)
                    o_pair = jnp.where(hm, _nn(e.astype(BF16), v) / l, o_pair)
                    lse = jnp.where(lane == h, m + jnp.log(l), lse)
                o_ref[:, cols] = o_pair
            l_ref[...] = lse

        _for_key_length(i, block)

    q, k, v, cc, cr = _fox_specs()
    return pl.pallas_call(body, grid=(S // TQ,), name=name, in_specs=[q, k, v, cc, cr],
                          out_specs=[pl.BlockSpec((TQ, 256), lambda i: (i, 0)), cc],
                          out_shape=[SDS((S, 256), F32), SDS((S, 128), F32)])(z, z, z, cumc, cumr)


def fox_bwd(z, cumc, cumr, lse, o, do, name):
    steps = S // TQ

    def body(q_ref, k_ref, v_ref, cc_ref, cr_ref, l_ref, o_ref, do_ref, dq_ref, dk_ref, dv_ref, dcq_ref, dck_ref,
             dk_acc, dv_acc):
        i = pl.program_id(0)

        @pl.when(i == 0)
        def _():
            dk_acc[...] = jnp.zeros_like(dk_acc)
            dv_acc[...] = jnp.zeros_like(dv_acc)
            dck_ref[...] = jnp.zeros_like(dck_ref)

        def block(klen):
            lane = lax.broadcasted_iota(jnp.int32, (TQ, 128), 1)
            cc = cc_ref[...]
            lse_all = l_ref[...]
            dcq = jnp.zeros((TQ, 128), F32)
            for p in range(2):
                cols = pl.ds(128 * p, 128)
                q = q_ref[:, cols]
                k = k_ref[0:klen, cols].astype(BF16)
                v = v_ref[0:klen, cols].astype(BF16)
                o_p = o_ref[:, cols]
                do_p = do_ref[:, cols]
                dq_pair = jnp.zeros((TQ, 128), F32)
                dk_pair = jnp.zeros((klen, 128), F32)
                dv_pair = jnp.zeros((klen, 128), F32)
                for hh in range(2):
                    h = 2 * p + hh
                    hm = _head_mask(hh)
                    qm = jnp.where(hm, q * SCALE, 0.0).astype(BF16)
                    s = _fox_scores(qm, k, cc[:, h:h + 1], cr_ref[h:h + 1, 0:klen], i)
                    pn = jnp.exp(s - lse_all[:, h:h + 1])
                    dom = jnp.where(hm, do_p, 0.0)
                    dl = jnp.sum(dom * o_p, axis=1, keepdims=True)
                    dom = dom.astype(BF16)
                    ds = pn * (_nt(dom, v) - dl)
                    dsb = ds.astype(BF16)
                    dq_pair = jnp.where(hm, _nn(dsb, k) * SCALE, dq_pair)
                    dk_pair = dk_pair + _tn(dsb, qm)
                    dv_pair = dv_pair + _tn(pn.astype(BF16), dom)
                    dck_ref[h:h + 1, 0:klen] -= jnp.sum(ds, axis=0, keepdims=True)
                    dcq = jnp.where(lane == h, jnp.sum(ds, axis=1, keepdims=True), dcq)
                dq_ref[:, cols] = dq_pair.astype(BF16)
                dk_acc[0:klen, cols] += dk_pair
                dv_acc[0:klen, cols] += dv_pair
            dcq_ref[...] = dcq

        _for_key_length(i, block)

        @pl.when(i == steps - 1)
        def _():
            dk_ref[...] = dk_acc[...].astype(BF16)
            dv_ref[...] = dv_acc[...].astype(BF16)

    q, k, v, cc, cr = _fox_specs()
    blk = pl.BlockSpec((TQ, 256), lambda i: (i, 0))
    full = pl.BlockSpec((S, 256), lambda i: (0, 0))
    return pl.pallas_call(body, grid=(steps,), name=name, in_specs=[q, k, v, cc, cr, cc, blk, blk],
                          out_specs=[blk, full, full, cc, cr],
                          out_shape=[SDS((S, 256), BF16), SDS((S, 256), BF16), SDS((S, 256), BF16), SDS((S, 128), F32),
                                     SDS((8, S), F32)],
                          scratch_shapes=[pltpu.VMEM((S, 256), F32), pltpu.VMEM((S, 256), F32)])(
                              z, z, z, cumc, cumr, lse, o, do)


W = 128
SWA_HEADS = 4


def _swa_core(first, qs, kcat, vcat, sink):
    r = lax.broadcasted_iota(jnp.int32, (SWA_HEADS * W, 2 * W), 0)
    j = lax.broadcasted_iota(jnp.int32, (SWA_HEADS * W, 2 * W), 1)
    qi = r & (W - 1)
    valid = ((j < W) & (j > qi) & jnp.logical_not(first)) | ((j >= W) & (j - W <= qi))
    s = jnp.where(valid, _nt((qs * SCALE).astype(BF16), kcat.astype(BF16)), NEG)
    m = lax.stop_gradient(jnp.maximum(jnp.max(s, axis=1, keepdims=True), sink))
    e = jnp.exp(s - m)
    den = jnp.sum(e, axis=1, keepdims=True) + jnp.exp(sink - m)
    return _nn((e / den).astype(BF16), vcat.astype(BF16))


def _kv_lanes(kv):
    lane = lax.broadcasted_iota(jnp.int32, (W, 128), 1)
    return (lane >= 64 * kv) & (lane < 64 * kv + 64)


def _swa_stack(pair0, pair1):
    blocks = []
    for h in range(SWA_HEADS):
        kv, hh = h // 2, h % 2
        a = (pair0, pair1)[kv]
        a = a if hh == kv else pltpu.roll(a, 64, axis=1)
        blocks.append(jnp.where(_kv_lanes(kv), a, 0.0))
    return jnp.concatenate(blocks, axis=0)


def _swa_unstack(stacked):
    pairs = [jnp.zeros((W, 128), F32), jnp.zeros((W, 128), F32)]
    for h in range(SWA_HEADS):
        kv, hh = h // 2, h % 2
        a = jnp.where(_kv_lanes(kv), stacked[h * W:(h + 1) * W], 0.0)
        pairs[kv] = pairs[kv] + (a if hh == kv else pltpu.roll(a, 64, axis=1))
    return pairs


def _head_rows():
    return lax.broadcasted_iota(jnp.int32, (SWA_HEADS * W, 1), 0) // W


SWA_PAIR = 2 * W


def _swa_operands(sub, q0_ref, q1_ref, kp_ref, kc_ref, vp_ref, vc_ref, sk_ref):
    rows = pl.ds(sub * W, W)
    qs = _swa_stack(q0_ref[rows, :], q1_ref[rows, :])
    k_prev, v_prev = (kp_ref[...], vp_ref[...]) if sub == 0 else (kc_ref[0:W, :], vc_ref[0:W, :])
    kcat = jnp.concatenate([k_prev, kc_ref[rows, :]], axis=0)
    vcat = jnp.concatenate([v_prev, vc_ref[rows, :]], axis=0)
    sink = jnp.zeros((SWA_HEADS * W, 1), F32)
    for h in range(SWA_HEADS):
        sink = jnp.where(_head_rows() == h, sk_ref[:, h:h + 1], sink)
    return qs, kcat, vcat, sink


def _swa_specs(blocks):
    rows = blocks * W
    q0 = pl.BlockSpec((rows, 128), lambda i: (i, Z_SWA_Q // 128))
    q1 = pl.BlockSpec((rows, 128), lambda i: (i, Z_SWA_Q // 128 + 1))
    kc = pl.BlockSpec((rows, 128), lambda i: (i, Z_SWA_K // 128))
    kp = pl.BlockSpec((W, 128), lambda i: (jnp.maximum(blocks * i - 1, 0), Z_SWA_K // 128))
    vc = pl.BlockSpec((rows, 128), lambda i: (i, Z_SWA_V // 128))
    vp = pl.BlockSpec((W, 128), lambda i: (jnp.maximum(blocks * i - 1, 0), Z_SWA_V // 128))
    sk = pl.BlockSpec((1, 128), lambda i: (0, 0))
    return q0, q1, kp, kc, vp, vc, sk


def swa_fwd(z, sinks, name):
    def body(*refs):
        o_ref = refs[-1]
        o0, o1 = _swa_unstack(_swa_core(pl.program_id(0) == 0, *_swa_operands(0, *refs[:-1])))
        o_ref[:, 0:128] = o0
        o_ref[:, 128:256] = o1

    return pl.pallas_call(body, grid=(S // W,), name=name, in_specs=list(_swa_specs(1)),
                          out_specs=pl.BlockSpec((W, 256), lambda i: (i, 0)),
                          out_shape=SDS((S, 256), F32))(z, z, z, z, z, z, sinks)


def swa_bwd(z, sinks, do, name):
    steps = S // SWA_PAIR

    def body(*refs):
        do_ref, dq_ref, dk_ref, dv_ref, dsk_ref, dk_acc, dv_acc = refs[7:]
        i = pl.program_id(0)

        @pl.when(i == 0)
        def _():
            dk_acc[...] = jnp.zeros_like(dk_acc)
            dv_acc[...] = jnp.zeros_like(dv_acc)

        lane = lax.broadcasted_iota(jnp.int32, (1, 128), 1)
        dsk = jnp.zeros((1, 128), F32)
        for sub in range(2):
            rows = pl.ds(sub * W, W)
            first = (i == 0) if sub == 0 else False
            _, vjp = jax.vjp(functools.partial(_swa_core, first), *_swa_operands(sub, *refs[:7]))
            dqs, dkcat, dvcat, dsink = vjp(_swa_stack(do_ref[rows, 0:128], do_ref[rows, 128:256]))
            dq0, dq1 = _swa_unstack(dqs)
            dq_ref[rows, 0:128] = dq0.astype(BF16)
            dq_ref[rows, 128:256] = dq1.astype(BF16)
            cur = pl.ds(pl.multiple_of((2 * i + sub) * W, W), W)
            dk_acc[cur, :] += dkcat[W:2 * W]
            dv_acc[cur, :] += dvcat[W:2 * W]

            def to_prev(dkcat=dkcat, dvcat=dvcat, sub=sub):
                prev = pl.ds(pl.multiple_of((2 * i + sub - 1) * W, W), W)
                dk_acc[prev, :] += dkcat[0:W]
                dv_acc[prev, :] += dvcat[0:W]

            if sub == 0:
                pl.when(i > 0)(to_prev)
            else:
                to_prev()
            for h in range(SWA_HEADS):
                d = jnp.sum(jnp.where(_head_rows() == h, dsink, 0.0), axis=0, keepdims=True)
                dsk = dsk + jnp.where(lane == h, d, 0.0)
        _acc(dsk_ref, dsk, i == 0)

        @pl.when(i == steps - 1)
        def _():
            dk_ref[...] = dk_acc[...].astype(BF16)
            dv_ref[...] = dv_acc[...].astype(BF16)

    blk = pl.BlockSpec((SWA_PAIR, 256), lambda i: (i, 0))
    full = pl.BlockSpec((S, 128), lambda i: (0, 0))
    return pl.pallas_call(body, grid=(steps,), name=name, in_specs=list(_swa_specs(2)) + [blk],
                          out_specs=[blk, full, full, pl.BlockSpec((1, 128), lambda n: (0, 0))],
                          out_shape=[SDS((S, 256), BF16), SDS((S, 128), BF16), SDS((S, 128), BF16), SDS((1, 128), F32)],
                          scratch_shapes=[pltpu.VMEM((S, 128), F32), pltpu.VMEM((S, 128), F32)])(
                              z, z, z, z, z, z, sinks, do)


def _glu(a, g):
    return a * jax.nn.sigmoid(g)


def _cv1_specs():
    a = pl.BlockSpec((S, 128), lambda j: (0, Z_CONV_A // 128 + j))
    g = pl.BlockSpec((S, 128), lambda j: (0, Z_CONV_G // 128 + j))
    w = pl.BlockSpec((32, 128), lambda j: (0, j))
    b = pl.BlockSpec((1, 128), lambda j: (0, j))
    h = pl.BlockSpec((S, 128), lambda j: (0, j))
    return a, g, w, b, h


def conv_dw_fwd(z, cw, cb, name):
    def body(a_ref, g_ref, w_ref, b_ref, o_ref):
        hh = _glu(a_ref[...], g_ref[...])
        acc = jnp.zeros((S, 128), F32) + b_ref[...]
        for k in range(CONV_K):
            acc = acc + _shift_down(hh, CONV_K - 1 - k) * w_ref[k:k + 1, :]
        o_ref[...] = acc

    a, g, w, b, h = _cv1_specs()
    return pl.pallas_call(body, grid=(2,), name=name, in_specs=[a, g, w, b], out_specs=h,
                          out_shape=SDS((S, 256), F32))(z, z, cw, cb)


def conv_dw_bwd(z, cw, dhc, name):
    def body(a_ref, g_ref, w_ref, dh_ref, da_ref, dg_ref, dw_ref, db_ref):
        hh, vjp = jax.vjp(_glu, a_ref[...], g_ref[...])
        dh = dh_ref[...]
        dhh = jnp.zeros((S, 128), F32)
        for k in range(CONV_K):
            n = CONV_K - 1 - k
            dhh = dhh + _shift_up(dh, n) * w_ref[k:k + 1, :]
            dw_ref[k:k + 1, :] = jnp.sum(dh * _shift_down(hh, n), axis=0, keepdims=True)
        dw_ref[CONV_K:32, :] = jnp.zeros((32 - CONV_K, 128), F32)
        db_ref[...] = jnp.sum(dh, axis=0, keepdims=True)
        da, dg = vjp(dhh)
        da_ref[...] = da.astype(BF16)
        dg_ref[...] = dg.astype(BF16)

    a, g, w, b, h = _cv1_specs()
    return pl.pallas_call(body, grid=(2,), name=name, in_specs=[a, g, w, h], out_specs=[h, h, w, b],
                          out_shape=[SDS((S, 256), BF16), SDS((S, 256), BF16), SDS((32, 256), F32), SDS((1, 256), F32)])(
                              z, z, cw, dhc)


def _ln(x, g, b):
    mu = jnp.mean(x, axis=-1, keepdims=True)
    xc = x - mu
    var = jnp.mean(xc * xc, axis=-1, keepdims=True)
    return xc * lax.rsqrt(var + EPS) * g + b


def _conv_pw(hc, lg, lb, pw, pb):
    y = jax.nn.silu(_ln(hc, lg, lb))
    return _nn(y.astype(BF16), pw.astype(BF16)) + pb


TS2 = 512
_ROW2 = pl.BlockSpec((TS2, 256), lambda i: (i, 0))
_VEC2 = pl.BlockSpec((1, 256), lambda i: (0, 0))
_MAT2 = pl.BlockSpec((256, 256), lambda i: (0, 0))


def conv_pw_fwd(hc, lg, lb, pw, pb, name):
    def body(h_ref, lg_ref, lb_ref, pw_ref, pb_ref, o_ref):
        o_ref[...] = _conv_pw(h_ref[...], lg_ref[...], lb_ref[...], pw_ref[...], pb_ref[...])

    return pl.pallas_call(body, grid=(S // TS2,), name=name, in_specs=[_ROW2, _VEC2, _VEC2, _MAT2, _VEC2],
                          out_specs=_ROW2, out_shape=SDS((S, 256), F32))(hc, lg, lb, pw, pb)


def conv_pw_bwd(hc, lg, lb, pw, pb, dy, name):
    def body(h_ref, lg_ref, lb_ref, pw_ref, pb_ref, dy_ref, dh_ref, dlg_ref, dlb_ref, dpw_ref, dpb_ref):
        _, vjp = jax.vjp(_conv_pw, h_ref[...], lg_ref[...], lb_ref[...], pw_ref[...], pb_ref[...])
        dh, dlg, dlb, dpw, dpb = vjp(dy_ref[...])
        dh_ref[...] = dh
        first = pl.program_id(0) == 0
        _acc(dlg_ref, dlg, first)
        _acc(dlb_ref, dlb, first)
        _acc(dpw_ref, dpw, first)
        _acc(dpb_ref, dpb, first)

    vec = SDS((1, 256), F32)
    return pl.pallas_call(body, grid=(S // TS2,), name=name, in_specs=[_ROW2, _VEC2, _VEC2, _MAT2, _VEC2, _ROW2],
                          out_specs=[_ROW2, _VEC2, _VEC2, _MAT2, _VEC2],
                          out_shape=[SDS((S, 256), F32), vec, vec, SDS((256, 256), F32), vec])(hc, lg, lb, pw, pb, dy)


def _sgu_block(u0, u1, v0, v1, lg0, lg1, lb0, lb1, w0, w1, w2, w3, bt):
    u0, u1, v0, v1 = (jax.nn.gelu(a) for a in (u0, u1, v0, v1))
    mu = (jnp.sum(v0, axis=1, keepdims=True) + jnp.sum(v1, axis=1, keepdims=True)) * (1.0 / GW)
    c0, c1 = v0 - mu, v1 - mu
    var = (jnp.sum(c0 * c0, axis=1, keepdims=True) + jnp.sum(c1 * c1, axis=1, keepdims=True)) * (1.0 / GW)
    r = lax.rsqrt(var + EPS)
    n0 = c0 * r * lg0 + lb0
    n1 = c1 * r * lg1 + lb1
    row = lax.broadcasted_iota(jnp.int32, (128, 128), 0)
    col = lax.broadcasted_iota(jnp.int32, (128, 128), 1)
    tri = row >= col
    outs = []
    for p, (n, u, wa, wb) in enumerate(((n0, u0, w0, w1), (n1, u1, w2, w3))):
        nb = n.astype(BF16)
        ma = _nn(jnp.where(tri, wa, 0.0).astype(BF16), nb)
        mb = _nn(jnp.where(tri, wb, 0.0).astype(BF16), nb)
        expand = (row == 2 * p + col // 64).astype(F32)
        bias = jnp.dot(bt, expand, precision=lax.Precision.HIGHEST, preferred_element_type=F32)
        outs.append(u * (jnp.where(col < 64, ma, mb) + bias))
    return outs[0], outs[1]


def _sgu_specs(chunks=1):
    def col(c):
        return pl.BlockSpec((chunks * 128, 128), lambda n, c=c: (n, c))
    zs = [col(Z_SGU_U // 128), col(Z_SGU_U // 128 + 1), col(Z_SGU_V // 128), col(Z_SGU_V // 128 + 1)]
    vec = [pl.BlockSpec((1, 128), lambda n: (0, 0)), pl.BlockSpec((1, 128), lambda n: (0, 1))]
    ws = [pl.BlockSpec((None, 128, 128), lambda n, g=g: (g, 0, 0)) for g in range(4)]
    bt = pl.BlockSpec((128, 128), lambda n: (0, 0))
    return zs + vec + vec + ws + [bt]


def sgu_fwd(z, lg, lb, w, bt, name):
    def body(*refs):
        o_ref = refs[-1]
        y0, y1 = _sgu_block(*[r[...] for r in refs[:-1]])
        o_ref[:, 0:128] = y0
        o_ref[:, 128:256] = y1

    return pl.pallas_call(body, grid=(S // 128,), name=name, in_specs=_sgu_specs(),
                          out_specs=pl.BlockSpec((128, 256), lambda n: (n, 0)),
                          out_shape=SDS((S, 256), F32))(z, z, z, z, lg, lg, lb, lb, w, w, w, w, bt)


def sgu_bwd(z, lg, lb, w, bt, dy, name):
    chunks = 2

    def body(*refs):
        ins, dy_ref = refs[:13], refs[13]
        du_ref, dv_ref, dlg_ref, dlb_ref, dw_ref, dbt_ref = refs[14:]

        @pl.when(pl.program_id(0) == 0)
        def _():
            dlg_ref[...] = jnp.zeros_like(dlg_ref)
            dlb_ref[...] = jnp.zeros_like(dlb_ref)
            dw_ref[...] = jnp.zeros_like(dw_ref)
            dbt_ref[...] = jnp.zeros_like(dbt_ref)

        for sub in range(chunks):
            rows = pl.ds(sub * 128, 128)
            _, vjp = jax.vjp(_sgu_block, *[r[rows, :] for r in ins[:4]], *[r[...] for r in ins[4:]])
            du0, du1, dv0, dv1, dlg0, dlg1, dlb0, dlb1, dw0, dw1, dw2, dw3, dbt = vjp(
                (dy_ref[rows, 0:128], dy_ref[rows, 128:256]))
            du_ref[rows, 0:128] = du0.astype(BF16)
            du_ref[rows, 128:256] = du1.astype(BF16)
            dv_ref[rows, 0:128] = dv0.astype(BF16)
            dv_ref[rows, 128:256] = dv1.astype(BF16)
            dlg_ref[:, 0:128] += dlg0
            dlg_ref[:, 128:256] += dlg1
            dlb_ref[:, 0:128] += dlb0
            dlb_ref[:, 128:256] += dlb1
            for g, d in enumerate((dw0, dw1, dw2, dw3)):
                dw_ref[g] += d
            dbt_ref[...] += dbt

    blk = pl.BlockSpec((chunks * 128, 256), lambda n: (n, 0))
    vec = pl.BlockSpec((1, 256), lambda n: (0, 0))
    return pl.pallas_call(body, grid=(S // (chunks * 128),), name=name, in_specs=_sgu_specs(chunks) + [blk],
                          out_specs=[blk, blk, vec, vec, pl.BlockSpec((4, 128, 128), lambda n: (0, 0, 0)),
                                     pl.BlockSpec((128, 128), lambda n: (0, 0))],
                          out_shape=[SDS((S, 256), BF16), SDS((S, 256), BF16), SDS((1, 256), F32), SDS((1, 256), F32),
                                     SDS((4, 128, 128), F32), SDS((128, 128), F32)])(
                                         z, z, z, z, lg, lg, lb, lb, w, w, w, w, bt, dy)


def _group_norm(y0, y1, y2, y3, g0, g1, g2, g3):
    return tuple(_rms(y, g) for y, g in zip((y0, y1, y2, y3), (g0, g1, g2, g3)))


_GROW = pl.BlockSpec((TS2, 256), lambda i: (i, 0))
_GCAT = pl.BlockSpec((TS2, D), lambda i: (i, 0))
_GVEC = [pl.BlockSpec((1, 256), lambda i, j=j: (0, j)) for j in range(4)]


def group_norm_fwd(ys, gg, name):
    def body(*refs):
        o_ref = refs[-1]
        outs = _group_norm(*[r[...] for r in refs[:-1]])
        for j, c in enumerate(outs):
            o_ref[:, 256 * j:256 * (j + 1)] = c.astype(BF16)

    return pl.pallas_call(body, grid=(S // TS2,), name=name, in_specs=[_GROW] * 4 + _GVEC, out_specs=_GCAT,
                          out_shape=SDS((S, D), BF16))(*ys, gg, gg, gg, gg)


def group_norm_bwd(ys, gg, dcat, name):
    def body(*refs):
        ins, dc_ref = refs[:8], refs[8]
        dy_refs, dg_ref = refs[9:13], refs[13]
        _, vjp = jax.vjp(_group_norm, *[r[...] for r in ins])
        grads = vjp(tuple(dc_ref[:, 256 * j:256 * (j + 1)] for j in range(4)))
        first = pl.program_id(0) == 0

        @pl.when(first)
        def _():
            dg_ref[...] = jnp.zeros_like(dg_ref)

        for j in range(4):
            dy_refs[j][...] = grads[j]
            dg_ref[:, 256 * j:256 * (j + 1)] += grads[4 + j]

    return pl.pallas_call(body, grid=(S // TS2,), name=name, in_specs=[_GROW] * 4 + _GVEC + [_GCAT],
                          out_specs=[_GROW] * 4 + [pl.BlockSpec((1, D), lambda i: (0, 0))],
                          out_shape=[SDS((S, 256), F32)] * 4 + [SDS((1, D), F32)])(*ys, gg, gg, gg, gg, dcat)


FB = 256
N_FB = DFF // FB


def _ffn_gate(ug, uv, wg0, wg1, wg2, wv0, wv1, wv2, bg, bv):
    cg = bg + _shift_down(ug, 2) * wg0 + _shift_down(ug, 1) * wg1 + ug * wg2
    cv = bv + _shift_down(uv, 2) * wv0 + _shift_down(uv, 1) * wv1 + uv * wv2
    return jax.nn.silu(cg) * cv


def _gate_specs():
    ug = pl.BlockSpec((S, FB), lambda j: (0, j))
    uv = pl.BlockSpec((S, FB), lambda j: (0, j + N_FB))
    wg = pl.BlockSpec((8, FB), lambda j: (0, j))
    wv = pl.BlockSpec((8, FB), lambda j: (0, j + N_FB))
    bg = pl.BlockSpec((1, FB), lambda j: (0, j))
    bv = pl.BlockSpec((1, FB), lambda j: (0, j + N_FB))
    return ug, uv, wg, wv, bg, bv


def _gate_args(ug_ref, uv_ref, wg_ref, wv_ref, bg_ref, bv_ref):
    return (ug_ref[...], uv_ref[...], wg_ref[0:1, :], wg_ref[1:2, :], wg_ref[2:3, :],
            wv_ref[0:1, :], wv_ref[1:2, :], wv_ref[2:3, :], bg_ref[...], bv_ref[...])


def ffn_gate_fwd(u, cw, cb, name):
    def body(ug_ref, uv_ref, wg_ref, wv_ref, bg_ref, bv_ref, o_ref):
        o_ref[...] = _ffn_gate(*_gate_args(ug_ref, uv_ref, wg_ref, wv_ref, bg_ref, bv_ref)).astype(BF16)

    return pl.pallas_call(body, grid=(N_FB,), name=name, in_specs=list(_gate_specs()),
                          out_specs=pl.BlockSpec((S, FB), lambda j: (0, j)),
                          out_shape=SDS((S, DFF), BF16))(u, u, cw, cw, cb, cb)


def ffn_gate_bwd(u, cw, cb, da, name):
    def body(ug_ref, uv_ref, wg_ref, wv_ref, bg_ref, bv_ref, da_ref, dug_ref, duv_ref, dwg_ref, dwv_ref, dbg_ref, dbv_ref):
        _, vjp = jax.vjp(_ffn_gate, *_gate_args(ug_ref, uv_ref, wg_ref, wv_ref, bg_ref, bv_ref))
        dug, duv, g0, g1, g2, v0, v1, v2, dbg, dbv = vjp(da_ref[...])
        dug_ref[...] = dug.astype(BF16)
        duv_ref[...] = duv.astype(BF16)
        for k, (a, b) in enumerate(((g0, v0), (g1, v1), (g2, v2))):
            dwg_ref[k:k + 1, :] = a
            dwv_ref[k:k + 1, :] = b
        dwg_ref[FFN_K:8, :] = jnp.zeros((8 - FFN_K, FB), F32)
        dwv_ref[FFN_K:8, :] = jnp.zeros((8 - FFN_K, FB), F32)
        dbg_ref[...] = dbg
        dbv_ref[...] = dbv

    ug, uv, wg, wv, bg, bv = _gate_specs()
    half = pl.BlockSpec((S, FB), lambda j: (0, j))
    whalf = pl.BlockSpec((8, FB), lambda j: (0, j))
    bhalf = pl.BlockSpec((1, FB), lambda j: (0, j))
    return pl.pallas_call(body, grid=(N_FB,), name=name, in_specs=[ug, uv, wg, wv, bg, bv, half],
                          out_specs=[half, half, whalf, whalf, bhalf, bhalf],
                          out_shape=[SDS((S, DFF), BF16), SDS((S, DFF), BF16), SDS((8, DFF), F32), SDS((8, DFF), F32),
                                     SDS((1, DFF), F32), SDS((1, DFF), F32)])(u, u, cw, cw, cb, cb, da)


def _adamw(w, g, m, v):
    m = ADAM_B1 * m + (1.0 - ADAM_B1) * g
    v = ADAM_B2 * v + (1.0 - ADAM_B2) * (g * g)
    m_hat = m / (1.0 - ADAM_B1 ** ADAM_STEP)
    v_hat = v / (1.0 - ADAM_B2 ** ADAM_STEP)
    delta = -ADAM_LR * (m_hat / (jnp.sqrt(v_hat) + ADAM_EPS) + ADAM_WD * w)
    return delta, m, v


def sum_pieces(chip, r, own, layer, base, name):
    n, rows, cols = r.shape
    tr = _row_tile(rows, cols)

    def body(chip_ref, r_ref, own_ref, *rest):
        o_ref = rest[-1]
        acc = jnp.zeros((tr, cols), F32)
        for j in range(n):
            acc = acc + jnp.where(chip_ref[0] == j, own_ref[0], r_ref[j]).astype(F32)
        o_ref[...] = acc

    extra = {} if base is None else dict(input_output_aliases={3: 0})
    grid_spec = pltpu.PrefetchScalarGridSpec(
        num_scalar_prefetch=1, grid=(rows // tr,),
        in_specs=[pl.BlockSpec((n, tr, cols), lambda i, c: (0, i, 0)), pl.BlockSpec((1, tr, cols), lambda i, c: (c[0], i, 0))]
        + ([] if base is None else [ANY]),
        out_specs=pl.BlockSpec((None, tr, cols), lambda i, c: (layer, i, 0)))
    return pl.pallas_call(body, grid_spec=grid_spec, name=name, out_shape=SDS((DEPTH, rows, cols), F32), **extra)(
        *([chip, r, own] if base is None else [chip, r, own, base]))


LANE_BLOCK = 128


def sum_pieces_t(chip, rs, owns, name):
    n, rows, cols = rs[0].shape

    def body(chip_ref, *refs):
        o_ref = refs[-1]
        for l in range(DEPTH):
            r_ref, own_ref = refs[2 * l], refs[2 * l + 1]
            acc = jnp.zeros((rows, LANE_BLOCK), F32)
            for j in range(n):
                acc = acc + jnp.where(chip_ref[0] == j, own_ref[0], r_ref[j]).astype(F32)
            o_ref[:, l, :] = acc

    r_spec = pl.BlockSpec((n, rows, LANE_BLOCK), lambda i, c: (0, 0, i))
    own_spec = pl.BlockSpec((1, rows, LANE_BLOCK), lambda i, c: (c[0], 0, i))
    grid_spec = pltpu.PrefetchScalarGridSpec(
        num_scalar_prefetch=1, grid=(cols // LANE_BLOCK,), in_specs=[r_spec, own_spec] * DEPTH,
        out_specs=pl.BlockSpec((rows, DEPTH, LANE_BLOCK), lambda i, c: (0, 0, i)))
    ops = [a for pair in zip(rs, owns) for a in pair]
    return pl.pallas_call(body, grid_spec=grid_spec, name=name, out_shape=SDS((rows, DEPTH, cols), F32))(chip, *ops)


def adamw_pair_t(w, p, q, m, v, name):
    rows, depth, cols = w.shape

    def body(w_ref, p_ref, q_ref, m_ref, v_ref, g_ref, d_ref, nm_ref, nv_ref):
        g = p_ref[...] + q_ref[...]
        g_ref[...] = g
        d_ref[...], nm_ref[...], nv_ref[...] = _adamw(w_ref[...], g, m_ref[...], v_ref[...])

    spec = pl.BlockSpec((rows, depth, LANE_BLOCK), lambda i: (0, 0, i))
    return pl.pallas_call(body, grid=(cols // LANE_BLOCK,), name=name, in_specs=[spec] * 5, out_specs=[spec] * 4,
                          out_shape=[SDS(w.shape, F32)] * 4)(w, p, q, m, v)


def adamw_pair(w, p, q, m, v, name):
    rows, cols = w.shape
    tr = _row_tile(rows, cols)

    def body(w_ref, p_ref, q_ref, m_ref, v_ref, g_ref, d_ref, nm_ref, nv_ref):
        g = p_ref[...] + q_ref[...]
        g_ref[...] = g
        d_ref[...], nm_ref[...], nv_ref[...] = _adamw(w_ref[...], g, m_ref[...], v_ref[...])

    spec = pl.BlockSpec((tr, cols), lambda i: (i, 0))
    return pl.pallas_call(body, grid=(rows // tr,), name=name, in_specs=[spec] * 5, out_specs=[spec] * 4,
                          out_shape=[SDS((rows, cols), F32)] * 4)(w, p, q, m, v)


_PACK_LAYOUT = (('b_ada', 6 * D), ('g_pre_mix', D), ('g_post_mix', D), ('g_pre_ffn', D), ('g_post_ffn', D),
                ('b_fgate', 128), ('conv_b', GW), ('conv_ln_g', GW), ('conv_ln_b', GW), ('conv_pw_b', GW),
                ('swa_sinks', 128), ('sgu_ln_g', GW), ('sgu_ln_b', GW), ('sgu_b', 4 * 128), ('g_group', D),
                ('ffn_conv_b', NUP))
_PACK_WIDTH = dict(_PACK_LAYOUT)
_PACK_ROW, _LAYER_ROWS = {}, 0
for _name, _width in _PACK_LAYOUT:
    _PACK_ROW[_name] = _LAYER_ROWS
    _LAYER_ROWS += -(-_width // D)
LOSS_ROW = DEPTH * _LAYER_ROWS
PACK_ROWS = -(-(LOSS_ROW + 1) // 8) * 8


def _segments(offset, width):
    out, s = [], 0
    while s < width:
        row, col = divmod(offset + s, D)
        n = min(width - s, D - col)
        out.append((s, row, col, n))
        s += n
    return out


def pack_small(grads, dmods, loss_row, name):
    ops, plan = [], []
    for l in range(DEPTH):
        pieces = [('b_ada', j * D, a) for j, a in enumerate(dmods[l])]
        pieces += [(n, 0, grads[l][n]) for n, _ in _PACK_LAYOUT if n not in ('b_ada', 'sgu_b', 'ffn_conv_b')]
        pieces += [('ffn_conv_b', j * DFF, a) for j, a in enumerate(grads[l]['ffn_conv_b'])]
        for n, off, a in pieces:
            plan.append((len(ops), l, n, off))
            ops.append(a)
    bts = [grads[l]['sgu_bt'] for l in range(DEPTH)]
    sws = [grads[l]['sgu_w'] for l in range(DEPTH)]
    n_vec = len(ops)

    def body(*refs):
        vec, bt, sw = refs[:n_vec], refs[n_vec:n_vec + DEPTH], refs[n_vec + DEPTH:n_vec + 2 * DEPTH]
        loss_ref, o_ref, ow_ref, scr = refs[n_vec + 2 * DEPTH:]
        o_ref[...] = jnp.zeros_like(o_ref)
        o_ref[LOSS_ROW:LOSS_ROW + 1, 0:128] = loss_ref[...]
        for idx, l, n, off in plan:
            base = l * _LAYER_ROWS + _PACK_ROW[n]
            width = min(vec[idx].shape[1], _PACK_WIDTH[n] - off)
            for s, row, col, lanes in _segments(off, width):
                o_ref[base + row:base + row + 1, col:col + lanes] = vec[idx][:, s:s + lanes]
        for l in range(DEPTH):
            scr[...] = bt[l][...].T
            row = l * _LAYER_ROWS + _PACK_ROW['sgu_b']
            for g in range(4):
                o_ref[row:row + 1, 128 * g:128 * (g + 1)] = scr[g:g + 1, :]
            ow_ref[l] = sw[l][...]

    vm = pl.BlockSpec(memory_space=pltpu.VMEM)
    return pl.pallas_call(body, name=name, in_specs=[vm] * (n_vec + 2 * DEPTH + 1), out_specs=[vm, vm],
                          out_shape=[SDS((PACK_ROWS, D), F32), SDS((DEPTH, 4, 128, 128), F32)],
                          scratch_shapes=[pltpu.VMEM((128, 128), F32)])(*ops, *bts, *sws, loss_row)


def adamw_small(gall, gall_w, w, m, v, name):
    names = [n for n in SMALL]
    n_par = len(names)

    def native(ref, n, l, col, lanes):
        if n == 'sgu_b':
            return ref.at[l, pl.ds(col // 128, 1), :]
        return ref.at[pl.ds(l, 1), pl.ds(col, lanes)]

    def body(*refs):
        ga_ref, gw_ref = refs[0], refs[1]
        w_refs, m_refs, v_refs = (refs[2 + k * n_par:2 + (k + 1) * n_par] for k in range(3))
        outs = refs[2 + 3 * n_par:2 + 7 * n_par]
        loss_ref, scr = refs[-2], refs[-1]
        g = ga_ref[0]
        for j in range(1, N_DEV):
            g = g + ga_ref[j]
        scr[...] = g
        loss_ref[...] = scr[LOSS_ROW:LOSS_ROW + 1, 0:128]
        for k, n in enumerate(names):
            o_g, o_d, o_m, o_v = outs[4 * k:4 * k + 4]
            if n == 'sgu_w':
                gw = gw_ref[0]
                for j in range(1, N_DEV):
                    gw = gw + gw_ref[j]
                o_g[...] = gw
                o_d[...], o_m[...], o_v[...] = _adamw(w_refs[k][...], gw, m_refs[k][...], v_refs[k][...])
                continue
            width = w_refs[k].shape[-1] if n != 'sgu_b' else 4 * 128
            for l in range(DEPTH):
                base = l * _LAYER_ROWS + _PACK_ROW[n]
                step = 128 if n == 'sgu_b' else D
                for col in range(0, width, step):
                    lanes = min(step, width - col)
                    row, lane0 = divmod(col, D)
                    gv = scr[base + row:base + row + 1, lane0:lane0 + lanes]
                    at = functools.partial(native, n=n, l=l, col=col, lanes=lanes)
                    at(o_g)[...] = gv
                    at(o_d)[...], at(o_m)[...], at(o_v)[...] = _adamw(at(w_refs[k])[...], gv, at(m_refs[k])[...],
                                                                      at(v_refs[k])[...])

    vm = pl.BlockSpec(memory_space=pltpu.VMEM)
    params = [d[n] for d in (w, m, v) for n in names]
    outs = pl.pallas_call(body, name=name, in_specs=[vm] * (2 + 3 * n_par), out_specs=[vm] * (4 * n_par + 1),
                          out_shape=[SDS(w[n].shape, F32) for n in names for _ in range(4)] + [SDS((1, 128), F32)],
                          scratch_shapes=[pltpu.VMEM((PACK_ROWS, D), F32)])(gall, gall_w, *params)
    return {n: tuple(outs[4 * k:4 * k + 4]) for k, n in enumerate(names)}, outs[-1]


ADA_COLS = 6 * D // N_CHIP
ADA_TN = 768


def ada_mod(c_all, w, b, name):
    def body(c_ref, w_ref, b_ref, o_ref):
        ca = jax.nn.silu(c_ref[...])
        o_ref[...] = jnp.dot(ca, w_ref[...], precision=lax.Precision.HIGHEST, preferred_element_type=F32) + b_ref[...]

    return pl.pallas_call(
        body, grid=(DEPTH, ADA_COLS // ADA_TN), name=name,
        in_specs=[pl.BlockSpec((N_DEV, D), lambda l, j: (0, 0)),
                  pl.BlockSpec((None, D, ADA_TN), lambda l, j: (l, 0, j)),
                  pl.BlockSpec((None, 1, ADA_TN), lambda l, j: (l, 0, j))],
        out_specs=pl.BlockSpec((None, N_DEV, ADA_TN), lambda l, j: (l, 0, j)),
        out_shape=SDS((DEPTH, N_DEV, ADA_COLS), F32))(c_all, w, b)


def ada_update(c_all_t, dmod, w, m, v, name):
    def body(c_ref, dm_ref, w_ref, m_ref, v_ref, g_ref, d_ref, nm_ref, nv_ref):
        ca = jax.nn.silu(c_ref[...])
        g = jnp.dot(ca, dm_ref[...], precision=lax.Precision.HIGHEST, preferred_element_type=F32)
        g_ref[...] = g
        d_ref[...], nm_ref[...], nv_ref[...] = _adamw(w_ref[...], g, m_ref[...], v_ref[...])

    wspec = pl.BlockSpec((None, D, ADA_TN), lambda l, j: (l, 0, j))
    return pl.pallas_call(
        body, grid=(DEPTH, ADA_COLS // ADA_TN), name=name,
        in_specs=[pl.BlockSpec((D, N_DEV), lambda l, j: (0, 0)),
                  pl.BlockSpec((None, N_DEV, ADA_TN), lambda l, j: (l, 0, j)), wspec, wspec, wspec],
        out_specs=[wspec] * 4, out_shape=[SDS((DEPTH, D, ADA_COLS), F32)] * 4)(c_all_t, dmod, w, m, v)


_CHIP_FLIPS = ((1, 0), (0, 1), (1, 1))
_DEV_FLIPS = tuple((a, b, c) for a in (0, 1) for b in (0, 1) for c in (0, 1) if (a, b, c) != (0, 0, 0))


def _position():
    return lax.axis_index("x"), lax.axis_index("y"), lax.axis_index("c")


def _hbm_call(body, arrs, out_shapes, n_remote, name):
    n = len(arrs)
    return pl.pallas_call(
        body, name=name, in_specs=[ANY] * n, out_specs=[ANY] * n, out_shape=out_shapes,
        scratch_shapes=[pltpu.SemaphoreType.DMA((n, n_remote)), pltpu.SemaphoreType.DMA((n, n_remote)),
                        pltpu.SemaphoreType.DMA((n,))])(*arrs)


_HBM = pl.BlockSpec(memory_space=pltpu.HBM)
_SEM = pl.BlockSpec(memory_space=pltpu.SEMAPHORE)
_EFFECT = pltpu.SideEffectType.DATAFLOW_SIDE_EFFECTING


GATHER, SCATTER, ALL, SWAP = "gather", "scatter", "all", "swap"
_PEERS = {GATHER: tuple((fx, fy, 0) for fx, fy in _CHIP_FLIPS), SCATTER: tuple((fx, fy, 0) for fx, fy in _CHIP_FLIPS),
          ALL: _DEV_FLIPS, SWAP: ((0, 0, 1),)}


def _peer_copies(kind, src, land, send, recv, arrivals):
    x, y, c = _position()
    out = []
    for k, (fx, fy, fc) in enumerate(_PEERS[kind]):
        tx, ty, tc = x ^ fx, y ^ fy, c ^ fc
        if kind == ALL:
            me, peer = 4 * x + 2 * y + c, 4 * tx + 2 * ty + tc
        else:
            me, peer = 2 * x + y, 2 * tx + ty
        if kind == SWAP:
            dst = land
        else:
            dst = land.at[peer if arrivals else me]
        out.append(pltpu.make_async_remote_copy(
            src_ref=src.at[peer] if kind == SCATTER else src, dst_ref=dst, send_sem=send.at[k], recv_sem=recv.at[k],
            device_id=(tx, ty, tc), device_id_type=MESH))
    return out


def _own_copy(kind, src, land, send):
    x, y, c = _position()
    me = 4 * x + 2 * y + c if kind == ALL else 2 * x + y
    return pltpu.make_async_copy(src, land.at[me], send.at[len(_PEERS[kind])])


def exchange_start(srcs, lands, kind, name):
    n = len(srcs)
    n_peers = len(_PEERS[kind])

    def body(*refs):
        src, land = refs[:n], refs[n:2 * n]
        send, recv = refs[2 * n:3 * n], refs[3 * n:4 * n]
        token = refs[-1]
        for a in range(n):
            for copy in _peer_copies(kind, src[a], land[a], send[a], recv[a], False):
                copy.start()
            if kind in (GATHER, ALL):
                _own_copy(kind, src[a], land[a], send[a]).start()
        token[...] = jnp.zeros_like(token)

    bufs = list(srcs) + list(lands)
    outs = pl.pallas_call(
        body, name=name, in_specs=[_HBM] * (2 * n),
        out_specs=[_SEM] * (2 * n) + [_HBM] * (2 * n) + [pl.BlockSpec(memory_space=pltpu.VMEM)],
        out_shape=[pltpu.SemaphoreType.DMA((n_peers + 1,))] * n + [pltpu.SemaphoreType.DMA((n_peers,))] * n
        + [pltpu.HBM(a.shape, a.dtype) for a in bufs] + [SDS((8, 128), F32)],
        input_output_aliases={i: 2 * n + i for i in range(2 * n)},
        compiler_params=pltpu.CompilerParams(has_side_effects=_EFFECT),
    )(*[pltpu.with_memory_space_constraint(a, pltpu.HBM) for a in bufs])
    flights = [(outs[a], outs[n + a], outs[2 * n + a], outs[3 * n + a]) for a in range(n)]
    return flights, outs[-1]


def exchange_wait(flights, kind, after, name):
    n = len(flights)

    def body(*refs):
        src, land = refs[:n], refs[n:2 * n]
        send, recv = refs[2 * n:3 * n], refs[3 * n:4 * n]
        for a in range(n):
            for arrival in _peer_copies(kind, src[a], land[a], send[a], recv[a], True):
                arrival.wait_send()
                arrival.wait_recv()
            if kind in (GATHER, ALL):
                _own_copy(kind, src[a], land[a], send[a]).wait()

    bufs = [f[2] for f in flights] + [f[3] for f in flights]
    sems = [f[0] for f in flights] + [f[1] for f in flights]
    outs = pl.pallas_call(
        body, name=name, in_specs=[_HBM] * (2 * n) + [_SEM] * (2 * n) + [ANY], out_specs=[_HBM] * (2 * n),
        out_shape=[pltpu.HBM(a.shape, a.dtype) for a in bufs],
        input_output_aliases={i: i for i in range(2 * n)},
        compiler_params=pltpu.CompilerParams(has_side_effects=_EFFECT),
    )(*bufs, *sems, after)
    return [(outs[a], outs[n + a]) for a in range(n)]


def chip_alltoall(arrs, name):
    n = len(arrs)

    def body(*refs):
        ins, outs = refs[:n], refs[n:2 * n]
        send, recv, loc = refs[2 * n:]
        x, y, c = _position()
        me = 2 * x + y
        copies = []
        for a in range(n):
            local = pltpu.make_async_copy(ins[a].at[me], outs[a].at[me], loc.at[a])
            local.start()
            copies.append(local)
            for k, (fx, fy) in enumerate(_CHIP_FLIPS):
                tx, ty = x ^ fx, y ^ fy
                cp = pltpu.make_async_remote_copy(
                    src_ref=ins[a].at[2 * tx + ty], dst_ref=outs[a].at[me], send_sem=send.at[a, k],
                    recv_sem=recv.at[a, k], device_id=(tx, ty, c), device_id_type=MESH)
                cp.start()
                copies.append(cp)
        for cp in copies:
            cp.wait()

    shapes = [SDS(a.shape, a.dtype) for a in arrs]
    return _hbm_call(body, arrs, shapes, 3, name)


def device_allgather(arrs, name):
    n = len(arrs)

    def body(*refs):
        ins, outs = refs[:n], refs[n:2 * n]
        send, recv, loc = refs[2 * n:]
        x, y, c = _position()
        me = 4 * x + 2 * y + c
        copies = []
        for a in range(n):
            local = pltpu.make_async_copy(ins[a], outs[a].at[me], loc.at[a])
            local.start()
            copies.append(local)
            for k, (fx, fy, fc) in enumerate(_DEV_FLIPS):
                cp = pltpu.make_async_remote_copy(
                    src_ref=ins[a], dst_ref=outs[a].at[me], send_sem=send.at[a, k], recv_sem=recv.at[a, k],
                    device_id=(x ^ fx, y ^ fy, c ^ fc), device_id_type=MESH)
                cp.start()
                copies.append(cp)
        for cp in copies:
            cp.wait()

    shapes = [SDS((N_DEV,) + a.shape, a.dtype) for a in arrs]
    return _hbm_call(body, arrs, shapes, 7, name)


def _pad_to(a, axis, size):
    pad = [(0, 0)] * a.ndim
    pad[axis] = (0, size - a.shape[axis])
    return jnp.pad(a, pad)


def _rows_to_z(wt):
    gap = Z_CONV_A - FG_END
    row = lax.broadcasted_iota(jnp.int32, (ZC, 1), 0)
    low = jnp.pad(wt, ((0, gap), (0, 0)))
    high = jnp.pad(wt, ((gap, 0), (0, 0)))
    return jnp.where(row < FG_END, low, jnp.where(row < Z_CONV_A, jnp.zeros_like(low), high))


def _rows_from_z(wt):
    row = lax.broadcasted_iota(jnp.int32, (IN_COLS, 1), 0)
    return jnp.where(row < FG_END, wt[:IN_COLS], wt[Z_CONV_A - FG_END:])


def _layer_fwd(l, x0, h1, mod, p, fetch, nxt):
    sh1, sc1, ga1, sh2, sc2, ga2 = mod
    t = f"l{l}_"
    wt = dict(fetch('w_in', h1))
    z = mm_nt([h1], wt['w_in'], t + "proj_in")
    cumc, cumr = fgate_fwd(z, p['b_fgate'], t + "fgate")
    y_fox, lse = fox_fwd(z, cumc, cumr, t + "fox")
    hc = conv_dw_fwd(z, wt['conv_w'], p['conv_b'], t + "conv_dw")
    y_conv = conv_pw_fwd(hc, p['conv_ln_g'], p['conv_ln_b'], wt['conv_pw_w'], p['conv_pw_b'], t + "conv_pw")
    y_swa = swa_fwd(z, p['swa_sinks'], t + "swa")
    y_sgu = sgu_fwd(z, p['sgu_ln_g'], p['sgu_ln_b'], p['sgu_w'], p['sgu_bt'], t + "sgu")
    ys = (y_fox, y_conv, y_swa, y_sgu)
    ycat = group_norm_fwd(ys, p['g_group'], t + "group_norm")
    wt.update(fetch('w_out', ycat))
    ymix = mm_nn([ycat], wt['w_out'], F32, t + "proj_out")
    x1, h2 = resid_modnorm_fwd(x0, ymix, ga1, p['g_post_mix'], p['g_pre_ffn'], sc2, sh2, t + "resid1_modnorm2")
    wt.update(fetch('ffn_w_up', h2))
    u = mm_nn([h2], wt['ffn_w_up'], F32, t + "ffn_up", tm=S)
    act = ffn_gate_fwd(u, wt['ffn_conv_w'], p['ffn_conv_b'], t + "ffn_gate")
    wt.update(fetch('ffn_w_down', act))
    yffn = mm_nn([act], wt['ffn_w_down'], F32, t + "ffn_down")
    if nxt is None:
        x2, h1_next = resid_fwd(x1, yffn, ga2, p['g_post_ffn'], t + "resid2"), None
    else:
        x2, h1_next = resid_modnorm_fwd(x1, yffn, ga2, p['g_post_ffn'], *nxt, t + "resid2_modnorm1")
    res = dict(x0=x0, h1=h1, z=z, cumc=cumc, cumr=cumr, y_fox=y_fox, lse=lse, hc=hc, ys=ys, ycat=ycat, ymix=ymix,
               x1=x1, h2=h2, u=u, act=act, yffn=yffn, wt=wt)
    return x2, h1_next, res


def _layer_bwd(l, dx2, head, mod, p, r, emit, prev):
    sh1, sc1, ga1, sh2, sc2, ga2 = mod
    t = f"l{l}_bwd_"
    g = {}
    wt = r['wt']
    dyffn, dga2, g['g_post_ffn'] = head
    tok = emit({'ffn_w_down': mm_tn([r['act']], [dyffn], t + "ffn_down_dw")})
    dact = mm_nt([dyffn], wt['ffn_w_down'], t + "ffn_down_dx")
    dug, duv, dwg, dwv, dbg, dbv = ffn_gate_bwd(r['u'], wt['ffn_conv_w'], p['ffn_conv_b'] + tok, dact, t + "ffn_gate")
    g['ffn_conv_b'] = (dbg, dbv)
    tok = emit({'ffn_w_up': mm_tn_halves(r['h2'], dug, duv, t + "ffn_up_dw")})
    dh2 = mm_nt_halves(dug, duv, wt['ffn_w_up'], t + "ffn_up_dx")
    dx1, g['g_pre_ffn'], dsc2, dsh2, dymix, dga1, g['g_post_mix'] = modnorm_resid_bwd(
        r['x1'], p['g_pre_ffn'] + tok, sc2, sh2, dh2, dx2, r['ymix'], ga1, p['g_post_mix'], t + "modnorm2_resid1")
    tok = emit({'w_out': mm_tn([r['ycat']], [dymix], t + "proj_out_dw")})
    dcat = mm_nt([dymix], wt['w_out'], t + "proj_out_dx")
    dy_fox, dy_conv, dy_swa, dy_sgu, g['g_group'] = group_norm_bwd(r['ys'], p['g_group'] + tok, dcat, t + "group_norm")
    z = r['z']
    fq, fk, fv, dcq, dck = fox_bwd(z, r['cumc'], r['cumr'], r['lse'], r['y_fox'], dy_fox, t + "fox")
    dzf, g['b_fgate'] = fgate_bwd(z, p['b_fgate'], dcq, dck, t + "fgate")
    dhc, g['conv_ln_g'], g['conv_ln_b'], dpw, g['conv_pw_b'] = conv_pw_bwd(
        r['hc'], p['conv_ln_g'], p['conv_ln_b'], wt['conv_pw_w'], p['conv_pw_b'], dy_conv, t + "conv_pw")
    ca, cg, dcw, g['conv_b'] = conv_dw_bwd(z, wt['conv_w'], dhc, t + "conv_dw")
    sq, sk, sv, g['swa_sinks'] = swa_bwd(z, p['swa_sinks'], dy_swa, t + "swa")
    gu, gv, g['sgu_ln_g'], g['sgu_ln_b'], g['sgu_w'], g['sgu_bt'] = sgu_bwd(
        z, p['sgu_ln_g'], p['sgu_ln_b'], p['sgu_w'], p['sgu_bt'], dy_sgu, t + "sgu")
    dz = [fq, fk, fv, dzf, ca, cg, sq, sk, sv, gu, gv]
    tok = emit({'w_in': mm_tn(dz, [r['h1']], t + "proj_in_dw"), 'conv_w': dcw, 'conv_pw_w': dpw,
                'ffn_conv_w': jnp.concatenate([dwg, dwv], axis=1)})
    dh1 = mm_nn(dz, wt['w_in'], F32, t + "proj_in_dx")
    if prev is None:
        dx0, g['g_pre_mix'], dsc1, dsh1 = modnorm_bwd(r['x0'], p['g_pre_mix'] + tok, sc1, sh1, dh1, dx1, t + "modnorm1")
        head = None
    else:
        dx0, g['g_pre_mix'], dsc1, dsh1, *head = modnorm_resid_bwd(
            r['x0'], p['g_pre_mix'] + tok, sc1, sh1, dh1, dx1, *prev, t + "modnorm1_resid2")
    return dx0, head, g, (dsh1, dsc1, dga1, dsh2, dsc2, dga2)


def _layer_params(l, w):
    def row(name, width=None):
        v = w[name][l].reshape(1, -1)
        return v if width is None else _pad_to(v, 1, width)
    return {
        'g_pre_mix': row('g_pre_mix'), 'g_post_mix': row('g_post_mix'), 'g_pre_ffn': row('g_pre_ffn'),
        'g_post_ffn': row('g_post_ffn'), 'b_fgate': row('b_fgate', 128), 'conv_b': row('conv_b'),
        'conv_ln_g': row('conv_ln_g'), 'conv_ln_b': row('conv_ln_b'), 'conv_pw_b': row('conv_pw_b'),
        'swa_sinks': row('swa_sinks', 128), 'sgu_ln_g': row('sgu_ln_g'), 'sgu_ln_b': row('sgu_ln_b'),
        'sgu_w': w['sgu_w'][l], 'sgu_bt': _pad_to(w['sgu_b'][l].T, 1, 128), 'g_group': row('g_group'),
        'ffn_conv_b': row('ffn_conv_b'),
    }


def _w_in_from_shards(s):
    return _rows_to_z(s.reshape(IN_COLS, D))


def _w_in_to_shards(g):
    return _rows_from_z(g).reshape(N_CHIP, IN_COLS // N_CHIP, D)


def _cols_from_shards(s, rows):
    _, r, n = s.shape
    return _pad_to(jnp.transpose(s, (1, 0, 2)).reshape(r, N_CHIP * n), 0, rows)


def _cols_to_shards(g, r):
    n = g.shape[1] // N_CHIP
    return jnp.transpose(g[:r].reshape(r, N_CHIP, n), (1, 0, 2))


_FROM_SHARDS = {
    'w_in': _w_in_from_shards,
    'w_out': lambda s: s.reshape(D, D),
    'ffn_w_up': lambda s: s,
    'ffn_w_down': lambda s: s.reshape(DFF, D),
    'conv_w': lambda s: _cols_from_shards(s, 32),
    'conv_pw_w': lambda s: s.reshape(GW, GW),
    'ffn_conv_w': lambda s: _cols_from_shards(s, 8),
}
_TO_SHARDS = {
    'w_in': _w_in_to_shards,
    'w_out': lambda g: g.reshape(N_CHIP, D // N_CHIP, D),
    'ffn_w_up': lambda g: g,
    'ffn_w_down': lambda g: g.reshape(N_CHIP, DFF // N_CHIP, D),
    'conv_w': lambda g: _cols_to_shards(g, CONV_K),
    'conv_pw_w': lambda g: g.reshape(N_CHIP, GW // N_CHIP, GW),
    'ffn_conv_w': lambda g: _cols_to_shards(g, FFN_K),
}


def _local_step(xs, target, mod, w, fetch, emit):
    params = [_layer_params(l, w) for l in range(DEPTH)]
    mods = [tuple(mod[l, j] for j in range(6)) for l in range(DEPTH)]
    pre = [(params[l]['g_pre_mix'], mods[l][1], mods[l][0]) for l in range(DEPTH)]
    post = [(mods[l][5], params[l]['g_post_ffn']) for l in range(DEPTH)]
    resids = []
    h1 = modnorm_fwd(xs, *pre[0], "l0_modnorm1")
    for l in range(DEPTH):
        nxt = pre[l + 1] if l + 1 < DEPTH else None
        xs, h1, r = _layer_fwd(l, xs, h1, mods[l], params[l], functools.partial(fetch, l), nxt)
        resids.append(r)
    dx, loss_row = loss_grad(xs, target, "loss")
    last = DEPTH - 1
    head = resid_bwd(resids[last]['yffn'], *post[last], dx, f"l{last}_bwd_resid2")
    grads, dmods = [None] * DEPTH, [None] * DEPTH
    for l in reversed(range(DEPTH)):
        prev = (resids[l - 1]['yffn'],) + post[l - 1] if l > 0 else None
        dx, head, grads[l], dmods[l] = _layer_bwd(l, dx, head, mods[l], params[l], resids[l], functools.partial(emit, l),
                                                  prev)
    return loss_row, dx, grads, dmods


_MATMUL_WEIGHTS = ('w_in', 'w_out', 'ffn_w_up', 'ffn_w_down')
_CONV_WEIGHTS = ('conv_w', 'conv_pw_w', 'ffn_conv_w')
_FETCH_GROUPS = {'w_in': ('w_in',) + _CONV_WEIGHTS, 'w_out': ('w_out',), 'ffn_w_up': ('ffn_w_up',),
                 'ffn_w_down': ('ffn_w_down',)}


def kernel(x, c, w_ada, b_ada, g_pre_mix, g_post_mix, g_pre_ffn, g_post_ffn, w_in, b_fgate, conv_w, conv_b, conv_ln_g, conv_ln_b, conv_pw_w, conv_pw_b, swa_sinks, sgu_ln_g, sgu_ln_b, sgu_w, sgu_b, g_group, w_out, ffn_w_up, ffn_conv_w, ffn_conv_b, ffn_w_down, loss_target, m_w_ada, m_b_ada, m_g_pre_mix, m_g_post_mix, m_g_pre_ffn, m_g_post_ffn, m_w_in, m_b_fgate, m_conv_w, m_conv_b, m_conv_ln_g, m_conv_ln_b, m_conv_pw_w, m_conv_pw_b, m_swa_sinks, m_sgu_ln_g, m_sgu_ln_b, m_sgu_w, m_sgu_b, m_g_group, m_w_out, m_ffn_w_up, m_ffn_conv_w, m_ffn_conv_b, m_ffn_w_down, v_w_ada, v_b_ada, v_g_pre_mix, v_g_post_mix, v_g_pre_ffn, v_g_post_ffn, v_w_in, v_b_fgate, v_conv_w, v_conv_b, v_conv_ln_g, v_conv_ln_b, v_conv_pw_w, v_conv_pw_b, v_swa_sinks, v_sgu_ln_g, v_sgu_ln_b, v_sgu_w, v_sgu_b, v_g_group, v_w_out, v_ffn_w_up, v_ffn_conv_w, v_ffn_conv_b, v_ffn_w_down):
    args = locals()
    w = {n: args[n] for n in WEIGHTS}
    m = {n: args['m_' + n] for n in WEIGHTS}
    v = {n: args['v_' + n] for n in WEIGHTS}
    xi, yi, ci = _position()
    chip = 2 * xi + yi

    (c_all,) = device_allgather([c], "gather_c")
    c_all = c_all.reshape(N_DEV, D)
    b_loc = lax.dynamic_slice_in_dim(b_ada, chip * ADA_COLS, ADA_COLS, axis=1).reshape(DEPTH, 1, ADA_COLS)
    mod_all = ada_mod(c_all, w_ada, b_loc, "ada_mod")
    mine = lax.dynamic_index_in_dim(mod_all.reshape(DEPTH, N_CHIP, 2, ADA_COLS), ci, axis=2, keepdims=False)
    (mod4,) = chip_alltoall([jnp.transpose(mine, (1, 0, 2))], "scatter_mod")

    w_in_t, m_in_t, v_in_t = (jnp.transpose(a, (2, 0, 1)) for a in (w_in, m_w_in, v_w_in))

    def shard(n, l):
        a = w_in_t[:, l] if n == 'w_in' else w[n][l]
        return a.astype(BF16) if n in _MATMUL_WEIGHTS else a

    keys = [(n, l) for l in range(DEPTH) for n in _CONV_WEIGHTS]
    keys += [(n, l) for l in range(DEPTH) for n in _MATMUL_WEIGHTS]
    srcs = [shard(n, l) for n, l in keys]
    mod4, srcs = lax.optimization_barrier((mod4, srcs))
    lands = [lax.empty((N_CHIP,) + s.shape, s.dtype) for s in srcs]
    flights, token = exchange_start(srcs, lands, GATHER, "gather_start")
    gathering = dict(zip(keys, flights))
    mod = jnp.transpose(mod4, (1, 0, 2)).reshape(DEPTH, 6, 1, D) + token[0:1, 0:1]

    def fetch(l, name, after):
        names = _FETCH_GROUPS[name]
        landed = exchange_wait([gathering[(n, l)] for n in names], GATHER, after, f"gather_wait_l{l}_{name}")
        return {n: _FROM_SHARDS[n](land) for n, (_, land) in zip(names, landed)}

    scattering = {}

    def emit(l, grads):
        names = list(grads)
        pieces = [_TO_SHARDS[n](grads[n]) for n in names]
        lands = [lax.empty(p.shape, p.dtype) for p in pieces]
        started, token = exchange_start(pieces, lands, SCATTER, f"scatter_start_l{l}_{names[0]}")
        scattering.update({(n, l): f for n, f in zip(names, started)})
        return token[0:1, 0:1]

    loss_row, dx, grads, dmods = _local_step(x.reshape(S, D), loss_target.reshape(S, D), mod, w, fetch, emit)
    grad_x = dx.reshape(1, S, D)

    small_srcs = pack_small(grads, dmods, loss_row, "pack_small")
    small_flights, small_token = exchange_start(small_srcs, [lax.empty((N_DEV,) + a.shape, F32) for a in small_srcs], ALL,
                                                "gather_small_start")

    order = list(scattering)
    landed = dict(zip(order, exchange_wait([scattering[k] for k in order], SCATTER, small_token, "scatter_wait")))
    chip1 = chip.astype(jnp.int32).reshape(1)
    part = {'w_in': sum_pieces_t(chip1, [landed[('w_in', l)][1] for l in range(DEPTH)],
                                 [landed[('w_in', l)][0] for l in range(DEPTH)], "sum_w_in")}
    for n in SHARDED[1:]:
        cols = w[n].shape[-1]
        for l in range(DEPTH):
            src, land = landed[(n, l)]
            part[n] = sum_pieces(chip1, land.reshape(N_CHIP, -1, cols), src.reshape(N_CHIP, -1, cols), l, part.get(n),
                                 f"sum_{n}_l{l}")
        part[n] = part[n].reshape(-1, cols)
    parts = [part[n] for n in SHARDED]
    swap_flights, swap_token = exchange_start(parts, [lax.empty(p.shape, F32) for p in parts], SWAP, "swap_start")

    g_all, gw_all = (land for _, land in exchange_wait(small_flights, ALL, swap_token, "gather_small_wait"))
    out, loss_sum = adamw_small(g_all, gw_all, w, m, v, "adamw_small")
    loss = loss_sum[0, 0]

    dmod_all = g_all[:, :DEPTH * _LAYER_ROWS].reshape(N_DEV, DEPTH, _LAYER_ROWS * D)[:, :, :6 * D]
    dmod_loc = jnp.transpose(lax.dynamic_slice_in_dim(dmod_all, chip * ADA_COLS, ADA_COLS, axis=2), (1, 0, 2))
    out['w_ada'] = tuple(ada_update(c_all.T, dmod_loc, w_ada, m['w_ada'], v['w_ada'], "adamw_ada"))

    swapped = exchange_wait(swap_flights, SWAP, out['w_ada'][0], "swap_wait")
    part = {n: own for n, (own, _) in zip(SHARDED, swapped)}
    other = {n: got for n, (_, got) in zip(SHARDED, swapped)}
    res = adamw_pair_t(w_in_t, part['w_in'], other['w_in'], m_in_t, v_in_t, "adamw_w_in")
    out['w_in'] = tuple(jnp.transpose(a, (1, 2, 0)) for a in res)
    for n in SHARDED[1:]:
        cols = w[n].shape[-1]
        res = adamw_pair(w[n].reshape(-1, cols), part[n], other[n], m[n].reshape(-1, cols), v[n].reshape(-1, cols),
                         "adamw_" + n)
        out[n] = tuple(a.reshape(w[n].shape) for a in res)

    return (loss, grad_x, *[out[n][0] for n in WEIGHTS], *[out[n][1] for n in WEIGHTS],
            *[out[n][2] for n in WEIGHTS], *[out[n][3] for n in WEIGHTS])
```

```python
import functools

import jax
import jax.numpy as jnp
from jax import lax
from jax.experimental import pallas as pl
from jax.experimental.pallas import tpu as pltpu

F32 = jnp.float32
BF16 = jnp.bfloat16
SDS = jax.ShapeDtypeStruct
MESH = pl.DeviceIdType.MESH
ANY = pl.BlockSpec(memory_space=pl.ANY)

DEPTH = 2
S = 2048
D = 1024
GW = 256
DFF = 2816
NUP = 2 * DFF
IN_COLS = 2308
ZC = 2432
CONV_K = 31
FFN_K = 3
EPS = 1e-6
SCALE = 0.125
NEG = -1e30
N_CHIP = 4
N_DEV = 8

Z_FOX_Q, Z_FOX_K, Z_FOX_V = 0, 256, 512
Z_FG = 768
Z_CONV_A, Z_CONV_G = 896, 1152
Z_SWA_Q, Z_SWA_K, Z_SWA_V = 1408, 1664, 1792
Z_SGU_U, Z_SGU_V = 1920, 2176
FG_END = 772

ADAM_LR, ADAM_B1, ADAM_B2, ADAM_EPS, ADAM_WD, ADAM_STEP = 0.001, 0.9, 0.999, 1e-08, 0.01, 10

TS = 512
TM = 1024
N_SPLIT = 2816
N_BLOCK = 1408

WEIGHTS = ['w_ada', 'b_ada', 'g_pre_mix', 'g_post_mix', 'g_pre_ffn', 'g_post_ffn', 'w_in', 'b_fgate', 'conv_w',
           'conv_b', 'conv_ln_g', 'conv_ln_b', 'conv_pw_w', 'conv_pw_b', 'swa_sinks', 'sgu_ln_g', 'sgu_ln_b',
           'sgu_w', 'sgu_b', 'g_group', 'w_out', 'ffn_w_up', 'ffn_conv_w', 'ffn_conv_b', 'ffn_w_down']
SHARDED = ['w_in', 'conv_w', 'conv_pw_w', 'w_out', 'ffn_w_up', 'ffn_conv_w', 'ffn_w_down']
SMALL = [n for n in WEIGHTS if n not in SHARDED and n != 'w_ada']


def _rms(x, g):
    return x * lax.rsqrt(jnp.mean(x * x, axis=-1, keepdims=True) + EPS) * g


def _modnorm(x, g, sc, sh):
    return _rms(x, g) * (1.0 + sc) + sh


def _resid(x, y, ga, g):
    return x + ga * _rms(y, g)


@functools.partial(jax.custom_vjp, nondiff_argnums=(1,))
def _shift_down(x, n):
    if n == 0:
        return x
    row = lax.broadcasted_iota(jnp.int32, x.shape, 0)
    return jnp.where(row >= n, pltpu.roll(x, n, axis=0), 0.0)


def _shift_up(x, n):
    if n == 0:
        return x
    rows = x.shape[0]
    row = lax.broadcasted_iota(jnp.int32, x.shape, 0)
    return jnp.where(row < rows - n, pltpu.roll(x, rows - n, axis=0), 0.0)


def _shift_down_fwd(x, n):
    return _shift_down(x, n), None


def _shift_down_bwd(n, _, ct):
    return (_shift_up(ct, n),)


_shift_down.defvjp(_shift_down_fwd, _shift_down_bwd)


def _nt(a, b):
    return lax.dot_general(a, b, (((1,), (1,)), ((), ())), preferred_element_type=F32)


def _tn(a, b):
    return lax.dot_general(a, b, (((0,), (0,)), ((), ())), preferred_element_type=F32)


def _nn(a, b):
    return jnp.dot(a, b, preferred_element_type=F32)


def _acc(ref, val, first):
    @pl.when(first)
    def _():
        ref[...] = val

    @pl.when(jnp.logical_not(first))
    def _():
        ref[...] += val


def _row_tile(rows, cols):
    limit = max(8, (1 << 21) // (4 * cols))
    best = None
    for t in range(8, rows + 1, 8):
        if rows % t == 0 and t <= limit:
            best = t
    return best if best is not None else rows


def _ncol(n):
    return n if n <= N_SPLIT else N_BLOCK


def _weight_spec(b, order):
    pick = (lambda j, i: j) if order == 0 else (lambda i, j: j)
    if b.ndim == 3:
        _, k, tn = b.shape
        return pl.BlockSpec((None, k, tn), lambda *g: (pick(*g), 0, 0)), k, N_CHIP * tn, tn
    k, n = b.shape
    tn = _ncol(n)
    return pl.BlockSpec((k, tn), lambda *g: (0, pick(*g))), k, n, tn


def _side_by_side(refs):
    return refs[0][...] if len(refs) == 1 else jnp.concatenate([r[...] for r in refs], axis=1)


def mm_nn(pieces, b, out_dtype, name, tm=TM):
    m = pieces[0].shape[0]
    b_spec, k, n, tn = _weight_spec(b, 0)

    def body(*refs):
        a_refs, b_ref, o_ref = refs[:-2], refs[-2], refs[-1]
        o_ref[...] = _nn(_side_by_side(a_refs), b_ref[...]).astype(out_dtype)

    return pl.pallas_call(
        body, grid=(n // tn, m // tm), name=name,
        in_specs=[pl.BlockSpec((tm, p.shape[1]), lambda j, i: (i, 0)) for p in pieces] + [b_spec],
        out_specs=pl.BlockSpec((tm, tn), lambda j, i: (i, j)),
        out_shape=SDS((m, n), out_dtype),
    )(*pieces, b)


def mm_nt(pieces, b, name):
    m = pieces[0].shape[0]
    k, n = b.shape
    assert n == sum(p.shape[1] for p in pieces) <= N_SPLIT

    def body(*refs):
        a_refs, b_ref, o_ref = refs[:-2], refs[-2], refs[-1]
        o_ref[...] = _nt(_side_by_side(a_refs), b_ref[...])

    return pl.pallas_call(
        body, grid=(m // TM,), name=name,
        in_specs=[pl.BlockSpec((TM, p.shape[1]), lambda i: (i, 0)) for p in pieces] + [pl.BlockSpec((k, n), lambda i: (0, 0))],
        out_specs=pl.BlockSpec((TM, k), lambda i: (i, 0)),
        out_shape=SDS((m, k), F32),
    )(*pieces, b)


def mm_nt_halves(a0, a1, b4, name):
    m = a0.shape[0]
    _, k, tc = b4.shape

    def body(a0_ref, a1_ref, b_ref, o_ref):
        c = pl.program_id(1)

        @pl.when(c == 0)
        def _():
            o_ref[...] = _nt(a0_ref[...], b_ref[...])

        @pl.when(c == 1)
        def _():
            o_ref[...] += _nt(a0_ref[...], b_ref[...])

        @pl.when(c >= 2)
        def _():
            o_ref[...] += _nt(a1_ref[...], b_ref[...])

    return pl.pallas_call(
        body, grid=(m // TM, N_CHIP), name=name,
        in_specs=[pl.BlockSpec((TM, tc), lambda i, c: (i, jnp.minimum(c, 1))),
                  pl.BlockSpec((TM, tc), lambda i, c: (i, jnp.maximum(c - 2, 0))),
                  pl.BlockSpec((None, k, tc), lambda i, c: (c, 0, 0))],
        out_specs=pl.BlockSpec((TM, k), lambda i, c: (i, 0)),
        out_shape=SDS((m, k), F32),
    )(a0, a1, b4)


def mm_tn(a_pieces, b_pieces, name):
    m = a_pieces[0].shape[0]
    k = sum(p.shape[1] for p in a_pieces)
    n = sum(p.shape[1] for p in b_pieces)
    assert n <= N_SPLIT
    n_a = len(a_pieces)

    def body(*refs):
        a_refs, b_refs, o_ref = refs[:n_a], refs[n_a:-1], refs[-1]
        o_ref[...] = _tn(_side_by_side(a_refs), _side_by_side(b_refs)).astype(BF16)

    return pl.pallas_call(
        body, grid=(1,), name=name,
        in_specs=[pl.BlockSpec((m, p.shape[1]), lambda i: (0, 0)) for p in list(a_pieces) + list(b_pieces)],
        out_specs=pl.BlockSpec((k, n), lambda i: (0, 0)), out_shape=SDS((k, n), BF16),
    )(*a_pieces, *b_pieces)


def mm_tn_halves(a, b0, b1, name):
    m, k = a.shape
    tn = b0.shape[1] // 2

    def body(a_ref, b0_ref, b1_ref, o_ref):
        j = pl.program_id(0)

        @pl.when(j < 2)
        def _():
            o_ref[...] = _tn(a_ref[...], b0_ref[...]).astype(BF16)

        @pl.when(j >= 2)
        def _():
            o_ref[...] = _tn(a_ref[...], b1_ref[...]).astype(BF16)

    return pl.pallas_call(
        body, grid=(N_CHIP,), name=name,
        in_specs=[pl.BlockSpec((m, k), lambda j: (0, 0)),
                  pl.BlockSpec((m, tn), lambda j: (0, jnp.minimum(j, 1))),
                  pl.BlockSpec((m, tn), lambda j: (0, jnp.maximum(j - 2, 0)))],
        out_specs=pl.BlockSpec((None, k, tn), lambda j: (j, 0, 0)), out_shape=SDS((N_CHIP, k, tn), BF16),
    )(a, b0, b1)


_ROW = pl.BlockSpec((TS, D), lambda i: (i, 0))
_VEC = pl.BlockSpec((1, D), lambda i: (0, 0))


def modnorm_fwd(x, g, sc, sh, name):
    def body(x_ref, g_ref, sc_ref, sh_ref, o_ref):
        o_ref[...] = _modnorm(x_ref[...], g_ref[...], sc_ref[...], sh_ref[...]).astype(BF16)

    return pl.pallas_call(body, grid=(S // TS,), name=name, in_specs=[_ROW, _VEC, _VEC, _VEC], out_specs=_ROW,
                          out_shape=SDS((S, D), BF16))(x, g, sc, sh)


def modnorm_bwd(x, g, sc, sh, dh, dx_in, name):
    def body(x_ref, g_ref, sc_ref, sh_ref, dh_ref, dxin_ref, dx_ref, dg_ref, dsc_ref, dsh_ref):
        _, vjp = jax.vjp(_modnorm, x_ref[...], g_ref[...], sc_ref[...], sh_ref[...])
        dx, dg, dsc, dsh = vjp(dh_ref[...])
        dx_ref[...] = dxin_ref[...] + dx
        first = pl.program_id(0) == 0
        _acc(dg_ref, dg, first)
        _acc(dsc_ref, dsc, first)
        _acc(dsh_ref, dsh, first)

    vec = SDS((1, D), F32)
    return pl.pallas_call(body, grid=(S // TS,), name=name, in_specs=[_ROW, _VEC, _VEC, _VEC, _ROW, _ROW],
                          out_specs=[_ROW, _VEC, _VEC, _VEC], out_shape=[SDS((S, D), F32), vec, vec, vec])(
                              x, g, sc, sh, dh, dx_in)


def resid_fwd(x, y, ga, g, name):
    def body(x_ref, y_ref, ga_ref, g_ref, o_ref):
        o_ref[...] = _resid(x_ref[...], y_ref[...], ga_ref[...], g_ref[...])

    return pl.pallas_call(body, grid=(S // TS,), name=name, in_specs=[_ROW, _ROW, _VEC, _VEC], out_specs=_ROW,
                          out_shape=SDS((S, D), F32))(x, y, ga, g)


def resid_bwd(y, ga, g, dxo, name):
    def body(y_ref, ga_ref, g_ref, dxo_ref, dy_ref, dga_ref, dg_ref):
        _, vjp = jax.vjp(_gated_norm, y_ref[...], ga_ref[...], g_ref[...])
        dy, dga, dg = vjp(dxo_ref[...])
        dy_ref[...] = dy.astype(BF16)
        first = pl.program_id(0) == 0
        _acc(dga_ref, dga, first)
        _acc(dg_ref, dg, first)

    vec = SDS((1, D), F32)
    return pl.pallas_call(body, grid=(S // TS,), name=name, in_specs=[_ROW, _VEC, _VEC, _ROW],
                          out_specs=[_ROW, _VEC, _VEC], out_shape=[SDS((S, D), BF16), vec, vec])(y, ga, g, dxo)


def _gated_norm(y, ga, g):
    return ga * _rms(y, g)


def resid_modnorm_fwd(x, y, ga, g_post, g_pre, sc, sh, name):
    def body(x_ref, y_ref, ga_ref, gp_ref, g_ref, sc_ref, sh_ref, xo_ref, h_ref):
        xn = _resid(x_ref[...], y_ref[...], ga_ref[...], gp_ref[...])
        xo_ref[...] = xn
        h_ref[...] = _modnorm(xn, g_ref[...], sc_ref[...], sh_ref[...]).astype(BF16)

    return pl.pallas_call(body, grid=(S // TS,), name=name, in_specs=[_ROW, _ROW] + [_VEC] * 5, out_specs=[_ROW, _ROW],
                          out_shape=[SDS((S, D), F32), SDS((S, D), BF16)])(x, y, ga, g_post, g_pre, sc, sh)


def modnorm_resid_bwd(x, g, sc, sh, dh, dx_in, y, ga, g_post, name):
    def body(x_ref, g_ref, sc_ref, sh_ref, dh_ref, dxin_ref, y_ref, ga_ref, gp_ref,
             dx_ref, dg_ref, dsc_ref, dsh_ref, dy_ref, dga_ref, dgp_ref):
        _, vjp = jax.vjp(_modnorm, x_ref[...], g_ref[...], sc_ref[...], sh_ref[...])
        dx, dg, dsc, dsh = vjp(dh_ref[...])
        dx = dxin_ref[...] + dx
        dx_ref[...] = dx
        _, vjp = jax.vjp(_gated_norm, y_ref[...], ga_ref[...], gp_ref[...])
        dy, dga, dgp = vjp(dx)
        dy_ref[...] = dy.astype(BF16)
        first = pl.program_id(0) == 0
        for ref, val in ((dg_ref, dg), (dsc_ref, dsc), (dsh_ref, dsh), (dga_ref, dga), (dgp_ref, dgp)):
            _acc(ref, val, first)

    vec = SDS((1, D), F32)
    return pl.pallas_call(body, grid=(S // TS,), name=name,
                          in_specs=[_ROW, _VEC, _VEC, _VEC, _ROW, _ROW, _ROW, _VEC, _VEC],
                          out_specs=[_ROW, _VEC, _VEC, _VEC, _ROW, _VEC, _VEC],
                          out_shape=[SDS((S, D), F32), vec, vec, vec, SDS((S, D), BF16), vec, vec])(
                              x, g, sc, sh, dh, dx_in, y, ga, g_post)


def loss_grad(xf, target, name):
    def body(x_ref, t_ref, dx_ref, l_ref):
        err = x_ref[...] - t_ref[...]
        dx_ref[...] = err * (1.0 / D)
        part = 0.5 * jnp.sum(jnp.mean(err * err, axis=-1, keepdims=True), axis=0, keepdims=True)
        _acc(l_ref, jnp.broadcast_to(part, (1, 128)), pl.program_id(0) == 0)

    return pl.pallas_call(body, grid=(S // TS,), name=name, in_specs=[_ROW, _ROW],
                          out_specs=[_ROW, pl.BlockSpec((1, 128), lambda i: (0, 0))],
                          out_shape=[SDS((S, D), F32), SDS((1, 128), F32)])(xf, target)


_FG_SPEC = pl.BlockSpec((S, 128), lambda i: (0, Z_FG // 128))


def _tri128(lower):
    r = lax.broadcasted_iota(jnp.int32, (128, 128), 0)
    c = lax.broadcasted_iota(jnp.int32, (128, 128), 1)
    return ((r >= c) if lower else (r <= c)).astype(F32)


def fgate_fwd(z, bf, name):
    def body(z_ref, b_ref, cc_ref, cr_ref):
        tri = _tri128(True)
        carry = jnp.zeros((1, 128), F32)
        for i in range(S // 128):
            rows = pl.ds(i * 128, 128)
            lf = jax.nn.log_sigmoid(z_ref[rows, :] + b_ref[...])
            c = jnp.dot(tri, lf, precision=lax.Precision.HIGHEST, preferred_element_type=F32) + carry
            cc_ref[rows, :] = c
            carry = c[127:128, :]
        cr_ref[...] = cc_ref[...].T

    return pl.pallas_call(body, name=name, grid=(1,),
                          in_specs=[_FG_SPEC, pl.BlockSpec((1, 128), lambda i: (0, 0))],
                          out_specs=[pl.BlockSpec((S, 128), lambda i: (0, 0)), pl.BlockSpec((128, S), lambda i: (0, 0))],
                          out_shape=[SDS((S, 128), F32), SDS((128, S), F32)])(z, bf)


def fgate_bwd(z, bf, dcq, dck, name):
    def body(z_ref, b_ref, dcq_ref, dck_ref, dz_ref, db_ref, col_ref):
        col_ref[...] = dcq_ref[...] + jnp.concatenate([dck_ref[...], jnp.zeros((120, S), F32)], axis=0).T
        tri = _tri128(False)
        carry = jnp.zeros((1, 128), F32)
        db = jnp.zeros((1, 128), F32)
        for i in reversed(range(S // 128)):
            rows = pl.ds(i * 128, 128)
            dlf = jnp.dot(tri, col_ref[rows, :], precision=lax.Precision.HIGHEST, preferred_element_type=F32) + carry
            carry = dlf[0:1, :]
            dz = dlf * jax.nn.sigmoid(-(z_ref[rows, :] + b_ref[...]))
            dz_ref[rows, :] = dz.astype(BF16)
            db = db + jnp.sum(dz, axis=0, keepdims=True)
        db_ref[...] = db

    return pl.pallas_call(body, name=name, grid=(1,),
                          in_specs=[_FG_SPEC, pl.BlockSpec((1, 128), lambda i: (0, 0)),
                                    pl.BlockSpec((S, 128), lambda i: (0, 0)), pl.BlockSpec((8, S), lambda i: (0, 0))],
                          out_specs=[pl.BlockSpec((S, 128), lambda i: (0, 0)), pl.BlockSpec((1, 128), lambda i: (0, 0))],
                          out_shape=[SDS((S, 128), BF16), SDS((1, 128), F32)],
                          scratch_shapes=[pltpu.VMEM((S, 128), F32)])(z, bf, dcq, dck)


TQ = 256


def _head_mask(hh):
    lane = lax.broadcasted_iota(jnp.int32, (TQ, 128), 1)
    return (lane >= 64 * hh) & (lane < 64 * hh + 64)


def _fox_specs():
    q = pl.BlockSpec((TQ, 256), lambda i: (i, Z_FOX_Q // 256))
    k = pl.BlockSpec((S, 256), lambda i: (0, Z_FOX_K // 256))
    v = pl.BlockSpec((S, 256), lambda i: (0, Z_FOX_V // 256))
    cc = pl.BlockSpec((TQ, 128), lambda i: (i, 0))
    cr = pl.BlockSpec((8, S), lambda i: (0, 0))
    return q, k, v, cc, cr


FOX_SPAN = 2
FOX_GROUPS = S // (FOX_SPAN * TQ)


def _fox_scores(qm, k, cc_h, cr_h, i):
    klen = k.shape[0]
    s = _nt(qm, k) + cc_h - cr_h
    qpos = i * TQ + lax.broadcasted_iota(jnp.int32, (TQ, klen), 0)
    kpos = lax.broadcasted_iota(jnp.int32, (TQ, klen), 1)
    return jnp.where(kpos <= qpos, s, NEG)


def _for_key_length(i, fn):
    for g in range(FOX_GROUPS):
        pl.when(i // FOX_SPAN == g)(functools.partial(fn, (g + 1) * FOX_SPAN * TQ))


def fox_fwd(z, cumc, cumr, name):
    def body(q_ref, k_ref, v_ref, cc_ref, cr_ref, o_ref, l_ref):
        i = pl.program_id(0)

        def block(klen):
            lane = lax.broadcasted_iota(jnp.int32, (TQ, 128), 1)
            cc = cc_ref[...]
            lse = jnp.zeros((TQ, 128), F32)
            for p in range(2):
                cols = pl.ds(128 * p, 128)
                q = q_ref[:, cols]
                k = k_ref[0:klen, cols].astype(BF16)
                v = v_ref[0:klen, cols].astype(BF16)
                o_pair = jnp.zeros((TQ, 128), F32)
                for hh in range(2):
                    h = 2 * p + hh
                    hm = _head_mask(hh)
                    qm = jnp.where(hm, q * SCALE, 0.0).astype(BF16)
                    s = _fox_scores(qm, k, cc[:, h:h + 1], cr_ref[h:h + 1, 0:klen], i)
                    m = jnp.max(s, axis=1, keepdims=True)
                    e = jnp.exp(s - m)
                    l = jnp.sum(e, axis=1, keepdims=True)
                    o_pair = jnp.where(hm, _nn(e.astype(BF16), v) / l, o_pair)
                    lse = jnp.where(lane == h, m + jnp.log(l), lse)
                o_ref[:, cols] = o_pair
            l_ref[...] = lse

        _for_key_length(i, block)

    q, k, v, cc, cr = _fox_specs()
    return pl.pallas_call(body, grid=(S // TQ,), name=name, in_specs=[q, k, v, cc, cr],
                          out_specs=[pl.BlockSpec((TQ, 256), lambda i: (i, 0)), cc],
                          out_shape=[SDS((S, 256), F32), SDS((S, 128), F32)])(z, z, z, cumc, cumr)


def fox_bwd(z, cumc, cumr, lse, o, do, name):
    steps = S // TQ

    def body(q_ref, k_ref, v_ref, cc_ref, cr_ref, l_ref, o_ref, do_ref, dq_ref, dk_ref, dv_ref, dcq_ref, dck_ref,
             dk_acc, dv_acc):
        i = pl.program_id(0)

        @pl.when(i == 0)
        def _():
            dk_acc[...] = jnp.zeros_like(dk_acc)
            dv_acc[...] = jnp.zeros_like(dv_acc)
            dck_ref[...] = jnp.zeros_like(dck_ref)

        def block(klen):
            lane = lax.broadcasted_iota(jnp.int32, (TQ, 128), 1)
            cc = cc_ref[...]
            lse_all = l_ref[...]
            dcq = jnp.zeros((TQ, 128), F32)
            for p in range(2):
                cols = pl.ds(128 * p, 128)
                q = q_ref[:, cols]
                k = k_ref[0:klen, cols].astype(BF16)
                v = v_ref[0:klen, cols].astype(BF16)
                o_p = o_ref[:, cols]
                do_p = do_ref[:, cols]
                dq_pair = jnp.zeros((TQ, 128), F32)
                dk_pair = jnp.zeros((klen, 128), F32)
                dv_pair = jnp.zeros((klen, 128), F32)
                for hh in range(2):
                    h = 2 * p + hh
                    hm = _head_mask(hh)
                    qm = jnp.where(hm, q * SCALE, 0.0).astype(BF16)
                    s = _fox_scores(qm, k, cc[:, h:h + 1], cr_ref[h:h + 1, 0:klen], i)
                    pn = jnp.exp(s - lse_all[:, h:h + 1])
                    dom = jnp.where(hm, do_p, 0.0)
                    dl = jnp.sum(dom * o_p, axis=1, keepdims=True)
                    dom = dom.astype(BF16)
                    ds = pn * (_nt(dom, v) - dl)
                    dsb = ds.astype(BF16)
                    dq_pair = jnp.where(hm, _nn(dsb, k) * SCALE, dq_pair)
                    dk_pair = dk_pair + _tn(dsb, qm)
                    dv_pair = dv_pair + _tn(pn.astype(BF16), dom)
                    dck_ref[h:h + 1, 0:klen] -= jnp.sum(ds, axis=0, keepdims=True)
                    dcq = jnp.where(lane == h, jnp.sum(ds, axis=1, keepdims=True), dcq)
                dq_ref[:, cols] = dq_pair.astype(BF16)
                dk_acc[0:klen, cols] += dk_pair
                dv_acc[0:klen, cols] += dv_pair
            dcq_ref[...] = dcq

        _for_key_length(i, block)

        @pl.when(i == steps - 1)
        def _():
            dk_ref[...] = dk_acc[...].astype(BF16)
            dv_ref[...] = dv_acc[...].astype(BF16)

    q, k, v, cc, cr = _fox_specs()
    blk = pl.BlockSpec((TQ, 256), lambda i: (i, 0))
    full = pl.BlockSpec((S, 256), lambda i: (0, 0))
    return pl.pallas_call(body, grid=(steps,), name=name, in_specs=[q, k, v, cc, cr, cc, blk, blk],
                          out_specs=[blk, full, full, cc, cr],
                          out_shape=[SDS((S, 256), BF16), SDS((S, 256), BF16), SDS((S, 256), BF16), SDS((S, 128), F32),
                                     SDS((8, S), F32)],
                          scratch_shapes=[pltpu.VMEM((S, 256), F32), pltpu.VMEM((S, 256), F32)])(
                              z, z, z, cumc, cumr, lse, o, do)


W = 128
SWA_HEADS = 4


def _swa_core(first, qs, kcat, vcat, sink):
    r = lax.broadcasted_iota(jnp.int32, (SWA_HEADS * W, 2 * W), 0)
    j = lax.broadcasted_iota(jnp.int32, (SWA_HEADS * W, 2 * W), 1)
    qi = r & (W - 1)
    valid = ((j < W) & (j > qi) & jnp.logical_not(first)) | ((j >= W) & (j - W <= qi))
    s = jnp.where(valid, _nt((qs * SCALE).astype(BF16), kcat.astype(BF16)), NEG)
    m = lax.stop_gradient(jnp.maximum(jnp.max(s, axis=1, keepdims=True), sink))
    e = jnp.exp(s - m)
    den = jnp.sum(e, axis=1, keepdims=True) + jnp.exp(sink - m)
    return _nn((e / den).astype(BF16), vcat.astype(BF16))


def _kv_lanes(kv):
    lane = lax.broadcasted_iota(jnp.int32, (W, 128), 1)
    return (lane >= 64 * kv) & (lane < 64 * kv + 64)


def _swa_stack(pair0, pair1):
    blocks = []
    for h in range(SWA_HEADS):
        kv, hh = h // 2, h % 2
        a = (pair0, pair1)[kv]
        a = a if hh == kv else pltpu.roll(a, 64, axis=1)
        blocks.append(jnp.where(_kv_lanes(kv), a, 0.0))
    return jnp.concatenate(blocks, axis=0)


def _swa_unstack(stacked):
    pairs = [jnp.zeros((W, 128), F32), jnp.zeros((W, 128), F32)]
    for h in range(SWA_HEADS):
        kv, hh = h // 2, h % 2
        a = jnp.where(_kv_lanes(kv), stacked[h * W:(h + 1) * W], 0.0)
        pairs[kv] = pairs[kv] + (a if hh == kv else pltpu.roll(a, 64, axis=1))
    return pairs


def _head_rows():
    return lax.broadcasted_iota(jnp.int32, (SWA_HEADS * W, 1), 0) // W


SWA_PAIR = 2 * W


def _swa_operands(sub, q0_ref, q1_ref, kp_ref, kc_ref, vp_ref, vc_ref, sk_ref):
    rows = pl.ds(sub * W, W)
    qs = _swa_stack(q0_ref[rows, :], q1_ref[rows, :])
    k_prev, v_prev = (kp_ref[...], vp_ref[...]) if sub == 0 else (kc_ref[0:W, :], vc_ref[0:W, :])
    kcat = jnp.concatenate([k_prev, kc_ref[rows, :]], axis=0)
    vcat = jnp.concatenate([v_prev, vc_ref[rows, :]], axis=0)
    sink = jnp.zeros((SWA_HEADS * W, 1), F32)
    for h in range(SWA_HEADS):
        sink = jnp.where(_head_rows() == h, sk_ref[:, h:h + 1], sink)
    return qs, kcat, vcat, sink


def _swa_specs(blocks):
    rows = blocks * W
    q0 = pl.BlockSpec((rows, 128), lambda i: (i, Z_SWA_Q // 128))
    q1 = pl.BlockSpec((rows, 128), lambda i: (i, Z_SWA_Q // 128 + 1))
    kc = pl.BlockSpec((rows, 128), lambda i: (i, Z_SWA_K // 128))
    kp = pl.BlockSpec((W, 128), lambda i: (jnp.maximum(blocks * i - 1, 0), Z_SWA_K // 128))
    vc = pl.BlockSpec((rows, 128), lambda i: (i, Z_SWA_V // 128))
    vp = pl.BlockSpec((W, 128), lambda i: (jnp.maximum(blocks * i - 1, 0), Z_SWA_V // 128))
    sk = pl.BlockSpec((1, 128), lambda i: (0, 0))
    return q0, q1, kp, kc, vp, vc, sk


def swa_fwd(z, sinks, name):
    def body(*refs):
        o_ref = refs[-1]
        o0, o1 = _swa_unstack(_swa_core(pl.program_id(0) == 0, *_swa_operands(0, *refs[:-1])))
        o_ref[:, 0:128] = o0
        o_ref[:, 128:256] = o1

    return pl.pallas_call(body, grid=(S // W,), name=name, in_specs=list(_swa_specs(1)),
                          out_specs=pl.BlockSpec((W, 256), lambda i: (i, 0)),
                          out_shape=SDS((S, 256), F32))(z, z, z, z, z, z, sinks)


def swa_bwd(z, sinks, do, name):
    steps = S // SWA_PAIR

    def body(*refs):
        do_ref, dq_ref, dk_ref, dv_ref, dsk_ref, dk_acc, dv_acc = refs[7:]
        i = pl.program_id(0)

        @pl.when(i == 0)
        def _():
            dk_acc[...] = jnp.zeros_like(dk_acc)
            dv_acc[...] = jnp.zeros_like(dv_acc)

        lane = lax.broadcasted_iota(jnp.int32, (1, 128), 1)
        dsk = jnp.zeros((1, 128), F32)
        for sub in range(2):
            rows = pl.ds(sub * W, W)
            first = (i == 0) if sub == 0 else False
            _, vjp = jax.vjp(functools.partial(_swa_core, first), *_swa_operands(sub, *refs[:7]))
            dqs, dkcat, dvcat, dsink = vjp(_swa_stack(do_ref[rows, 0:128], do_ref[rows, 128:256]))
            dq0, dq1 = _swa_unstack(dqs)
            dq_ref[rows, 0:128] = dq0.astype(BF16)
            dq_ref[rows, 128:256] = dq1.astype(BF16)
            cur = pl.ds(pl.multiple_of((2 * i + sub) * W, W), W)
            dk_acc[cur, :] += dkcat[W:2 * W]
            dv_acc[cur, :] += dvcat[W:2 * W]

            def to_prev(dkcat=dkcat, dvcat=dvcat, sub=sub):
                prev = pl.ds(pl.multiple_of((2 * i + sub - 1) * W, W), W)
                dk_acc[prev, :] += dkcat[0:W]
                dv_acc[prev, :] += dvcat[0:W]

            if sub == 0:
                pl.when(i > 0)(to_prev)
            else:
                to_prev()
            for h in range(SWA_HEADS):
                d = jnp.sum(jnp.where(_head_rows() == h, dsink, 0.0), axis=0, keepdims=True)
                dsk = dsk + jnp.where(lane == h, d, 0.0)
        _acc(dsk_ref, dsk, i == 0)

        @pl.when(i == steps - 1)
        def _():
            dk_ref[...] = dk_acc[...].astype(BF16)
            dv_ref[...] = dv_acc[...].astype(BF16)

    blk = pl.BlockSpec((SWA_PAIR, 256), lambda i: (i, 0))
    full = pl.BlockSpec((S, 128), lambda i: (0, 0))
    return pl.pallas_call(body, grid=(steps,), name=name, in_specs=list(_swa_specs(2)) + [blk],
                          out_specs=[blk, full, full, pl.BlockSpec((1, 128), lambda n: (0, 0))],
                          out_shape=[SDS((S, 256), BF16), SDS((S, 128), BF16), SDS((S, 128), BF16), SDS((1, 128), F32)],
                          scratch_shapes=[pltpu.VMEM((S, 128), F32), pltpu.VMEM((S, 128), F32)])(
                              z, z, z, z, z, z, sinks, do)


def _glu(a, g):
    return a * jax.nn.sigmoid(g)


def _cv1_specs():
    a = pl.BlockSpec((S, 128), lambda j: (0, Z_CONV_A // 128 + j))
    g = pl.BlockSpec((S, 128), lambda j: (0, Z_CONV_G // 128 + j))
    w = pl.BlockSpec((32, 128), lambda j: (0, j))
    b = pl.BlockSpec((1, 128), lambda j: (0, j))
    h = pl.BlockSpec((S, 128), lambda j: (0, j))
    return a, g, w, b, h


def conv_dw_fwd(z, cw, cb, name):
    def body(a_ref, g_ref, w_ref, b_ref, o_ref):
        hh = _glu(a_ref[...], g_ref[...])
        acc = jnp.zeros((S, 128), F32) + b_ref[...]
        for k in range(CONV_K):
            acc = acc + _shift_down(hh, CONV_K - 1 - k) * w_ref[k:k + 1, :]
        o_ref[...] = acc

    a, g, w, b, h = _cv1_specs()
    return pl.pallas_call(body, grid=(2,), name=name, in_specs=[a, g, w, b], out_specs=h,
                          out_shape=SDS((S, 256), F32))(z, z, cw, cb)


def conv_dw_bwd(z, cw, dhc, name):
    def body(a_ref, g_ref, w_ref, dh_ref, da_ref, dg_ref, dw_ref, db_ref):
        hh, vjp = jax.vjp(_glu, a_ref[...], g_ref[...])
        dh = dh_ref[...]
        dhh = jnp.zeros((S, 128), F32)
        for k in range(CONV_K):
            n = CONV_K - 1 - k
            dhh = dhh + _shift_up(dh, n) * w_ref[k:k + 1, :]
            dw_ref[k:k + 1, :] = jnp.sum(dh * _shift_down(hh, n), axis=0, keepdims=True)
        dw_ref[CONV_K:32, :] = jnp.zeros((32 - CONV_K, 128), F32)
        db_ref[...] = jnp.sum(dh, axis=0, keepdims=True)
        da, dg = vjp(dhh)
        da_ref[...] = da.astype(BF16)
        dg_ref[...] = dg.astype(BF16)

    a, g, w, b, h = _cv1_specs()
    return pl.pallas_call(body, grid=(2,), name=name, in_specs=[a, g, w, h], out_specs=[h, h, w, b],
                          out_shape=[SDS((S, 256), BF16), SDS((S, 256), BF16), SDS((32, 256), F32), SDS((1, 256), F32)])(
                              z, z, cw, dhc)


def _ln(x, g, b):
    mu = jnp.mean(x, axis=-1, keepdims=True)
    xc = x - mu
    var = jnp.mean(xc * xc, axis=-1, keepdims=True)
    return xc * lax.rsqrt(var + EPS) * g + b


def _conv_pw(hc, lg, lb, pw, pb):
    y = jax.nn.silu(_ln(hc, lg, lb))
    return _nn(y.astype(BF16), pw.astype(BF16)) + pb


TS2 = 512
_ROW2 = pl.BlockSpec((TS2, 256), lambda i: (i, 0))
_VEC2 = pl.BlockSpec((1, 256), lambda i: (0, 0))
_MAT2 = pl.BlockSpec((256, 256), lambda i: (0, 0))


def conv_pw_fwd(hc, lg, lb, pw, pb, name):
    def body(h_ref, lg_ref, lb_ref, pw_ref, pb_ref, o_ref):
        o_ref[...] = _conv_pw(h_ref[...], lg_ref[...], lb_ref[...], pw_ref[...], pb_ref[...])

    return pl.pallas_call(body, grid=(S // TS2,), name=name, in_specs=[_ROW2, _VEC2, _VEC2, _MAT2, _VEC2],
                          out_specs=_ROW2, out_shape=SDS((S, 256), F32))(hc, lg, lb, pw, pb)


def conv_pw_bwd(hc, lg, lb, pw, pb, dy, name):
    def body(h_ref, lg_ref, lb_ref, pw_ref, pb_ref, dy_ref, dh_ref, dlg_ref, dlb_ref, dpw_ref, dpb_ref):
        _, vjp = jax.vjp(_conv_pw, h_ref[...], lg_ref[...], lb_ref[...], pw_ref[...], pb_ref[...])
        dh, dlg, dlb, dpw, dpb = vjp(dy_ref[...])
        dh_ref[...] = dh
        first = pl.program_id(0) == 0
        _acc(dlg_ref, dlg, first)
        _acc(dlb_ref, dlb, first)
        _acc(dpw_ref, dpw, first)
        _acc(dpb_ref, dpb, first)

    vec = SDS((1, 256), F32)
    return pl.pallas_call(body, grid=(S // TS2,), name=name, in_specs=[_ROW2, _VEC2, _VEC2, _MAT2, _VEC2, _ROW2],
                          out_specs=[_ROW2, _VEC2, _VEC2, _MAT2, _VEC2],
                          out_shape=[SDS((S, 256), F32), vec, vec, SDS((256, 256), F32), vec])(hc, lg, lb, pw, pb, dy)


def _sgu_block(u0, u1, v0, v1, lg0, lg1, lb0, lb1, w0, w1, w2, w3, bt):
    u0, u1, v0, v1 = (jax.nn.gelu(a) for a in (u0, u1, v0, v1))
    mu = (jnp.sum(v0, axis=1, keepdims=True) + jnp.sum(v1, axis=1, keepdims=True)) * (1.0 / GW)
    c0, c1 = v0 - mu, v1 - mu
    var = (jnp.sum(c0 * c0, axis=1, keepdims=True) + jnp.sum(c1 * c1, axis=1, keepdims=True)) * (1.0 / GW)
    r = lax.rsqrt(var + EPS)
    n0 = c0 * r * lg0 + lb0
    n1 = c1 * r * lg1 + lb1
    row = lax.broadcasted_iota(jnp.int32, (128, 128), 0)
    col = lax.broadcasted_iota(jnp.int32, (128, 128), 1)
    tri = row >= col
    outs = []
    for p, (n, u, wa, wb) in enumerate(((n0, u0, w0, w1), (n1, u1, w2, w3))):
        nb = n.astype(BF16)
        ma = _nn(jnp.where(tri, wa, 0.0).astype(BF16), nb)
        mb = _nn(jnp.where(tri, wb, 0.0).astype(BF16), nb)
        expand = (row == 2 * p + col // 64).astype(F32)
        bias = jnp.dot(bt, expand, precision=lax.Precision.HIGHEST, preferred_element_type=F32)
        outs.append(u * (jnp.where(col < 64, ma, mb) + bias))
    return outs[0], outs[1]


def _sgu_specs(chunks=1):
    def col(c):
        return pl.BlockSpec((chunks * 128, 128), lambda n, c=c: (n, c))
    zs = [col(Z_SGU_U // 128), col(Z_SGU_U // 128 + 1), col(Z_SGU_V // 128), col(Z_SGU_V // 128 + 1)]
    vec = [pl.BlockSpec((1, 128), lambda n: (0, 0)), pl.BlockSpec((1, 128), lambda n: (0, 1))]
    ws = [pl.BlockSpec((None, 128, 128), lambda n, g=g: (g, 0, 0)) for g in range(4)]
    bt = pl.BlockSpec((128, 128), lambda n: (0, 0))
    return zs + vec + vec + ws + [bt]


def sgu_fwd(z, lg, lb, w, bt, name):
    def body(*refs):
        o_ref = refs[-1]
        y0, y1 = _sgu_block(*[r[...] for r in refs[:-1]])
        o_ref[:, 0:128] = y0
        o_ref[:, 128:256] = y1

    return pl.pallas_call(body, grid=(S // 128,), name=name, in_specs=_sgu_specs(),
                          out_specs=pl.BlockSpec((128, 256), lambda n: (n, 0)),
                          out_shape=SDS((S, 256), F32))(z, z, z, z, lg, lg, lb, lb, w, w, w, w, bt)


def sgu_bwd(z, lg, lb, w, bt, dy, name):
    chunks = 4

    def body(*refs):
        ins, dy_ref = refs[:13], refs[13]
        du_ref, dv_ref, dlg_ref, dlb_ref, dw_ref, dbt_ref = refs[14:]

        @pl.when(pl.program_id(0) == 0)
        def _():
            dlg_ref[...] = jnp.zeros_like(dlg_ref)
            dlb_ref[...] = jnp.zeros_like(dlb_ref)
            dw_ref[...] = jnp.zeros_like(dw_ref)
            dbt_ref[...] = jnp.zeros_like(dbt_ref)

        for sub in range(chunks):
            rows = pl.ds(sub * 128, 128)
            _, vjp = jax.vjp(_sgu_block, *[r[rows, :] for r in ins[:4]], *[r[...] for r in ins[4:]])
            du0, du1, dv0, dv1, dlg0, dlg1, dlb0, dlb1, dw0, dw1, dw2, dw3, dbt = vjp(
                (dy_ref[rows, 0:128], dy_ref[rows, 128:256]))
            du_ref[rows, 0:128] = du0.astype(BF16)
            du_ref[rows, 128:256] = du1.astype(BF16)
            dv_ref[rows, 0:128] = dv0.astype(BF16)
            dv_ref[rows, 128:256] = dv1.astype(BF16)
            dlg_ref[:, 0:128] += dlg0
            dlg_ref[:, 128:256] += dlg1
            dlb_ref[:, 0:128] += dlb0
            dlb_ref[:, 128:256] += dlb1
            for g, d in enumerate((dw0, dw1, dw2, dw3)):
                dw_ref[g] += d
            dbt_ref[...] += dbt

    blk = pl.BlockSpec((chunks * 128, 256), lambda n: (n, 0))
    vec = pl.BlockSpec((1, 256), lambda n: (0, 0))
    return pl.pallas_call(body, grid=(S // (chunks * 128),), name=name, in_specs=_sgu_specs(chunks) + [blk],
                          out_specs=[blk, blk, vec, vec, pl.BlockSpec((4, 128, 128), lambda n: (0, 0, 0)),
                                     pl.BlockSpec((128, 128), lambda n: (0, 0))],
                          out_shape=[SDS((S, 256), BF16), SDS((S, 256), BF16), SDS((1, 256), F32), SDS((1, 256), F32),
                                     SDS((4, 128, 128), F32), SDS((128, 128), F32)])(
                                         z, z, z, z, lg, lg, lb, lb, w, w, w, w, bt, dy)


def _group_norm(y0, y1, y2, y3, g0, g1, g2, g3):
    return tuple(_rms(y, g) for y, g in zip((y0, y1, y2, y3), (g0, g1, g2, g3)))


_GROW = pl.BlockSpec((TS2, 256), lambda i: (i, 0))
_GCAT = pl.BlockSpec((TS2, D), lambda i: (i, 0))
_GVEC = [pl.BlockSpec((1, 256), lambda i, j=j: (0, j)) for j in range(4)]


def group_norm_fwd(ys, gg, name):
    def body(*refs):
        o_ref = refs[-1]
        outs = _group_norm(*[r[...] for r in refs[:-1]])
        for j, c in enumerate(outs):
            o_ref[:, 256 * j:256 * (j + 1)] = c.astype(BF16)

    return pl.pallas_call(body, grid=(S // TS2,), name=name, in_specs=[_GROW] * 4 + _GVEC, out_specs=_GCAT,
                          out_shape=SDS((S, D), BF16))(*ys, gg, gg, gg, gg)


def group_norm_bwd(ys, gg, dcat, name):
    def body(*refs):
        ins, dc_ref = refs[:8], refs[8]
        dy_refs, dg_ref = refs[9:13], refs[13]
        _, vjp = jax.vjp(_group_norm, *[r[...] for r in ins])
        grads = vjp(tuple(dc_ref[:, 256 * j:256 * (j + 1)] for j in range(4)))
        first = pl.program_id(0) == 0

        @pl.when(first)
        def _():
            dg_ref[...] = jnp.zeros_like(dg_ref)

        for j in range(4):
            dy_refs[j][...] = grads[j]
            dg_ref[:, 256 * j:256 * (j + 1)] += grads[4 + j]

    return pl.pallas_call(body, grid=(S // TS2,), name=name, in_specs=[_GROW] * 4 + _GVEC + [_GCAT],
                          out_specs=[_GROW] * 4 + [pl.BlockSpec((1, D), lambda i: (0, 0))],
                          out_shape=[SDS((S, 256), F32)] * 4 + [SDS((1, D), F32)])(*ys, gg, gg, gg, gg, dcat)


FB = 256
N_FB = DFF // FB


def _ffn_gate(ug, uv, wg0, wg1, wg2, wv0, wv1, wv2, bg, bv):
    cg = bg + _shift_down(ug, 2) * wg0 + _shift_down(ug, 1) * wg1 + ug * wg2
    cv = bv + _shift_down(uv, 2) * wv0 + _shift_down(uv, 1) * wv1 + uv * wv2
    return jax.nn.silu(cg) * cv


def _gate_specs():
    ug = pl.BlockSpec((S, FB), lambda j: (0, j))
    uv = pl.BlockSpec((S, FB), lambda j: (0, j + N_FB))
    wg = pl.BlockSpec((8, FB), lambda j: (0, j))
    wv = pl.BlockSpec((8, FB), lambda j: (0, j + N_FB))
    bg = pl.BlockSpec((1, FB), lambda j: (0, j))
    bv = pl.BlockSpec((1, FB), lambda j: (0, j + N_FB))
    return ug, uv, wg, wv, bg, bv


def _gate_args(ug_ref, uv_ref, wg_ref, wv_ref, bg_ref, bv_ref):
    return (ug_ref[...], uv_ref[...], wg_ref[0:1, :], wg_ref[1:2, :], wg_ref[2:3, :],
            wv_ref[0:1, :], wv_ref[1:2, :], wv_ref[2:3, :], bg_ref[...], bv_ref[...])


def ffn_gate_fwd(u, cw, cb, name):
    def body(ug_ref, uv_ref, wg_ref, wv_ref, bg_ref, bv_ref, o_ref):
        o_ref[...] = _ffn_gate(*_gate_args(ug_ref, uv_ref, wg_ref, wv_ref, bg_ref, bv_ref)).astype(BF16)

    return pl.pallas_call(body, grid=(N_FB,), name=name, in_specs=list(_gate_specs()),
                          out_specs=pl.BlockSpec((S, FB), lambda j: (0, j)),
                          out_shape=SDS((S, DFF), BF16))(u, u, cw, cw, cb, cb)


def ffn_gate_bwd(u, cw, cb, da, name):
    def body(ug_ref, uv_ref, wg_ref, wv_ref, bg_ref, bv_ref, da_ref, dug_ref, duv_ref, dwg_ref, dwv_ref, dbg_ref, dbv_ref):
        _, vjp = jax.vjp(_ffn_gate, *_gate_args(ug_ref, uv_ref, wg_ref, wv_ref, bg_ref, bv_ref))
        dug, duv, g0, g1, g2, v0, v1, v2, dbg, dbv = vjp(da_ref[...])
        dug_ref[...] = dug.astype(BF16)
        duv_ref[...] = duv.astype(BF16)
        for k, (a, b) in enumerate(((g0, v0), (g1, v1), (g2, v2))):
            dwg_ref[k:k + 1, :] = a
            dwv_ref[k:k + 1, :] = b
        dwg_ref[FFN_K:8, :] = jnp.zeros((8 - FFN_K, FB), F32)
        dwv_ref[FFN_K:8, :] = jnp.zeros((8 - FFN_K, FB), F32)
        dbg_ref[...] = dbg
        dbv_ref[...] = dbv

    ug, uv, wg, wv, bg, bv = _gate_specs()
    half = pl.BlockSpec((S, FB), lambda j: (0, j))
    whalf = pl.BlockSpec((8, FB), lambda j: (0, j))
    bhalf = pl.BlockSpec((1, FB), lambda j: (0, j))
    return pl.pallas_call(body, grid=(N_FB,), name=name, in_specs=[ug, uv, wg, wv, bg, bv, half],
                          out_specs=[half, half, whalf, whalf, bhalf, bhalf],
                          out_shape=[SDS((S, DFF), BF16), SDS((S, DFF), BF16), SDS((8, DFF), F32), SDS((8, DFF), F32),
                                     SDS((1, DFF), F32), SDS((1, DFF), F32)])(u, u, cw, cw, cb, cb, da)


def _adamw(w, g, m, v):
    m = ADAM_B1 * m + (1.0 - ADAM_B1) * g
    v = ADAM_B2 * v + (1.0 - ADAM_B2) * (g * g)
    m_hat = m / (1.0 - ADAM_B1 ** ADAM_STEP)
    v_hat = v / (1.0 - ADAM_B2 ** ADAM_STEP)
    delta = -ADAM_LR * (m_hat / (jnp.sqrt(v_hat) + ADAM_EPS) + ADAM_WD * w)
    return delta, m, v


def sum_pieces(chip, r, own, layer, base, name):
    n, rows, cols = r.shape
    tr = _row_tile(rows, cols)

    def body(chip_ref, r_ref, own_ref, *rest):
        o_ref = rest[-1]
        acc = jnp.zeros((tr, cols), F32)
        for j in range(n):
            acc = acc + jnp.where(chip_ref[0] == j, own_ref[0], r_ref[j]).astype(F32)
        o_ref[...] = acc

    extra = {} if base is None else dict(input_output_aliases={3: 0})
    grid_spec = pltpu.PrefetchScalarGridSpec(
        num_scalar_prefetch=1, grid=(rows // tr,),
        in_specs=[pl.BlockSpec((n, tr, cols), lambda i, c: (0, i, 0)), pl.BlockSpec((1, tr, cols), lambda i, c: (c[0], i, 0))]
        + ([] if base is None else [ANY]),
        out_specs=pl.BlockSpec((None, tr, cols), lambda i, c: (layer, i, 0)))
    return pl.pallas_call(body, grid_spec=grid_spec, name=name, out_shape=SDS((DEPTH, rows, cols), F32), **extra)(
        *([chip, r, own] if base is None else [chip, r, own, base]))


LANE_BLOCK = 128


def sum_pieces_t(chip, rs, owns, name):
    n, rows, cols = rs[0].shape

    def body(chip_ref, *refs):
        o_ref = refs[-1]
        for l in range(DEPTH):
            r_ref, own_ref = refs[2 * l], refs[2 * l + 1]
            acc = jnp.zeros((rows, LANE_BLOCK), F32)
            for j in range(n):
                acc = acc + jnp.where(chip_ref[0] == j, own_ref[0], r_ref[j]).astype(F32)
            o_ref[:, l, :] = acc

    r_spec = pl.BlockSpec((n, rows, LANE_BLOCK), lambda i, c: (0, 0, i))
    own_spec = pl.BlockSpec((1, rows, LANE_BLOCK), lambda i, c: (c[0], 0, i))
    grid_spec = pltpu.PrefetchScalarGridSpec(
        num_scalar_prefetch=1, grid=(cols // LANE_BLOCK,), in_specs=[r_spec, own_spec] * DEPTH,
        out_specs=pl.BlockSpec((rows, DEPTH, LANE_BLOCK), lambda i, c: (0, 0, i)))
    ops = [a for pair in zip(rs, owns) for a in pair]
    return pl.pallas_call(body, grid_spec=grid_spec, name=name, out_shape=SDS((rows, DEPTH, cols), F32))(chip, *ops)


def adamw_pair_t(w, p, q, m, v, name):
    rows, depth, cols = w.shape

    def body(w_ref, p_ref, q_ref, m_ref, v_ref, g_ref, d_ref, nm_ref, nv_ref):
        g = p_ref[...] + q_ref[...]
        g_ref[...] = g
        d_ref[...], nm_ref[...], nv_ref[...] = _adamw(w_ref[...], g, m_ref[...], v_ref[...])

    spec = pl.BlockSpec((rows, depth, LANE_BLOCK), lambda i: (0, 0, i))
    return pl.pallas_call(body, grid=(cols // LANE_BLOCK,), name=name, in_specs=[spec] * 5, out_specs=[spec] * 4,
                          out_shape=[SDS(w.shape, F32)] * 4)(w, p, q, m, v)


def adamw_pair(w, p, q, m, v, name):
    rows, cols = w.shape
    tr = _row_tile(rows, cols)

    def body(w_ref, p_ref, q_ref, m_ref, v_ref, g_ref, d_ref, nm_ref, nv_ref):
        g = p_ref[...] + q_ref[...]
        g_ref[...] = g
        d_ref[...], nm_ref[...], nv_ref[...] = _adamw(w_ref[...], g, m_ref[...], v_ref[...])

    spec = pl.BlockSpec((tr, cols), lambda i: (i, 0))
    return pl.pallas_call(body, grid=(rows // tr,), name=name, in_specs=[spec] * 5, out_specs=[spec] * 4,
                          out_shape=[SDS((rows, cols), F32)] * 4)(w, p, q, m, v)


_PACK_LAYOUT = (('b_ada', 6 * D), ('g_pre_mix', D), ('g_post_mix', D), ('g_pre_ffn', D), ('g_post_ffn', D),
                ('b_fgate', 128), ('conv_b', GW), ('conv_ln_g', GW), ('conv_ln_b', GW), ('conv_pw_b', GW),
                ('swa_sinks', 128), ('sgu_ln_g', GW), ('sgu_ln_b', GW), ('sgu_b', 4 * 128), ('g_group', D),
                ('ffn_conv_b', NUP))
_PACK_WIDTH = dict(_PACK_LAYOUT)
_PACK_ROW, _LAYER_ROWS = {}, 0
for _name, _width in _PACK_LAYOUT:
    _PACK_ROW[_name] = _LAYER_ROWS
    _LAYER_ROWS += -(-_width // D)
LOSS_ROW = DEPTH * _LAYER_ROWS
PACK_ROWS = -(-(LOSS_ROW + 1) // 8) * 8


def _segments(offset, width):
    out, s = [], 0
    while s < width:
        row, col = divmod(offset + s, D)
        n = min(width - s, D - col)
        out.append((s, row, col, n))
        s += n
    return out


def pack_small(grads, dmods, loss_row, name):
    ops, plan = [], []
    for l in range(DEPTH):
        pieces = [('b_ada', j * D, a) for j, a in enumerate(dmods[l])]
        pieces += [(n, 0, grads[l][n]) for n, _ in _PACK_LAYOUT if n not in ('b_ada', 'sgu_b', 'ffn_conv_b')]
        pieces += [('ffn_conv_b', j * DFF, a) for j, a in enumerate(grads[l]['ffn_conv_b'])]
        for n, off, a in pieces:
            plan.append((len(ops), l, n, off))
            ops.append(a)
    bts = [grads[l]['sgu_bt'] for l in range(DEPTH)]
    sws = [grads[l]['sgu_w'] for l in range(DEPTH)]
    n_vec = len(ops)

    def body(*refs):
        vec, bt, sw = refs[:n_vec], refs[n_vec:n_vec + DEPTH], refs[n_vec + DEPTH:n_vec + 2 * DEPTH]
        loss_ref, o_ref, ow_ref, scr = refs[n_vec + 2 * DEPTH:]
        o_ref[...] = jnp.zeros_like(o_ref)
        o_ref[LOSS_ROW:LOSS_ROW + 1, 0:128] = loss_ref[...]
        for idx, l, n, off in plan:
            base = l * _LAYER_ROWS + _PACK_ROW[n]
            width = min(vec[idx].shape[1], _PACK_WIDTH[n] - off)
            for s, row, col, lanes in _segments(off, width):
                o_ref[base + row:base + row + 1, col:col + lanes] = vec[idx][:, s:s + lanes]
        for l in range(DEPTH):
            scr[...] = bt[l][...].T
            row = l * _LAYER_ROWS + _PACK_ROW['sgu_b']
            for g in range(4):
                o_ref[row:row + 1, 128 * g:128 * (g + 1)] = scr[g:g + 1, :]
            ow_ref[l] = sw[l][...]

    vm = pl.BlockSpec(memory_space=pltpu.VMEM)
    return pl.pallas_call(body, name=name, in_specs=[vm] * (n_vec + 2 * DEPTH + 1), out_specs=[vm, vm],
                          out_shape=[SDS((PACK_ROWS, D), F32), SDS((DEPTH, 4, 128, 128), F32)],
                          scratch_shapes=[pltpu.VMEM((128, 128), F32)])(*ops, *bts, *sws, loss_row)


def adamw_small(gall, gall_w, w, m, v, name):
    names = [n for n in SMALL]
    n_par = len(names)

    def native(ref, n, l, col, lanes):
        if n == 'sgu_b':
            return ref.at[l, pl.ds(col // 128, 1), :]
        return ref.at[pl.ds(l, 1), pl.ds(col, lanes)]

    def body(*refs):
        ga_ref, gw_ref = refs[0], refs[1]
        w_refs, m_refs, v_refs = (refs[2 + k * n_par:2 + (k + 1) * n_par] for k in range(3))
        outs = refs[2 + 3 * n_par:2 + 7 * n_par]
        loss_ref, scr = refs[-2], refs[-1]
        g = ga_ref[0]
        for j in range(1, N_DEV):
            g = g + ga_ref[j]
        scr[...] = g
        loss_ref[...] = scr[LOSS_ROW:LOSS_ROW + 1, 0:128]
        for k, n in enumerate(names):
            o_g, o_d, o_m, o_v = outs[4 * k:4 * k + 4]
            if n == 'sgu_w':
                gw = gw_ref[0]
                for j in range(1, N_DEV):
                    gw = gw + gw_ref[j]
                o_g[...] = gw
                o_d[...], o_m[...], o_v[...] = _adamw(w_refs[k][...], gw, m_refs[k][...], v_refs[k][...])
                continue
            width = w_refs[k].shape[-1] if n != 'sgu_b' else 4 * 128
            for l in range(DEPTH):
                base = l * _LAYER_ROWS + _PACK_ROW[n]
                step = 128 if n == 'sgu_b' else D
                for col in range(0, width, step):
                    lanes = min(step, width - col)
                    row, lane0 = divmod(col, D)
                    gv = scr[base + row:base + row + 1, lane0:lane0 + lanes]
                    at = functools.partial(native, n=n, l=l, col=col, lanes=lanes)
                    at(o_g)[...] = gv
                    at(o_d)[...], at(o_m)[...], at(o_v)[...] = _adamw(at(w_refs[k])[...], gv, at(m_refs[k])[...],
                                                                      at(v_refs[k])[...])

    vm = pl.BlockSpec(memory_space=pltpu.VMEM)
    params = [d[n] for d in (w, m, v) for n in names]
    outs = pl.pallas_call(body, name=name, in_specs=[vm] * (2 + 3 * n_par), out_specs=[vm] * (4 * n_par + 1),
                          out_shape=[SDS(w[n].shape, F32) for n in names for _ in range(4)] + [SDS((1, 128), F32)],
                          scratch_shapes=[pltpu.VMEM((PACK_ROWS, D), F32)])(gall, gall_w, *params)
    return {n: tuple(outs[4 * k:4 * k + 4]) for k, n in enumerate(names)}, outs[-1]


ADA_COLS = 6 * D // N_CHIP
ADA_TN = 768


def ada_mod(c_all, w, b, name):
    def body(c_ref, w_ref, b_ref, o_ref):
        ca = jax.nn.silu(c_ref[...])
        o_ref[...] = jnp.dot(ca, w_ref[...], precision=lax.Precision.HIGHEST, preferred_element_type=F32) + b_ref[...]

    return pl.pallas_call(
        body, grid=(DEPTH, ADA_COLS // ADA_TN), name=name,
        in_specs=[pl.BlockSpec((N_DEV, D), lambda l, j: (0, 0)),
                  pl.BlockSpec((None, D, ADA_TN), lambda l, j: (l, 0, j)),
                  pl.BlockSpec((None, 1, ADA_TN), lambda l, j: (l, 0, j))],
        out_specs=pl.BlockSpec((None, N_DEV, ADA_TN), lambda l, j: (l, 0, j)),
        out_shape=SDS((DEPTH, N_DEV, ADA_COLS), F32))(c_all, w, b)


def ada_update(c_all_t, dmod, w, m, v, name):
    def body(c_ref, dm_ref, w_ref, m_ref, v_ref, g_ref, d_ref, nm_ref, nv_ref):
        ca = jax.nn.silu(c_ref[...])
        g = jnp.dot(ca, dm_ref[...], precision=lax.Precision.HIGHEST, preferred_element_type=F32)
        g_ref[...] = g
        d_ref[...], nm_ref[...], nv_ref[...] = _adamw(w_ref[...], g, m_ref[...], v_ref[...])

    wspec = pl.BlockSpec((None, D, ADA_TN), lambda l, j: (l, 0, j))
    return pl.pallas_call(
        body, grid=(DEPTH, ADA_COLS // ADA_TN), name=name,
        in_specs=[pl.BlockSpec((D, N_DEV), lambda l, j: (0, 0)),
                  pl.BlockSpec((None, N_DEV, ADA_TN), lambda l, j: (l, 0, j)), wspec, wspec, wspec],
        out_specs=[wspec] * 4, out_shape=[SDS((DEPTH, D, ADA_COLS), F32)] * 4)(c_all_t, dmod, w, m, v)


_CHIP_FLIPS = ((1, 0), (0, 1), (1, 1))
_DEV_FLIPS = tuple((a, b, c) for a in (0, 1) for b in (0, 1) for c in (0, 1) if (a, b, c) != (0, 0, 0))


def _position():
    return lax.axis_index("x"), lax.axis_index("y"), lax.axis_index("c")


def _hbm_call(body, arrs, out_shapes, n_remote, name):
    n = len(arrs)
    return pl.pallas_call(
        body, name=name, in_specs=[ANY] * n, out_specs=[ANY] * n, out_shape=out_shapes,
        scratch_shapes=[pltpu.SemaphoreType.DMA((n, n_remote)), pltpu.SemaphoreType.DMA((n, n_remote)),
                        pltpu.SemaphoreType.DMA((n,))])(*arrs)


_HBM = pl.BlockSpec(memory_space=pltpu.HBM)
_SEM = pl.BlockSpec(memory_space=pltpu.SEMAPHORE)
_EFFECT = pltpu.SideEffectType.DATAFLOW_SIDE_EFFECTING


GATHER, SCATTER, ALL, SWAP = "gather", "scatter", "all", "swap"
_PEERS = {GATHER: tuple((fx, fy, 0) for fx, fy in _CHIP_FLIPS), SCATTER: tuple((fx, fy, 0) for fx, fy in _CHIP_FLIPS),
          ALL: _DEV_FLIPS, SWAP: ((0, 0, 1),)}


def _peer_copies(kind, src, land, send, recv, arrivals):
    x, y, c = _position()
    out = []
    for k, (fx, fy, fc) in enumerate(_PEERS[kind]):
        tx, ty, tc = x ^ fx, y ^ fy, c ^ fc
        if kind == ALL:
            me, peer = 4 * x + 2 * y + c, 4 * tx + 2 * ty + tc
        else:
            me, peer = 2 * x + y, 2 * tx + ty
        if kind == SWAP:
            dst = land
        else:
            dst = land.at[peer if arrivals else me]
        out.append(pltpu.make_async_remote_copy(
            src_ref=src.at[peer] if kind == SCATTER else src, dst_ref=dst, send_sem=send.at[k], recv_sem=recv.at[k],
            device_id=(tx, ty, tc), device_id_type=MESH))
    return out


def _own_copy(kind, src, land, send):
    x, y, c = _position()
    me = 4 * x + 2 * y + c if kind == ALL else 2 * x + y
    return pltpu.make_async_copy(src, land.at[me], send.at[len(_PEERS[kind])])


def exchange_start(srcs, lands, kind, name):
    n = len(srcs)
    n_peers = len(_PEERS[kind])

    def body(*refs):
        src, land = refs[:n], refs[n:2 * n]
        send, recv = refs[2 * n:3 * n], refs[3 * n:4 * n]
        token = refs[-1]
        for a in range(n):
            for copy in _peer_copies(kind, src[a], land[a], send[a], recv[a], False):
                copy.start()
            if kind in (GATHER, ALL):
                _own_copy(kind, src[a], land[a], send[a]).start()
        token[...] = jnp.zeros_like(token)

    bufs = list(srcs) + list(lands)
    outs = pl.pallas_call(
        body, name=name, in_specs=[_HBM] * (2 * n),
        out_specs=[_SEM] * (2 * n) + [_HBM] * (2 * n) + [pl.BlockSpec(memory_space=pltpu.VMEM)],
        out_shape=[pltpu.SemaphoreType.DMA((n_peers + 1,))] * n + [pltpu.SemaphoreType.DMA((n_peers,))] * n
        + [pltpu.HBM(a.shape, a.dtype) for a in bufs] + [SDS((8, 128), F32)],
        input_output_aliases={i: 2 * n + i for i in range(2 * n)},
        compiler_params=pltpu.CompilerParams(has_side_effects=_EFFECT),
    )(*[pltpu.with_memory_space_constraint(a, pltpu.HBM) for a in bufs])
    flights = [(outs[a], outs[n + a], outs[2 * n + a], outs[3 * n + a]) for a in range(n)]
    return flights, outs[-1]


def exchange_wait(flights, kind, after, name):
    n = len(flights)

    def body(*refs):
        src, land = refs[:n], refs[n:2 * n]
        send, recv = refs[2 * n:3 * n], refs[3 * n:4 * n]
        for a in range(n):
            for arrival in _peer_copies(kind, src[a], land[a], send[a], recv[a], True):
                arrival.wait_send()
                arrival.wait_recv()
            if kind in (GATHER, ALL):
                _own_copy(kind, src[a], land[a], send[a]).wait()

    bufs = [f[2] for f in flights] + [f[3] for f in flights]
    sems = [f[0] for f in flights] + [f[1] for f in flights]
    outs = pl.pallas_call(
        body, name=name, in_specs=[_HBM] * (2 * n) + [_SEM] * (2 * n) + [ANY], out_specs=[_HBM] * (2 * n),
        out_shape=[pltpu.HBM(a.shape, a.dtype) for a in bufs],
        input_output_aliases={i: i for i in range(2 * n)},
        compiler_params=pltpu.CompilerParams(has_side_effects=_EFFECT),
    )(*bufs, *sems, after)
    return [(outs[a], outs[n + a]) for a in range(n)]


def chip_alltoall(arrs, name):
    n = len(arrs)

    def body(*refs):
        ins, outs = refs[:n], refs[n:2 * n]
        send, recv, loc = refs[2 * n:]
        x, y, c = _position()
        me = 2 * x + y
        copies = []
        for a in range(n):
            local = pltpu.make_async_copy(ins[a].at[me], outs[a].at[me], loc.at[a])
            local.start()
            copies.append(local)
            for k, (fx, fy) in enumerate(_CHIP_FLIPS):
                tx, ty = x ^ fx, y ^ fy
                cp = pltpu.make_async_remote_copy(
                    src_ref=ins[a].at[2 * tx + ty], dst_ref=outs[a].at[me], send_sem=send.at[a, k],
                    recv_sem=recv.at[a, k], device_id=(tx, ty, c), device_id_type=MESH)
                cp.start()
                copies.append(cp)
        for cp in copies:
            cp.wait()

    shapes = [SDS(a.shape, a.dtype) for a in arrs]
    return _hbm_call(body, arrs, shapes, 3, name)


def device_allgather(arrs, name):
    n = len(arrs)

    def body(*refs):
        ins, outs = refs[:n], refs[n:2 * n]
        send, recv, loc = refs[2 * n:]
        x, y, c = _position()
        me = 4 * x + 2 * y + c
        copies = []
        for a in range(n):
            local = pltpu.make_async_copy(ins[a], outs[a].at[me], loc.at[a])
            local.start()
            copies.append(local)
            for k, (fx, fy, fc) in enumerate(_DEV_FLIPS):
                cp = pltpu.make_async_remote_copy(
                    src_ref=ins[a], dst_ref=outs[a].at[me], send_sem=send.at[a, k], recv_sem=recv.at[a, k],
                    device_id=(x ^ fx, y ^ fy, c ^ fc), device_id_type=MESH)
                cp.start()
                copies.append(cp)
        for cp in copies:
            cp.wait()

    shapes = [SDS((N_DEV,) + a.shape, a.dtype) for a in arrs]
    return _hbm_call(body, arrs, shapes, 7, name)


def _pad_to(a, axis, size):
    pad = [(0, 0)] * a.ndim
    pad[axis] = (0, size - a.shape[axis])
    return jnp.pad(a, pad)


def _rows_to_z(wt):
    gap = Z_CONV_A - FG_END
    row = lax.broadcasted_iota(jnp.int32, (ZC, 1), 0)
    low = jnp.pad(wt, ((0, gap), (0, 0)))
    high = jnp.pad(wt, ((gap, 0), (0, 0)))
    return jnp.where(row < FG_END, low, jnp.where(row < Z_CONV_A, jnp.zeros_like(low), high))


def _rows_from_z(wt):
    row = lax.broadcasted_iota(jnp.int32, (IN_COLS, 1), 0)
    return jnp.where(row < FG_END, wt[:IN_COLS], wt[Z_CONV_A - FG_END:])


def _layer_fwd(l, x0, h1, mod, p, fetch, nxt):
    sh1, sc1, ga1, sh2, sc2, ga2 = mod
    t = f"l{l}_"
    wt = dict(fetch('w_in', h1))
    z = mm_nt([h1], wt['w_in'], t + "proj_in")
    cumc, cumr = fgate_fwd(z, p['b_fgate'], t + "fgate")
    y_fox, lse = fox_fwd(z, cumc, cumr, t + "fox")
    hc = conv_dw_fwd(z, wt['conv_w'], p['conv_b'], t + "conv_dw")
    y_conv = conv_pw_fwd(hc, p['conv_ln_g'], p['conv_ln_b'], wt['conv_pw_w'], p['conv_pw_b'], t + "conv_pw")
    y_swa = swa_fwd(z, p['swa_sinks'], t + "swa")
    y_sgu = sgu_fwd(z, p['sgu_ln_g'], p['sgu_ln_b'], p['sgu_w'], p['sgu_bt'], t + "sgu")
    ys = (y_fox, y_conv, y_swa, y_sgu)
    ycat = group_norm_fwd(ys, p['g_group'], t + "group_norm")
    wt.update(fetch('w_out', ycat))
    ymix = mm_nn([ycat], wt['w_out'], F32, t + "proj_out")
    x1, h2 = resid_modnorm_fwd(x0, ymix, ga1, p['g_post_mix'], p['g_pre_ffn'], sc2, sh2, t + "resid1_modnorm2")
    wt.update(fetch('ffn_w_up', h2))
    u = mm_nn([h2], wt['ffn_w_up'], F32, t + "ffn_up", tm=S)
    act = ffn_gate_fwd(u, wt['ffn_conv_w'], p['ffn_conv_b'], t + "ffn_gate")
    wt.update(fetch('ffn_w_down', act))
    yffn = mm_nn([act], wt['ffn_w_down'], F32, t + "ffn_down")
    if nxt is None:
        x2, h1_next = resid_fwd(x1, yffn, ga2, p['g_post_ffn'], t + "resid2"), None
    else:
        x2, h1_next = resid_modnorm_fwd(x1, yffn, ga2, p['g_post_ffn'], *nxt, t + "resid2_modnorm1")
    res = dict(x0=x0, h1=h1, z=z, cumc=cumc, cumr=cumr, y_fox=y_fox, lse=lse, hc=hc, ys=ys, ycat=ycat, ymix=ymix,
               x1=x1, h2=h2, u=u, act=act, yffn=yffn, wt=wt)
    return x2, h1_next, res


def _layer_bwd(l, dx2, head, mod, p, r, emit, prev):
    sh1, sc1, ga1, sh2, sc2, ga2 = mod
    t = f"l{l}_bwd_"
    g = {}
    wt = r['wt']
    dyffn, dga2, g['g_post_ffn'] = head
    tok = emit({'ffn_w_down': mm_tn([r['act']], [dyffn], t + "ffn_down_dw")})
    dact = mm_nt([dyffn], wt['ffn_w_down'], t + "ffn_down_dx")
    dug, duv, dwg, dwv, dbg, dbv = ffn_gate_bwd(r['u'], wt['ffn_conv_w'], p['ffn_conv_b'] + tok, dact, t + "ffn_gate")
    g['ffn_conv_b'] = (dbg, dbv)
    tok = emit({'ffn_w_up': mm_tn_halves(r['h2'], dug, duv, t + "ffn_up_dw")})
    dh2 = mm_nt_halves(dug, duv, wt['ffn_w_up'], t + "ffn_up_dx")
    dx1, g['g_pre_ffn'], dsc2, dsh2, dymix, dga1, g['g_post_mix'] = modnorm_resid_bwd(
        r['x1'], p['g_pre_ffn'] + tok, sc2, sh2, dh2, dx2, r['ymix'], ga1, p['g_post_mix'], t + "modnorm2_resid1")
    tok = emit({'w_out': mm_tn([r['ycat']], [dymix], t + "proj_out_dw")})
    dcat = mm_nt([dymix], wt['w_out'], t + "proj_out_dx")
    dy_fox, dy_conv, dy_swa, dy_sgu, g['g_group'] = group_norm_bwd(r['ys'], p['g_group'] + tok, dcat, t + "group_norm")
    z = r['z']
    fq, fk, fv, dcq, dck = fox_bwd(z, r['cumc'], r['cumr'], r['lse'], r['y_fox'], dy_fox, t + "fox")
    dzf, g['b_fgate'] = fgate_bwd(z, p['b_fgate'], dcq, dck, t + "fgate")
    dhc, g['conv_ln_g'], g['conv_ln_b'], dpw, g['conv_pw_b'] = conv_pw_bwd(
        r['hc'], p['conv_ln_g'], p['conv_ln_b'], wt['conv_pw_w'], p['conv_pw_b'], dy_conv, t + "conv_pw")
    ca, cg, dcw, g['conv_b'] = conv_dw_bwd(z, wt['conv_w'], dhc, t + "conv_dw")
    sq, sk, sv, g['swa_sinks'] = swa_bwd(z, p['swa_sinks'], dy_swa, t + "swa")
    gu, gv, g['sgu_ln_g'], g['sgu_ln_b'], g['sgu_w'], g['sgu_bt'] = sgu_bwd(
        z, p['sgu_ln_g'], p['sgu_ln_b'], p['sgu_w'], p['sgu_bt'], dy_sgu, t + "sgu")
    dz = [fq, fk, fv, dzf, ca, cg, sq, sk, sv, gu, gv]
    tok = emit({'w_in': mm_tn(dz, [r['h1']], t + "proj_in_dw"), 'conv_w': dcw, 'conv_pw_w': dpw,
                'ffn_conv_w': jnp.concatenate([dwg, dwv], axis=1)})
    dh1 = mm_nn(dz, wt['w_in'], F32, t + "proj_in_dx")
    if prev is None:
        dx0, g['g_pre_mix'], dsc1, dsh1 = modnorm_bwd(r['x0'], p['g_pre_mix'] + tok, sc1, sh1, dh1, dx1, t + "modnorm1")
        head = None
    else:
        dx0, g['g_pre_mix'], dsc1, dsh1, *head = modnorm_resid_bwd(
            r['x0'], p['g_pre_mix'] + tok, sc1, sh1, dh1, dx1, *prev, t + "modnorm1_resid2")
    return dx0, head, g, (dsh1, dsc1, dga1, dsh2, dsc2, dga2)


def _layer_params(l, w):
    def row(name, width=None):
        v = w[name][l].reshape(1, -1)
        return v if width is None else _pad_to(v, 1, width)
    return {
        'g_pre_mix': row('g_pre_mix'), 'g_post_mix': row('g_post_mix'), 'g_pre_ffn': row('g_pre_ffn'),
        'g_post_ffn': row('g_post_ffn'), 'b_fgate': row('b_fgate', 128), 'conv_b': row('conv_b'),
        'conv_ln_g': row('conv_ln_g'), 'conv_ln_b': row('conv_ln_b'), 'conv_pw_b': row('conv_pw_b'),
        'swa_sinks': row('swa_sinks', 128), 'sgu_ln_g': row('sgu_ln_g'), 'sgu_ln_b': row('sgu_ln_b'),
        'sgu_w': w['sgu_w'][l], 'sgu_bt': _pad_to(w['sgu_b'][l].T, 1, 128), 'g_group': row('g_group'),
        'ffn_conv_b': row('ffn_conv_b'),
    }


def _w_in_from_shards(s):
    return _rows_to_z(s.reshape(IN_COLS, D))


def _w_in_to_shards(g):
    return _rows_from_z(g).reshape(N_CHIP, IN_COLS // N_CHIP, D)


def _cols_from_shards(s, rows):
    _, r, n = s.shape
    return _pad_to(jnp.transpose(s, (1, 0, 2)).reshape(r, N_CHIP * n), 0, rows)


def _cols_to_shards(g, r):
    n = g.shape[1] // N_CHIP
    return jnp.transpose(g[:r].reshape(r, N_CHIP, n), (1, 0, 2))


_FROM_SHARDS = {
    'w_in': _w_in_from_shards,
    'w_out': lambda s: s.reshape(D, D),
    'ffn_w_up': lambda s: s,
    'ffn_w_down': lambda s: s.reshape(DFF, D),
    'conv_w': lambda s: _cols_from_shards(s, 32),
    'conv_pw_w': lambda s: s.reshape(GW, GW),
    'ffn_conv_w': lambda s: _cols_from_shards(s, 8),
}
_TO_SHARDS = {
    'w_in': _w_in_to_shards,
    'w_out': lambda g: g.reshape(N_CHIP, D // N_CHIP, D),
    'ffn_w_up': lambda g: g,
    'ffn_w_down': lambda g: g.reshape(N_CHIP, DFF // N_CHIP, D),
    'conv_w': lambda g: _cols_to_shards(g, CONV_K),
    'conv_pw_w': lambda g: g.reshape(N_CHIP, GW // N_CHIP, GW),
    'ffn_conv_w': lambda g: _cols_to_shards(g, FFN_K),
}


def _local_step(xs, target, mod, w, fetch, emit):
    params = [_layer_params(l, w) for l in range(DEPTH)]
    mods = [tuple(mod[l, j] for j in range(6)) for l in range(DEPTH)]
    pre = [(params[l]['g_pre_mix'], mods[l][1], mods[l][0]) for l in range(DEPTH)]
    post = [(mods[l][5], params[l]['g_post_ffn']) for l in range(DEPTH)]
    resids = []
    h1 = modnorm_fwd(xs, *pre[0], "l0_modnorm1")
    for l in range(DEPTH):
        nxt = pre[l + 1] if l + 1 < DEPTH else None
        xs, h1, r = _layer_fwd(l, xs, h1, mods[l], params[l], functools.partial(fetch, l), nxt)
        resids.append(r)
    dx, loss_row = loss_grad(xs, target, "loss")
    last = DEPTH - 1
    head = resid_bwd(resids[last]['yffn'], *post[last], dx, f"l{last}_bwd_resid2")
    grads, dmods = [None] * DEPTH, [None] * DEPTH
    for l in reversed(range(DEPTH)):
        prev = (resids[l - 1]['yffn'],) + post[l - 1] if l > 0 else None
        dx, head, grads[l], dmods[l] = _layer_bwd(l, dx, head, mods[l], params[l], resids[l], functools.partial(emit, l),
                                                  prev)
    return loss_row, dx, grads, dmods


_MATMUL_WEIGHTS = ('w_in', 'w_out', 'ffn_w_up', 'ffn_w_down')
_CONV_WEIGHTS = ('conv_w', 'conv_pw_w', 'ffn_conv_w')
_FETCH_GROUPS = {'w_in': ('w_in',) + _CONV_WEIGHTS, 'w_out': ('w_out',), 'ffn_w_up': ('ffn_w_up',),
                 'ffn_w_down': ('ffn_w_down',)}


def kernel(x, c, w_ada, b_ada, g_pre_mix, g_post_mix, g_pre_ffn, g_post_ffn, w_in, b_fgate, conv_w, conv_b, conv_ln_g, conv_ln_b, conv_pw_w, conv_pw_b, swa_sinks, sgu_ln_g, sgu_ln_b, sgu_w, sgu_b, g_group, w_out, ffn_w_up, ffn_conv_w, ffn_conv_b, ffn_w_down, loss_target, m_w_ada, m_b_ada, m_g_pre_mix, m_g_post_mix, m_g_pre_ffn, m_g_post_ffn, m_w_in, m_b_fgate, m_conv_w, m_conv_b, m_conv_ln_g, m_conv_ln_b, m_conv_pw_w, m_conv_pw_b, m_swa_sinks, m_sgu_ln_g, m_sgu_ln_b, m_sgu_w, m_sgu_b, m_g_group, m_w_out, m_ffn_w_up, m_ffn_conv_w, m_ffn_conv_b, m_ffn_w_down, v_w_ada, v_b_ada, v_g_pre_mix, v_g_post_mix, v_g_pre_ffn, v_g_post_ffn, v_w_in, v_b_fgate, v_conv_w, v_conv_b, v_conv_ln_g, v_conv_ln_b, v_conv_pw_w, v_conv_pw_b, v_swa_sinks, v_sgu_ln_g, v_sgu_ln_b, v_sgu_w, v_sgu_b, v_g_group, v_w_out, v_ffn_w_up, v_ffn_conv_w, v_ffn_conv_b, v_ffn_w_down):
    args = locals()
    w = {n: args[n] for n in WEIGHTS}
    m = {n: args['m_' + n] for n in WEIGHTS}
    v = {n: args['v_' + n] for n in WEIGHTS}
    xi, yi, ci = _position()
    chip = 2 * xi + yi

    (c_all,) = device_allgather([c], "gather_c")
    c_all = c_all.reshape(N_DEV, D)
    b_loc = lax.dynamic_slice_in_dim(b_ada, chip * ADA_COLS, ADA_COLS, axis=1).reshape(DEPTH, 1, ADA_COLS)
    mod_all = ada_mod(c_all, w_ada, b_loc, "ada_mod")
    mine = lax.dynamic_index_in_dim(mod_all.reshape(DEPTH, N_CHIP, 2, ADA_COLS), ci, axis=2, keepdims=False)
    (mod4,) = chip_alltoall([jnp.transpose(mine, (1, 0, 2))], "scatter_mod")

    w_in_t, m_in_t, v_in_t = (jnp.transpose(a, (2, 0, 1)) for a in (w_in, m_w_in, v_w_in))

    def shard(n, l):
        a = w_in_t[:, l] if n == 'w_in' else w[n][l]
        return a.astype(BF16) if n in _MATMUL_WEIGHTS else a

    keys = [(n, l) for l in range(DEPTH) for n in _CONV_WEIGHTS]
    keys += [(n, l) for l in range(DEPTH) for n in _MATMUL_WEIGHTS]
    srcs = [shard(n, l) for n, l in keys]
    mod4, srcs = lax.optimization_barrier((mod4, srcs))
    lands = [lax.empty((N_CHIP,) + s.shape, s.dtype) for s in srcs]
    flights, token = exchange_start(srcs, lands, GATHER, "gather_start")
    gathering = dict(zip(keys, flights))
    mod = jnp.transpose(mod4, (1, 0, 2)).reshape(DEPTH, 6, 1, D) + token[0:1, 0:1]

    def fetch(l, name, after):
        names = _FETCH_GROUPS[name]
        landed = exchange_wait([gathering[(n, l)] for n in names], GATHER, after, f"gather_wait_l{l}_{name}")
        return {n: _FROM_SHARDS[n](land) for n, (_, land) in zip(names, landed)}

    scattering = {}

    def emit(l, grads):
        names = list(grads)
        pieces = [_TO_SHARDS[n](grads[n]) for n in names]
        lands = [lax.empty(p.shape, p.dtype) for p in pieces]
        started, token = exchange_start(pieces, lands, SCATTER, f"scatter_start_l{l}_{names[0]}")
        scattering.update({(n, l): f for n, f in zip(names, started)})
        return token[0:1, 0:1]

    loss_row, dx, grads, dmods = _local_step(x.reshape(S, D), loss_target.reshape(S, D), mod, w, fetch, emit)
    grad_x = dx.reshape(1, S, D)

    small_srcs = pack_small(grads, dmods, loss_row, "pack_small")
    small_flights, small_token = exchange_start(small_srcs, [lax.empty((N_DEV,) + a.shape, F32) for a in small_srcs], ALL,
                                                "gather_small_start")

    order = list(scattering)
    landed = dict(zip(order, exchange_wait([scattering[k] for k in order], SCATTER, small_token, "scatter_wait")))
    chip1 = chip.astype(jnp.int32).reshape(1)
    part = {'w_in': sum_pieces_t(chip1, [landed[('w_in', l)][1] for l in range(DEPTH)],
                                 [landed[('w_in', l)][0] for l in range(DEPTH)], "sum_w_in")}
    for n in SHARDED[1:]:
        cols = w[n].shape[-1]
        for l in range(DEPTH):
            src, land = landed[(n, l)]
            part[n] = sum_pieces(chip1, land.reshape(N_CHIP, -1, cols), src.reshape(N_CHIP, -1, cols), l, part.get(n),
                                 f"sum_{n}_l{l}")
        part[n] = part[n].reshape(-1, cols)
    parts = [part[n] for n in SHARDED]
    swap_flights, swap_token = exchange_start(parts, [lax.empty(p.shape, F32) for p in parts], SWAP, "swap_start")

    g_all, gw_all = (land for _, land in exchange_wait(small_flights, ALL, swap_token, "gather_small_wait"))
    out, loss_sum = adamw_small(g_all, gw_all, w, m, v, "adamw_small")
    loss = loss_sum[0, 0]

    dmod_all = g_all[:, :DEPTH * _LAYER_ROWS].reshape(N_DEV, DEPTH, _LAYER_ROWS * D)[:, :, :6 * D]
    dmod_loc = jnp.transpose(lax.dynamic_slice_in_dim(dmod_all, chip * ADA_COLS, ADA_COLS, axis=2), (1, 0, 2))
    out['w_ada'] = tuple(ada_update(c_all.T, dmod_loc, w_ada, m['w_ada'], v['w_ada'], "adamw_ada"))

    swapped = exchange_wait(swap_flights, SWAP, out['w_ada'][0], "swap_wait")
    part = {n: own for n, (own, _) in zip(SHARDED, swapped)}
    other = {n: got for n, (_, got) in zip(SHARDED, swapped)}
    res = adamw_pair_t(w_in_t, part['w_in'], other['w_in'], m_in_t, v_in_t, "adamw_w_in")
    out['w_in'] = tuple(jnp.transpose(a, (1, 2, 0)) for a in res)
    for n in SHARDED[1:]:
        cols = w[n].shape[-1]
        res = adamw_pair(w[n].reshape(-1, cols), part[n], other[n], m[n].reshape(-1, cols), v[n].reshape(-1, cols),
                         "adamw_" + n)
        out[n] = tuple(a.reshape(w[n].shape) for a in res)

    return (loss, grad_x, *[out[n][0] for n in WEIGHTS], *[out[n][1] for n in WEIGHTS],
            *[out[n][2] for n in WEIGHTS], *[out[n][3] for n in WEIGHTS])
```
